```python
import jax, jax.numpy as jnp
from jax import lax
import numpy as np

D_MODEL = 1024
BATCH = 1
SEQ = 16384
DEPTH = 2
DEC_BATCH = 8
DEC_SEQ = 64
PAST_LEN = 4096

CHUNK = 64
MIX_WIDTH = D_MODEL
GMLP_WIDTH = MIX_WIDTH // 2
GMLP_GROUPS = 4
GMLP_GROUP_DIM = GMLP_WIDTH // GMLP_GROUPS
GMLP_CHUNK = 128
MLA_HEADS = 4
QK_NOPE_DIM = 128
QK_ROPE_DIM = 64
QK_HEAD_DIM = QK_NOPE_DIM + QK_ROPE_DIM
V_HEAD_DIM = 128
Q_LORA_RANK = 384
KV_LORA_RANK = 256
ROPE_THETA = 10000.0
IN_WIDTH = 2 * GMLP_WIDTH + Q_LORA_RANK + KV_LORA_RANK + QK_ROPE_DIM
N_EXPERTS = 16
N_GROUPS = 4
EXPERTS_PER_GROUP = N_EXPERTS // N_GROUPS
TOP_K = 2
EXPERT_FF = 512
SHARED_FF = 512
QUERY_BLOCK = 128
EPS = 1e-6

kernel_name = "hybrid_gmlp_mla_moe_streaming_step"


def rms_norm(x, g):
    xf = x.astype(jnp.float32)
    y = xf * lax.rsqrt(jnp.mean(xf * xf, axis=-1, keepdims=True) + EPS)
    return (y * g.astype(jnp.float32)).astype(x.dtype)


def rope(x, pos):
    half = x.shape[-1] // 2
    inv = ROPE_THETA ** (-jnp.arange(half, dtype=jnp.float32) / half)
    ang = pos.astype(jnp.float32)[:, None] * inv[None, :]
    cos = jnp.cos(ang)[None, :, None, :]
    sin = jnp.sin(ang)[None, :, None, :]
    x1 = x[..., :half].astype(jnp.float32)
    x2 = x[..., half:].astype(jnp.float32)
    return jnp.concatenate([x1 * cos - x2 * sin, x1 * sin + x2 * cos], axis=-1).astype(x.dtype)


def gmlp_spatial(u, v, ws, b):
    B, S, G, C = v.shape
    pad = (-S) % GMLP_CHUNK
    vp = jnp.pad(v, ((0, 0), (0, pad), (0, 0), (0, 0)))
    n = (S + pad) // GMLP_CHUNK
    vp = vp.reshape(B, n, GMLP_CHUNK, G, C)
    causal = jnp.tril(jnp.ones((GMLP_CHUNK, GMLP_CHUNK), dtype=bool))
    wm = jnp.where(causal[None], ws, jnp.zeros_like(ws))
    s = jnp.einsum('gts,bnsgc->bntgc', wm, vp) + b.T[None, None, :, :, None]
    s = s.reshape(B, n * GMLP_CHUNK, G, C)[:, :S]
    return u * s


def block_attention(q, k, v, q_pos, k_pos):
    B, Sq, H, Dk = q.shape
    Dv = v.shape[-1]
    qb = QUERY_BLOCK if Sq % QUERY_BLOCK == 0 else Sq
    nb = Sq // qb
    qs = q.reshape(B, nb, qb, H, Dk).transpose(1, 0, 2, 3, 4)
    ps = q_pos.reshape(nb, qb)
    k_chunk = k_pos // CHUNK
    scale = 1.0 / np.sqrt(Dk)

    def one_block(args):
        q_blk, p_blk = args
        s = jnp.einsum('bqhd,bkhd->bhqk', q_blk, k).astype(jnp.float32) * scale
        allowed = k_chunk[None, :] <= (p_blk // CHUNK)[:, None]
        s = jnp.where(allowed[None, None], s, -1e30)
        p = jax.nn.softmax(s, axis=-1).astype(v.dtype)
        return jnp.einsum('bhqk,bkhd->bqhd', p, v)

    out = lax.map(one_block, (qs, ps))
    return out.transpose(1, 0, 2, 3, 4).reshape(B, Sq, H, Dv)


def swiglu(t, wg, wu, wd):
    return (jax.nn.silu(t @ wg) * (t @ wu)) @ wd


def moe(h, router_w, router_bias, w_gate, w_up, w_down, sh_gate, sh_up, sh_down):
    B, S, D = h.shape
    t = h.reshape(-1, D)
    scores = jax.nn.sigmoid(t.astype(jnp.float32) @ router_w.astype(jnp.float32))
    biased = scores + router_bias.astype(jnp.float32)
    grouped = biased.reshape(-1, N_GROUPS, EXPERTS_PER_GROUP)
    group_score = jnp.sum(lax.top_k(grouped, TOP_K)[0], axis=-1)
    best_group = jnp.argmax(group_score, axis=-1)
    in_group = (jnp.arange(N_EXPERTS) // EXPERTS_PER_GROUP)[None, :] == best_group[:, None]
    masked = jnp.where(in_group, biased, -jnp.inf)
    _, idx = lax.top_k(masked, TOP_K)
    w = jnp.take_along_axis(scores, idx, axis=-1)
    w = w / jnp.sum(w, axis=-1, keepdims=True)
    gates = jnp.sum(jax.nn.one_hot(idx, N_EXPERTS, dtype=jnp.float32) * w[..., None], axis=1)
    gates = gates.astype(h.dtype)
    out = swiglu(t, sh_gate, sh_up, sh_down)
    for e in range(N_EXPERTS):
        out = out + gates[:, e:e + 1] * swiglu(t, w_gate[e], w_up[e], w_down[e])
    return out.reshape(B, S, D)


def trunk_layer(x, c, pos, past_lat, past_rope, p, router_w, router_bias):
    B, S, _ = x.shape
    mod = jax.nn.silu(c) @ p['w_ada'] + p['b_ada']
    sh1, sc1, g1, sh2, sc2, g2 = jnp.split(mod, 6, axis=-1)
    h = rms_norm(x, p['ln1']) * (1 + sc1[:, None]) + sh1[:, None]
    z = h @ p['w_in']
    o1 = GMLP_WIDTH
    o2 = 2 * GMLP_WIDTH
    o3 = o2 + Q_LORA_RANK
    o4 = o3 + KV_LORA_RANK
    u, v, q_lat, kv_lat, k_pe = z[..., :o1], z[..., o1:o2], z[..., o2:o3], z[..., o3:o4], z[..., o4:]

    u = jax.nn.gelu(u).reshape(B, S, GMLP_GROUPS, GMLP_GROUP_DIM)
    v = rms_norm(jax.nn.gelu(v).reshape(B, S, GMLP_GROUPS, GMLP_GROUP_DIM), p['gmlp_v_g'])
    a_out = gmlp_spatial(u, v, p['gmlp_ws'], p['gmlp_b']).reshape(B, S, GMLP_WIDTH)

    q = (rms_norm(q_lat, p['q_a_g']) @ p['w_qb']).reshape(B, S, MLA_HEADS, QK_HEAD_DIM)
    q = jnp.concatenate([q[..., :QK_NOPE_DIM], rope(q[..., QK_NOPE_DIM:], pos)], axis=-1)
    c_kv = rms_norm(kv_lat, p['kv_a_g'])
    k_pe = rope(k_pe[:, :, None, :], pos)[:, :, 0, :]
    if past_lat is None:
        all_lat, all_pe, k_pos = c_kv, k_pe, pos
    else:
        all_lat = jnp.concatenate([past_lat, c_kv], axis=1)
        all_pe = jnp.concatenate([past_rope, k_pe], axis=1)
        k_pos = jnp.arange(past_lat.shape[1] + S, dtype=jnp.int32)
    Sk = all_lat.shape[1]
    kv = (all_lat @ p['w_kvb']).reshape(B, Sk, MLA_HEADS, QK_NOPE_DIM + V_HEAD_DIM)
    k = jnp.concatenate([kv[..., :QK_NOPE_DIM],
                         jnp.broadcast_to(all_pe[:, :, None, :], (B, Sk, MLA_HEADS, QK_ROPE_DIM))], axis=-1)
    v_att = kv[..., QK_NOPE_DIM:]
    q = rms_norm(q, p['q_norm_g'])
    k = rms_norm(k, p['k_norm_g'])
    b_out = block_attention(q, k, v_att, pos, k_pos).reshape(B, S, MLA_HEADS * V_HEAD_DIM)

    mix = jnp.concatenate([a_out, b_out], axis=-1) @ p['w_out']
    x = x + g1[:, None] * mix

    h2 = rms_norm(x, p['ln2']) * (1 + sc2[:, None]) + sh2[:, None]
    x = x + g2[:, None] * moe(h2, router_w, router_bias, p['exp_w_gate'], p['exp_w_up'], p['exp_w_down'],
                              p['sh_w_gate'], p['sh_w_up'], p['sh_w_down'])
    return x, c_kv, k_pe, v.reshape(B, S, GMLP_WIDTH)


def setup_inputs(seed: int = 0) -> dict:
    key = jax.random.key(seed)
    ks = jax.random.split(key, 32)
    f32 = jnp.float32
    nrm = lambda k, shape, s: jax.random.normal(k, shape, f32) * s
    gain = lambda k, shape: 1.0 + 0.05 * jax.random.normal(k, shape, f32)
    D = D_MODEL
    return {
        "x_prompt": nrm(ks[0], (BATCH, SEQ, D), 1.0),
        "x_sample": nrm(ks[1], (DEC_BATCH, DEC_SEQ, D), 1.0),
        "cache_kv_latent": nrm(ks[2], (DEPTH, DEC_BATCH, PAST_LEN, KV_LORA_RANK), 1.0),
        "cache_k_rope": nrm(ks[3], (DEPTH, DEC_BATCH, PAST_LEN, QK_ROPE_DIM), 1.0),
        "c_prompt": nrm(ks[4], (BATCH, D), 1.0),
        "c_sample": nrm(ks[5], (DEC_BATCH, D), 1.0),
        "w_ada": nrm(ks[6], (DEPTH, D, 6 * D), 0.5 * D ** -0.5),
        "b_ada": nrm(ks[7], (DEPTH, 6 * D), 0.02),
        "ln1_g": gain(ks[8], (DEPTH, D)),
        "w_in": nrm(ks[9], (DEPTH, D, IN_WIDTH), D ** -0.5),
        "gmlp_v_g": gain(ks[10], (DEPTH, GMLP_GROUPS, GMLP_GROUP_DIM)),
        "gmlp_ws": nrm(ks[11], (DEPTH, GMLP_GROUPS, GMLP_CHUNK, GMLP_CHUNK), GMLP_CHUNK ** -0.5),
        "gmlp_b": 1.0 + nrm(ks[12], (DEPTH, GMLP_GROUPS, GMLP_CHUNK), 0.1),
        "q_a_g": gain(ks[13], (DEPTH, Q_LORA_RANK)),
        "w_qb": nrm(ks[14], (DEPTH, Q_LORA_RANK, MLA_HEADS * QK_HEAD_DIM), Q_LORA_RANK ** -0.5),
        "kv_a_g": gain(ks[15], (DEPTH, KV_LORA_RANK)),
        "w_kvb": nrm(ks[16], (DEPTH, KV_LORA_RANK, MLA_HEADS * (QK_NOPE_DIM + V_HEAD_DIM)), KV_LORA_RANK ** -0.5),
        "q_norm_g": gain(ks[17], (DEPTH, QK_HEAD_DIM)),
        "k_norm_g": gain(ks[18], (DEPTH, QK_HEAD_DIM)),
        "w_out": nrm(ks[19], (DEPTH, MIX_WIDTH, D), MIX_WIDTH ** -0.5),
        "ln2_g": gain(ks[20], (DEPTH, D)),
        "router_w": nrm(ks[21], (D, N_EXPERTS), D ** -0.5),
        "router_bias": nrm(ks[22], (N_EXPERTS,), 0.01),
        "exp_w_gate": nrm(ks[23], (DEPTH, N_EXPERTS, D, EXPERT_FF), D ** -0.5),
        "exp_w_up": nrm(ks[24], (DEPTH, N_EXPERTS, D, EXPERT_FF), D ** -0.5),
        "exp_w_down": nrm(ks[25], (DEPTH, N_EXPERTS, EXPERT_FF, D), EXPERT_FF ** -0.5),
        "sh_w_gate": nrm(ks[26], (DEPTH, D, SHARED_FF), D ** -0.5),
        "sh_w_up": nrm(ks[27], (DEPTH, D, SHARED_FF), D ** -0.5),
        "sh_w_down": nrm(ks[28], (DEPTH, SHARED_FF, D), SHARED_FF ** -0.5),
    }


def reference(x_prompt, x_sample, cache_kv_latent, cache_k_rope, c_prompt, c_sample,
              w_ada, b_ada, ln1_g, w_in, gmlp_v_g, gmlp_ws, gmlp_b, q_a_g, w_qb, kv_a_g, w_kvb,
              q_norm_g, k_norm_g, w_out, ln2_g, router_w, router_bias,
              exp_w_gate, exp_w_up, exp_w_down, sh_w_gate, sh_w_up, sh_w_down):
    S_p = x_prompt.shape[1]
    S_s = x_sample.shape[1]
    past = cache_kv_latent.shape[2]
    pos_p = jnp.arange(S_p, dtype=jnp.int32)
    pos_s = past + jnp.arange(S_s, dtype=jnp.int32)
    xp, xs = x_prompt, x_sample
    lat_p, pe_p, lat_s, pe_s, v_s = [], [], [], [], []
    for l in range(DEPTH):
        p = {
            'w_ada': w_ada[l], 'b_ada': b_ada[l], 'ln1': ln1_g[l], 'w_in': w_in[l],
            'gmlp_v_g': gmlp_v_g[l], 'gmlp_ws': gmlp_ws[l], 'gmlp_b': gmlp_b[l],
            'q_a_g': q_a_g[l], 'w_qb': w_qb[l], 'kv_a_g': kv_a_g[l], 'w_kvb': w_kvb[l],
            'q_norm_g': q_norm_g[l], 'k_norm_g': k_norm_g[l], 'w_out': w_out[l], 'ln2': ln2_g[l],
            'exp_w_gate': exp_w_gate[l], 'exp_w_up': exp_w_up[l], 'exp_w_down': exp_w_down[l],
            'sh_w_gate': sh_w_gate[l], 'sh_w_up': sh_w_up[l], 'sh_w_down': sh_w_down[l],
        }
        xp, ckv_p, kpe_p, _ = trunk_layer(xp, c_prompt, pos_p, None, None, p, router_w, router_bias)
        xs, ckv_s, kpe_s, vr_s = trunk_layer(xs, c_sample, pos_s, cache_kv_latent[l], cache_k_rope[l],
                                             p, router_w, router_bias)
        lat_p.append(ckv_p)
        pe_p.append(kpe_p)
        lat_s.append(ckv_s)
        pe_s.append(kpe_s)
        v_s.append(vr_s)
    new_kv_latent_prompt = jnp.stack(lat_p)
    new_k_rope_prompt = jnp.stack(pe_p)
    new_kv_latent_sample = jnp.stack(lat_s)
    new_k_rope_sample = jnp.stack(pe_s)
    new_gmlp_v_sample = jnp.stack(v_s)
    return (xp, xs, new_kv_latent_prompt, new_k_rope_prompt, new_kv_latent_sample, new_k_rope_sample, new_gmlp_v_sample)
```

```python
import functools
import math

import jax
import jax.numpy as jnp
from jax import lax
from jax.experimental import pallas as pl
from jax.experimental.pallas import tpu as pltpu

F32 = jnp.float32
BF16 = jnp.bfloat16

D_MODEL = 1024
CHUNK = 64
GMLP_WIDTH = 512
GMLP_GROUPS = 4
GMLP_CHUNK = 128
MLA_HEADS = 4
QK_NOPE_DIM = 128
QK_ROPE_DIM = 64
QK_HEAD_DIM = 192
V_HEAD_DIM = 128
Q_LORA_RANK = 384
KV_LORA_RANK = 256
ROPE_THETA = 10000.0
N_EXPERTS = 16
N_GROUPS = 4
EXPERTS_PER_GROUP = 4
EXPERT_FF = 512
EPS = 1e-6

LANES = 128
TILE = 512
SUB = 64
N_SUB = TILE // SUB
QK_PAD = 256
IN_COLS = 2 * GMLP_WIDTH + Q_LORA_RANK + KV_LORA_RANK + 2 * LANES
Q_COLS = MLA_HEADS * 3 * LANES
ATT_TQ = 512
ATT_TK = 512
VMEM_LIMIT = 56 * 1024 * 1024
LOG2E = 1.4426950408889634


def _cparams(n_axes):
    return pltpu.CompilerParams(dimension_semantics=("arbitrary",) * n_axes,
                                vmem_limit_bytes=VMEM_LIMIT)


def _rms(x, eps=EPS):
    return x * lax.rsqrt(jnp.mean(x * x, axis=-1, keepdims=True) + eps)


def _gelu(x):
    c = math.sqrt(2.0 / math.pi)
    return 0.5 * x * (1.0 + jnp.tanh(c * (x + 0.044715 * (x * x * x))))


def _modulate(h, scale, shift):
    h3 = h.reshape(N_SUB, SUB, h.shape[-1])
    h3 = h3 * (1.0 + scale[:, None, :]) + shift[:, None, :]
    return h3.reshape(h.shape)


def _gated_add(x, gate, y):
    y3 = y.reshape(N_SUB, SUB, y.shape[-1]) * gate[:, None, :]
    return x + y3.reshape(y.shape)


def _ada_kernel(c_ref, w_ref, b_ref, o_ref):
    c = c_ref[...]
    cs = (c / (1.0 + jnp.exp(-c))).astype(BF16)
    w = w_ref[0].astype(BF16)
    o_ref[0] = jnp.dot(cs, w, preferred_element_type=F32) + b_ref[0]


def _ada_call(c_all, w_ada, b_ada):
    depth = w_ada.shape[0]
    nblk = w_ada.shape[2] // D_MODEL
    return pl.pallas_call(
        _ada_kernel,
        grid=(depth, nblk),
        in_specs=[
            pl.BlockSpec((16, D_MODEL), lambda l, j: (0, 0)),
            pl.BlockSpec((1, D_MODEL, D_MODEL), lambda l, j: (l, 0, j)),
            pl.BlockSpec((1, 1, D_MODEL), lambda l, j: (l, 0, j)),
        ],
        out_specs=pl.BlockSpec((1, 16, D_MODEL), lambda l, j: (l, 0, j)),
        out_shape=jax.ShapeDtypeStruct((depth, 16, w_ada.shape[2]), F32),
        compiler_params=_cparams(2),
        name="ada_mod",
    )(c_all, w_ada, b_ada.reshape(depth, 1, -1))


def _mix_in_kernel(n_prompt_tiles, xp_ref, xs_ref, sc_ref, sh_ref, ln_ref, win_ref, cos_ref, sin_ref,
                   gv_ref, gw_ref, gb_ref, qag_ref, wqb_ref, kvag_ref, wkvb_ref, gq_ref, gk_ref,
                   a_ref, q_ref, k_ref, v_ref, ckv_ref, kpe_ref, vn_ref):
    i = pl.program_id(0)
    is_sample = i >= n_prompt_tiles
    x = jnp.where(is_sample, xs_ref[...], xp_ref[...])
    h = _rms(x) * ln_ref[...]
    h = _modulate(h, sc_ref[0], sh_ref[0])
    z = jnp.dot(h.astype(BF16), win_ref[...], preferred_element_type=F32)

    u = _gelu(z[:, :GMLP_WIDTH])
    v = _gelu(z[:, GMLP_WIDTH:2 * GMLP_WIDTH])
    gv = gv_ref[...]
    vn_parts = []
    for g in range(GMLP_GROUPS):
        sl = slice(g * LANES, (g + 1) * LANES)
        vn_parts.append(_rms(v[:, sl]) * gv[:, sl])

    @pl.when(is_sample)
    def _():
        for g in range(GMLP_GROUPS):
            vn_ref[:, g * LANES:(g + 1) * LANES] = vn_parts[g]

    for g in range(GMLP_GROUPS):
        sl = slice(g * LANES, (g + 1) * LANES)
        vb = vn_parts[g].astype(BF16)
        wg = gw_ref[0, g]
        bg = gb_ref[0, g]
        for c in range(TILE // GMLP_CHUNK):
            rows = slice(c * GMLP_CHUNK, (c + 1) * GMLP_CHUNK)
            s = jnp.dot(wg, vb[rows], preferred_element_type=F32) + bg
            a_ref[rows, sl] = (u[rows, sl] * s).astype(BF16)

    cos = cos_ref[...]
    sin = sin_ref[...]

    o2 = 2 * GMLP_WIDTH
    o3 = o2 + Q_LORA_RANK
    o4 = o3 + KV_LORA_RANK
    ql = _rms(z[:, o2:o3]) * qag_ref[...]
    qq = jnp.dot(ql.astype(BF16), wqb_ref[...], preferred_element_type=F32)
    gq = gq_ref[...]
    for hd in range(MLA_HEADS):
        base = hd * 3 * LANES
        nope = qq[:, base:base + LANES]
        rope = qq[:, base + LANES:base + 2 * LANES] * cos + qq[:, base + 2 * LANES:base + 3 * LANES] * sin
        ss = jnp.sum(nope * nope, axis=-1, keepdims=True) + jnp.sum(rope * rope, axis=-1, keepdims=True)
        rinv = lax.rsqrt(ss * (1.0 / QK_HEAD_DIM) + EPS)
        q_ref[hd, :, :LANES] = (nope * rinv * gq[:, :LANES]).astype(BF16)
        q_ref[hd, :, LANES:] = (rope * rinv * gq[:, LANES:]).astype(BF16)

    ckv = _rms(z[:, o3:o4]) * kvag_ref[...]
    ckv_ref[...] = ckv
    kpe = z[:, o4:o4 + LANES] * cos + z[:, o4 + LANES:o4 + 2 * LANES] * sin
    kpe_ref[...] = kpe
    kv = jnp.dot(ckv.astype(BF16), wkvb_ref[...], preferred_element_type=F32)
    gk = gk_ref[...]
    kpe_ss = jnp.sum(kpe * kpe, axis=-1, keepdims=True)
    for hd in range(MLA_HEADS):
        nope = kv[:, hd * LANES:(hd + 1) * LANES]
        ss = jnp.sum(nope * nope, axis=-1, keepdims=True) + kpe_ss
        rinv = lax.rsqrt(ss * (1.0 / QK_HEAD_DIM) + EPS)
        k_ref[hd, :, :LANES] = (nope * rinv * gk[:, :LANES]).astype(BF16)
        k_ref[hd, :, LANES:] = (kpe * rinv * gk[:, LANES:]).astype(BF16)
        v_ref[hd] = kv[:, (MLA_HEADS + hd) * LANES:(MLA_HEADS + hd + 1) * LANES].astype(BF16)


def _mix_in_call(xp, xs, sc1, sh1, w):
    rp, rs = xp.shape[0], xs.shape[0]
    npt = rp // TILE
    nt = npt + rs // TILE
    r = rp + rs
    last = npt - 1

    def full(a):
        nd = a.ndim
        return pl.BlockSpec(a.shape, lambda i: (0,) * nd)

    def variant(a):
        nd = a.ndim
        return pl.BlockSpec((1,) + a.shape[1:], lambda i: (i // npt,) + (0,) * (nd - 1))

    row = lambda width: pl.BlockSpec((TILE, width), lambda i: (i, 0))
    head = lambda width: pl.BlockSpec((MLA_HEADS, TILE, width), lambda i: (0, i, 0))
    in_specs = [
        pl.BlockSpec((TILE, D_MODEL), lambda i: (jnp.minimum(i, last), 0)),
        pl.BlockSpec((TILE, D_MODEL), lambda i: (0, 0)),
        variant(sc1), variant(sh1), full(w["ln1"]), full(w["w_in"]),
        row(LANES), row(LANES),
        full(w["gv"]), variant(w["gw"]), variant(w["gb"]),
        full(w["q_a_g"]), full(w["w_qb"]), full(w["kv_a_g"]), full(w["w_kvb"]),
        full(w["gq"]), full(w["gk"]),
    ]
    out_specs = [
        row(GMLP_WIDTH), head(QK_PAD), head(QK_PAD), head(V_HEAD_DIM),
        row(KV_LORA_RANK), row(LANES),
        pl.BlockSpec((TILE, GMLP_WIDTH), lambda i: (0, 0)),
    ]
    out_shape = [
        jax.ShapeDtypeStruct((r, GMLP_WIDTH), BF16),
        jax.ShapeDtypeStruct((MLA_HEADS, r, QK_PAD), BF16),
        jax.ShapeDtypeStruct((MLA_HEADS, r, QK_PAD), BF16),
        jax.ShapeDtypeStruct((MLA_HEADS, r, V_HEAD_DIM), BF16),
        jax.ShapeDtypeStruct((r, KV_LORA_RANK), F32),
        jax.ShapeDtypeStruct((r, LANES), F32),
        jax.ShapeDtypeStruct((rs, GMLP_WIDTH), F32),
    ]
    return pl.pallas_call(
        functools.partial(_mix_in_kernel, npt),
        grid=(nt,), in_specs=in_specs, out_specs=out_specs, out_shape=out_shape,
        compiler_params=_cparams(1), name="mix_in",
    )(xp, xs, sc1, sh1, w["ln1"], w["w_in"], w["cos"], w["sin"], w["gv"], w["gw"], w["gb"],
      w["q_a_g"], w["w_qb"], w["kv_a_g"], w["w_kvb"], w["gq"], w["gk"])


def _cache_kv_kernel(lat_ref, pe_ref, wkvb_ref, gk_ref, k_ref, v_ref):
    lat = lat_ref[0]
    kpe = pe_ref[0]
    kv = jnp.dot(lat.astype(BF16), wkvb_ref[0], preferred_element_type=F32)
    gk = gk_ref[0]
    kpe_ss = jnp.sum(kpe * kpe, axis=-1, keepdims=True)
    for hd in range(MLA_HEADS):
        nope = kv[:, hd * LANES:(hd + 1) * LANES]
        ss = jnp.sum(nope * nope, axis=-1, keepdims=True) + kpe_ss
        rinv = lax.rsqrt(ss * (1.0 / QK_HEAD_DIM) + EPS)
        k_ref[0, hd, :, :LANES] = (nope * rinv * gk[:, :LANES]).astype(BF16)
        k_ref[0, hd, :, LANES:] = (kpe * rinv * gk[:, LANES:]).astype(BF16)
        v_ref[0, hd] = kv[:, (MLA_HEADS + hd) * LANES:(MLA_HEADS + hd + 1) * LANES].astype(BF16)


def _cache_kv_call(lat, pe, w_kvb, gk):
    depth, rows, _ = lat.shape
    t = 1024
    return pl.pallas_call(
        _cache_kv_kernel,
        grid=(depth, rows // t),
        in_specs=[
            pl.BlockSpec((1, t, KV_LORA_RANK), lambda l, i: (l, i, 0)),
            pl.BlockSpec((1, t, LANES), lambda l, i: (l, i, 0)),
            pl.BlockSpec((1,) + w_kvb.shape[1:], lambda l, i: (l, 0, 0)),
            pl.BlockSpec((1, 1, QK_PAD), lambda l, i: (l, 0, 0)),
        ],
        out_specs=[
            pl.BlockSpec((1, MLA_HEADS, t, QK_PAD), lambda l, i: (l, 0, i, 0)),
            pl.BlockSpec((1, MLA_HEADS, t, V_HEAD_DIM), lambda l, i: (l, 0, i, 0)),
        ],
        out_shape=[
            jax.ShapeDtypeStruct((depth, MLA_HEADS, rows, QK_PAD), BF16),
            jax.ShapeDtypeStruct((depth, MLA_HEADS, rows, V_HEAD_DIM), BF16),
        ],
        compiler_params=_cparams(2), name="cache_kv",
    )(lat, pe, w_kvb, gk)


def _attn_step(q, k, v, m, l, acc, mask):
    s = lax.dot_general(q, k, (((1,), (1,)), ((), ())), preferred_element_type=F32)
    if mask is not None:
        s = jnp.where(mask, s, -1e30)
    m_new = jnp.maximum(m, jnp.max(s, axis=-1, keepdims=True))
    alpha = jnp.exp2(m - m_new)
    p = jnp.exp2(s - m_new)
    l = alpha * l + jnp.sum(p, axis=-1, keepdims=True)
    acc = alpha * acc + jnp.dot(p.astype(BF16), v, preferred_element_type=F32)
    return m_new, l, acc


def _prompt_attn_kernel(q_ref, k_ref, v_ref, o_ref):
    qi = pl.program_id(1)
    q = q_ref[0]
    m0 = jnp.full((ATT_TQ, 1), -1e30, F32)
    l0 = jnp.zeros((ATT_TQ, 1), F32)
    a0 = jnp.zeros((ATT_TQ, V_HEAD_DIM), F32)

    def body(j, carry):
        off = pl.multiple_of(j * ATT_TK, ATT_TK)
        return _attn_step(q, k_ref[0, pl.ds(off, ATT_TK), :], v_ref[0, pl.ds(off, ATT_TK), :], *carry, None)

    m, l, acc = lax.fori_loop(0, qi, body, (m0, l0, a0))
    off = pl.multiple_of(qi * ATT_TK, ATT_TK)
    row = lax.broadcasted_iota(jnp.int32, (ATT_TQ, ATT_TK), 0)
    col = lax.broadcasted_iota(jnp.int32, (ATT_TQ, ATT_TK), 1)
    mask = (col // CHUNK) <= (row // CHUNK)
    m, l, acc = _attn_step(q, k_ref[0, pl.ds(off, ATT_TK), :], v_ref[0, pl.ds(off, ATT_TK), :], m, l, acc, mask)
    o_ref[...] = (acc / l).astype(BF16)


def _prompt_attn_call(q, k, v, rp):
    r = q.shape[1]
    return pl.pallas_call(
        _prompt_attn_kernel,
        grid=(MLA_HEADS, rp // ATT_TQ),
        in_specs=[
            pl.BlockSpec((1, ATT_TQ, QK_PAD), lambda h, i: (h, i, 0)),
            pl.BlockSpec((1, r, QK_PAD), lambda h, i: (h, 0, 0)),
            pl.BlockSpec((1, r, V_HEAD_DIM), lambda h, i: (h, 0, 0)),
        ],
        out_specs=pl.BlockSpec((ATT_TQ, V_HEAD_DIM), lambda h, i: (i, h)),
        out_shape=jax.ShapeDtypeStruct((rp, MLA_HEADS * V_HEAD_DIM), BF16),
        compiler_params=_cparams(2), name="prompt_attn",
    )(q, k, v)


def _sample_attn_kernel(q_ref, kp_ref, vp_ref, kn_ref, vn_ref, o_ref):
    q = q_ref[0]
    nt = (((1,), (1,)), ((), ()))
    s1 = lax.dot_general(q, kp_ref[0, 0], nt, preferred_element_type=F32)
    s2 = lax.dot_general(q, kn_ref[0], nt, preferred_element_type=F32)
    m = jnp.maximum(jnp.max(s1, axis=-1, keepdims=True), jnp.max(s2, axis=-1, keepdims=True))
    p1 = jnp.exp2(s1 - m)
    p2 = jnp.exp2(s2 - m)
    l = jnp.sum(p1, axis=-1, keepdims=True) + jnp.sum(p2, axis=-1, keepdims=True)
    o = (jnp.dot(p1.astype(BF16), vp_ref[0, 0], preferred_element_type=F32)
         + jnp.dot(p2.astype(BF16), vn_ref[0], preferred_element_type=F32))
    o_ref[...] = (o / l).astype(BF16)


def _sample_attn_call(layer, q, k, v, k_past, v_past, rp, n_seq, s_len, past):
    first = rp // s_len
    return pl.pallas_call(
        _sample_attn_kernel,
        grid=(n_seq, MLA_HEADS),
        in_specs=[
            pl.BlockSpec((1, s_len, QK_PAD), lambda b, h: (h, first + b, 0)),
            pl.BlockSpec((1, 1, past, QK_PAD), lambda b, h: (layer, h, b, 0)),
            pl.BlockSpec((1, 1, past, V_HEAD_DIM), lambda b, h: (layer, h, b, 0)),
            pl.BlockSpec((1, s_len, QK_PAD), lambda b, h: (h, first + b, 0)),
            pl.BlockSpec((1, s_len, V_HEAD_DIM), lambda b, h: (h, first + b, 0)),
        ],
        out_specs=pl.BlockSpec((s_len, V_HEAD_DIM), lambda b, h: (b, h)),
        out_shape=jax.ShapeDtypeStruct((n_seq * s_len, MLA_HEADS * V_HEAD_DIM), BF16),
        compiler_params=_cparams(2), name="sample_attn",
    )(q, k_past, v_past, k, v)


def _route(logits_t, bias_col):
    scores = 1.0 / (1.0 + jnp.exp(-logits_t))
    biased = scores + bias_col
    s_rows = [scores[e:e + 1, :] for e in range(N_EXPERTS)]
    b_rows = [biased[e:e + 1, :] for e in range(N_EXPERTS)]
    group_score = []
    for g in range(N_GROUPS):
        rows = b_rows[g * EXPERTS_PER_GROUP:(g + 1) * EXPERTS_PER_GROUP]
        best = None
        for a in range(EXPERTS_PER_GROUP):
            for b in range(a + 1, EXPERTS_PER_GROUP):
                pair = rows[a] + rows[b]
                best = pair if best is None else jnp.maximum(best, pair)
        group_score.append(best)
    best_group = jnp.zeros_like(group_score[0], dtype=jnp.int32)
    best_val = group_score[0]
    for g in range(1, N_GROUPS):
        better = group_score[g] > best_val
        best_group = jnp.where(better, g, best_group)
        best_val = jnp.where(better, group_score[g], best_val)
    selected = []
    for e in range(N_EXPERTS):
        g = e // EXPERTS_PER_GROUP
        rank = jnp.zeros_like(best_group)
        for j in range(g * EXPERTS_PER_GROUP, (g + 1) * EXPERTS_PER_GROUP):
            if j == e:
                continue
            ahead = b_rows[j] > b_rows[e]
            if j < e:
                ahead = ahead | (b_rows[j] == b_rows[e])
            rank = rank + ahead.astype(jnp.int32)
        selected.append((best_group == g) & (rank < 2))
    denom = jnp.zeros_like(s_rows[0])
    for e in range(N_EXPERTS):
        denom = denom + jnp.where(selected[e], s_rows[e], 0.0)
    return [jnp.where(selected[e], s_rows[e] / denom, 0.0) for e in range(N_EXPERTS)]


def _mix_out_kernel(n_prompt_tiles, a_ref, bp_ref, bs_ref, xp_ref, xs_ref, wa_ref, wb_ref, g1_ref, sc_ref,
                    sh_ref, ln_ref, rw_ref, rb_ref, xm_ref, h2_ref, gates_ref, gt_ref):
    i = pl.program_id(0)
    is_sample = i >= n_prompt_tiles
    x = jnp.where(is_sample, xs_ref[...], xp_ref[...])
    b = jnp.where(is_sample, bs_ref[...], bp_ref[...])
    mix = (jnp.dot(a_ref[...], wa_ref[...], preferred_element_type=F32)
           + jnp.dot(b, wb_ref[...], preferred_element_type=F32))
    xm = _gated_add(x, g1_ref[0], mix)
    xm_ref[...] = xm
    h2 = _modulate(_rms(xm) * ln_ref[...], sc_ref[0], sh_ref[0])
    h2_ref[...] = h2.astype(BF16)
    logits = jnp.dot(h2, rw_ref[...], preferred_element_type=F32, precision=lax.Precision.HIGHEST)
    gate_rows = _route(logits.T[:N_EXPERTS, :], rb_ref[...])
    gt_ref[...] = jnp.zeros_like(gt_ref)
    for e in range(N_EXPERTS):
        gt_ref[e:e + 1, :] = gate_rows[e]
    gt_ref[N_EXPERTS:N_EXPERTS + 1, :] = jnp.ones((1, TILE), F32)
    gates_ref[...] = gt_ref[...].T


def _mix_out_call(a, bp, bs, xp, xs, g1, sc2, sh2, w):
    rp, rs = xp.shape[0], xs.shape[0]
    npt = rp // TILE
    nt = npt + rs // TILE
    r = rp + rs
    last = npt - 1

    def full(arr):
        nd = arr.ndim
        return pl.BlockSpec(arr.shape, lambda i: (0,) * nd)

    def variant(arr):
        nd = arr.ndim
        return pl.BlockSpec((1,) + arr.shape[1:], lambda i: (i // npt,) + (0,) * (nd - 1))

    row = lambda width: pl.BlockSpec((TILE, width), lambda i: (i, 0))
    prow = lambda width: pl.BlockSpec((TILE, width), lambda i: (jnp.minimum(i, last), 0))
    srow = lambda width: pl.BlockSpec((TILE, width), lambda i: (0, 0))
    return pl.pallas_call(
        functools.partial(_mix_out_kernel, npt),
        grid=(nt,),
        in_specs=[row(GMLP_WIDTH), prow(GMLP_WIDTH), srow(GMLP_WIDTH), prow(D_MODEL), srow(D_MODEL),
                  full(w["w_out_a"]), full(w["w_out_b"]), variant(g1), variant(sc2), variant(sh2),
                  full(w["ln2"]), full(w["rw"]), full(w["rb"])],
        out_specs=[row(D_MODEL), row(D_MODEL), row(LANES)],
        out_shape=[jax.ShapeDtypeStruct((r, D_MODEL), F32),
                   jax.ShapeDtypeStruct((r, D_MODEL), BF16),
                   jax.ShapeDtypeStruct((r, LANES), F32)],
        scratch_shapes=[pltpu.VMEM((LANES, TILE), F32)],
        compiler_params=_cparams(1), name="mix_out",
    )(a, bp, bs, xp, xs, w["w_out_a"], w["w_out_b"], g1, sc2, sh2, w["ln2"], w["rw"], w["rb"])


def _moe_kernel(n_prompt_tiles, n_exp, h_ref, gates_ref, xm_ref, g2_ref, wg_ref, wu_ref, wd_ref,
                yp_ref, ys_ref, acc_ref):
    i = pl.program_id(0)
    e = pl.program_id(1)
    h = h_ref[...]
    hg = jnp.dot(h, wg_ref[0], preferred_element_type=F32)
    hu = jnp.dot(h, wu_ref[0], preferred_element_type=F32)
    lane = lax.broadcasted_iota(jnp.int32, (TILE, LANES), 1)
    gate = jnp.sum(jnp.where(lane == e, gates_ref[...], 0.0), axis=-1, keepdims=True)
    act = (hg / (1.0 + jnp.exp(-hg))) * hu
    out = jnp.dot(act.astype(BF16), wd_ref[0], preferred_element_type=F32) * gate

    @pl.when(e == 0)
    def _():
        acc_ref[...] = out

    @pl.when(e > 0)
    def _():
        acc_ref[...] += out

    @pl.when((e == n_exp - 1) & (i < n_prompt_tiles))
    def _():
        yp_ref[...] = _gated_add(xm_ref[...], g2_ref[0], acc_ref[...])

    @pl.when((e == n_exp - 1) & (i >= n_prompt_tiles))
    def _():
        ys_ref[...] = _gated_add(xm_ref[...], g2_ref[0], acc_ref[...])


def _moe_call(h2, gates, xm, g2, wg, wu, wd, rp):
    r = h2.shape[0]
    rs = r - rp
    npt = rp // TILE
    nt = r // TILE
    n_exp = wg.shape[0]
    last = npt - 1
    wspec = lambda arr: pl.BlockSpec((1,) + arr.shape[1:], lambda i, e: (e, 0, 0))
    return pl.pallas_call(
        functools.partial(_moe_kernel, npt, n_exp),
        grid=(nt, n_exp),
        in_specs=[pl.BlockSpec((TILE, D_MODEL), lambda i, e: (i, 0)),
                  pl.BlockSpec((TILE, LANES), lambda i, e: (i, 0)),
                  pl.BlockSpec((TILE, D_MODEL), lambda i, e: (i, 0)),
                  pl.BlockSpec((1,) + g2.shape[1:], lambda i, e: (i // npt, 0, 0)),
                  wspec(wg), wspec(wu), wspec(wd)],
        out_specs=[pl.BlockSpec((TILE, D_MODEL), lambda i, e: (jnp.minimum(i, last), 0)),
                   pl.BlockSpec((TILE, D_MODEL), lambda i, e: (0, 0))],
        out_shape=[jax.ShapeDtypeStruct((rp, D_MODEL), F32),
                   jax.ShapeDtypeStruct((rs, D_MODEL), F32)],
        scratch_shapes=[pltpu.VMEM((TILE, D_MODEL), F32)],
        compiler_params=_cparams(2), name="moe",
    )(h2, gates, xm, g2, wg, wu, wd)


def _rot_half_cols(wcols):
    half = QK_ROPE_DIM // 2
    return jnp.concatenate([-wcols[:, half:], wcols[:, :half]], axis=1)


def _pad_cols(wcols, width):
    return jnp.pad(wcols, ((0, 0), (0, width - wcols.shape[1])))


def _rope_tables(pos):
    half = QK_ROPE_DIM // 2
    inv = ROPE_THETA ** (-jnp.arange(half, dtype=F32) / half)
    ang = pos.astype(F32)[:, None] * inv[None, :]
    cos = jnp.cos(ang)
    sin = jnp.sin(ang)
    zeros = jnp.zeros((pos.shape[0], LANES - QK_ROPE_DIM), F32)
    return (jnp.concatenate([cos, cos, zeros], axis=1), jnp.concatenate([sin, sin, zeros], axis=1))


def _layer_weights(l, p, cos, sin):
    o1 = GMLP_WIDTH
    o2 = 2 * GMLP_WIDTH
    o3 = o2 + Q_LORA_RANK
    o4 = o3 + KV_LORA_RANK
    w_in = p["w_in"][l]
    kpe_cols = w_in[:, o4:]
    w_in2 = jnp.concatenate([w_in[:, :o4], _pad_cols(kpe_cols, LANES),
                             _pad_cols(_rot_half_cols(kpe_cols), LANES)], axis=1).astype(BF16)
    w_qb = p["w_qb"][l]
    q_parts = []
    for hd in range(MLA_HEADS):
        base = hd * QK_HEAD_DIM
        rope_cols = w_qb[:, base + QK_NOPE_DIM:base + QK_HEAD_DIM]
        q_parts += [w_qb[:, base:base + QK_NOPE_DIM], _pad_cols(rope_cols, LANES),
                    _pad_cols(_rot_half_cols(rope_cols), LANES)]
    w_qb2 = jnp.concatenate(q_parts, axis=1).astype(BF16)
    w_kvb = p["w_kvb"][l].reshape(KV_LORA_RANK, MLA_HEADS, QK_NOPE_DIM + V_HEAD_DIM)
    w_kvb2 = jnp.concatenate([w_kvb[:, :, :QK_NOPE_DIM].reshape(KV_LORA_RANK, -1),
                              w_kvb[:, :, QK_NOPE_DIM:].reshape(KV_LORA_RANK, -1)], axis=1).astype(BF16)
    qscale = LOG2E / math.sqrt(QK_HEAD_DIM)
    gq = _pad_cols(p["q_norm_g"][l][None, :] * qscale, QK_PAD)
    gk = _pad_cols(p["k_norm_g"][l][None, :], QK_PAD)

    ws = p["gmlp_ws"][l]
    tri = jnp.tril(jnp.ones((GMLP_CHUNK, GMLP_CHUNK), dtype=bool))
    wt = jnp.where(tri[None], ws, 0.0)
    hc = GMLP_CHUNK // 2
    top = wt[:, :hc, :hc]
    zero = jnp.zeros_like(top)
    wt_s = jnp.concatenate([jnp.concatenate([top, zero], axis=2), jnp.concatenate([zero, top], axis=2)], axis=1)
    gw = jnp.stack([wt, wt_s]).astype(BF16)
    b = p["gmlp_b"][l]
    b_s = jnp.concatenate([b[:, :hc], b[:, :hc]], axis=1)
    gb = jnp.broadcast_to(jnp.stack([b, b_s])[..., None], (2, GMLP_GROUPS, GMLP_CHUNK, LANES)).astype(F32)

    w_out = p["w_out"][l].astype(BF16)
    wg = jnp.concatenate([p["exp_w_gate"][l], p["sh_w_gate"][l][None]], axis=0).astype(BF16)
    wu = jnp.concatenate([p["exp_w_up"][l], p["sh_w_up"][l][None]], axis=0).astype(BF16)
    wd = jnp.concatenate([p["exp_w_down"][l], p["sh_w_down"][l][None]], axis=0).astype(BF16)
    return dict(
        ln1=p["ln1_g"][l][None, :], w_in=w_in2, cos=cos, sin=sin,
        gv=p["gmlp_v_g"][l].reshape(1, GMLP_WIDTH), gw=gw, gb=gb,
        q_a_g=p["q_a_g"][l][None, :], w_qb=w_qb2, kv_a_g=p["kv_a_g"][l][None, :], w_kvb=w_kvb2,
        gq=gq, gk=gk, w_out_a=w_out[:GMLP_WIDTH], w_out_b=w_out[GMLP_WIDTH:],
        ln2=p["ln2_g"][l][None, :], rw=_pad_cols(p["router_w"], LANES),
        rb=p["router_bias"].reshape(N_EXPERTS, 1), wg=wg, wu=wu, wd=wd,
    )


def kernel(x_prompt, x_sample, cache_kv_latent, cache_k_rope, c_prompt, c_sample, w_ada, b_ada, ln1_g, w_in,
           gmlp_v_g, gmlp_ws, gmlp_b, q_a_g, w_qb, kv_a_g, w_kvb, q_norm_g, k_norm_g, w_out, ln2_g, router_w,
           router_bias, exp_w_gate, exp_w_up, exp_w_down, sh_w_gate, sh_w_up, sh_w_down):
    p = dict(w_in=w_in, gmlp_v_g=gmlp_v_g, gmlp_ws=gmlp_ws, gmlp_b=gmlp_b, q_a_g=q_a_g, w_qb=w_qb,
             kv_a_g=kv_a_g, w_kvb=w_kvb, q_norm_g=q_norm_g, k_norm_g=k_norm_g, w_out=w_out, ln1_g=ln1_g,
             ln2_g=ln2_g, router_w=router_w, router_bias=router_bias, exp_w_gate=exp_w_gate,
             exp_w_up=exp_w_up, exp_w_down=exp_w_down, sh_w_gate=sh_w_gate, sh_w_up=sh_w_up,
             sh_w_down=sh_w_down)
    batch, seq, _ = x_prompt.shape
    n_seq, s_len, _ = x_sample.shape
    depth, _, past, _ = cache_kv_latent.shape
    assert batch == 1 and s_len == SUB and n_seq == N_SUB and n_seq * s_len == TILE
    assert seq % TILE == 0 and seq % ATT_TQ == 0 and past % GMLP_CHUNK == 0 and past % CHUNK == 0
    rp = seq
    rs = n_seq * s_len

    pos = jnp.concatenate([jnp.arange(seq, dtype=jnp.int32),
                           jnp.tile(past + jnp.arange(s_len, dtype=jnp.int32), n_seq)])
    cos, sin = _rope_tables(pos)

    c_all = jnp.concatenate([jnp.broadcast_to(c_prompt, (N_SUB, D_MODEL)), c_sample], axis=0)
    mod = _ada_call(c_all, w_ada, b_ada)
    mod = mod.reshape(depth, 2, N_SUB, 6, D_MODEL)

    weights = [_layer_weights(l, p, cos, sin) for l in range(depth)]
    lat_all = cache_kv_latent.reshape(depth, n_seq * past, KV_LORA_RANK)
    pe_all = jnp.pad(cache_k_rope.reshape(depth, n_seq * past, QK_ROPE_DIM),
                     ((0, 0), (0, 0), (0, LANES - QK_ROPE_DIM)))
    k_past, v_past = _cache_kv_call(lat_all, pe_all, jnp.stack([w["w_kvb"] for w in weights]),
                                    jnp.stack([w["gk"] for w in weights]))

    xp = x_prompt.reshape(rp, D_MODEL)
    xs = x_sample.reshape(rs, D_MODEL)
    lat_p, pe_p, lat_s, pe_s, v_s = [], [], [], [], []
    for l in range(depth):
        w = weights[l]
        sh1, sc1, g1, sh2, sc2, g2 = [mod[l, :, :, j, :] for j in range(6)]
        a, q, k, v, ckv, kpe, vn = _mix_in_call(xp, xs, sc1, sh1, w)
        bp = _prompt_attn_call(q, k, v, rp)
        bs = _sample_attn_call(l, q, k, v, k_past, v_past, rp, n_seq, s_len, past)
        xm, h2, gates = _mix_out_call(a, bp, bs, xp, xs, g1, sc2, sh2, w)
        xp, xs = _moe_call(h2, gates, xm, g2, w["wg"], w["wu"], w["wd"], rp)
        lat_p.append(ckv[:rp].reshape(batch, seq, KV_LORA_RANK))
        pe_p.append(kpe[:rp, :QK_ROPE_DIM].reshape(batch, seq, QK_ROPE_DIM))
        lat_s.append(ckv[rp:].reshape(n_seq, s_len, KV_LORA_RANK))
        pe_s.append(kpe[rp:, :QK_ROPE_DIM].reshape(n_seq, s_len, QK_ROPE_DIM))
        v_s.append(vn.reshape(n_seq, s_len, GMLP_WIDTH))
    return (xp.reshape(batch, seq, D_MODEL), xs.reshape(n_seq, s_len, D_MODEL),
            jnp.stack(lat_p), jnp.stack(pe_p), jnp.stack(lat_s), jnp.stack(pe_s), jnp.stack(v_s))
```

```python
import functools
import math

import jax
import jax.numpy as jnp
from jax import lax
from jax.experimental import pallas as pl
from jax.experimental.pallas import tpu as pltpu

F32 = jnp.float32
BF16 = jnp.bfloat16

D_MODEL = 1024
CHUNK = 64
GMLP_WIDTH = 512
GMLP_GROUPS = 4
GMLP_CHUNK = 128
MLA_HEADS = 4
QK_NOPE_DIM = 128
QK_ROPE_DIM = 64
QK_HEAD_DIM = 192
V_HEAD_DIM = 128
Q_LORA_RANK = 384
KV_LORA_RANK = 256
ROPE_THETA = 10000.0
N_EXPERTS = 16
N_GROUPS = 4
EXPERTS_PER_GROUP = 4
EXPERT_FF = 512
EPS = 1e-6

LANES = 128
TILE = 512
SUB = 64
N_SUB = TILE // SUB
QK_PAD = 256
IN_COLS = 2 * GMLP_WIDTH + Q_LORA_RANK + KV_LORA_RANK + 2 * LANES
Q_COLS = MLA_HEADS * 3 * LANES
H2E_COLS = D_MODEL + LANES
ROW_DMA_UNROLL = 8
ATT_TK = 512
ATT_TQ = 2 * ATT_TK
VMEM_LIMIT = 56 * 1024 * 1024
LOG2E = 1.4426950408889634


def _cparams(n_axes):
    return pltpu.CompilerParams(dimension_semantics=("arbitrary",) * n_axes,
                                vmem_limit_bytes=VMEM_LIMIT)


def _rms(x, eps=EPS):
    return x * lax.rsqrt(jnp.mean(x * x, axis=-1, keepdims=True) + eps)


def _gelu(x):
    c = math.sqrt(2.0 / math.pi)
    return 0.5 * x * (1.0 + jnp.tanh(c * (x + 0.044715 * (x * x * x))))


def _modulate(h, scale, shift):
    h3 = h.reshape(N_SUB, SUB, h.shape[-1])
    h3 = h3 * (1.0 + scale[:, None, :]) + shift[:, None, :]
    return h3.reshape(h.shape)


def _gated_add(x, gate, y):
    y3 = y.reshape(N_SUB, SUB, y.shape[-1]) * gate[:, None, :]
    return x + y3.reshape(y.shape)


def _ada_kernel(c_ref, w_ref, b_ref, o_ref):
    c = c_ref[...]
    cs = (c / (1.0 + jnp.exp(-c))).astype(BF16)
    w = w_ref[0].astype(BF16)
    o_ref[0] = jnp.dot(cs, w, preferred_element_type=F32) + b_ref[0]


def _ada_call(c_all, w_ada, b_ada):
    depth = w_ada.shape[0]
    nblk = w_ada.shape[2] // D_MODEL
    return pl.pallas_call(
        _ada_kernel,
        grid=(depth, nblk),
        in_specs=[
            pl.BlockSpec((16, D_MODEL), lambda l, j: (0, 0)),
            pl.BlockSpec((1, D_MODEL, D_MODEL), lambda l, j: (l, 0, j)),
            pl.BlockSpec((1, 1, D_MODEL), lambda l, j: (l, 0, j)),
        ],
        out_specs=pl.BlockSpec((1, 16, D_MODEL), lambda l, j: (l, 0, j)),
        out_shape=jax.ShapeDtypeStruct((depth, 16, w_ada.shape[2]), F32),
        compiler_params=_cparams(2),
        name="ada_mod",
    )(c_all, w_ada, b_ada.reshape(depth, 1, -1))


def _mix_in_kernel(n_prompt_tiles, xp_ref, xs_ref, sc_ref, sh_ref, ln_ref, win_ref, cos_ref, sin_ref,
                   gv_ref, gw_ref, gb_ref, qag_ref, wqb_ref, kvag_ref, wkvb_ref, gq_ref, gk_ref,
                   a_ref, q_ref, k_ref, v_ref, ckv_ref, kpe_ref, vn_ref):
    i = pl.program_id(0)
    is_sample = i >= n_prompt_tiles
    x = jnp.where(is_sample, xs_ref[...], xp_ref[...])
    h = _rms(x) * ln_ref[...]
    h = _modulate(h, sc_ref[0], sh_ref[0])
    z = jnp.dot(h.astype(BF16), win_ref[...], preferred_element_type=F32)

    u = _gelu(z[:, :GMLP_WIDTH])
    v = _gelu(z[:, GMLP_WIDTH:2 * GMLP_WIDTH])
    gv = gv_ref[...]
    vn_parts = []
    for g in range(GMLP_GROUPS):
        sl = slice(g * LANES, (g + 1) * LANES)
        vn_parts.append(_rms(v[:, sl]) * gv[:, sl])

    @pl.when(is_sample)
    def _():
        for g in range(GMLP_GROUPS):
            vn_ref[:, g * LANES:(g + 1) * LANES] = vn_parts[g]

    for g in range(GMLP_GROUPS):
        sl = slice(g * LANES, (g + 1) * LANES)
        vb = vn_parts[g].astype(BF16)
        wg = gw_ref[0, g]
        bg = gb_ref[0, g]
        for c in range(TILE // GMLP_CHUNK):
            rows = slice(c * GMLP_CHUNK, (c + 1) * GMLP_CHUNK)
            s = jnp.dot(wg, vb[rows], preferred_element_type=F32) + bg
            a_ref[rows, sl] = (u[rows, sl] * s).astype(BF16)

    cos = cos_ref[...]
    sin = sin_ref[...]

    o2 = 2 * GMLP_WIDTH
    o3 = o2 + Q_LORA_RANK
    o4 = o3 + KV_LORA_RANK
    ql = _rms(z[:, o2:o3]) * qag_ref[...]
    qq = jnp.dot(ql.astype(BF16), wqb_ref[...], preferred_element_type=F32)
    gq = gq_ref[...]
    for hd in range(MLA_HEADS):
        base = hd * 3 * LANES
        nope = qq[:, base:base + LANES]
        rope = qq[:, base + LANES:base + 2 * LANES] * cos + qq[:, base + 2 * LANES:base + 3 * LANES] * sin
        ss = jnp.sum(nope * nope, axis=-1, keepdims=True) + jnp.sum(rope * rope, axis=-1, keepdims=True)
        rinv = lax.rsqrt(ss * (1.0 / QK_HEAD_DIM) + EPS)
        q_ref[hd, :, :LANES] = (nope * rinv * gq[:, :LANES]).astype(BF16)
        q_ref[hd, :, LANES:] = (rope * rinv * gq[:, LANES:]).astype(BF16)

    ckv = _rms(z[:, o3:o4]) * kvag_ref[...]
    ckv_ref[...] = ckv
    kpe = z[:, o4:o4 + LANES] * cos + z[:, o4 + LANES:o4 + 2 * LANES] * sin
    kpe_ref[...] = kpe
    kv = jnp.dot(ckv.astype(BF16), wkvb_ref[...], preferred_element_type=F32)
    gk = gk_ref[...]
    kpe_ss = jnp.sum(kpe * kpe, axis=-1, keepdims=True)
    for hd in range(MLA_HEADS):
        nope = kv[:, hd * LANES:(hd + 1) * LANES]
        ss = jnp.sum(nope * nope, axis=-1, keepdims=True) + kpe_ss
        rinv = lax.rsqrt(ss * (1.0 / QK_HEAD_DIM) + EPS)
        k_ref[hd, :, :LANES] = (nope * rinv * gk[:, :LANES]).astype(BF16)
        k_ref[hd, :, LANES:] = (kpe * rinv * gk[:, LANES:]).astype(BF16)
        v_ref[hd] = kv[:, (MLA_HEADS + hd) * LANES:(MLA_HEADS + hd + 1) * LANES].astype(BF16)


def _mix_in_call(xp, xs, sc1, sh1, w):
    rp, rs = xp.shape[0], xs.shape[0]
    npt = rp // TILE
    nt = npt + rs // TILE
    r = rp + rs
    last = npt - 1

    def full(a):
        nd = a.ndim
        return pl.BlockSpec(a.shape, lambda i: (0,) * nd)

    def variant(a):
        nd = a.ndim
        return pl.BlockSpec((1,) + a.shape[1:], lambda i: (i // npt,) + (0,) * (nd - 1))

    row = lambda width: pl.BlockSpec((TILE, width), lambda i: (i, 0))
    head = lambda width: pl.BlockSpec((MLA_HEADS, TILE, width), lambda i: (0, i, 0))
    in_specs = [
        pl.BlockSpec((TILE, D_MODEL), lambda i: (jnp.minimum(i, last), 0)),
        pl.BlockSpec((TILE, D_MODEL), lambda i: (0, 0)),
        variant(sc1), variant(sh1), full(w["ln1"]), full(w["w_in"]),
        row(LANES), row(LANES),
        full(w["gv"]), variant(w["gw"]), variant(w["gb"]),
        full(w["q_a_g"]), full(w["w_qb"]), full(w["kv_a_g"]), full(w["w_kvb"]),
        full(w["gq"]), full(w["gk"]),
    ]
    out_specs = [
        row(GMLP_WIDTH), head(QK_PAD), head(QK_PAD), head(V_HEAD_DIM),
        row(KV_LORA_RANK), row(LANES),
        pl.BlockSpec((TILE, GMLP_WIDTH), lambda i: (0, 0)),
    ]
    out_shape = [
        jax.ShapeDtypeStruct((r, GMLP_WIDTH), BF16),
        jax.ShapeDtypeStruct((MLA_HEADS, r, QK_PAD), BF16),
        jax.ShapeDtypeStruct((MLA_HEADS, r, QK_PAD), BF16),
        jax.ShapeDtypeStruct((MLA_HEADS, r, V_HEAD_DIM), BF16),
        jax.ShapeDtypeStruct((r, KV_LORA_RANK), F32),
        jax.ShapeDtypeStruct((r, LANES), F32),
        jax.ShapeDtypeStruct((rs, GMLP_WIDTH), F32),
    ]
    return pl.pallas_call(
        functools.partial(_mix_in_kernel, npt),
        grid=(nt,), in_specs=in_specs, out_specs=out_specs, out_shape=out_shape,
        compiler_params=_cparams(1), name="mix_in",
    )(xp, xs, sc1, sh1, w["ln1"], w["w_in"], w["cos"], w["sin"], w["gv"], w["gw"], w["gb"],
      w["q_a_g"], w["w_qb"], w["kv_a_g"], w["w_kvb"], w["gq"], w["gk"])


def _cache_kv_kernel(lat_ref, pe_ref, wkvb_ref, gk_ref, k_ref, v_ref):
    lat = lat_ref[0]
    kpe = pe_ref[0]
    kv = jnp.dot(lat.astype(BF16), wkvb_ref[0], preferred_element_type=F32)
    gk = gk_ref[0]
    kpe_ss = jnp.sum(kpe * kpe, axis=-1, keepdims=True)
    for hd in range(MLA_HEADS):
        nope = kv[:, hd * LANES:(hd + 1) * LANES]
        ss = jnp.sum(nope * nope, axis=-1, keepdims=True) + kpe_ss
        rinv = lax.rsqrt(ss * (1.0 / QK_HEAD_DIM) + EPS)
        k_ref[0, hd, :, :LANES] = (nope * rinv * gk[:, :LANES]).astype(BF16)
        k_ref[0, hd, :, LANES:] = (kpe * rinv * gk[:, LANES:]).astype(BF16)
        v_ref[0, hd] = kv[:, (MLA_HEADS + hd) * LANES:(MLA_HEADS + hd + 1) * LANES].astype(BF16)


def _cache_kv_call(lat, pe, w_kvb, gk):
    depth, rows, _ = lat.shape
    t = 1024
    return pl.pallas_call(
        _cache_kv_kernel,
        grid=(depth, rows // t),
        in_specs=[
            pl.BlockSpec((1, t, KV_LORA_RANK), lambda l, i: (l, i, 0)),
            pl.BlockSpec((1, t, LANES), lambda l, i: (l, i, 0)),
            pl.BlockSpec((1,) + w_kvb.shape[1:], lambda l, i: (l, 0, 0)),
            pl.BlockSpec((1, 1, QK_PAD), lambda l, i: (l, 0, 0)),
        ],
        out_specs=[
            pl.BlockSpec((1, MLA_HEADS, t, QK_PAD), lambda l, i: (l, 0, i, 0)),
            pl.BlockSpec((1, MLA_HEADS, t, V_HEAD_DIM), lambda l, i: (l, 0, i, 0)),
        ],
        out_shape=[
            jax.ShapeDtypeStruct((depth, MLA_HEADS, rows, QK_PAD), BF16),
            jax.ShapeDtypeStruct((depth, MLA_HEADS, rows, V_HEAD_DIM), BF16),
        ],
        compiler_params=_cparams(2), name="cache_kv",
    )(lat, pe, w_kvb, gk)


def _attn_step(q, k, v, m, l, acc, mask):
    s = lax.dot_general(q, k, (((1,), (1,)), ((), ())), preferred_element_type=F32)
    if mask is not None:
        s = jnp.where(mask, s, -1e30)
    m_new = jnp.maximum(m, jnp.max(s, axis=-1, keepdims=True))
    alpha = jnp.exp2(m - m_new)
    p = jnp.exp2(s - m_new)
    l = alpha * l + jnp.sum(p, axis=-1, keepdims=True)
    acc = alpha * acc + jnp.dot(p.astype(BF16), v, preferred_element_type=F32)
    return m_new, l, acc


def _prompt_attn_kernel(q_ref, k_ref, v_ref, o_ref):
    qi = pl.program_id(1)
    qa = q_ref[0, :ATT_TK]
    qb = q_ref[0, ATT_TK:]

    def kv(j):
        off = pl.multiple_of(j * ATT_TK, ATT_TK)
        return k_ref[0, pl.ds(off, ATT_TK), :], v_ref[0, pl.ds(off, ATT_TK), :]

    def init():
        return (jnp.full((ATT_TK, 1), -1e30, F32), jnp.zeros((ATT_TK, 1), F32),
                jnp.zeros((ATT_TK, V_HEAD_DIM), F32))

    def body(j, carry):
        k, v = kv(j)
        return _attn_step(qa, k, v, *carry[0], None), _attn_step(qb, k, v, *carry[1], None)

    sa, sb = lax.fori_loop(0, 2 * qi, body, (init(), init()))
    row = lax.broadcasted_iota(jnp.int32, (ATT_TK, ATT_TK), 0)
    col = lax.broadcasted_iota(jnp.int32, (ATT_TK, ATT_TK), 1)
    mask = (col // CHUNK) <= (row // CHUNK)
    k, v = kv(2 * qi)
    sa = _attn_step(qa, k, v, *sa, mask)
    sb = _attn_step(qb, k, v, *sb, None)
    k, v = kv(2 * qi + 1)
    sb = _attn_step(qb, k, v, *sb, mask)
    o_ref[:ATT_TK, :] = (sa[2] / sa[1]).astype(BF16)
    o_ref[ATT_TK:, :] = (sb[2] / sb[1]).astype(BF16)


def _prompt_attn_call(q, k, v, rp):
    r = q.shape[1]
    return pl.pallas_call(
        _prompt_attn_kernel,
        grid=(MLA_HEADS, rp // ATT_TQ),
        in_specs=[
            pl.BlockSpec((1, ATT_TQ, QK_PAD), lambda h, i: (h, i, 0)),
            pl.BlockSpec((1, r, QK_PAD), lambda h, i: (h, 0, 0)),
            pl.BlockSpec((1, r, V_HEAD_DIM), lambda h, i: (h, 0, 0)),
        ],
        out_specs=pl.BlockSpec((ATT_TQ, V_HEAD_DIM), lambda h, i: (i, h)),
        out_shape=jax.ShapeDtypeStruct((rp, MLA_HEADS * V_HEAD_DIM), BF16),
        compiler_params=_cparams(2), name="prompt_attn",
    )(q, k, v)


def _sample_attn_kernel(q_ref, kp_ref, vp_ref, kn_ref, vn_ref, o_ref):
    q = q_ref[0]
    nt = (((1,), (1,)), ((), ()))
    s1 = lax.dot_general(q, kp_ref[0, 0], nt, preferred_element_type=F32)
    s2 = lax.dot_general(q, kn_ref[0], nt, preferred_element_type=F32)
    m = jnp.maximum(jnp.max(s1, axis=-1, keepdims=True), jnp.max(s2, axis=-1, keepdims=True))
    p1 = jnp.exp2(s1 - m)
    p2 = jnp.exp2(s2 - m)
    l = jnp.sum(p1, axis=-1, keepdims=True) + jnp.sum(p2, axis=-1, keepdims=True)
    o = (jnp.dot(p1.astype(BF16), vp_ref[0, 0], preferred_element_type=F32)
         + jnp.dot(p2.astype(BF16), vn_ref[0], preferred_element_type=F32))
    o_ref[...] = (o / l).astype(BF16)


def _sample_attn_call(layer, q, k, v, k_past, v_past, rp, n_seq, s_len, past):
    first = rp // s_len
    return pl.pallas_call(
        _sample_attn_kernel,
        grid=(n_seq, MLA_HEADS),
        in_specs=[
            pl.BlockSpec((1, s_len, QK_PAD), lambda b, h: (h, first + b, 0)),
            pl.BlockSpec((1, 1, past, QK_PAD), lambda b, h: (layer, h, b, 0)),
            pl.BlockSpec((1, 1, past, V_HEAD_DIM), lambda b, h: (layer, h, b, 0)),
            pl.BlockSpec((1, s_len, QK_PAD), lambda b, h: (h, first + b, 0)),
            pl.BlockSpec((1, s_len, V_HEAD_DIM), lambda b, h: (h, first + b, 0)),
        ],
        out_specs=pl.BlockSpec((s_len, V_HEAD_DIM), lambda b, h: (b, h)),
        out_shape=jax.ShapeDtypeStruct((n_seq * s_len, MLA_HEADS * V_HEAD_DIM), BF16),
        compiler_params=_cparams(2), name="sample_attn",
    )(q, k_past, v_past, k, v)


def _route(logits_t, bias_col):
    scores = 1.0 / (1.0 + jnp.exp(-logits_t))
    biased = scores + bias_col
    s_rows = [scores[e:e + 1, :] for e in range(N_EXPERTS)]
    b_rows = [biased[e:e + 1, :] for e in range(N_EXPERTS)]
    group_score = []
    for g in range(N_GROUPS):
        rows = b_rows[g * EXPERTS_PER_GROUP:(g + 1) * EXPERTS_PER_GROUP]
        best = None
        for a in range(EXPERTS_PER_GROUP):
            for b in range(a + 1, EXPERTS_PER_GROUP):
                pair = rows[a] + rows[b]
                best = pair if best is None else jnp.maximum(best, pair)
        group_score.append(best)
    best_group = jnp.zeros_like(group_score[0], dtype=jnp.int32)
    best_val = group_score[0]
    for g in range(1, N_GROUPS):
        better = group_score[g] > best_val
        best_group = jnp.where(better, g, best_group)
        best_val = jnp.where(better, group_score[g], best_val)
    selected = []
    for e in range(N_EXPERTS):
        g = e // EXPERTS_PER_GROUP
        rank = jnp.zeros_like(best_group)
        for j in range(g * EXPERTS_PER_GROUP, (g + 1) * EXPERTS_PER_GROUP):
            if j == e:
                continue
            ahead = b_rows[j] > b_rows[e]
            if j < e:
                ahead = ahead | (b_rows[j] == b_rows[e])
            rank = rank + ahead.astype(jnp.int32)
        selected.append((best_group == g) & (rank < 2))
    denom = jnp.zeros_like(s_rows[0])
    for e in range(N_EXPERTS):
        denom = denom + jnp.where(selected[e], s_rows[e], 0.0)
    return [jnp.where(selected[e], s_rows[e] / denom, 0.0) for e in range(N_EXPERTS)], best_group


def _mix_out_kernel(n_prompt_tiles, a_ref, bp_ref, bs_ref, xp_ref, xs_ref, wa_ref, wb_ref, g1_ref, sc_ref,
                    sh_ref, ln_ref, rw_ref, rb_ref, xm_ref, h2e_ref, grp_ref, rank_ref, cnt_ref,
                    gt_ref, carry_ref):
    i = pl.program_id(0)
    is_sample = i >= n_prompt_tiles
    x = jnp.where(is_sample, xs_ref[...], xp_ref[...])
    b = jnp.where(is_sample, bs_ref[...], bp_ref[...])
    mix = (jnp.dot(a_ref[...], wa_ref[...], preferred_element_type=F32)
           + jnp.dot(b, wb_ref[...], preferred_element_type=F32))
    xm = _gated_add(x, g1_ref[0], mix)
    xm_ref[...] = xm
    h2 = _modulate(_rms(xm) * ln_ref[...], sc_ref[0], sh_ref[0])
    h2e_ref[:, :D_MODEL] = h2
    logits = jnp.dot(h2, rw_ref[...], preferred_element_type=F32, precision=lax.Precision.HIGHEST)
    gate_rows, best_group = _route(logits.T[:N_EXPERTS, :], rb_ref[...])
    gt_ref[...] = jnp.zeros_like(gt_ref)
    for e in range(N_EXPERTS):
        gt_ref[e:e + 1, :] = gate_rows[e]
    h2e_ref[:, D_MODEL:] = gt_ref[...].T

    @pl.when(i == 0)
    def _():
        carry_ref[...] = jnp.zeros_like(carry_ref)

    sub = lax.broadcasted_iota(jnp.int32, (8, TILE), 0)
    onehot = jnp.where(sub == best_group, 1.0, 0.0)
    r_idx = lax.broadcasted_iota(jnp.int32, (TILE, TILE), 0)
    c_idx = lax.broadcasted_iota(jnp.int32, (TILE, TILE), 1)
    upper = jnp.where(r_idx <= c_idx, 1.0, 0.0).astype(BF16)
    cum = jnp.dot(onehot.astype(BF16), upper, preferred_element_type=F32)
    carry = carry_ref[...]
    rank = jnp.sum(onehot * (cum - 1.0 + carry[:, :1]), axis=0, keepdims=True)
    grp_ref[0] = best_group
    rank_ref[0] = rank.astype(jnp.int32)
    carry = carry + jnp.sum(onehot, axis=1, keepdims=True)
    carry_ref[...] = carry
    cnt_ref[...] = carry


def _mix_out_call(a, bp, bs, xp, xs, g1, sc2, sh2, w):
    rp, rs = xp.shape[0], xs.shape[0]
    npt = rp // TILE
    nt = npt + rs // TILE
    r = rp + rs
    last = npt - 1

    def full(arr):
        nd = arr.ndim
        return pl.BlockSpec(arr.shape, lambda i: (0,) * nd)

    def variant(arr):
        nd = arr.ndim
        return pl.BlockSpec((1,) + arr.shape[1:], lambda i: (i // npt,) + (0,) * (nd - 1))

    row = lambda width: pl.BlockSpec((TILE, width), lambda i: (i, 0))
    prow = lambda width: pl.BlockSpec((TILE, width), lambda i: (jnp.minimum(i, last), 0))
    srow = lambda width: pl.BlockSpec((TILE, width), lambda i: (0, 0))
    return pl.pallas_call(
        functools.partial(_mix_out_kernel, npt),
        grid=(nt,),
        in_specs=[row(GMLP_WIDTH), prow(GMLP_WIDTH), srow(GMLP_WIDTH), prow(D_MODEL), srow(D_MODEL),
                  full(w["w_out_a"]), full(w["w_out_b"]), variant(g1), variant(sc2), variant(sh2),
                  full(w["ln2"]), full(w["rw"]), full(w["rb"])],
        out_specs=[row(D_MODEL), row(H2E_COLS),
                   pl.BlockSpec((1, 1, TILE), lambda i: (i, 0, 0)),
                   pl.BlockSpec((1, 1, TILE), lambda i: (i, 0, 0)),
                   pl.BlockSpec((8, LANES), lambda i: (0, 0))],
        out_shape=[jax.ShapeDtypeStruct((r, D_MODEL), F32),
                   jax.ShapeDtypeStruct((r, H2E_COLS), F32),
                   jax.ShapeDtypeStruct((nt, 1, TILE), jnp.int32),
                   jax.ShapeDtypeStruct((nt, 1, TILE), jnp.int32),
                   jax.ShapeDtypeStruct((8, LANES), F32)],
        scratch_shapes=[pltpu.VMEM((LANES, TILE), F32), pltpu.VMEM((8, LANES), F32)],
        compiler_params=_cparams(1), name="mix_out",
    )(a, bp, bs, xp, xs, w["w_out_a"], w["w_out_b"], g1, sc2, sh2, w["ln2"], w["rw"], w["rb"])


def _silu(x):
    return x / (1.0 + jnp.exp(-x))


def _dispatch_kernel(pos_ref, src_ref, zero_ref, dst_ref, sem):
    del zero_ref
    i = pl.program_id(0)
    n = pl.num_programs(0)
    base = i * TILE
    slot = i % 2

    def issue(r, carry):
        p = pos_ref[base + r]
        pltpu.make_async_copy(src_ref.at[pl.ds(base + r, 1)], dst_ref.at[pl.ds(p, 1)], sem.at[slot]).start()
        return carry

    lax.fori_loop(0, TILE, issue, 0, unroll=ROW_DMA_UNROLL)

    def wait_tile(s):
        pltpu.make_async_copy(src_ref.at[pl.ds(0, TILE)], dst_ref.at[pl.ds(0, TILE)], sem.at[s]).wait()

    @pl.when(i > 0)
    def _():
        wait_tile(1 - slot)

    @pl.when(i == n - 1)
    def _():
        wait_tile(slot)


def _dispatch_call(pos, h2e, n_sorted_tiles):
    r = h2e.shape[0]
    zeros = jnp.zeros((n_sorted_tiles * TILE, H2E_COLS), F32)
    return pl.pallas_call(
        _dispatch_kernel,
        grid_spec=pltpu.PrefetchScalarGridSpec(
            num_scalar_prefetch=1, grid=(r // TILE,),
            in_specs=[pl.BlockSpec(memory_space=pl.ANY), pl.BlockSpec(memory_space=pl.ANY)],
            out_specs=pl.BlockSpec(memory_space=pl.ANY),
            scratch_shapes=[pltpu.SemaphoreType.DMA((2,))]),
        out_shape=jax.ShapeDtypeStruct(zeros.shape, F32),
        input_output_aliases={2: 0},
        compiler_params=_cparams(1), name="moe_dispatch",
    )(pos, h2e, zeros)


def _routed_kernel(grp_ref, on_ref, h_ref, wg_ref, wu_ref, wd_ref, o_ref, hb_ref):
    t = pl.program_id(0)
    e = pl.program_id(1)
    on = on_ref[t] > 0

    @pl.when(jnp.logical_not(on) & (e == 0))
    def _():
        o_ref[...] = jnp.zeros_like(o_ref)

    @pl.when(on)
    def _():
        @pl.when(e == 0)
        def _():
            hb_ref[...] = h_ref[:, :D_MODEL].astype(BF16)

        h = hb_ref[...]
        hg = jnp.dot(h, wg_ref[0], preferred_element_type=F32)
        hu = jnp.dot(h, wu_ref[0], preferred_element_type=F32)
        lane = lax.broadcasted_iota(jnp.int32, (TILE, LANES), 1)
        expert = grp_ref[t] * EXPERTS_PER_GROUP + e
        gate = jnp.sum(jnp.where(lane == expert, h_ref[:, D_MODEL:], 0.0), axis=-1, keepdims=True)
        act = _silu(hg) * hu
        out = jnp.dot(act.astype(BF16), wd_ref[0], preferred_element_type=F32) * gate

        @pl.when(e == 0)
        def _():
            o_ref[...] = out

        @pl.when(e > 0)
        def _():
            o_ref[...] += out


def _routed_call(tile_grp, tile_on, h2e_sorted, wg, wu, wd):
    nts = h2e_sorted.shape[0] // TILE

    def wspec(arr):
        def imap(t, e, grp, on):
            return (jnp.where(on[t] > 0, grp[t] * EXPERTS_PER_GROUP + e, N_EXPERTS - 1), 0, 0)
        return pl.BlockSpec((1,) + arr.shape[1:], imap)

    return pl.pallas_call(
        _routed_kernel,
        grid_spec=pltpu.PrefetchScalarGridSpec(
            num_scalar_prefetch=2, grid=(nts, EXPERTS_PER_GROUP),
            in_specs=[pl.BlockSpec((TILE, H2E_COLS), lambda t, e, grp, on: (t, 0)),
                      wspec(wg), wspec(wu), wspec(wd)],
            out_specs=pl.BlockSpec((TILE, D_MODEL), lambda t, e, grp, on: (t, 0)),
            scratch_shapes=[pltpu.VMEM((TILE, D_MODEL), BF16)]),
        out_shape=jax.ShapeDtypeStruct((nts * TILE, D_MODEL), F32),
        compiler_params=_cparams(2), name="moe_routed",
    )(tile_grp, tile_on, h2e_sorted, wg, wu, wd)


def _combine_kernel(n_prompt_tiles, pos_ref, h_ref, xm_ref, g2_ref, wg_ref, wu_ref, wd_ref, routed_ref,
                    yp_ref, ys_ref, buf_ref, sem):
    i = pl.program_id(0)
    n = pl.num_programs(0)
    slot = i % 2

    def gather(tile, s):
        base = tile * TILE

        def issue(r, carry):
            p = pos_ref[base + r]
            pltpu.make_async_copy(routed_ref.at[pl.ds(p, 1)], buf_ref.at[s, pl.ds(r, 1)], sem.at[s]).start()
            return carry

        lax.fori_loop(0, TILE, issue, 0, unroll=ROW_DMA_UNROLL)

    @pl.when(i == 0)
    def _():
        gather(0, 0)

    @pl.when(i + 1 < n)
    def _():
        gather(i + 1, 1 - slot)

    h = h_ref[:, :D_MODEL].astype(BF16)
    act = _silu(jnp.dot(h, wg_ref[...], preferred_element_type=F32)) * jnp.dot(
        h, wu_ref[...], preferred_element_type=F32)
    shared = jnp.dot(act.astype(BF16), wd_ref[...], preferred_element_type=F32)
    pltpu.make_async_copy(routed_ref.at[pl.ds(0, TILE)], buf_ref.at[slot], sem.at[slot]).wait()
    y = _gated_add(xm_ref[...], g2_ref[0], shared + buf_ref[slot])

    @pl.when(i < n_prompt_tiles)
    def _():
        yp_ref[...] = y

    @pl.when(i >= n_prompt_tiles)
    def _():
        ys_ref[...] = y


def _combine_call(pos, h2e, xm, g2, wg, wu, wd, routed, rp):
    r = h2e.shape[0]
    rs = r - rp
    npt = rp // TILE
    last = npt - 1
    full = lambda arr: pl.BlockSpec(arr.shape, lambda i, pos: (0,) * arr.ndim)
    return pl.pallas_call(
        functools.partial(_combine_kernel, npt),
        grid_spec=pltpu.PrefetchScalarGridSpec(
            num_scalar_prefetch=1, grid=(r // TILE,),
            in_specs=[pl.BlockSpec((TILE, H2E_COLS), lambda i, pos: (i, 0)),
                      pl.BlockSpec((TILE, D_MODEL), lambda i, pos: (i, 0)),
                      pl.BlockSpec((1,) + g2.shape[1:], lambda i, pos: (i // npt, 0, 0)),
                      full(wg), full(wu), full(wd),
                      pl.BlockSpec(memory_space=pl.ANY)],
            out_specs=[pl.BlockSpec((TILE, D_MODEL), lambda i, pos: (jnp.minimum(i, last), 0)),
                       pl.BlockSpec((TILE, D_MODEL), lambda i, pos: (0, 0))],
            scratch_shapes=[pltpu.VMEM((2, TILE, D_MODEL), F32), pltpu.SemaphoreType.DMA((2,))]),
        out_shape=[jax.ShapeDtypeStruct((rp, D_MODEL), F32),
                   jax.ShapeDtypeStruct((rs, D_MODEL), F32)],
        compiler_params=_cparams(1), name="moe_combine",
    )(pos, h2e, xm, g2, wg, wu, wd, routed)


def _moe(h2e, grp, rank, counts, xm, g2, w, rp):
    r = h2e.shape[0]
    n_sorted_tiles = r // TILE + N_GROUPS
    cnt = counts[:N_GROUPS, 0].astype(jnp.int32)
    tiles_g = (cnt + TILE - 1) // TILE
    end_g = jnp.cumsum(tiles_g)
    off_g = (end_g - tiles_g) * TILE
    grp_flat = grp.reshape(r)
    pos = rank.reshape(r)
    for g in range(N_GROUPS):
        pos = pos + jnp.where(grp_flat == g, off_g[g], 0)
    t_idx = jnp.arange(n_sorted_tiles, dtype=jnp.int32)
    tile_grp = jnp.zeros_like(t_idx)
    for g in range(N_GROUPS - 1):
        tile_grp = tile_grp + (t_idx >= end_g[g]).astype(jnp.int32)
    tile_on = (t_idx < end_g[N_GROUPS - 1]).astype(jnp.int32)
    h2e_sorted = _dispatch_call(pos, h2e, n_sorted_tiles)
    routed = _routed_call(tile_grp, tile_on, h2e_sorted, w["wg"], w["wu"], w["wd"])
    return _combine_call(pos, h2e, xm, g2, w["sg"], w["su"], w["sd"], routed, rp)


def _rot_half_cols(wcols):
    half = QK_ROPE_DIM // 2
    return jnp.concatenate([-wcols[:, half:], wcols[:, :half]], axis=1)


def _pad_cols(wcols, width):
    return jnp.pad(wcols, ((0, 0), (0, width - wcols.shape[1])))


def _rope_tables(pos):
    half = QK_ROPE_DIM // 2
    inv = ROPE_THETA ** (-jnp.arange(half, dtype=F32) / half)
    ang = pos.astype(F32)[:, None] * inv[None, :]
    cos = jnp.cos(ang)
    sin = jnp.sin(ang)
    zeros = jnp.zeros((pos.shape[0], LANES - QK_ROPE_DIM), F32)
    return (jnp.concatenate([cos, cos, zeros], axis=1), jnp.concatenate([sin, sin, zeros], axis=1))


def _layer_weights(l, p, cos, sin):
    o1 = GMLP_WIDTH
    o2 = 2 * GMLP_WIDTH
    o3 = o2 + Q_LORA_RANK
    o4 = o3 + KV_LORA_RANK
    w_in = p["w_in"][l]
    kpe_cols = w_in[:, o4:]
    w_in2 = jnp.concatenate([w_in[:, :o4], _pad_cols(kpe_cols, LANES),
                             _pad_cols(_rot_half_cols(kpe_cols), LANES)], axis=1).astype(BF16)
    w_qb = p["w_qb"][l]
    q_parts = []
    for hd in range(MLA_HEADS):
        base = hd * QK_HEAD_DIM
        rope_cols = w_qb[:, base + QK_NOPE_DIM:base + QK_HEAD_DIM]
        q_parts += [w_qb[:, base:base + QK_NOPE_DIM], _pad_cols(rope_cols, LANES),
                    _pad_cols(_rot_half_cols(rope_cols), LANES)]
    w_qb2 = jnp.concatenate(q_parts, axis=1).astype(BF16)
    w_kvb = p["w_kvb"][l].reshape(KV_LORA_RANK, MLA_HEADS, QK_NOPE_DIM + V_HEAD_DIM)
    w_kvb2 = jnp.concatenate([w_kvb[:, :, :QK_NOPE_DIM].reshape(KV_LORA_RANK, -1),
                              w_kvb[:, :, QK_NOPE_DIM:].reshape(KV_LORA_RANK, -1)], axis=1).astype(BF16)
    qscale = LOG2E / math.sqrt(QK_HEAD_DIM)
    gq = _pad_cols(p["q_norm_g"][l][None, :] * qscale, QK_PAD)
    gk = _pad_cols(p["k_norm_g"][l][None, :], QK_PAD)

    ws = p["gmlp_ws"][l]
    tri = jnp.tril(jnp.ones((GMLP_CHUNK, GMLP_CHUNK), dtype=bool))
    wt = jnp.where(tri[None], ws, 0.0)
    hc = GMLP_CHUNK // 2
    top = wt[:, :hc, :hc]
    zero = jnp.zeros_like(top)
    wt_s = jnp.concatenate([jnp.concatenate([top, zero], axis=2), jnp.concatenate([zero, top], axis=2)], axis=1)
    gw = jnp.stack([wt, wt_s]).astype(BF16)
    b = p["gmlp_b"][l]
    b_s = jnp.concatenate([b[:, :hc], b[:, :hc]], axis=1)
    gb = jnp.broadcast_to(jnp.stack([b, b_s])[..., None], (2, GMLP_GROUPS, GMLP_CHUNK, LANES)).astype(F32)

    w_out = p["w_out"][l].astype(BF16)
    return dict(
        wg=p["exp_w_gate"][l].astype(BF16), wu=p["exp_w_up"][l].astype(BF16),
        wd=p["exp_w_down"][l].astype(BF16), sg=p["sh_w_gate"][l].astype(BF16),
        su=p["sh_w_up"][l].astype(BF16), sd=p["sh_w_down"][l].astype(BF16),
        ln1=p["ln1_g"][l][None, :], w_in=w_in2, cos=cos, sin=sin,
        gv=p["gmlp_v_g"][l].reshape(1, GMLP_WIDTH), gw=gw, gb=gb,
        q_a_g=p["q_a_g"][l][None, :], w_qb=w_qb2, kv_a_g=p["kv_a_g"][l][None, :], w_kvb=w_kvb2,
        gq=gq, gk=gk, w_out_a=w_out[:GMLP_WIDTH], w_out_b=w_out[GMLP_WIDTH:],
        ln2=p["ln2_g"][l][None, :], rw=_pad_cols(p["router_w"], LANES),
        rb=p["router_bias"].reshape(N_EXPERTS, 1),
    )


def kernel(x_prompt, x_sample, cache_kv_latent, cache_k_rope, c_prompt, c_sample, w_ada, b_ada, ln1_g, w_in,
           gmlp_v_g, gmlp_ws, gmlp_b, q_a_g, w_qb, kv_a_g, w_kvb, q_norm_g, k_norm_g, w_out, ln2_g, router_w,
           router_bias, exp_w_gate, exp_w_up, exp_w_down, sh_w_gate, sh_w_up, sh_w_down):
    p = dict(w_in=w_in, gmlp_v_g=gmlp_v_g, gmlp_ws=gmlp_ws, gmlp_b=gmlp_b, q_a_g=q_a_g, w_qb=w_qb,
             kv_a_g=kv_a_g, w_kvb=w_kvb, q_norm_g=q_norm_g, k_norm_g=k_norm_g, w_out=w_out, ln1_g=ln1_g,
             ln2_g=ln2_g, router_w=router_w, router_bias=router_bias, exp_w_gate=exp_w_gate,
             exp_w_up=exp_w_up, exp_w_down=exp_w_down, sh_w_gate=sh_w_gate, sh_w_up=sh_w_up,
             sh_w_down=sh_w_down)
    batch, seq, _ = x_prompt.shape
    n_seq, s_len, _ = x_sample.shape
    depth, _, past, _ = cache_kv_latent.shape
    assert batch == 1 and s_len == SUB and n_seq == N_SUB and n_seq * s_len == TILE
    assert seq % TILE == 0 and seq % ATT_TQ == 0 and past % GMLP_CHUNK == 0 and past % CHUNK == 0
    rp = seq
    rs = n_seq * s_len

    pos = jnp.concatenate([jnp.arange(seq, dtype=jnp.int32),
                           jnp.tile(past + jnp.arange(s_len, dtype=jnp.int32), n_seq)])
    cos, sin = _rope_tables(pos)

    c_all = jnp.concatenate([jnp.broadcast_to(c_prompt, (N_SUB, D_MODEL)), c_sample], axis=0)
    mod = _ada_call(c_all, w_ada, b_ada)
    mod = mod.reshape(depth, 2, N_SUB, 6, D_MODEL)

    weights = [_layer_weights(l, p, cos, sin) for l in range(depth)]
    lat_all = cache_kv_latent.reshape(depth, n_seq * past, KV_LORA_RANK)
    pe_all = jnp.pad(cache_k_rope.reshape(depth, n_seq * past, QK_ROPE_DIM),
                     ((0, 0), (0, 0), (0, LANES - QK_ROPE_DIM)))
    k_past, v_past = _cache_kv_call(lat_all, pe_all, jnp.stack([w["w_kvb"] for w in weights]),
                                    jnp.stack([w["gk"] for w in weights]))

    xp = x_prompt.reshape(rp, D_MODEL)
    xs = x_sample.reshape(rs, D_MODEL)
    lat_p, pe_p, lat_s, pe_s, v_s = [], [], [], [], []
    for l in range(depth):
        w = weights[l]
        sh1, sc1, g1, sh2, sc2, g2 = [mod[l, :, :, j, :] for j in range(6)]
        a, q, k, v, ckv, kpe, vn = _mix_in_call(xp, xs, sc1, sh1, w)
        bp = _prompt_attn_call(q, k, v, rp)
        bs = _sample_attn_call(l, q, k, v, k_past, v_past, rp, n_seq, s_len, past)
        xm, h2e, grp, rank, counts = _mix_out_call(a, bp, bs, xp, xs, g1, sc2, sh2, w)
        xp, xs = _moe(h2e, grp, rank, counts, xm, g2, w, rp)
        lat_p.append(ckv[:rp].reshape(batch, seq, KV_LORA_RANK))
        pe_p.append(kpe[:rp, :QK_ROPE_DIM].reshape(batch, seq, QK_ROPE_DIM))
        lat_s.append(ckv[rp:].reshape(n_seq, s_len, KV_LORA_RANK))
        pe_s.append(kpe[rp:, :QK_ROPE_DIM].reshape(n_seq, s_len, QK_ROPE_DIM))
        v_s.append(vn.reshape(n_seq, s_len, GMLP_WIDTH))
    return (xp.reshape(batch, seq, D_MODEL), xs.reshape(n_seq, s_len, D_MODEL),
            jnp.stack(lat_p), jnp.stack(pe_p), jnp.stack(lat_s), jnp.stack(pe_s), jnp.stack(v_s))
```

```python
import functools
import math

import jax
import jax.numpy as jnp
from jax import lax
from jax.experimental import pallas as pl
from jax.experimental.pallas import tpu as pltpu

F32 = jnp.float32
BF16 = jnp.bfloat16

D_MODEL = 1024
CHUNK = 64
GMLP_WIDTH = 512
GMLP_GROUPS = 4
GMLP_CHUNK = 128
MLA_HEADS = 4
QK_NOPE_DIM = 128
QK_ROPE_DIM = 64
QK_HEAD_DIM = 192
V_HEAD_DIM = 128
Q_LORA_RANK = 384
KV_LORA_RANK = 256
ROPE_THETA = 10000.0
N_EXPERTS = 16
N_GROUPS = 4
EXPERTS_PER_GROUP = 4
EXPERT_FF = 512
EPS = 1e-6

LANES = 128
TILE = 512
SUB = 64
N_SUB = TILE // SUB
QK_PAD = 256
IN_COLS = 2 * GMLP_WIDTH + Q_LORA_RANK + KV_LORA_RANK + 2 * LANES
Q_COLS = MLA_HEADS * 3 * LANES
H2E_COLS = D_MODEL + LANES
ROW_DMA_UNROLL = 8
ATT_TK = 512
ATT_TQ = 2 * ATT_TK
ATT_UNROLL = 4
VMEM_LIMIT = 56 * 1024 * 1024
LOG2E = 1.4426950408889634
SCORE_BOUND_MARGIN = 1.02
MAX_SCORE_BOUND = 48.0


def _cparams(n_axes):
    return pltpu.CompilerParams(dimension_semantics=("arbitrary",) * n_axes,
                                vmem_limit_bytes=VMEM_LIMIT)


def _rms(x, eps=EPS):
    return x * lax.rsqrt(jnp.mean(x * x, axis=-1, keepdims=True) + eps)


def _gelu(x):
    c = math.sqrt(2.0 / math.pi)
    return 0.5 * x * (1.0 + jnp.tanh(c * (x + 0.044715 * (x * x * x))))


def _modulate(h, scale, shift):
    h3 = h.reshape(N_SUB, SUB, h.shape[-1])
    h3 = h3 * (1.0 + scale[:, None, :]) + shift[:, None, :]
    return h3.reshape(h.shape)


def _gated_add(x, gate, y):
    y3 = y.reshape(N_SUB, SUB, y.shape[-1]) * gate[:, None, :]
    return x + y3.reshape(y.shape)


def _ada_kernel(c_ref, w_ref, b_ref, o_ref):
    c = c_ref[...]
    cs = (c / (1.0 + jnp.exp(-c))).astype(BF16)
    w = w_ref[0].astype(BF16)
    o_ref[0] = jnp.dot(cs, w, preferred_element_type=F32) + b_ref[0]


def _ada_call(c_all, w_ada, b_ada):
    depth = w_ada.shape[0]
    nblk = w_ada.shape[2] // D_MODEL
    return pl.pallas_call(
        _ada_kernel,
        grid=(depth, nblk),
        in_specs=[
            pl.BlockSpec((16, D_MODEL), lambda l, j: (0, 0)),
            pl.BlockSpec((1, D_MODEL, D_MODEL), lambda l, j: (l, 0, j)),
            pl.BlockSpec((1, 1, D_MODEL), lambda l, j: (l, 0, j)),
        ],
        out_specs=pl.BlockSpec((1, 16, D_MODEL), lambda l, j: (l, 0, j)),
        out_shape=jax.ShapeDtypeStruct((depth, 16, w_ada.shape[2]), F32),
        compiler_params=_cparams(2),
        name="ada_mod",
    )(c_all, w_ada, b_ada.reshape(depth, 1, -1))


def _mix_in_kernel(n_prompt_tiles, xp_ref, xs_ref, sc_ref, sh_ref, ln_ref, win_ref, cos_ref, sin_ref,
                   gv_ref, gw_ref, gb_ref, qag_ref, wqb_ref, kvag_ref, wkvb_ref, gq_ref, gk_ref, off_ref,
                   a_ref, q_ref, k_ref, v_ref, ckv_ref, kpe_ref, vn_ref):
    i = pl.program_id(0)
    is_sample = i >= n_prompt_tiles
    x = jnp.where(is_sample, xs_ref[...], xp_ref[...])
    h = _rms(x) * ln_ref[...]
    h = _modulate(h, sc_ref[0], sh_ref[0])
    z = jnp.dot(h.astype(BF16), win_ref[...], preferred_element_type=F32)

    u = _gelu(z[:, :GMLP_WIDTH])
    v = _gelu(z[:, GMLP_WIDTH:2 * GMLP_WIDTH])
    gv = gv_ref[...]
    vn_parts = []
    for g in range(GMLP_GROUPS):
        sl = slice(g * LANES, (g + 1) * LANES)
        vn_parts.append(_rms(v[:, sl]) * gv[:, sl])

    @pl.when(is_sample)
    def _():
        for g in range(GMLP_GROUPS):
            vn_ref[:, g * LANES:(g + 1) * LANES] = vn_parts[g]

    for g in range(GMLP_GROUPS):
        sl = slice(g * LANES, (g + 1) * LANES)
        vb = vn_parts[g].astype(BF16)
        wg = gw_ref[0, g]
        bg = gb_ref[0, g]
        for c in range(TILE // GMLP_CHUNK):
            rows = slice(c * GMLP_CHUNK, (c + 1) * GMLP_CHUNK)
            s = jnp.dot(wg, vb[rows], preferred_element_type=F32) + bg
            a_ref[rows, sl] = (u[rows, sl] * s).astype(BF16)

    cos = cos_ref[...]
    sin = sin_ref[...]

    o2 = 2 * GMLP_WIDTH
    o3 = o2 + Q_LORA_RANK
    o4 = o3 + KV_LORA_RANK
    ql = _rms(z[:, o2:o3]) * qag_ref[...]
    qq = jnp.dot(ql.astype(BF16), wqb_ref[...], preferred_element_type=F32)
    gq = gq_ref[...]
    for hd in range(MLA_HEADS):
        base = hd * 3 * LANES
        nope = qq[:, base:base + LANES]
        rope = qq[:, base + LANES:base + 2 * LANES] * cos + qq[:, base + 2 * LANES:base + 3 * LANES] * sin
        ss = jnp.sum(nope * nope, axis=-1, keepdims=True) + jnp.sum(rope * rope, axis=-1, keepdims=True)
        rinv = lax.rsqrt(ss * (1.0 / QK_HEAD_DIM) + EPS)
        q_ref[hd, :, :LANES] = (nope * rinv * gq[:, :LANES]).astype(BF16)
        q_ref[hd, :, LANES:] = (rope * rinv * gq[:, LANES:] + off_ref[0:1, :]).astype(BF16)

    ckv = _rms(z[:, o3:o4]) * kvag_ref[...]
    ckv_ref[...] = ckv
    kpe = z[:, o4:o4 + LANES] * cos + z[:, o4 + LANES:o4 + 2 * LANES] * sin
    kpe_ref[...] = kpe
    kv = jnp.dot(ckv.astype(BF16), wkvb_ref[...], preferred_element_type=F32)
    gk = gk_ref[...]
    kpe_ss = jnp.sum(kpe * kpe, axis=-1, keepdims=True)
    for hd in range(MLA_HEADS):
        nope = kv[:, hd * LANES:(hd + 1) * LANES]
        ss = jnp.sum(nope * nope, axis=-1, keepdims=True) + kpe_ss
        rinv = lax.rsqrt(ss * (1.0 / QK_HEAD_DIM) + EPS)
        k_ref[hd, :, :LANES] = (nope * rinv * gk[:, :LANES]).astype(BF16)
        k_ref[hd, :, LANES:] = (kpe * rinv * gk[:, LANES:] + off_ref[1:2, :]).astype(BF16)
        v_ref[hd] = kv[:, (MLA_HEADS + hd) * LANES:(MLA_HEADS + hd + 1) * LANES].astype(BF16)


def _mix_in_call(xp, xs, sc1, sh1, w):
    rp, rs = xp.shape[0], xs.shape[0]
    npt = rp // TILE
    nt = npt + rs // TILE
    r = rp + rs
    last = npt - 1

    def full(a):
        nd = a.ndim
        return pl.BlockSpec(a.shape, lambda i: (0,) * nd)

    def variant(a):
        nd = a.ndim
        return pl.BlockSpec((1,) + a.shape[1:], lambda i: (i // npt,) + (0,) * (nd - 1))

    row = lambda width: pl.BlockSpec((TILE, width), lambda i: (i, 0))
    head = lambda width: pl.BlockSpec((MLA_HEADS, TILE, width), lambda i: (0, i, 0))
    in_specs = [
        pl.BlockSpec((TILE, D_MODEL), lambda i: (jnp.minimum(i, last), 0)),
        pl.BlockSpec((TILE, D_MODEL), lambda i: (0, 0)),
        variant(sc1), variant(sh1), full(w["ln1"]), full(w["w_in"]),
        row(LANES), row(LANES),
        full(w["gv"]), variant(w["gw"]), variant(w["gb"]),
        full(w["q_a_g"]), full(w["w_qb"]), full(w["kv_a_g"]), full(w["w_kvb"]),
        full(w["gq"]), full(w["gk"]), full(w["off"]),
    ]
    out_specs = [
        row(GMLP_WIDTH), head(QK_PAD), head(QK_PAD), head(V_HEAD_DIM),
        row(KV_LORA_RANK), row(LANES),
        pl.BlockSpec((TILE, GMLP_WIDTH), lambda i: (0, 0)),
    ]
    out_shape = [
        jax.ShapeDtypeStruct((r, GMLP_WIDTH), BF16),
        jax.ShapeDtypeStruct((MLA_HEADS, r, QK_PAD), BF16),
        jax.ShapeDtypeStruct((MLA_HEADS, r, QK_PAD), BF16),
        jax.ShapeDtypeStruct((MLA_HEADS, r, V_HEAD_DIM), BF16),
        jax.ShapeDtypeStruct((r, KV_LORA_RANK), F32),
        jax.ShapeDtypeStruct((r, LANES), F32),
        jax.ShapeDtypeStruct((rs, GMLP_WIDTH), F32),
    ]
    return pl.pallas_call(
        functools.partial(_mix_in_kernel, npt),
        grid=(nt,), in_specs=in_specs, out_specs=out_specs, out_shape=out_shape,
        compiler_params=_cparams(1), name="mix_in",
    )(xp, xs, sc1, sh1, w["ln1"], w["w_in"], w["cos"], w["sin"], w["gv"], w["gw"], w["gb"],
      w["q_a_g"], w["w_qb"], w["kv_a_g"], w["w_kvb"], w["gq"], w["gk"], w["off"])


def _cache_kv_kernel(lat_ref, pe_ref, wkvb_ref, gk_ref, off_ref, k_ref, v_ref):
    lat = lat_ref[0]
    kpe = pe_ref[0]
    kv = jnp.dot(lat.astype(BF16), wkvb_ref[0], preferred_element_type=F32)
    gk = gk_ref[0]
    kpe_ss = jnp.sum(kpe * kpe, axis=-1, keepdims=True)
    for hd in range(MLA_HEADS):
        nope = kv[:, hd * LANES:(hd + 1) * LANES]
        ss = jnp.sum(nope * nope, axis=-1, keepdims=True) + kpe_ss
        rinv = lax.rsqrt(ss * (1.0 / QK_HEAD_DIM) + EPS)
        k_ref[0, hd, :, :LANES] = (nope * rinv * gk[:, :LANES]).astype(BF16)
        k_ref[0, hd, :, LANES:] = (kpe * rinv * gk[:, LANES:] + off_ref[0, 1:2, :]).astype(BF16)
        v_ref[0, hd] = kv[:, (MLA_HEADS + hd) * LANES:(MLA_HEADS + hd + 1) * LANES].astype(BF16)


def _cache_kv_call(lat, pe, w_kvb, gk, off):
    depth, rows, _ = lat.shape
    t = 1024
    return pl.pallas_call(
        _cache_kv_kernel,
        grid=(depth, rows // t),
        in_specs=[
            pl.BlockSpec((1, t, KV_LORA_RANK), lambda l, i: (l, i, 0)),
            pl.BlockSpec((1, t, LANES), lambda l, i: (l, i, 0)),
            pl.BlockSpec((1,) + w_kvb.shape[1:], lambda l, i: (l, 0, 0)),
            pl.BlockSpec((1, 1, QK_PAD), lambda l, i: (l, 0, 0)),
            pl.BlockSpec((1, 2, LANES), lambda l, i: (l, 0, 0)),
        ],
        out_specs=[
            pl.BlockSpec((1, MLA_HEADS, t, QK_PAD), lambda l, i: (l, 0, i, 0)),
            pl.BlockSpec((1, MLA_HEADS, t, V_HEAD_DIM), lambda l, i: (l, 0, i, 0)),
        ],
        out_shape=[
            jax.ShapeDtypeStruct((depth, MLA_HEADS, rows, QK_PAD), BF16),
            jax.ShapeDtypeStruct((depth, MLA_HEADS, rows, V_HEAD_DIM), BF16),
        ],
        compiler_params=_cparams(2), name="cache_kv",
    )(lat, pe, w_kvb, gk, off)


def _attn_step(q, k, v, m, l, acc, mask):
    s = lax.dot_general(q, k, (((1,), (1,)), ((), ())), preferred_element_type=F32)
    if mask is not None:
        s = jnp.where(mask, s, -1e30)
    m_new = jnp.maximum(m, jnp.max(s, axis=-1, keepdims=True))
    alpha = jnp.exp2(m - m_new)
    p = jnp.exp2(s - m_new)
    l = alpha * l + jnp.sum(p, axis=-1, keepdims=True)
    acc = alpha * acc + jnp.dot(p.astype(BF16), v, preferred_element_type=F32)
    return m_new, l, acc


def _bounded_step(q, k, v_aug, acc, mask):
    s = lax.dot_general(q, k, (((1,), (1,)), ((), ())), preferred_element_type=F32)
    if mask is not None:
        s = jnp.where(mask, s, -1e30)
    return acc + jnp.dot(jnp.exp2(s).astype(BF16), v_aug, preferred_element_type=F32)


def _prompt_attn_kernel(bounded_ref, q_ref, k_ref, v_ref, o_ref):
    qi = pl.program_id(1)
    qa = q_ref[0, :ATT_TK]
    qb = q_ref[0, ATT_TK:]
    row = lax.broadcasted_iota(jnp.int32, (ATT_TK, ATT_TK), 0)
    col = lax.broadcasted_iota(jnp.int32, (ATT_TK, ATT_TK), 1)
    mask = (col // CHUNK) <= (row // CHUNK)

    def kv(j):
        off = pl.multiple_of(j * ATT_TK, ATT_TK)
        return k_ref[0, pl.ds(off, ATT_TK), :], v_ref[0, pl.ds(off, ATT_TK), :]

    @pl.when(bounded_ref[0] > 0)
    def _():
        lane = lax.broadcasted_iota(jnp.int32, (ATT_TK, LANES), 1)
        ones_col = jnp.where(lane == 0, 1.0, 0.0).astype(BF16)

        def kv_aug(j):
            k, v = kv(j)
            return k, jnp.concatenate([v, ones_col], axis=1)

        def blocks(first, count, carry):
            acc_a, acc_b = carry
            for u in range(count):
                k, v = kv_aug(first + u)
                acc_a = _bounded_step(qa, k, v, acc_a, None)
                acc_b = _bounded_step(qb, k, v, acc_b, None)
            return acc_a, acc_b

        zero = jnp.zeros((ATT_TK, 2 * LANES), F32)
        n_trips = (2 * qi) // ATT_UNROLL
        carry = lax.fori_loop(0, n_trips, lambda j, c: blocks(ATT_UNROLL * j, ATT_UNROLL, c), (zero, zero))
        acc_a, acc_b = lax.cond(qi % 2 == 1, lambda c: blocks(ATT_UNROLL * n_trips, 2, c), lambda c: c, carry)
        k, v = kv_aug(2 * qi)
        acc_a = _bounded_step(qa, k, v, acc_a, mask)
        acc_b = _bounded_step(qb, k, v, acc_b, None)
        k, v = kv_aug(2 * qi + 1)
        acc_b = _bounded_step(qb, k, v, acc_b, mask)
        o_ref[:ATT_TK, :] = (acc_a[:, :LANES] / acc_a[:, LANES:LANES + 1]).astype(BF16)
        o_ref[ATT_TK:, :] = (acc_b[:, :LANES] / acc_b[:, LANES:LANES + 1]).astype(BF16)

    @pl.when(bounded_ref[0] == 0)
    def _():
        def init():
            return (jnp.full((ATT_TK, 1), -1e30, F32), jnp.zeros((ATT_TK, 1), F32),
                    jnp.zeros((ATT_TK, V_HEAD_DIM), F32))

        def body(j, carry):
            k, v = kv(j)
            return _attn_step(qa, k, v, *carry[0], None), _attn_step(qb, k, v, *carry[1], None)

        sa, sb = lax.fori_loop(0, 2 * qi, body, (init(), init()))
        k, v = kv(2 * qi)
        sa = _attn_step(qa, k, v, *sa, mask)
        sb = _attn_step(qb, k, v, *sb, None)
        k, v = kv(2 * qi + 1)
        sb = _attn_step(qb, k, v, *sb, mask)
        o_ref[:ATT_TK, :] = (sa[2] / sa[1]).astype(BF16)
        o_ref[ATT_TK:, :] = (sb[2] / sb[1]).astype(BF16)


def _prompt_attn_call(bounded, q, k, v, rp):
    r = q.shape[1]
    return pl.pallas_call(
        _prompt_attn_kernel,
        grid_spec=pltpu.PrefetchScalarGridSpec(
            num_scalar_prefetch=1, grid=(MLA_HEADS, rp // ATT_TQ),
            in_specs=[
                pl.BlockSpec((1, ATT_TQ, QK_PAD), lambda h, i, b: (h, i, 0)),
                pl.BlockSpec((1, r, QK_PAD), lambda h, i, b: (h, 0, 0)),
                pl.BlockSpec((1, r, V_HEAD_DIM), lambda h, i, b: (h, 0, 0)),
            ],
            out_specs=pl.BlockSpec((ATT_TQ, V_HEAD_DIM), lambda h, i, b: (i, h))),
        out_shape=jax.ShapeDtypeStruct((rp, MLA_HEADS * V_HEAD_DIM), BF16),
        compiler_params=_cparams(2), name="prompt_attn",
    )(bounded, q, k, v)


def _sample_attn_kernel(q_ref, kp_ref, vp_ref, kn_ref, vn_ref, o_ref):
    q = q_ref[0]
    nt = (((1,), (1,)), ((), ()))
    s1 = lax.dot_general(q, kp_ref[0, 0], nt, preferred_element_type=F32)
    s2 = lax.dot_general(q, kn_ref[0], nt, preferred_element_type=F32)
    m = jnp.maximum(jnp.max(s1, axis=-1, keepdims=True), jnp.max(s2, axis=-1, keepdims=True))
    p1 = jnp.exp2(s1 - m)
    p2 = jnp.exp2(s2 - m)
    l = jnp.sum(p1, axis=-1, keepdims=True) + jnp.sum(p2, axis=-1, keepdims=True)
    o = (jnp.dot(p1.astype(BF16), vp_ref[0, 0], preferred_element_type=F32)
         + jnp.dot(p2.astype(BF16), vn_ref[0], preferred_element_type=F32))
    o_ref[...] = (o / l).astype(BF16)


def _sample_attn_call(layer, q, k, v, k_past, v_past, rp, n_seq, s_len, past):
    first = rp // s_len
    return pl.pallas_call(
        _sample_attn_kernel,
        grid=(n_seq, MLA_HEADS),
        in_specs=[
            pl.BlockSpec((1, s_len, QK_PAD), lambda b, h: (h, first + b, 0)),
            pl.BlockSpec((1, 1, past, QK_PAD), lambda b, h: (layer, h, b, 0)),
            pl.BlockSpec((1, 1, past, V_HEAD_DIM), lambda b, h: (layer, h, b, 0)),
            pl.BlockSpec((1, s_len, QK_PAD), lambda b, h: (h, first + b, 0)),
            pl.BlockSpec((1, s_len, V_HEAD_DIM), lambda b, h: (h, first + b, 0)),
        ],
        out_specs=pl.BlockSpec((s_len, V_HEAD_DIM), lambda b, h: (b, h)),
        out_shape=jax.ShapeDtypeStruct((n_seq * s_len, MLA_HEADS * V_HEAD_DIM), BF16),
        compiler_params=_cparams(2), name="sample_attn",
    )(q, k_past, v_past, k, v)


def _route(logits_t, bias_col):
    scores = 1.0 / (1.0 + jnp.exp(-logits_t))
    biased = scores + bias_col
    s_rows = [scores[e:e + 1, :] for e in range(N_EXPERTS)]
    b_rows = [biased[e:e + 1, :] for e in range(N_EXPERTS)]
    group_score = []
    for g in range(N_GROUPS):
        rows = b_rows[g * EXPERTS_PER_GROUP:(g + 1) * EXPERTS_PER_GROUP]
        best = None
        for a in range(EXPERTS_PER_GROUP):
            for b in range(a + 1, EXPERTS_PER_GROUP):
                pair = rows[a] + rows[b]
                best = pair if best is None else jnp.maximum(best, pair)
        group_score.append(best)
    best_group = jnp.zeros_like(group_score[0], dtype=jnp.int32)
    best_val = group_score[0]
    for g in range(1, N_GROUPS):
        better = group_score[g] > best_val
        best_group = jnp.where(better, g, best_group)
        best_val = jnp.where(better, group_score[g], best_val)
    selected = []
    for e in range(N_EXPERTS):
        g = e // EXPERTS_PER_GROUP
        rank = jnp.zeros_like(best_group)
        for j in range(g * EXPERTS_PER_GROUP, (g + 1) * EXPERTS_PER_GROUP):
            if j == e:
                continue
            ahead = b_rows[j] > b_rows[e]
            if j < e:
                ahead = ahead | (b_rows[j] == b_rows[e])
            rank = rank + ahead.astype(jnp.int32)
        selected.append((best_group == g) & (rank < 2))
    denom = jnp.zeros_like(s_rows[0])
    for e in range(N_EXPERTS):
        denom = denom + jnp.where(selected[e], s_rows[e], 0.0)
    return [jnp.where(selected[e], s_rows[e] / denom, 0.0) for e in range(N_EXPERTS)], best_group


def _mix_out_kernel(n_prompt_tiles, a_ref, bp_ref, bs_ref, xp_ref, xs_ref, wa_ref, wb_ref, g1_ref, sc_ref,
                    sh_ref, ln_ref, rw_ref, rb_ref, xm_ref, h2e_ref, grp_ref, rank_ref, cnt_ref,
                    gt_ref, carry_ref):
    i = pl.program_id(0)
    is_sample = i >= n_prompt_tiles
    x = jnp.where(is_sample, xs_ref[...], xp_ref[...])
    b = jnp.where(is_sample, bs_ref[...], bp_ref[...])
    mix = (jnp.dot(a_ref[...], wa_ref[...], preferred_element_type=F32)
           + jnp.dot(b, wb_ref[...], preferred_element_type=F32))
    xm = _gated_add(x, g1_ref[0], mix)
    xm_ref[...] = xm
    h2 = _modulate(_rms(xm) * ln_ref[...], sc_ref[0], sh_ref[0])
    h2e_ref[:, :D_MODEL] = h2
    logits = jnp.dot(h2, rw_ref[...], preferred_element_type=F32, precision=lax.Precision.HIGHEST)
    gate_rows, best_group = _route(logits.T[:N_EXPERTS, :], rb_ref[...])
    gt_ref[...] = jnp.zeros_like(gt_ref)
    for e in range(N_EXPERTS):
        gt_ref[e:e + 1, :] = gate_rows[e]
    h2e_ref[:, D_MODEL:] = gt_ref[...].T

    @pl.when(i == 0)
    def _():
        carry_ref[...] = jnp.zeros_like(carry_ref)

    sub = lax.broadcasted_iota(jnp.int32, (8, TILE), 0)
    onehot = jnp.where(sub == best_group, 1.0, 0.0)
    r_idx = lax.broadcasted_iota(jnp.int32, (TILE, TILE), 0)
    c_idx = lax.broadcasted_iota(jnp.int32, (TILE, TILE), 1)
    upper = jnp.where(r_idx <= c_idx, 1.0, 0.0).astype(BF16)
    cum = jnp.dot(onehot.astype(BF16), upper, preferred_element_type=F32)
    carry = carry_ref[...]
    rank = jnp.sum(onehot * (cum - 1.0 + carry[:, :1]), axis=0, keepdims=True)
    grp_ref[0] = best_group
    rank_ref[0] = rank.astype(jnp.int32)
    carry = carry + jnp.sum(onehot, axis=1, keepdims=True)
    carry_ref[...] = carry
    cnt_ref[...] = carry


def _mix_out_call(a, bp, bs, xp, xs, g1, sc2, sh2, w):
    rp, rs = xp.shape[0], xs.shape[0]
    npt = rp // TILE
    nt = npt + rs // TILE
    r = rp + rs
    last = npt - 1

    def full(arr):
        nd = arr.ndim
        return pl.BlockSpec(arr.shape, lambda i: (0,) * nd)

    def variant(arr):
        nd = arr.ndim
        return pl.BlockSpec((1,) + arr.shape[1:], lambda i: (i // npt,) + (0,) * (nd - 1))

    row = lambda width: pl.BlockSpec((TILE, width), lambda i: (i, 0))
    prow = lambda width: pl.BlockSpec((TILE, width), lambda i: (jnp.minimum(i, last), 0))
    srow = lambda width: pl.BlockSpec((TILE, width), lambda i: (0, 0))
    return pl.pallas_call(
        functools.partial(_mix_out_kernel, npt),
        grid=(nt,),
        in_specs=[row(GMLP_WIDTH), prow(GMLP_WIDTH), srow(GMLP_WIDTH), prow(D_MODEL), srow(D_MODEL),
                  full(w["w_out_a"]), full(w["w_out_b"]), variant(g1), variant(sc2), variant(sh2),
                  full(w["ln2"]), full(w["rw"]), full(w["rb"])],
        out_specs=[row(D_MODEL), row(H2E_COLS),
                   pl.BlockSpec((1, 1, TILE), lambda i: (i, 0, 0)),
                   pl.BlockSpec((1, 1, TILE), lambda i: (i, 0, 0)),
                   pl.BlockSpec((8, LANES), lambda i: (0, 0))],
        out_shape=[jax.ShapeDtypeStruct((r, D_MODEL), F32),
                   jax.ShapeDtypeStruct((r, H2E_COLS), F32),
                   jax.ShapeDtypeStruct((nt, 1, TILE), jnp.int32),
                   jax.ShapeDtypeStruct((nt, 1, TILE), jnp.int32),
                   jax.ShapeDtypeStruct((8, LANES), F32)],
        scratch_shapes=[pltpu.VMEM((LANES, TILE), F32), pltpu.VMEM((8, LANES), F32)],
        compiler_params=_cparams(1), name="mix_out",
    )(a, bp, bs, xp, xs, w["w_out_a"], w["w_out_b"], g1, sc2, sh2, w["ln2"], w["rw"], w["rb"])


def _silu(x):
    return x / (1.0 + jnp.exp(-x))


def _invert_kernel(pos_ref, src_ref):
    def clear(s, carry):
        src_ref[s] = 0
        return carry

    lax.fori_loop(0, src_ref.shape[0], clear, 0, unroll=ROW_DMA_UNROLL)

    def place(i, carry):
        src_ref[pos_ref[i]] = i
        return carry

    lax.fori_loop(0, pos_ref.shape[0], place, 0, unroll=ROW_DMA_UNROLL)


def _invert_call(pos, n_slots):
    return pl.pallas_call(
        _invert_kernel,
        in_specs=[pl.BlockSpec(memory_space=pltpu.SMEM)],
        out_specs=pl.BlockSpec(memory_space=pltpu.SMEM),
        out_shape=jax.ShapeDtypeStruct((n_slots,), jnp.int32),
        name="moe_invert",
    )(pos)


def _routed_kernel(grp_ref, on_ref, src_ref, h_ref, wg_ref, wu_ref, wd_ref, o_ref, buf_ref, hb_ref, sem):
    t = pl.program_id(0)
    e = pl.program_id(1)
    n = pl.num_programs(0)
    slot = t % 2
    on = on_ref[t] > 0

    def gather(tile, s):
        base = tile * TILE

        def issue(r, carry):
            tok = src_ref[base + r]
            pltpu.make_async_copy(h_ref.at[pl.ds(tok, 1)], buf_ref.at[s, pl.ds(r, 1)], sem.at[s]).start()
            return carry

        lax.fori_loop(0, TILE, issue, 0, unroll=ROW_DMA_UNROLL)

    @pl.when((t == 0) & (e == 0) & on)
    def _():
        gather(0, 0)

    @pl.when((e == 0) & (on_ref[jnp.minimum(t + 1, n - 1)] > 0) & (t + 1 < n))
    def _():
        gather(t + 1, 1 - slot)

    @pl.when(jnp.logical_not(on) & (e == 0))
    def _():
        o_ref[...] = jnp.zeros_like(o_ref)

    @pl.when(on)
    def _():
        @pl.when(e == 0)
        def _():
            pltpu.make_async_copy(h_ref.at[pl.ds(0, TILE)], buf_ref.at[slot], sem.at[slot]).wait()
            hb_ref[...] = buf_ref[slot, :, :D_MODEL].astype(BF16)

        h = hb_ref[...]
        hg = jnp.dot(h, wg_ref[0], preferred_element_type=F32)
        hu = jnp.dot(h, wu_ref[0], preferred_element_type=F32)
        lane = lax.broadcasted_iota(jnp.int32, (TILE, LANES), 1)
        expert = grp_ref[t] * EXPERTS_PER_GROUP + e
        gate = jnp.sum(jnp.where(lane == expert, buf_ref[slot, :, D_MODEL:], 0.0), axis=-1, keepdims=True)
        act = _silu(hg) * hu
        out = jnp.dot(act.astype(BF16), wd_ref[0], preferred_element_type=F32) * gate

        @pl.when(e == 0)
        def _():
            o_ref[...] = out

        @pl.when(e > 0)
        def _():
            o_ref[...] += out


def _routed_call(tile_grp, tile_on, src, h2e, wg, wu, wd):
    nts = src.shape[0] // TILE

    def wspec(arr):
        def imap(t, e, grp, on, src):
            return (jnp.where(on[t] > 0, grp[t] * EXPERTS_PER_GROUP + e, N_EXPERTS - 1), 0, 0)
        return pl.BlockSpec((1,) + arr.shape[1:], imap)

    return pl.pallas_call(
        _routed_kernel,
        grid_spec=pltpu.PrefetchScalarGridSpec(
            num_scalar_prefetch=3, grid=(nts, EXPERTS_PER_GROUP),
            in_specs=[pl.BlockSpec(memory_space=pl.ANY), wspec(wg), wspec(wu), wspec(wd)],
            out_specs=pl.BlockSpec((TILE, D_MODEL), lambda t, e, grp, on, src: (t, 0)),
            scratch_shapes=[pltpu.VMEM((2, TILE, H2E_COLS), F32), pltpu.VMEM((TILE, D_MODEL), BF16),
                            pltpu.SemaphoreType.DMA((2,))]),
        out_shape=jax.ShapeDtypeStruct((nts * TILE, D_MODEL), F32),
        compiler_params=_cparams(2), name="moe_routed",
    )(tile_grp, tile_on, src, h2e, wg, wu, wd)


def _combine_kernel(n_prompt_tiles, pos_ref, h_ref, xm_ref, g2_ref, wg_ref, wu_ref, wd_ref, routed_ref,
                    yp_ref, ys_ref, buf_ref, sem):
    i = pl.program_id(0)
    n = pl.num_programs(0)
    slot = i % 2

    def gather(tile, s):
        base = tile * TILE

        def issue(r, carry):
            p = pos_ref[base + r]
            pltpu.make_async_copy(routed_ref.at[pl.ds(p, 1)], buf_ref.at[s, pl.ds(r, 1)], sem.at[s]).start()
            return carry

        lax.fori_loop(0, TILE, issue, 0, unroll=ROW_DMA_UNROLL)

    @pl.when(i == 0)
    def _():
        gather(0, 0)

    @pl.when(i + 1 < n)
    def _():
        gather(i + 1, 1 - slot)

    h = h_ref[:, :D_MODEL].astype(BF16)
    act = _silu(jnp.dot(h, wg_ref[...], preferred_element_type=F32)) * jnp.dot(
        h, wu_ref[...], preferred_element_type=F32)
    shared = jnp.dot(act.astype(BF16), wd_ref[...], preferred_element_type=F32)
    pltpu.make_async_copy(routed_ref.at[pl.ds(0, TILE)], buf_ref.at[slot], sem.at[slot]).wait()
    y = _gated_add(xm_ref[...], g2_ref[0], shared + buf_ref[slot])

    @pl.when(i < n_prompt_tiles)
    def _():
        yp_ref[...] = y

    @pl.when(i >= n_prompt_tiles)
    def _():
        ys_ref[...] = y


def _combine_call(pos, h2e, xm, g2, wg, wu, wd, routed, rp):
    r = h2e.shape[0]
    rs = r - rp
    npt = rp // TILE
    last = npt - 1
    full = lambda arr: pl.BlockSpec(arr.shape, lambda i, pos: (0,) * arr.ndim)
    return pl.pallas_call(
        functools.partial(_combine_kernel, npt),
        grid_spec=pltpu.PrefetchScalarGridSpec(
            num_scalar_prefetch=1, grid=(r // TILE,),
            in_specs=[pl.BlockSpec((TILE, H2E_COLS), lambda i, pos: (i, 0)),
                      pl.BlockSpec((TILE, D_MODEL), lambda i, pos: (i, 0)),
                      pl.BlockSpec((1,) + g2.shape[1:], lambda i, pos: (i // npt, 0, 0)),
                      full(wg), full(wu), full(wd),
                      pl.BlockSpec(memory_space=pl.ANY)],
            out_specs=[pl.BlockSpec((TILE, D_MODEL), lambda i, pos: (jnp.minimum(i, last), 0)),
                       pl.BlockSpec((TILE, D_MODEL), lambda i, pos: (0, 0))],
            scratch_shapes=[pltpu.VMEM((2, TILE, D_MODEL), F32), pltpu.SemaphoreType.DMA((2,))]),
        out_shape=[jax.ShapeDtypeStruct((rp, D_MODEL), F32),
                   jax.ShapeDtypeStruct((rs, D_MODEL), F32)],
        compiler_params=_cparams(1), name="moe_combine",
    )(pos, h2e, xm, g2, wg, wu, wd, routed)


def _moe(h2e, grp, rank, counts, xm, g2, w, rp):
    r = h2e.shape[0]
    n_sorted_tiles = r // TILE + N_GROUPS
    cnt = counts[:N_GROUPS, 0].astype(jnp.int32)
    tiles_g = (cnt + TILE - 1) // TILE
    end_g = jnp.cumsum(tiles_g)
    off_g = (end_g - tiles_g) * TILE
    grp_flat = grp.reshape(r)
    pos = rank.reshape(r)
    for g in range(N_GROUPS):
        pos = pos + jnp.where(grp_flat == g, off_g[g], 0)
    t_idx = jnp.arange(n_sorted_tiles, dtype=jnp.int32)
    tile_grp = jnp.zeros_like(t_idx)
    for g in range(N_GROUPS - 1):
        tile_grp = tile_grp + (t_idx >= end_g[g]).astype(jnp.int32)
    tile_on = (t_idx < end_g[N_GROUPS - 1]).astype(jnp.int32)
    src = _invert_call(pos, n_sorted_tiles * TILE)
    routed = _routed_call(tile_grp, tile_on, src, h2e, w["wg"], w["wu"], w["wd"])
    return _combine_call(pos, h2e, xm, g2, w["sg"], w["su"], w["sd"], routed, rp)


def _rot_half_cols(wcols):
    half = QK_ROPE_DIM // 2
    return jnp.concatenate([-wcols[:, half:], wcols[:, :half]], axis=1)


def _pad_cols(wcols, width):
    return jnp.pad(wcols, ((0, 0), (0, width - wcols.shape[1])))


def _rope_tables(pos):
    half = QK_ROPE_DIM // 2
    inv = ROPE_THETA ** (-jnp.arange(half, dtype=F32) / half)
    ang = pos.astype(F32)[:, None] * inv[None, :]
    cos = jnp.cos(ang)
    sin = jnp.sin(ang)
    zeros = jnp.zeros((pos.shape[0], LANES - QK_ROPE_DIM), F32)
    return (jnp.concatenate([cos, cos, zeros], axis=1), jnp.concatenate([sin, sin, zeros], axis=1))


def _layer_weights(l, p, cos, sin):
    o1 = GMLP_WIDTH
    o2 = 2 * GMLP_WIDTH
    o3 = o2 + Q_LORA_RANK
    o4 = o3 + KV_LORA_RANK
    w_in = p["w_in"][l]
    kpe_cols = w_in[:, o4:]
    w_in2 = jnp.concatenate([w_in[:, :o4], _pad_cols(kpe_cols, LANES),
                             _pad_cols(_rot_half_cols(kpe_cols), LANES)], axis=1).astype(BF16)
    w_qb = p["w_qb"][l]
    q_parts = []
    for hd in range(MLA_HEADS):
        base = hd * QK_HEAD_DIM
        rope_cols = w_qb[:, base + QK_NOPE_DIM:base + QK_HEAD_DIM]
        q_parts += [w_qb[:, base:base + QK_NOPE_DIM], _pad_cols(rope_cols, LANES),
                    _pad_cols(_rot_half_cols(rope_cols), LANES)]
    w_qb2 = jnp.concatenate(q_parts, axis=1).astype(BF16)
    w_kvb = p["w_kvb"][l].reshape(KV_LORA_RANK, MLA_HEADS, QK_NOPE_DIM + V_HEAD_DIM)
    w_kvb2 = jnp.concatenate([w_kvb[:, :, :QK_NOPE_DIM].reshape(KV_LORA_RANK, -1),
                              w_kvb[:, :, QK_NOPE_DIM:].reshape(KV_LORA_RANK, -1)], axis=1).astype(BF16)
    qscale = LOG2E / math.sqrt(QK_HEAD_DIM)
    gq = _pad_cols(p["q_norm_g"][l][None, :] * qscale, QK_PAD)
    gk = _pad_cols(p["k_norm_g"][l][None, :], QK_PAD)
    bound = (QK_HEAD_DIM * qscale * SCORE_BOUND_MARGIN * jnp.max(jnp.abs(p["q_norm_g"][l]))
             * jnp.max(jnp.abs(p["k_norm_g"][l])))
    bounded = bound <= MAX_SCORE_BOUND
    pad_lane = jnp.arange(LANES) == QK_ROPE_DIM
    off = jnp.stack([jnp.where(pad_lane, 1.0, 0.0),
                     jnp.where(pad_lane & bounded, -bound, 0.0)]).astype(F32)

    ws = p["gmlp_ws"][l]
    tri = jnp.tril(jnp.ones((GMLP_CHUNK, GMLP_CHUNK), dtype=bool))
    wt = jnp.where(tri[None], ws, 0.0)
    hc = GMLP_CHUNK // 2
    top = wt[:, :hc, :hc]
    zero = jnp.zeros_like(top)
    wt_s = jnp.concatenate([jnp.concatenate([top, zero], axis=2), jnp.concatenate([zero, top], axis=2)], axis=1)
    gw = jnp.stack([wt, wt_s]).astype(BF16)
    b = p["gmlp_b"][l]
    b_s = jnp.concatenate([b[:, :hc], b[:, :hc]], axis=1)
    gb = jnp.broadcast_to(jnp.stack([b, b_s])[..., None], (2, GMLP_GROUPS, GMLP_CHUNK, LANES)).astype(F32)

    w_out = p["w_out"][l].astype(BF16)
    return dict(
        wg=p["exp_w_gate"][l].astype(BF16), wu=p["exp_w_up"][l].astype(BF16),
        wd=p["exp_w_down"][l].astype(BF16), sg=p["sh_w_gate"][l].astype(BF16),
        su=p["sh_w_up"][l].astype(BF16), sd=p["sh_w_down"][l].astype(BF16),
        ln1=p["ln1_g"][l][None, :], w_in=w_in2, cos=cos, sin=sin,
        gv=p["gmlp_v_g"][l].reshape(1, GMLP_WIDTH), gw=gw, gb=gb,
        q_a_g=p["q_a_g"][l][None, :], w_qb=w_qb2, kv_a_g=p["kv_a_g"][l][None, :], w_kvb=w_kvb2,
        gq=gq, gk=gk, w_out_a=w_out[:GMLP_WIDTH], w_out_b=w_out[GMLP_WIDTH:],
        ln2=p["ln2_g"][l][None, :], rw=_pad_cols(p["router_w"], LANES),
        rb=p["router_bias"].reshape(N_EXPERTS, 1), off=off,
        bounded=bounded.astype(jnp.int32).reshape(1),
    )


def kernel(x_prompt, x_sample, cache_kv_latent, cache_k_rope, c_prompt, c_sample, w_ada, b_ada, ln1_g, w_in,
           gmlp_v_g, gmlp_ws, gmlp_b, q_a_g, w_qb, kv_a_g, w_kvb, q_norm_g, k_norm_g, w_out, ln2_g, router_w,
           router_bias, exp_w_gate, exp_w_up, exp_w_down, sh_w_gate, sh_w_up, sh_w_down):
    p = dict(w_in=w_in, gmlp_v_g=gmlp_v_g, gmlp_ws=gmlp_ws, gmlp_b=gmlp_b, q_a_g=q_a_g, w_qb=w_qb,
             kv_a_g=kv_a_g, w_kvb=w_kvb, q_norm_g=q_norm_g, k_norm_g=k_norm_g, w_out=w_out, ln1_g=ln1_g,
             ln2_g=ln2_g, router_w=router_w, router_bias=router_bias, exp_w_gate=exp_w_gate,
             exp_w_up=exp_w_up, exp_w_down=exp_w_down, sh_w_gate=sh_w_gate, sh_w_up=sh_w_up,
             sh_w_down=sh_w_down)
    batch, seq, _ = x_prompt.shape
    n_seq, s_len, _ = x_sample.shape
    depth, _, past, _ = cache_kv_latent.shape
    assert batch == 1 and s_len == SUB and n_seq == N_SUB and n_seq * s_len == TILE
    assert seq % TILE == 0 and seq % ATT_TQ == 0 and past % GMLP_CHUNK == 0 and past % CHUNK == 0
    rp = seq
    rs = n_seq * s_len

    pos = jnp.concatenate([jnp.arange(seq, dtype=jnp.int32),
                           jnp.tile(past + jnp.arange(s_len, dtype=jnp.int32), n_seq)])
    cos, sin = _rope_tables(pos)

    c_all = jnp.concatenate([jnp.broadcast_to(c_prompt, (N_SUB, D_MODEL)), c_sample], axis=0)
    mod = _ada_call(c_all, w_ada, b_ada)
    mod = mod.reshape(depth, 2, N_SUB, 6, D_MODEL)

    weights = [_layer_weights(l, p, cos, sin) for l in range(depth)]
    lat_all = cache_kv_latent.reshape(depth, n_seq * past, KV_LORA_RANK)
    pe_all = jnp.pad(cache_k_rope.reshape(depth, n_seq * past, QK_ROPE_DIM),
                     ((0, 0), (0, 0), (0, LANES - QK_ROPE_DIM)))
    k_past, v_past = _cache_kv_call(lat_all, pe_all, jnp.stack([w["w_kvb"] for w in weights]),
                                    jnp.stack([w["gk"] for w in weights]),
                                    jnp.stack([w["off"] for w in weights]))

    xp = x_prompt.reshape(rp, D_MODEL)
    xs = x_sample.reshape(rs, D_MODEL)
    lat_p, pe_p, lat_s, pe_s, v_s = [], [], [], [], []
    for l in range(depth):
        w = weights[l]
        sh1, sc1, g1, sh2, sc2, g2 = [mod[l, :, :, j, :] for j in range(6)]
        a, q, k, v, ckv, kpe, vn = _mix_in_call(xp, xs, sc1, sh1, w)
        bp = _prompt_attn_call(w["bounded"], q, k, v, rp)
        bs = _sample_attn_call(l, q, k, v, k_past, v_past, rp, n_seq, s_len, past)
        xm, h2e, grp, rank, counts = _mix_out_call(a, bp, bs, xp, xs, g1, sc2, sh2, w)
        xp, xs = _moe(h2e, grp, rank, counts, xm, g2, w, rp)
        lat_p.append(ckv[:rp].reshape(batch, seq, KV_LORA_RANK))
        pe_p.append(kpe[:rp, :QK_ROPE_DIM].reshape(batch, seq, QK_ROPE_DIM))
        lat_s.append(ckv[rp:].reshape(n_seq, s_len, KV_LORA_RANK))
        pe_s.append(kpe[rp:, :QK_ROPE_DIM].reshape(n_seq, s_len, QK_ROPE_DIM))
        v_s.append(vn.reshape(n_seq, s_len, GMLP_WIDTH))
    return (xp.reshape(batch, seq, D_MODEL), xs.reshape(n_seq, s_len, D_MODEL),
            jnp.stack(lat_p), jnp.stack(pe_p), jnp.stack(lat_s), jnp.stack(pe_s), jnp.stack(v_s))
```

```python
import functools
import math

import jax
import jax.numpy as jnp
from jax import lax
from jax.experimental import pallas as pl
from jax.experimental.pallas import tpu as pltpu

F32 = jnp.float32
BF16 = jnp.bfloat16

D_MODEL = 1024
CHUNK = 64
GMLP_WIDTH = 512
GMLP_GROUPS = 4
GMLP_CHUNK = 128
MLA_HEADS = 4
QK_NOPE_DIM = 128
QK_ROPE_DIM = 64
QK_HEAD_DIM = 192
V_HEAD_DIM = 128
Q_LORA_RANK = 384
KV_LORA_RANK = 256
ROPE_THETA = 10000.0
N_EXPERTS = 16
N_GROUPS = 4
EXPERTS_PER_GROUP = 4
EXPERT_FF = 512
EPS = 1e-6

LANES = 128
TILE = 512
SUB = 64
N_SUB = TILE // SUB
QK_PAD = 256
IN_COLS = 2 * GMLP_WIDTH + Q_LORA_RANK + KV_LORA_RANK + 2 * LANES
Q_COLS = MLA_HEADS * 3 * LANES
H2E_COLS = D_MODEL + LANES
ROW_DMA_UNROLL = 8
ATT_TK = 512
ATT_TQ = 2 * ATT_TK
ATT_UNROLL = 4
VMEM_LIMIT = 56 * 1024 * 1024
LOG2E = 1.4426950408889634
SCORE_BOUND_MARGIN = 1.02
MAX_SCORE_BOUND = 48.0


def _cparams(n_axes):
    return pltpu.CompilerParams(dimension_semantics=("arbitrary",) * n_axes,
                                vmem_limit_bytes=VMEM_LIMIT)


def _rms(x, eps=EPS):
    return x * lax.rsqrt(jnp.mean(x * x, axis=-1, keepdims=True) + eps)


def _gelu(x):
    c = math.sqrt(2.0 / math.pi)
    return 0.5 * x * (1.0 + jnp.tanh(c * (x + 0.044715 * (x * x * x))))


def _modulate(h, scale, shift):
    h3 = h.reshape(N_SUB, SUB, h.shape[-1])
    h3 = h3 * (1.0 + scale[:, None, :]) + shift[:, None, :]
    return h3.reshape(h.shape)


def _gated_add(x, gate, y):
    y3 = y.reshape(N_SUB, SUB, y.shape[-1]) * gate[:, None, :]
    return x + y3.reshape(y.shape)


def _ada_kernel(c_ref, w_ref, b_ref, o_ref):
    c = c_ref[...]
    cs = (c / (1.0 + jnp.exp(-c))).astype(BF16)
    w = w_ref[0].astype(BF16)
    o_ref[0] = jnp.dot(cs, w, preferred_element_type=F32) + b_ref[0]


def _ada_call(c_all, w_ada, b_ada):
    depth = w_ada.shape[0]
    nblk = w_ada.shape[2] // D_MODEL
    return pl.pallas_call(
        _ada_kernel,
        grid=(depth, nblk),
        in_specs=[
            pl.BlockSpec((16, D_MODEL), lambda l, j: (0, 0)),
            pl.BlockSpec((1, D_MODEL, D_MODEL), lambda l, j: (l, 0, j)),
            pl.BlockSpec((1, 1, D_MODEL), lambda l, j: (l, 0, j)),
        ],
        out_specs=pl.BlockSpec((1, 16, D_MODEL), lambda l, j: (l, 0, j)),
        out_shape=jax.ShapeDtypeStruct((depth, 16, w_ada.shape[2]), F32),
        compiler_params=_cparams(2),
        name="ada_mod",
    )(c_all, w_ada, b_ada.reshape(depth, 1, -1))


def _mix_in_kernel(n_prompt_tiles, xp_ref, xs_ref, sc_ref, sh_ref, ln_ref, win_ref, cos_ref, sin_ref,
                   gv_ref, gw_ref, gb_ref, qag_ref, wqb_ref, kvag_ref, wkvb_ref, gq_ref, gk_ref, off_ref,
                   a_ref, q_ref, k_ref, v_ref, ckv_ref, kpe_ref, vn_ref):
    i = pl.program_id(0)
    is_sample = i >= n_prompt_tiles
    x = jnp.where(is_sample, xs_ref[...], xp_ref[...])
    h = _rms(x) * ln_ref[...]
    h = _modulate(h, sc_ref[0], sh_ref[0])
    z = jnp.dot(h.astype(BF16), win_ref[...], preferred_element_type=F32)

    u = _gelu(z[:, :GMLP_WIDTH])
    v = _gelu(z[:, GMLP_WIDTH:2 * GMLP_WIDTH])
    gv = gv_ref[...]
    vn_parts = []
    for g in range(GMLP_GROUPS):
        sl = slice(g * LANES, (g + 1) * LANES)
        vn_parts.append(_rms(v[:, sl]) * gv[:, sl])

    @pl.when(is_sample)
    def _():
        for g in range(GMLP_GROUPS):
            vn_ref[:, g * LANES:(g + 1) * LANES] = vn_parts[g]

    for g in range(GMLP_GROUPS):
        sl = slice(g * LANES, (g + 1) * LANES)
        vb = vn_parts[g].astype(BF16)
        wg = gw_ref[0, g]
        bg = gb_ref[0, g]
        for c in range(TILE // GMLP_CHUNK):
            rows = slice(c * GMLP_CHUNK, (c + 1) * GMLP_CHUNK)
            s = jnp.dot(wg, vb[rows], preferred_element_type=F32) + bg
            a_ref[rows, sl] = (u[rows, sl] * s).astype(BF16)

    cos = cos_ref[...]
    sin = sin_ref[...]

    o2 = 2 * GMLP_WIDTH
    o3 = o2 + Q_LORA_RANK
    o4 = o3 + KV_LORA_RANK
    ql = _rms(z[:, o2:o3]) * qag_ref[...]
    qq = jnp.dot(ql.astype(BF16), wqb_ref[...], preferred_element_type=F32)
    gq = gq_ref[...]
    for hd in range(MLA_HEADS):
        base = hd * 3 * LANES
        nope = qq[:, base:base + LANES]
        rope = qq[:, base + LANES:base + 2 * LANES] * cos + qq[:, base + 2 * LANES:base + 3 * LANES] * sin
        ss = jnp.sum(nope * nope, axis=-1, keepdims=True) + jnp.sum(rope * rope, axis=-1, keepdims=True)
        rinv = lax.rsqrt(ss * (1.0 / QK_HEAD_DIM) + EPS)
        q_ref[hd, :, :LANES] = (nope * rinv * gq[:, :LANES]).astype(BF16)
        q_ref[hd, :, LANES:] = (rope * rinv * gq[:, LANES:] + off_ref[0:1, :]).astype(BF16)

    ckv = _rms(z[:, o3:o4]) * kvag_ref[...]
    ckv_ref[...] = ckv
    kpe = z[:, o4:o4 + LANES] * cos + z[:, o4 + LANES:o4 + 2 * LANES] * sin
    kpe_ref[...] = kpe
    kv = jnp.dot(ckv.astype(BF16), wkvb_ref[...], preferred_element_type=F32)
    gk = gk_ref[...]
    kpe_ss = jnp.sum(kpe * kpe, axis=-1, keepdims=True)
    for hd in range(MLA_HEADS):
        nope = kv[:, hd * LANES:(hd + 1) * LANES]
        ss = jnp.sum(nope * nope, axis=-1, keepdims=True) + kpe_ss
        rinv = lax.rsqrt(ss * (1.0 / QK_HEAD_DIM) + EPS)
        k_ref[hd, :, :LANES] = (nope * rinv * gk[:, :LANES]).astype(BF16)
        k_ref[hd, :, LANES:] = (kpe * rinv * gk[:, LANES:] + off_ref[1:2, :]).astype(BF16)
        v_ref[hd] = kv[:, (MLA_HEADS + hd) * LANES:(MLA_HEADS + hd + 1) * LANES].astype(BF16)


def _mix_in_call(xp, xs, sc1, sh1, w):
    rp, rs = xp.shape[0], xs.shape[0]
    npt = rp // TILE
    nt = npt + rs // TILE
    r = rp + rs
    last = npt - 1

    def full(a):
        nd = a.ndim
        return pl.BlockSpec(a.shape, lambda i: (0,) * nd)

    def variant(a):
        nd = a.ndim
        return pl.BlockSpec((1,) + a.shape[1:], lambda i: (i // npt,) + (0,) * (nd - 1))

    row = lambda width: pl.BlockSpec((TILE, width), lambda i: (i, 0))
    head = lambda width: pl.BlockSpec((MLA_HEADS, TILE, width), lambda i: (0, i, 0))
    in_specs = [
        pl.BlockSpec((TILE, D_MODEL), lambda i: (jnp.minimum(i, last), 0)),
        pl.BlockSpec((TILE, D_MODEL), lambda i: (0, 0)),
        variant(sc1), variant(sh1), full(w["ln1"]), full(w["w_in"]),
        row(LANES), row(LANES),
        full(w["gv"]), variant(w["gw"]), variant(w["gb"]),
        full(w["q_a_g"]), full(w["w_qb"]), full(w["kv_a_g"]), full(w["w_kvb"]),
        full(w["gq"]), full(w["gk"]), full(w["off"]),
    ]
    out_specs = [
        row(GMLP_WIDTH), head(QK_PAD), head(QK_PAD), head(V_HEAD_DIM),
        row(KV_LORA_RANK), row(LANES),
        pl.BlockSpec((TILE, GMLP_WIDTH), lambda i: (0, 0)),
    ]
    out_shape = [
        jax.ShapeDtypeStruct((r, GMLP_WIDTH), BF16),
        jax.ShapeDtypeStruct((MLA_HEADS, r, QK_PAD), BF16),
        jax.ShapeDtypeStruct((MLA_HEADS, r, QK_PAD), BF16),
        jax.ShapeDtypeStruct((MLA_HEADS, r, V_HEAD_DIM), BF16),
        jax.ShapeDtypeStruct((r, KV_LORA_RANK), F32),
        jax.ShapeDtypeStruct((r, LANES), F32),
        jax.ShapeDtypeStruct((rs, GMLP_WIDTH), F32),
    ]
    return pl.pallas_call(
        functools.partial(_mix_in_kernel, npt),
        grid=(nt,), in_specs=in_specs, out_specs=out_specs, out_shape=out_shape,
        compiler_params=_cparams(1), name="mix_in",
    )(xp, xs, sc1, sh1, w["ln1"], w["w_in"], w["cos"], w["sin"], w["gv"], w["gw"], w["gb"],
      w["q_a_g"], w["w_qb"], w["kv_a_g"], w["w_kvb"], w["gq"], w["gk"], w["off"])


def _cache_kv_kernel(lat_ref, pe_ref, wkvb_ref, gk_ref, off_ref, k_ref, v_ref):
    lat = lat_ref[0]
    kpe = pe_ref[0]
    kv = jnp.dot(lat.astype(BF16), wkvb_ref[0], preferred_element_type=F32)
    gk = gk_ref[0]
    kpe_ss = jnp.sum(kpe * kpe, axis=-1, keepdims=True)
    for hd in range(MLA_HEADS):
        nope = kv[:, hd * LANES:(hd + 1) * LANES]
        ss = jnp.sum(nope * nope, axis=-1, keepdims=True) + kpe_ss
        rinv = lax.rsqrt(ss * (1.0 / QK_HEAD_DIM) + EPS)
        k_ref[0, hd, :, :LANES] = (nope * rinv * gk[:, :LANES]).astype(BF16)
        k_ref[0, hd, :, LANES:] = (kpe * rinv * gk[:, LANES:] + off_ref[0, 1:2, :]).astype(BF16)
        v_ref[0, hd] = kv[:, (MLA_HEADS + hd) * LANES:(MLA_HEADS + hd + 1) * LANES].astype(BF16)


def _cache_kv_call(lat, pe, w_kvb, gk, off):
    depth, rows, _ = lat.shape
    t = 1024
    return pl.pallas_call(
        _cache_kv_kernel,
        grid=(depth, rows // t),
        in_specs=[
            pl.BlockSpec((1, t, KV_LORA_RANK), lambda l, i: (l, i, 0)),
            pl.BlockSpec((1, t, LANES), lambda l, i: (l, i, 0)),
            pl.BlockSpec((1,) + w_kvb.shape[1:], lambda l, i: (l, 0, 0)),
            pl.BlockSpec((1, 1, QK_PAD), lambda l, i: (l, 0, 0)),
            pl.BlockSpec((1, 2, LANES), lambda l, i: (l, 0, 0)),
        ],
        out_specs=[
            pl.BlockSpec((1, MLA_HEADS, t, QK_PAD), lambda l, i: (l, 0, i, 0)),
            pl.BlockSpec((1, MLA_HEADS, t, V_HEAD_DIM), lambda l, i: (l, 0, i, 0)),
        ],
        out_shape=[
            jax.ShapeDtypeStruct((depth, MLA_HEADS, rows, QK_PAD), BF16),
            jax.ShapeDtypeStruct((depth, MLA_HEADS, rows, V_HEAD_DIM), BF16),
        ],
        compiler_params=_cparams(2), name="cache_kv",
    )(lat, pe, w_kvb, gk, off)


def _attn_step(q, k, v, m, l, acc, mask):
    s = lax.dot_general(q, k, (((1,), (1,)), ((), ())), preferred_element_type=F32)
    if mask is not None:
        s = jnp.where(mask, s, -1e30)
    m_new = jnp.maximum(m, jnp.max(s, axis=-1, keepdims=True))
    alpha = jnp.exp2(m - m_new)
    p = jnp.exp2(s - m_new)
    l = alpha * l + jnp.sum(p, axis=-1, keepdims=True)
    acc = alpha * acc + jnp.dot(p.astype(BF16), v, preferred_element_type=F32)
    return m_new, l, acc


def _bounded_step(q, k, v_aug, acc, mask):
    s = lax.dot_general(q, k, (((1,), (1,)), ((), ())), preferred_element_type=F32)
    if mask is not None:
        s = jnp.where(mask, s, -1e30)
    return acc + jnp.dot(jnp.exp2(s).astype(BF16), v_aug, preferred_element_type=F32)


def _prompt_attn_kernel(bounded_ref, q_ref, k_ref, v_ref, o_ref):
    qi = pl.program_id(1)
    qa = q_ref[0, :ATT_TK]
    qb = q_ref[0, ATT_TK:]
    row = lax.broadcasted_iota(jnp.int32, (ATT_TK, ATT_TK), 0)
    col = lax.broadcasted_iota(jnp.int32, (ATT_TK, ATT_TK), 1)
    mask = (col // CHUNK) <= (row // CHUNK)

    def kv(j):
        off = pl.multiple_of(j * ATT_TK, ATT_TK)
        return k_ref[0, pl.ds(off, ATT_TK), :], v_ref[0, pl.ds(off, ATT_TK), :]

    @pl.when(bounded_ref[0] > 0)
    def _():
        lane = lax.broadcasted_iota(jnp.int32, (ATT_TK, LANES), 1)
        ones_col = jnp.where(lane == 0, 1.0, 0.0).astype(BF16)

        def kv_aug(j):
            k, v = kv(j)
            return k, jnp.concatenate([v, ones_col], axis=1)

        def blocks(first, count, carry):
            acc_a, acc_b = carry
            for u in range(count):
                k, v = kv_aug(first + u)
                acc_a = _bounded_step(qa, k, v, acc_a, None)
                acc_b = _bounded_step(qb, k, v, acc_b, None)
            return acc_a, acc_b

        zero = jnp.zeros((ATT_TK, 2 * LANES), F32)
        n_trips = (2 * qi) // ATT_UNROLL
        carry = lax.fori_loop(0, n_trips, lambda j, c: blocks(ATT_UNROLL * j, ATT_UNROLL, c), (zero, zero))
        acc_a, acc_b = lax.cond(qi % 2 == 1, lambda c: blocks(ATT_UNROLL * n_trips, 2, c), lambda c: c, carry)
        k, v = kv_aug(2 * qi)
        acc_a = _bounded_step(qa, k, v, acc_a, mask)
        acc_b = _bounded_step(qb, k, v, acc_b, None)
        k, v = kv_aug(2 * qi + 1)
        acc_b = _bounded_step(qb, k, v, acc_b, mask)
        o_ref[:ATT_TK, :] = (acc_a[:, :LANES] / acc_a[:, LANES:LANES + 1]).astype(BF16)
        o_ref[ATT_TK:, :] = (acc_b[:, :LANES] / acc_b[:, LANES:LANES + 1]).astype(BF16)

    @pl.when(bounded_ref[0] == 0)
    def _():
        def init():
            return (jnp.full((ATT_TK, 1), -1e30, F32), jnp.zeros((ATT_TK, 1), F32),
                    jnp.zeros((ATT_TK, V_HEAD_DIM), F32))

        def body(j, carry):
            k, v = kv(j)
            return _attn_step(qa, k, v, *carry[0], None), _attn_step(qb, k, v, *carry[1], None)

        sa, sb = lax.fori_loop(0, 2 * qi, body, (init(), init()))
        k, v = kv(2 * qi)
        sa = _attn_step(qa, k, v, *sa, mask)
        sb = _attn_step(qb, k, v, *sb, None)
        k, v = kv(2 * qi + 1)
        sb = _attn_step(qb, k, v, *sb, mask)
        o_ref[:ATT_TK, :] = (sa[2] / sa[1]).astype(BF16)
        o_ref[ATT_TK:, :] = (sb[2] / sb[1]).astype(BF16)


def _prompt_attn_call(bounded, q, k, v, rp):
    r = q.shape[1]
    return pl.pallas_call(
        _prompt_attn_kernel,
        grid_spec=pltpu.PrefetchScalarGridSpec(
            num_scalar_prefetch=1, grid=(MLA_HEADS, rp // ATT_TQ),
            in_specs=[
                pl.BlockSpec((1, ATT_TQ, QK_PAD), lambda h, i, b: (h, i, 0)),
                pl.BlockSpec((1, r, QK_PAD), lambda h, i, b: (h, 0, 0)),
                pl.BlockSpec((1, r, V_HEAD_DIM), lambda h, i, b: (h, 0, 0)),
            ],
            out_specs=pl.BlockSpec((ATT_TQ, V_HEAD_DIM), lambda h, i, b: (i, h))),
        out_shape=jax.ShapeDtypeStruct((rp, MLA_HEADS * V_HEAD_DIM), BF16),
        compiler_params=_cparams(2), name="prompt_attn",
    )(bounded, q, k, v)


def _sample_attn_kernel(q_ref, kp_ref, vp_ref, kn_ref, vn_ref, o_ref):
    q = q_ref[0]
    nt = (((1,), (1,)), ((), ()))
    s1 = lax.dot_general(q, kp_ref[0, 0], nt, preferred_element_type=F32)
    s2 = lax.dot_general(q, kn_ref[0], nt, preferred_element_type=F32)
    m = jnp.maximum(jnp.max(s1, axis=-1, keepdims=True), jnp.max(s2, axis=-1, keepdims=True))
    p1 = jnp.exp2(s1 - m)
    p2 = jnp.exp2(s2 - m)
    l = jnp.sum(p1, axis=-1, keepdims=True) + jnp.sum(p2, axis=-1, keepdims=True)
    o = (jnp.dot(p1.astype(BF16), vp_ref[0, 0], preferred_element_type=F32)
         + jnp.dot(p2.astype(BF16), vn_ref[0], preferred_element_type=F32))
    o_ref[...] = (o / l).astype(BF16)


def _sample_attn_call(layer, q, k, v, k_past, v_past, rp, n_seq, s_len, past):
    first = rp // s_len
    return pl.pallas_call(
        _sample_attn_kernel,
        grid=(n_seq, MLA_HEADS),
        in_specs=[
            pl.BlockSpec((1, s_len, QK_PAD), lambda b, h: (h, first + b, 0)),
            pl.BlockSpec((1, 1, past, QK_PAD), lambda b, h: (layer, h, b, 0)),
            pl.BlockSpec((1, 1, past, V_HEAD_DIM), lambda b, h: (layer, h, b, 0)),
            pl.BlockSpec((1, s_len, QK_PAD), lambda b, h: (h, first + b, 0)),
            pl.BlockSpec((1, s_len, V_HEAD_DIM), lambda b, h: (h, first + b, 0)),
        ],
        out_specs=pl.BlockSpec((s_len, V_HEAD_DIM), lambda b, h: (b, h)),
        out_shape=jax.ShapeDtypeStruct((n_seq * s_len, MLA_HEADS * V_HEAD_DIM), BF16),
        compiler_params=_cparams(2), name="sample_attn",
    )(q, k_past, v_past, k, v)


def _route(logits_t, bias_col):
    scores = 1.0 / (1.0 + jnp.exp(-logits_t))
    biased = scores + bias_col
    s_rows = [scores[e:e + 1, :] for e in range(N_EXPERTS)]
    b_rows = [biased[e:e + 1, :] for e in range(N_EXPERTS)]
    group_score = []
    for g in range(N_GROUPS):
        rows = b_rows[g * EXPERTS_PER_GROUP:(g + 1) * EXPERTS_PER_GROUP]
        best = None
        for a in range(EXPERTS_PER_GROUP):
            for b in range(a + 1, EXPERTS_PER_GROUP):
                pair = rows[a] + rows[b]
                best = pair if best is None else jnp.maximum(best, pair)
        group_score.append(best)
    best_group = jnp.zeros_like(group_score[0], dtype=jnp.int32)
    best_val = group_score[0]
    for g in range(1, N_GROUPS):
        better = group_score[g] > best_val
        best_group = jnp.where(better, g, best_group)
        best_val = jnp.where(better, group_score[g], best_val)
    selected = []
    for e in range(N_EXPERTS):
        g = e // EXPERTS_PER_GROUP
        rank = jnp.zeros_like(best_group)
        for j in range(g * EXPERTS_PER_GROUP, (g + 1) * EXPERTS_PER_GROUP):
            if j == e:
                continue
            ahead = b_rows[j] > b_rows[e]
            if j < e:
                ahead = ahead | (b_rows[j] == b_rows[e])
            rank = rank + ahead.astype(jnp.int32)
        selected.append((best_group == g) & (rank < 2))
    denom = jnp.zeros_like(s_rows[0])
    for e in range(N_EXPERTS):
        denom = denom + jnp.where(selected[e], s_rows[e], 0.0)
    return [jnp.where(selected[e], s_rows[e] / denom, 0.0) for e in range(N_EXPERTS)], best_group


def _mix_out_kernel(n_prompt_tiles, a_ref, bp_ref, bs_ref, xp_ref, xs_ref, wa_ref, wb_ref, g1_ref, sc_ref,
                    sh_ref, ln_ref, rw_ref, rb_ref, xm_ref, h2e_ref, grp_ref, rank_ref, cnt_ref,
                    gt_ref, carry_ref):
    i = pl.program_id(0)
    is_sample = i >= n_prompt_tiles
    x = jnp.where(is_sample, xs_ref[...], xp_ref[...])
    b = jnp.where(is_sample, bs_ref[...], bp_ref[...])
    mix = (jnp.dot(a_ref[...], wa_ref[...], preferred_element_type=F32)
           + jnp.dot(b, wb_ref[...], preferred_element_type=F32))
    xm = _gated_add(x, g1_ref[0], mix)
    xm_ref[...] = xm
    h2 = _modulate(_rms(xm) * ln_ref[...], sc_ref[0], sh_ref[0])
    h2e_ref[:, :D_MODEL] = h2
    h2_hi = h2.astype(BF16)
    h2_lo = (h2 - h2_hi.astype(F32)).astype(BF16)
    by_hi = jnp.dot(h2_hi, rw_ref[...], preferred_element_type=F32)
    logits = (by_hi[:, :LANES] + by_hi[:, LANES:]
              + jnp.dot(h2_lo, rw_ref[:, :LANES], preferred_element_type=F32))
    gate_rows, best_group = _route(logits.T[:N_EXPERTS, :], rb_ref[...])
    gt_ref[...] = jnp.zeros_like(gt_ref)
    for e in range(N_EXPERTS):
        gt_ref[e:e + 1, :] = gate_rows[e]
    h2e_ref[:, D_MODEL:] = gt_ref[...].T

    @pl.when(i == 0)
    def _():
        carry_ref[...] = jnp.zeros_like(carry_ref)

    sub = lax.broadcasted_iota(jnp.int32, (8, TILE), 0)
    onehot = jnp.where(sub == best_group, 1.0, 0.0)
    r_idx = lax.broadcasted_iota(jnp.int32, (TILE, TILE), 0)
    c_idx = lax.broadcasted_iota(jnp.int32, (TILE, TILE), 1)
    upper = jnp.where(r_idx <= c_idx, 1.0, 0.0).astype(BF16)
    cum = jnp.dot(onehot.astype(BF16), upper, preferred_element_type=F32)
    carry = carry_ref[...]
    rank = jnp.sum(onehot * (cum - 1.0 + carry[:, :1]), axis=0, keepdims=True)
    grp_ref[0] = best_group
    rank_ref[0] = rank.astype(jnp.int32)
    carry = carry + jnp.sum(onehot, axis=1, keepdims=True)
    carry_ref[...] = carry
    cnt_ref[...] = carry


def _mix_out_call(a, bp, bs, xp, xs, g1, sc2, sh2, w):
    rp, rs = xp.shape[0], xs.shape[0]
    npt = rp // TILE
    nt = npt + rs // TILE
    r = rp + rs
    last = npt - 1

    def full(arr):
        nd = arr.ndim
        return pl.BlockSpec(arr.shape, lambda i: (0,) * nd)

    def variant(arr):
        nd = arr.ndim
        return pl.BlockSpec((1,) + arr.shape[1:], lambda i: (i // npt,) + (0,) * (nd - 1))

    row = lambda width: pl.BlockSpec((TILE, width), lambda i: (i, 0))
    prow = lambda width: pl.BlockSpec((TILE, width), lambda i: (jnp.minimum(i, last), 0))
    srow = lambda width: pl.BlockSpec((TILE, width), lambda i: (0, 0))
    return pl.pallas_call(
        functools.partial(_mix_out_kernel, npt),
        grid=(nt,),
        in_specs=[row(GMLP_WIDTH), prow(GMLP_WIDTH), srow(GMLP_WIDTH), prow(D_MODEL), srow(D_MODEL),
                  full(w["w_out_a"]), full(w["w_out_b"]), variant(g1), variant(sc2), variant(sh2),
                  full(w["ln2"]), full(w["rw"]), full(w["rb"])],
        out_specs=[row(D_MODEL), row(H2E_COLS),
                   pl.BlockSpec((1, 1, TILE), lambda i: (i, 0, 0)),
                   pl.BlockSpec((1, 1, TILE), lambda i: (i, 0, 0)),
                   pl.BlockSpec((8, LANES), lambda i: (0, 0))],
        out_shape=[jax.ShapeDtypeStruct((r, D_MODEL), F32),
                   jax.ShapeDtypeStruct((r, H2E_COLS), F32),
                   jax.ShapeDtypeStruct((nt, 1, TILE), jnp.int32),
                   jax.ShapeDtypeStruct((nt, 1, TILE), jnp.int32),
                   jax.ShapeDtypeStruct((8, LANES), F32)],
        scratch_shapes=[pltpu.VMEM((LANES, TILE), F32), pltpu.VMEM((8, LANES), F32)],
        compiler_params=_cparams(1), name="mix_out",
    )(a, bp, bs, xp, xs, w["w_out_a"], w["w_out_b"], g1, sc2, sh2, w["ln2"], w["rw"], w["rb"])


def _silu(x):
    return x / (1.0 + jnp.exp(-x))


def _invert_kernel(pos_ref, src_ref):
    def clear(s, carry):
        src_ref[s] = 0
        return carry

    lax.fori_loop(0, src_ref.shape[0], clear, 0, unroll=ROW_DMA_UNROLL)

    def place(i, carry):
        src_ref[pos_ref[i]] = i
        return carry

    lax.fori_loop(0, pos_ref.shape[0], place, 0, unroll=ROW_DMA_UNROLL)


def _invert_call(pos, n_slots):
    return pl.pallas_call(
        _invert_kernel,
        in_specs=[pl.BlockSpec(memory_space=pltpu.SMEM)],
        out_specs=pl.BlockSpec(memory_space=pltpu.SMEM),
        out_shape=jax.ShapeDtypeStruct((n_slots,), jnp.int32),
        name="moe_invert",
    )(pos)


def _gather_rows(idx_ref, base, src_ref, dst_ref, sem):
    for r in range(TILE):
        pltpu.make_async_copy(src_ref.at[pl.ds(idx_ref[base + r], 1)], dst_ref.at[pl.ds(r, 1)], sem).start()


def _wait_rows(src_ref, dst_ref, sem):
    pltpu.make_async_copy(src_ref.at[pl.ds(0, TILE)], dst_ref, sem).wait()


def _routed_kernel(grp_ref, on_ref, src_ref, h_ref, wg_ref, wu_ref, wd_ref, o_ref, buf_ref, sem):
    t = pl.program_id(0)
    n = pl.num_programs(0)
    slot = t % 2
    on = on_ref[t] > 0

    @pl.when((t == 0) & on)
    def _():
        _gather_rows(src_ref, 0, h_ref, buf_ref.at[0], sem.at[0])

    @pl.when((on_ref[jnp.minimum(t + 1, n - 1)] > 0) & (t + 1 < n))
    def _():
        _gather_rows(src_ref, (t + 1) * TILE, h_ref, buf_ref.at[1 - slot], sem.at[1 - slot])

    @pl.when(jnp.logical_not(on))
    def _():
        o_ref[...] = jnp.zeros_like(o_ref)

    @pl.when(on)
    def _():
        _wait_rows(h_ref, buf_ref.at[slot], sem.at[slot])
        h = buf_ref[slot, :, :D_MODEL].astype(BF16)
        gates = buf_ref[slot, :, D_MODEL:]
        lane = lax.broadcasted_iota(jnp.int32, (TILE, LANES), 1)
        first = grp_ref[t] * EXPERTS_PER_GROUP
        acc = None
        for e in range(EXPERTS_PER_GROUP):
            hg = jnp.dot(h, wg_ref[0, e], preferred_element_type=F32)
            hu = jnp.dot(h, wu_ref[0, e], preferred_element_type=F32)
            gate = jnp.sum(jnp.where(lane == first + e, gates, 0.0), axis=-1, keepdims=True)
            out = jnp.dot((_silu(hg) * hu).astype(BF16), wd_ref[0, e], preferred_element_type=F32) * gate
            acc = out if acc is None else acc + out
        o_ref[...] = acc


def _routed_call(tile_grp, tile_on, src, h2e, wg, wu, wd):
    nts = src.shape[0] // TILE
    wspec = lambda arr: pl.BlockSpec((1,) + arr.shape[1:], lambda t, grp, on, src: (grp[t], 0, 0, 0))
    return pl.pallas_call(
        _routed_kernel,
        grid_spec=pltpu.PrefetchScalarGridSpec(
            num_scalar_prefetch=3, grid=(nts,),
            in_specs=[pl.BlockSpec(memory_space=pl.ANY), wspec(wg), wspec(wu), wspec(wd)],
            out_specs=pl.BlockSpec((TILE, D_MODEL), lambda t, grp, on, src: (t, 0)),
            scratch_shapes=[pltpu.VMEM((2, TILE, H2E_COLS), F32), pltpu.SemaphoreType.DMA((2,))]),
        out_shape=jax.ShapeDtypeStruct((nts * TILE, D_MODEL), F32),
        compiler_params=_cparams(1), name="moe_routed",
    )(tile_grp, tile_on, src, h2e, wg, wu, wd)


def _combine_kernel(n_prompt_tiles, pos_ref, h_ref, xm_ref, g2_ref, wg_ref, wu_ref, wd_ref, routed_ref,
                    yp_ref, ys_ref, buf_ref, sem):
    i = pl.program_id(0)
    n = pl.num_programs(0)
    slot = i % 2

    @pl.when(i == 0)
    def _():
        _gather_rows(pos_ref, 0, routed_ref, buf_ref.at[0], sem.at[0])

    @pl.when(i + 1 < n)
    def _():
        _gather_rows(pos_ref, (i + 1) * TILE, routed_ref, buf_ref.at[1 - slot], sem.at[1 - slot])

    h = h_ref[:, :D_MODEL].astype(BF16)
    act = _silu(jnp.dot(h, wg_ref[...], preferred_element_type=F32)) * jnp.dot(
        h, wu_ref[...], preferred_element_type=F32)
    shared = jnp.dot(act.astype(BF16), wd_ref[...], preferred_element_type=F32)
    _wait_rows(routed_ref, buf_ref.at[slot], sem.at[slot])
    y = _gated_add(xm_ref[...], g2_ref[0], shared + buf_ref[slot])

    @pl.when(i < n_prompt_tiles)
    def _():
        yp_ref[...] = y

    @pl.when(i >= n_prompt_tiles)
    def _():
        ys_ref[...] = y


def _combine_call(pos, h2e, xm, g2, wg, wu, wd, routed, rp):
    r = h2e.shape[0]
    rs = r - rp
    npt = rp // TILE
    last = npt - 1
    full = lambda arr: pl.BlockSpec(arr.shape, lambda i, pos: (0,) * arr.ndim)
    return pl.pallas_call(
        functools.partial(_combine_kernel, npt),
        grid_spec=pltpu.PrefetchScalarGridSpec(
            num_scalar_prefetch=1, grid=(r // TILE,),
            in_specs=[pl.BlockSpec((TILE, H2E_COLS), lambda i, pos: (i, 0)),
                      pl.BlockSpec((TILE, D_MODEL), lambda i, pos: (i, 0)),
                      pl.BlockSpec((1,) + g2.shape[1:], lambda i, pos: (i // npt, 0, 0)),
                      full(wg), full(wu), full(wd),
                      pl.BlockSpec(memory_space=pl.ANY)],
            out_specs=[pl.BlockSpec((TILE, D_MODEL), lambda i, pos: (jnp.minimum(i, last), 0)),
                       pl.BlockSpec((TILE, D_MODEL), lambda i, pos: (0, 0))],
            scratch_shapes=[pltpu.VMEM((2, TILE, D_MODEL), F32), pltpu.SemaphoreType.DMA((2,))]),
        out_shape=[jax.ShapeDtypeStruct((rp, D_MODEL), F32),
                   jax.ShapeDtypeStruct((rs, D_MODEL), F32)],
        compiler_params=_cparams(1), name="moe_combine",
    )(pos, h2e, xm, g2, wg, wu, wd, routed)


def _moe(h2e, grp, rank, counts, xm, g2, w, rp):
    r = h2e.shape[0]
    n_sorted_tiles = r // TILE + N_GROUPS
    cnt = counts[:N_GROUPS, 0].astype(jnp.int32)
    tiles_g = (cnt + TILE - 1) // TILE
    end_g = jnp.cumsum(tiles_g)
    off_g = (end_g - tiles_g) * TILE
    grp_flat = grp.reshape(r)
    pos = rank.reshape(r)
    for g in range(N_GROUPS):
        pos = pos + jnp.where(grp_flat == g, off_g[g], 0)
    t_idx = jnp.arange(n_sorted_tiles, dtype=jnp.int32)
    tile_grp = jnp.zeros_like(t_idx)
    for g in range(N_GROUPS - 1):
        tile_grp = tile_grp + (t_idx >= end_g[g]).astype(jnp.int32)
    tile_on = (t_idx < end_g[N_GROUPS - 1]).astype(jnp.int32)
    src = _invert_call(pos, n_sorted_tiles * TILE)
    routed = _routed_call(tile_grp, tile_on, src, h2e, w["wg"], w["wu"], w["wd"])
    return _combine_call(pos, h2e, xm, g2, w["sg"], w["su"], w["sd"], routed, rp)


def _rot_half_cols(wcols):
    half = QK_ROPE_DIM // 2
    return jnp.concatenate([-wcols[:, half:], wcols[:, :half]], axis=1)


def _pad_cols(wcols, width):
    return jnp.pad(wcols, ((0, 0), (0, width - wcols.shape[1])))


def _split_bf16(w):
    hi = w.astype(BF16)
    lo = (w - hi.astype(F32)).astype(BF16)
    return jnp.concatenate([hi, lo], axis=1)


def _rope_tables(seq, past, s_len, n_seq):
    half = QK_ROPE_DIM // 2
    inv = ROPE_THETA ** (-jnp.arange(half, dtype=F32) / half)

    def cos_sin(pos):
        ang = pos.astype(F32)[:, None] * inv[None, :]
        return jnp.cos(ang), jnp.sin(ang)

    cc, sc = cos_sin(jnp.arange(0, seq, GMLP_CHUNK, dtype=jnp.int32))
    cf, sf = cos_sin(jnp.arange(GMLP_CHUNK, dtype=jnp.int32))
    cos_p = (cc[:, None, :] * cf[None, :, :] - sc[:, None, :] * sf[None, :, :]).reshape(seq, half)
    sin_p = (sc[:, None, :] * cf[None, :, :] + cc[:, None, :] * sf[None, :, :]).reshape(seq, half)
    cos_s, sin_s = cos_sin(past + jnp.arange(s_len, dtype=jnp.int32))
    cos = jnp.concatenate([cos_p, jnp.tile(cos_s, (n_seq, 1))], axis=0)
    sin = jnp.concatenate([sin_p, jnp.tile(sin_s, (n_seq, 1))], axis=0)
    zeros = jnp.zeros((cos.shape[0], LANES - QK_ROPE_DIM), F32)
    return (jnp.concatenate([cos, cos, zeros], axis=1), jnp.concatenate([sin, sin, zeros], axis=1))


def _layer_weights(l, p, cos, sin):
    o1 = GMLP_WIDTH
    o2 = 2 * GMLP_WIDTH
    o3 = o2 + Q_LORA_RANK
    o4 = o3 + KV_LORA_RANK
    w_in = p["w_in"][l]
    kpe_cols = w_in[:, o4:]
    w_in2 = jnp.concatenate([w_in[:, :o4], _pad_cols(kpe_cols, LANES),
                             _pad_cols(_rot_half_cols(kpe_cols), LANES)], axis=1).astype(BF16)
    w_qb = p["w_qb"][l]
    q_parts = []
    for hd in range(MLA_HEADS):
        base = hd * QK_HEAD_DIM
        rope_cols = w_qb[:, base + QK_NOPE_DIM:base + QK_HEAD_DIM]
        q_parts += [w_qb[:, base:base + QK_NOPE_DIM], _pad_cols(rope_cols, LANES),
                    _pad_cols(_rot_half_cols(rope_cols), LANES)]
    w_qb2 = jnp.concatenate(q_parts, axis=1).astype(BF16)
    w_kvb = p["w_kvb"][l].reshape(KV_LORA_RANK, MLA_HEADS, QK_NOPE_DIM + V_HEAD_DIM)
    w_kvb2 = jnp.concatenate([w_kvb[:, :, :QK_NOPE_DIM].reshape(KV_LORA_RANK, -1),
                              w_kvb[:, :, QK_NOPE_DIM:].reshape(KV_LORA_RANK, -1)], axis=1).astype(BF16)
    qscale = LOG2E / math.sqrt(QK_HEAD_DIM)
    gq = _pad_cols(p["q_norm_g"][l][None, :] * qscale, QK_PAD)
    gk = _pad_cols(p["k_norm_g"][l][None, :], QK_PAD)
    bound = (QK_HEAD_DIM * qscale * SCORE_BOUND_MARGIN * jnp.max(jnp.abs(p["q_norm_g"][l]))
             * jnp.max(jnp.abs(p["k_norm_g"][l])))
    bounded = bound <= MAX_SCORE_BOUND
    pad_lane = jnp.arange(LANES) == QK_ROPE_DIM
    off = jnp.stack([jnp.where(pad_lane, 1.0, 0.0),
                     jnp.where(pad_lane & bounded, -bound, 0.0)]).astype(F32)

    ws = p["gmlp_ws"][l]
    tri = jnp.tril(jnp.ones((GMLP_CHUNK, GMLP_CHUNK), dtype=bool))
    wt = jnp.where(tri[None], ws, 0.0)
    hc = GMLP_CHUNK // 2
    top = wt[:, :hc, :hc]
    zero = jnp.zeros_like(top)
    wt_s = jnp.concatenate([jnp.concatenate([top, zero], axis=2), jnp.concatenate([zero, top], axis=2)], axis=1)
    gw = jnp.stack([wt, wt_s]).astype(BF16)
    b = p["gmlp_b"][l]
    b_s = jnp.concatenate([b[:, :hc], b[:, :hc]], axis=1)
    gb = jnp.broadcast_to(jnp.stack([b, b_s])[..., None], (2, GMLP_GROUPS, GMLP_CHUNK, LANES)).astype(F32)

    w_out = p["w_out"][l].astype(BF16)
    return dict(
        wg=p["exp_w_gate"][l].astype(BF16).reshape(N_GROUPS, EXPERTS_PER_GROUP, D_MODEL, EXPERT_FF),
        wu=p["exp_w_up"][l].astype(BF16).reshape(N_GROUPS, EXPERTS_PER_GROUP, D_MODEL, EXPERT_FF),
        wd=p["exp_w_down"][l].astype(BF16).reshape(N_GROUPS, EXPERTS_PER_GROUP, EXPERT_FF, D_MODEL),
        sg=p["sh_w_gate"][l].astype(BF16),
        su=p["sh_w_up"][l].astype(BF16), sd=p["sh_w_down"][l].astype(BF16),
        ln1=p["ln1_g"][l][None, :], w_in=w_in2, cos=cos, sin=sin,
        gv=p["gmlp_v_g"][l].reshape(1, GMLP_WIDTH), gw=gw, gb=gb,
        q_a_g=p["q_a_g"][l][None, :], w_qb=w_qb2, kv_a_g=p["kv_a_g"][l][None, :], w_kvb=w_kvb2,
        gq=gq, gk=gk, w_out_a=w_out[:GMLP_WIDTH], w_out_b=w_out[GMLP_WIDTH:],
        ln2=p["ln2_g"][l][None, :], rw=_split_bf16(_pad_cols(p["router_w"], LANES)),
        rb=p["router_bias"].reshape(N_EXPERTS, 1), off=off,
        bounded=bounded.astype(jnp.int32).reshape(1),
    )


def kernel(x_prompt, x_sample, cache_kv_latent, cache_k_rope, c_prompt, c_sample, w_ada, b_ada, ln1_g, w_in,
           gmlp_v_g, gmlp_ws, gmlp_b, q_a_g, w_qb, kv_a_g, w_kvb, q_norm_g, k_norm_g, w_out, ln2_g, router_w,
           router_bias, exp_w_gate, exp_w_up, exp_w_down, sh_w_gate, sh_w_up, sh_w_down):
    p = dict(w_in=w_in, gmlp_v_g=gmlp_v_g, gmlp_ws=gmlp_ws, gmlp_b=gmlp_b, q_a_g=q_a_g, w_qb=w_qb,
             kv_a_g=kv_a_g, w_kvb=w_kvb, q_norm_g=q_norm_g, k_norm_g=k_norm_g, w_out=w_out, ln1_g=ln1_g,
             ln2_g=ln2_g, router_w=router_w, router_bias=router_bias, exp_w_gate=exp_w_gate,
             exp_w_up=exp_w_up, exp_w_down=exp_w_down, sh_w_gate=sh_w_gate, sh_w_up=sh_w_up,
             sh_w_down=sh_w_down)
    batch, seq, _ = x_prompt.shape
    n_seq, s_len, _ = x_sample.shape
    depth, _, past, _ = cache_kv_latent.shape
    assert batch == 1 and s_len == SUB and n_seq == N_SUB and n_seq * s_len == TILE
    assert seq % TILE == 0 and seq % ATT_TQ == 0 and past % GMLP_CHUNK == 0 and past % CHUNK == 0
    rp = seq
    rs = n_seq * s_len

    cos, sin = _rope_tables(seq, past, s_len, n_seq)

    c_all = jnp.concatenate([jnp.broadcast_to(c_prompt, (N_SUB, D_MODEL)), c_sample], axis=0)
    mod = _ada_call(c_all, w_ada, b_ada)
    mod = mod.reshape(depth, 2, N_SUB, 6, D_MODEL)

    weights = [_layer_weights(l, p, cos, sin) for l in range(depth)]
    lat_all = cache_kv_latent.reshape(depth, n_seq * past, KV_LORA_RANK)
    pe_all = jnp.pad(cache_k_rope.reshape(depth, n_seq * past, QK_ROPE_DIM),
                     ((0, 0), (0, 0), (0, LANES - QK_ROPE_DIM)))
    k_past, v_past = _cache_kv_call(lat_all, pe_all, jnp.stack([w["w_kvb"] for w in weights]),
                                    jnp.stack([w["gk"] for w in weights]),
                                    jnp.stack([w["off"] for w in weights]))

    xp = x_prompt.reshape(rp, D_MODEL)
    xs = x_sample.reshape(rs, D_MODEL)
    lat_p, pe_p, lat_s, pe_s, v_s = [], [], [], [], []
    for l in range(depth):
        w = weights[l]
        sh1, sc1, g1, sh2, sc2, g2 = [mod[l, :, :, j, :] for j in range(6)]
        a, q, k, v, ckv, kpe, vn = _mix_in_call(xp, xs, sc1, sh1, w)
        bp = _prompt_attn_call(w["bounded"], q, k, v, rp)
        bs = _sample_attn_call(l, q, k, v, k_past, v_past, rp, n_seq, s_len, past)
        xm, h2e, grp, rank, counts = _mix_out_call(a, bp, bs, xp, xs, g1, sc2, sh2, w)
        xp, xs = _moe(h2e, grp, rank, counts, xm, g2, w, rp)
        lat_p.append(ckv[:rp].reshape(batch, seq, KV_LORA_RANK))
        pe_p.append(kpe[:rp, :QK_ROPE_DIM].reshape(batch, seq, QK_ROPE_DIM))
        lat_s.append(ckv[rp:].reshape(n_seq, s_len, KV_LORA_RANK))
        pe_s.append(kpe[rp:, :QK_ROPE_DIM].reshape(n_seq, s_len, QK_ROPE_DIM))
        v_s.append(vn.reshape(n_seq, s_len, GMLP_WIDTH))
    return (xp.reshape(batch, seq, D_MODEL), xs.reshape(n_seq, s_len, D_MODEL),
            jnp.stack(lat_p), jnp.stack(pe_p), jnp.stack(lat_s), jnp.stack(pe_s), jnp.stack(v_s))
```

```python
import functools
import math

import jax
import jax.numpy as jnp
from jax import lax
from jax.experimental import pallas as pl
from jax.experimental.pallas import tpu as pltpu

F32 = jnp.float32
BF16 = jnp.bfloat16

D_MODEL = 1024
CHUNK = 64
GMLP_WIDTH = 512
GMLP_GROUPS = 4
GMLP_CHUNK = 128
MLA_HEADS = 4
QK_NOPE_DIM = 128
QK_ROPE_DIM = 64
QK_HEAD_DIM = 192
V_HEAD_DIM = 128
Q_LORA_RANK = 384
KV_LORA_RANK = 256
ROPE_THETA = 10000.0
N_EXPERTS = 16
N_GROUPS = 4
EXPERTS_PER_GROUP = 4
EXPERT_FF = 512
EPS = 1e-6

LANES = 128
TILE = 512
SUB = 64
N_SUB = TILE // SUB
QK_PAD = 256
IN_COLS = 2 * GMLP_WIDTH + Q_LORA_RANK + KV_LORA_RANK + 2 * LANES
Q_COLS = MLA_HEADS * 3 * LANES
H2E_COLS = D_MODEL + LANES
ROW_DMA_UNROLL = 8
ATT_TK = 512
ATT_TQ = 2 * ATT_TK
ATT_UNROLL = 4
ATT_HEADS = 2
VMEM_LIMIT = 56 * 1024 * 1024
LOG2E = 1.4426950408889634
SCORE_BOUND_MARGIN = 1.02
MAX_SCORE_BOUND = 48.0


def _cparams(n_axes):
    return pltpu.CompilerParams(dimension_semantics=("arbitrary",) * n_axes,
                                vmem_limit_bytes=VMEM_LIMIT)


def _rms(x, eps=EPS):
    return x * lax.rsqrt(jnp.mean(x * x, axis=-1, keepdims=True) + eps)


def _gelu(x):
    c = math.sqrt(2.0 / math.pi)
    return 0.5 * x * (1.0 + jnp.tanh(c * (x + 0.044715 * (x * x * x))))


def _modulate(h, scale, shift):
    h3 = h.reshape(N_SUB, SUB, h.shape[-1])
    h3 = h3 * (1.0 + scale[:, None, :]) + shift[:, None, :]
    return h3.reshape(h.shape)


def _gated_add(x, gate, y):
    y3 = y.reshape(N_SUB, SUB, y.shape[-1]) * gate[:, None, :]
    return x + y3.reshape(y.shape)


def _ada_kernel(c_ref, w_ref, b_ref, o_ref):
    c = c_ref[...]
    cs = (c / (1.0 + jnp.exp(-c))).astype(BF16)
    w = w_ref[0].astype(BF16)
    o_ref[0] = jnp.dot(cs, w, preferred_element_type=F32) + b_ref[0]


def _ada_call(c_all, w_ada, b_ada):
    depth = w_ada.shape[0]
    nblk = w_ada.shape[2] // D_MODEL
    return pl.pallas_call(
        _ada_kernel,
        grid=(depth, nblk),
        in_specs=[
            pl.BlockSpec((16, D_MODEL), lambda l, j: (0, 0)),
            pl.BlockSpec((1, D_MODEL, D_MODEL), lambda l, j: (l, 0, j)),
            pl.BlockSpec((1, 1, D_MODEL), lambda l, j: (l, 0, j)),
        ],
        out_specs=pl.BlockSpec((1, 16, D_MODEL), lambda l, j: (l, 0, j)),
        out_shape=jax.ShapeDtypeStruct((depth, 16, w_ada.shape[2]), F32),
        compiler_params=_cparams(2),
        name="ada_mod",
    )(c_all, w_ada, b_ada.reshape(depth, 1, -1))


def _mix_in_kernel(n_prompt_tiles, n_prev, xp_ref, xs_ref, sc_ref, sh_ref, ln_ref, win_ref, cos_ref, sin_ref,
                   gv_ref, gw_ref, gb_ref, qag_ref, wqb_ref, kvag_ref, wkvb_ref, gq_ref, gk_ref, off_ref,
                   *refs):
    a_ref, q_ref, k_ref, v_ref, latp_ref, pep_ref, lats_ref, pes_ref, vn_ref = refs[n_prev:]
    i = pl.program_id(0)
    is_sample = i >= n_prompt_tiles
    x = jnp.where(is_sample, xs_ref[...], xp_ref[...])
    h = _rms(x) * ln_ref[...]
    h = _modulate(h, sc_ref[0], sh_ref[0])
    z = jnp.dot(h.astype(BF16), win_ref[...], preferred_element_type=F32)

    u = _gelu(z[:, :GMLP_WIDTH])
    v = _gelu(z[:, GMLP_WIDTH:2 * GMLP_WIDTH])
    gv = gv_ref[...]
    vn_parts = []
    for g in range(GMLP_GROUPS):
        sl = slice(g * LANES, (g + 1) * LANES)
        vn_parts.append(_rms(v[:, sl]) * gv[:, sl])

    @pl.when(is_sample)
    def _():
        for g in range(GMLP_GROUPS):
            vn_ref[0, :, g * LANES:(g + 1) * LANES] = vn_parts[g]

    for g in range(GMLP_GROUPS):
        sl = slice(g * LANES, (g + 1) * LANES)
        vb = vn_parts[g].astype(BF16)
        wg = gw_ref[0, g]
        bg = gb_ref[0, g]
        for c in range(TILE // GMLP_CHUNK):
            rows = slice(c * GMLP_CHUNK, (c + 1) * GMLP_CHUNK)
            s = jnp.dot(wg, vb[rows], preferred_element_type=F32) + bg
            a_ref[rows, sl] = (u[rows, sl] * s).astype(BF16)

    cos = cos_ref[...]
    sin = sin_ref[...]

    o2 = 2 * GMLP_WIDTH
    o3 = o2 + Q_LORA_RANK
    o4 = o3 + KV_LORA_RANK
    ql = _rms(z[:, o2:o3]) * qag_ref[...]
    qq = jnp.dot(ql.astype(BF16), wqb_ref[...], preferred_element_type=F32)
    gq = gq_ref[...]
    for hd in range(MLA_HEADS):
        base = hd * 3 * LANES
        nope = qq[:, base:base + LANES]
        rope = qq[:, base + LANES:base + 2 * LANES] * cos + qq[:, base + 2 * LANES:base + 3 * LANES] * sin
        ss = jnp.sum(nope * nope, axis=-1, keepdims=True) + jnp.sum(rope * rope, axis=-1, keepdims=True)
        rinv = lax.rsqrt(ss * (1.0 / QK_HEAD_DIM) + EPS)
        q_ref[hd, :, :LANES] = (nope * rinv * gq[:, :LANES]).astype(BF16)
        q_ref[hd, :, LANES:] = (rope * rinv * gq[:, LANES:] + off_ref[0:1, :]).astype(BF16)

    ckv = _rms(z[:, o3:o4]) * kvag_ref[...]
    kpe = z[:, o4:o4 + LANES] * cos + z[:, o4 + LANES:o4 + 2 * LANES] * sin

    @pl.when(jnp.logical_not(is_sample))
    def _():
        latp_ref[0] = ckv
        pep_ref[0] = kpe[:, :QK_ROPE_DIM]

    @pl.when(is_sample)
    def _():
        lats_ref[0] = ckv
        pes_ref[0] = kpe[:, :QK_ROPE_DIM]

    kv = jnp.dot(ckv.astype(BF16), wkvb_ref[...], preferred_element_type=F32)
    gk = gk_ref[...]
    kpe_ss = jnp.sum(kpe * kpe, axis=-1, keepdims=True)
    for hd in range(MLA_HEADS):
        nope = kv[:, hd * LANES:(hd + 1) * LANES]
        ss = jnp.sum(nope * nope, axis=-1, keepdims=True) + kpe_ss
        rinv = lax.rsqrt(ss * (1.0 / QK_HEAD_DIM) + EPS)
        k_ref[hd, :, :LANES] = (nope * rinv * gk[:, :LANES]).astype(BF16)
        k_ref[hd, :, LANES:] = (kpe * rinv * gk[:, LANES:] + off_ref[1:2, :]).astype(BF16)
        v_ref[hd] = kv[:, (MLA_HEADS + hd) * LANES:(MLA_HEADS + hd + 1) * LANES].astype(BF16)


def _mix_in_call(xp, xs, sc1, sh1, w, layer, depth, prev):
    rp, rs = xp.shape[0], xs.shape[0]
    npt = rp // TILE
    nt = npt + rs // TILE
    r = rp + rs
    last = npt - 1

    def full(a):
        nd = a.ndim
        return pl.BlockSpec(a.shape, lambda i: (0,) * nd)

    def variant(a):
        nd = a.ndim
        return pl.BlockSpec((1,) + a.shape[1:], lambda i: (i // npt,) + (0,) * (nd - 1))

    row = lambda width: pl.BlockSpec((TILE, width), lambda i: (i, 0))
    head = lambda width: pl.BlockSpec((MLA_HEADS, TILE, width), lambda i: (0, i, 0))
    in_specs = [
        pl.BlockSpec((TILE, D_MODEL), lambda i: (jnp.minimum(i, last), 0)),
        pl.BlockSpec((TILE, D_MODEL), lambda i: (0, 0)),
        variant(sc1), variant(sh1), full(w["ln1"]), full(w["w_in"]),
        row(LANES), row(LANES),
        full(w["gv"]), variant(w["gw"]), variant(w["gb"]),
        full(w["q_a_g"]), full(w["w_qb"]), full(w["kv_a_g"]), full(w["w_kvb"]),
        full(w["gq"]), full(w["gk"]), full(w["off"]),
    ] + [pl.BlockSpec(memory_space=pl.ANY)] * len(prev)
    prompt_plane = lambda width: pl.BlockSpec((1, TILE, width), lambda i: (layer, jnp.minimum(i, last), 0))
    sample_plane = lambda width: pl.BlockSpec((1, TILE, width), lambda i: (layer, 0, 0))
    out_specs = [
        row(GMLP_WIDTH), head(QK_PAD), head(QK_PAD), head(V_HEAD_DIM),
        prompt_plane(KV_LORA_RANK), prompt_plane(QK_ROPE_DIM),
        sample_plane(KV_LORA_RANK), sample_plane(QK_ROPE_DIM), sample_plane(GMLP_WIDTH),
    ]
    out_shape = [
        jax.ShapeDtypeStruct((r, GMLP_WIDTH), BF16),
        jax.ShapeDtypeStruct((MLA_HEADS, r, QK_PAD), BF16),
        jax.ShapeDtypeStruct((MLA_HEADS, r, QK_PAD), BF16),
        jax.ShapeDtypeStruct((MLA_HEADS, r, V_HEAD_DIM), BF16),
        jax.ShapeDtypeStruct((depth, rp, KV_LORA_RANK), F32),
        jax.ShapeDtypeStruct((depth, rp, QK_ROPE_DIM), F32),
        jax.ShapeDtypeStruct((depth, rs, KV_LORA_RANK), F32),
        jax.ShapeDtypeStruct((depth, rs, QK_ROPE_DIM), F32),
        jax.ShapeDtypeStruct((depth, rs, GMLP_WIDTH), F32),
    ]
    n_fixed = len(in_specs) - len(prev)
    n_own = 4
    return pl.pallas_call(
        functools.partial(_mix_in_kernel, npt, len(prev)),
        grid=(nt,), in_specs=in_specs, out_specs=out_specs, out_shape=out_shape,
        input_output_aliases={n_fixed + j: n_own + j for j in range(len(prev))},
        compiler_params=_cparams(1), name="mix_in",
    )(xp, xs, sc1, sh1, w["ln1"], w["w_in"], w["cos"], w["sin"], w["gv"], w["gw"], w["gb"],
      w["q_a_g"], w["w_qb"], w["kv_a_g"], w["w_kvb"], w["gq"], w["gk"], w["off"], *prev)


def _cache_kv_kernel(lat_ref, pe_ref, wkvb_ref, gk_ref, off_ref, k_ref, v_ref):
    lat = lat_ref[0]
    pe = pe_ref[0]
    kpe = jnp.concatenate([pe, jnp.zeros_like(pe)], axis=1)
    kv = jnp.dot(lat.astype(BF16), wkvb_ref[0], preferred_element_type=F32)
    gk = gk_ref[0]
    kpe_ss = jnp.sum(kpe * kpe, axis=-1, keepdims=True)
    for hd in range(MLA_HEADS):
        nope = kv[:, hd * LANES:(hd + 1) * LANES]
        ss = jnp.sum(nope * nope, axis=-1, keepdims=True) + kpe_ss
        rinv = lax.rsqrt(ss * (1.0 / QK_HEAD_DIM) + EPS)
        k_ref[0, hd, :, :LANES] = (nope * rinv * gk[:, :LANES]).astype(BF16)
        k_ref[0, hd, :, LANES:] = (kpe * rinv * gk[:, LANES:] + off_ref[0, 1:2, :]).astype(BF16)
        v_ref[0, hd] = kv[:, (MLA_HEADS + hd) * LANES:(MLA_HEADS + hd + 1) * LANES].astype(BF16)


def _cache_kv_call(lat, pe, w_kvb, gk, off):
    depth, rows, _ = lat.shape
    t = 1024
    return pl.pallas_call(
        _cache_kv_kernel,
        grid=(depth, rows // t),
        in_specs=[
            pl.BlockSpec((1, t, KV_LORA_RANK), lambda l, i: (l, i, 0)),
            pl.BlockSpec((1, t, QK_ROPE_DIM), lambda l, i: (l, i, 0)),
            pl.BlockSpec((1,) + w_kvb.shape[1:], lambda l, i: (l, 0, 0)),
            pl.BlockSpec((1, 1, QK_PAD), lambda l, i: (l, 0, 0)),
            pl.BlockSpec((1, 2, LANES), lambda l, i: (l, 0, 0)),
        ],
        out_specs=[
            pl.BlockSpec((1, MLA_HEADS, t, QK_PAD), lambda l, i: (l, 0, i, 0)),
            pl.BlockSpec((1, MLA_HEADS, t, V_HEAD_DIM), lambda l, i: (l, 0, i, 0)),
        ],
        out_shape=[
            jax.ShapeDtypeStruct((depth, MLA_HEADS, rows, QK_PAD), BF16),
            jax.ShapeDtypeStruct((depth, MLA_HEADS, rows, V_HEAD_DIM), BF16),
        ],
        compiler_params=_cparams(2), name="cache_kv",
    )(lat, pe, w_kvb, gk, off)


def _attn_step(q, k, v, m, l, acc, mask):
    s = lax.dot_general(q, k, (((1,), (1,)), ((), ())), preferred_element_type=F32)
    if mask is not None:
        s = jnp.where(mask, s, -1e30)
    m_new = jnp.maximum(m, jnp.max(s, axis=-1, keepdims=True))
    alpha = jnp.exp2(m - m_new)
    p = jnp.exp2(s - m_new)
    l = alpha * l + jnp.sum(p, axis=-1, keepdims=True)
    acc = alpha * acc + jnp.dot(p.astype(BF16), v, preferred_element_type=F32)
    return m_new, l, acc


def _bounded_step(q, k, v_aug, acc, mask):
    s = lax.dot_general(q, k, (((1,), (1,)), ((), ())), preferred_element_type=F32)
    if mask is not None:
        s = jnp.where(mask, s, -1e30)
    return acc + jnp.dot(jnp.exp2(s).astype(BF16), v_aug, preferred_element_type=F32)


def _prompt_attn_kernel(bounded_ref, q_ref, k_ref, v_ref, o_ref):
    qi = pl.program_id(1)
    row = lax.broadcasted_iota(jnp.int32, (ATT_TK, ATT_TK), 0)
    col = lax.broadcasted_iota(jnp.int32, (ATT_TK, ATT_TK), 1)
    mask = (col // CHUNK) <= (row // CHUNK)

    def q_half(h, half):
        return q_ref[h, half * ATT_TK:(half + 1) * ATT_TK, :]

    def kv(h, j):
        off = pl.multiple_of(j * ATT_TK, ATT_TK)
        return k_ref[h, pl.ds(off, ATT_TK), :], v_ref[h, pl.ds(off, ATT_TK), :]

    def out(h, half):
        return o_ref.at[half * ATT_TK:(half + 1) * ATT_TK, h * V_HEAD_DIM:(h + 1) * V_HEAD_DIM]

    @pl.when(bounded_ref[0] > 0)
    def _():
        lane = lax.broadcasted_iota(jnp.int32, (ATT_TK, LANES), 1)
        ones_col = jnp.where(lane == 0, 1.0, 0.0).astype(BF16)

        def kv_aug(h, j):
            k, v = kv(h, j)
            return k, jnp.concatenate([v, ones_col], axis=1)

        def blocks(first, count, accs):
            accs = [list(a) for a in accs]
            for u in range(count):
                for h in range(ATT_HEADS):
                    k, v = kv_aug(h, first + u)
                    for half in range(2):
                        accs[h][half] = _bounded_step(q_half(h, half), k, v, accs[h][half], None)
            return tuple(tuple(a) for a in accs)

        zero = jnp.zeros((ATT_TK, 2 * LANES), F32)
        n_trips = (2 * qi) // ATT_UNROLL
        accs = lax.fori_loop(0, n_trips, lambda j, c: blocks(ATT_UNROLL * j, ATT_UNROLL, c),
                             ((zero, zero),) * ATT_HEADS)
        accs = lax.cond(qi % 2 == 1, lambda c: blocks(ATT_UNROLL * n_trips, 2, c), lambda c: c, accs)
        for h in range(ATT_HEADS):
            acc_a, acc_b = accs[h]
            k, v = kv_aug(h, 2 * qi)
            acc_a = _bounded_step(q_half(h, 0), k, v, acc_a, mask)
            acc_b = _bounded_step(q_half(h, 1), k, v, acc_b, None)
            k, v = kv_aug(h, 2 * qi + 1)
            acc_b = _bounded_step(q_half(h, 1), k, v, acc_b, mask)
            out(h, 0)[...] = (acc_a[:, :LANES] / acc_a[:, LANES:LANES + 1]).astype(BF16)
            out(h, 1)[...] = (acc_b[:, :LANES] / acc_b[:, LANES:LANES + 1]).astype(BF16)

    @pl.when(bounded_ref[0] == 0)
    def _():
        def init():
            return (jnp.full((ATT_TK, 1), -1e30, F32), jnp.zeros((ATT_TK, 1), F32),
                    jnp.zeros((ATT_TK, V_HEAD_DIM), F32))

        for h in range(ATT_HEADS):
            qa, qb = q_half(h, 0), q_half(h, 1)

            def body(j, carry, h=h, qa=qa, qb=qb):
                k, v = kv(h, j)
                return _attn_step(qa, k, v, *carry[0], None), _attn_step(qb, k, v, *carry[1], None)

            sa, sb = lax.fori_loop(0, 2 * qi, body, (init(), init()))
            k, v = kv(h, 2 * qi)
            sa = _attn_step(qa, k, v, *sa, mask)
            sb = _attn_step(qb, k, v, *sb, None)
            k, v = kv(h, 2 * qi + 1)
            sb = _attn_step(qb, k, v, *sb, mask)
            out(h, 0)[...] = (sa[2] / sa[1]).astype(BF16)
            out(h, 1)[...] = (sb[2] / sb[1]).astype(BF16)


def _prompt_attn_call(bounded, q, k, v, rp):
    r = q.shape[1]
    resident = lambda width: pl.BlockSpec((ATT_HEADS, r, width), lambda h, i, b: (h, 0, 0),
                                          pipeline_mode=pl.Buffered(1))
    return pl.pallas_call(
        _prompt_attn_kernel,
        grid_spec=pltpu.PrefetchScalarGridSpec(
            num_scalar_prefetch=1, grid=(MLA_HEADS // ATT_HEADS, rp // ATT_TQ),
            in_specs=[
                pl.BlockSpec((ATT_HEADS, ATT_TQ, QK_PAD), lambda h, i, b: (h, i, 0)),
                resident(QK_PAD), resident(V_HEAD_DIM),
            ],
            out_specs=pl.BlockSpec((ATT_TQ, ATT_HEADS * V_HEAD_DIM), lambda h, i, b: (i, h))),
        out_shape=jax.ShapeDtypeStruct((rp, MLA_HEADS * V_HEAD_DIM), BF16),
        compiler_params=_cparams(2), name="prompt_attn",
    )(bounded, q, k, v)


def _sample_attn_kernel(q_ref, kp_ref, vp_ref, kn_ref, vn_ref, o_ref):
    q = q_ref[0]
    nt = (((1,), (1,)), ((), ()))
    s1 = lax.dot_general(q, kp_ref[0, 0], nt, preferred_element_type=F32)
    s2 = lax.dot_general(q, kn_ref[0], nt, preferred_element_type=F32)
    m = jnp.maximum(jnp.max(s1, axis=-1, keepdims=True), jnp.max(s2, axis=-1, keepdims=True))
    p1 = jnp.exp2(s1 - m)
    p2 = jnp.exp2(s2 - m)
    l = jnp.sum(p1, axis=-1, keepdims=True) + jnp.sum(p2, axis=-1, keepdims=True)
    o = (jnp.dot(p1.astype(BF16), vp_ref[0, 0], preferred_element_type=F32)
         + jnp.dot(p2.astype(BF16), vn_ref[0], preferred_element_type=F32))
    o_ref[...] = (o / l).astype(BF16)


def _sample_attn_call(layer, q, k, v, k_past, v_past, rp, n_seq, s_len, past):
    first = rp // s_len
    return pl.pallas_call(
        _sample_attn_kernel,
        grid=(n_seq, MLA_HEADS),
        in_specs=[
            pl.BlockSpec((1, s_len, QK_PAD), lambda b, h: (h, first + b, 0)),
            pl.BlockSpec((1, 1, past, QK_PAD), lambda b, h: (layer, h, b, 0)),
            pl.BlockSpec((1, 1, past, V_HEAD_DIM), lambda b, h: (layer, h, b, 0)),
            pl.BlockSpec((1, s_len, QK_PAD), lambda b, h: (h, first + b, 0)),
            pl.BlockSpec((1, s_len, V_HEAD_DIM), lambda b, h: (h, first + b, 0)),
        ],
        out_specs=pl.BlockSpec((s_len, V_HEAD_DIM), lambda b, h: (b, h)),
        out_shape=jax.ShapeDtypeStruct((n_seq * s_len, MLA_HEADS * V_HEAD_DIM), BF16),
        compiler_params=_cparams(2), name="sample_attn",
    )(q, k_past, v_past, k, v)


def _route(logits_t, bias_col):
    scores = 1.0 / (1.0 + jnp.exp(-logits_t))
    biased = scores + bias_col
    s_rows = [scores[e:e + 1, :] for e in range(N_EXPERTS)]
    b_rows = [biased[e:e + 1, :] for e in range(N_EXPERTS)]
    group_score = []
    for g in range(N_GROUPS):
        rows = b_rows[g * EXPERTS_PER_GROUP:(g + 1) * EXPERTS_PER_GROUP]
        best = None
        for a in range(EXPERTS_PER_GROUP):
            for b in range(a + 1, EXPERTS_PER_GROUP):
                pair = rows[a] + rows[b]
                best = pair if best is None else jnp.maximum(best, pair)
        group_score.append(best)
    best_group = jnp.zeros_like(group_score[0], dtype=jnp.int32)
    best_val = group_score[0]
    for g in range(1, N_GROUPS):
        better = group_score[g] > best_val
        best_group = jnp.where(better, g, best_group)
        best_val = jnp.where(better, group_score[g], best_val)
    selected = []
    for e in range(N_EXPERTS):
        g = e // EXPERTS_PER_GROUP
        rank = jnp.zeros_like(best_group)
        for j in range(g * EXPERTS_PER_GROUP, (g + 1) * EXPERTS_PER_GROUP):
            if j == e:
                continue
            ahead = b_rows[j] > b_rows[e]
            if j < e:
                ahead = ahead | (b_rows[j] == b_rows[e])
            rank = rank + ahead.astype(jnp.int32)
        selected.append((best_group == g) & (rank < 2))
    denom = jnp.zeros_like(s_rows[0])
    for e in range(N_EXPERTS):
        denom = denom + jnp.where(selected[e], s_rows[e], 0.0)
    return [jnp.where(selected[e], s_rows[e] / denom, 0.0) for e in range(N_EXPERTS)], best_group


def _mix_out_kernel(n_prompt_tiles, a_ref, bp_ref, bs_ref, xp_ref, xs_ref, wa_ref, wb_ref, g1_ref, sc_ref,
                    sh_ref, ln_ref, rw_ref, rb_ref, xm_ref, h2e_ref, grp_ref, rank_ref, cnt_ref,
                    gt_ref, carry_ref):
    i = pl.program_id(0)
    is_sample = i >= n_prompt_tiles
    x = jnp.where(is_sample, xs_ref[...], xp_ref[...])
    b = jnp.where(is_sample, bs_ref[...], bp_ref[...])
    mix = (jnp.dot(a_ref[...], wa_ref[...], preferred_element_type=F32)
           + jnp.dot(b, wb_ref[...], preferred_element_type=F32))
    xm = _gated_add(x, g1_ref[0], mix)
    xm_ref[...] = xm
    h2 = _modulate(_rms(xm) * ln_ref[...], sc_ref[0], sh_ref[0])
    h2e_ref[:, :D_MODEL] = h2
    h2_hi = h2.astype(BF16)
    h2_lo = (h2 - h2_hi.astype(F32)).astype(BF16)
    by_hi = jnp.dot(h2_hi, rw_ref[...], preferred_element_type=F32)
    logits = (by_hi[:, :LANES] + by_hi[:, LANES:]
              + jnp.dot(h2_lo, rw_ref[:, :LANES], preferred_element_type=F32))
    gate_rows, best_group = _route(logits.T[:N_EXPERTS, :], rb_ref[...])
    gt_ref[...] = jnp.zeros_like(gt_ref)
    for e in range(N_EXPERTS):
        gt_ref[e:e + 1, :] = gate_rows[e]
    h2e_ref[:, D_MODEL:] = gt_ref[...].T

    @pl.when(i == 0)
    def _():
        carry_ref[...] = jnp.zeros_like(carry_ref)

    sub = lax.broadcasted_iota(jnp.int32, (8, TILE), 0)
    onehot = jnp.where(sub == best_group, 1.0, 0.0)
    r_idx = lax.broadcasted_iota(jnp.int32, (TILE, TILE), 0)
    c_idx = lax.broadcasted_iota(jnp.int32, (TILE, TILE), 1)
    upper = jnp.where(r_idx <= c_idx, 1.0, 0.0).astype(BF16)
    cum = jnp.dot(onehot.astype(BF16), upper, preferred_element_type=F32)
    carry = carry_ref[...]
    rank = jnp.sum(onehot * (cum - 1.0 + carry[:, :1]), axis=0, keepdims=True)
    grp_ref[0] = best_group
    rank_ref[0] = rank.astype(jnp.int32)
    carry = carry + jnp.sum(onehot, axis=1, keepdims=True)
    carry_ref[...] = carry
    cnt_ref[...] = carry


def _mix_out_call(a, bp, bs, xp, xs, g1, sc2, sh2, w):
    rp, rs = xp.shape[0], xs.shape[0]
    npt = rp // TILE
    nt = npt + rs // TILE
    r = rp + rs
    last = npt - 1

    def full(arr):
        nd = arr.ndim
        return pl.BlockSpec(arr.shape, lambda i: (0,) * nd)

    def variant(arr):
        nd = arr.ndim
        return pl.BlockSpec((1,) + arr.shape[1:], lambda i: (i // npt,) + (0,) * (nd - 1))

    row = lambda width: pl.BlockSpec((TILE, width), lambda i: (i, 0))
    prow = lambda width: pl.BlockSpec((TILE, width), lambda i: (jnp.minimum(i, last), 0))
    srow = lambda width: pl.BlockSpec((TILE, width), lambda i: (0, 0))
    return pl.pallas_call(
        functools.partial(_mix_out_kernel, npt),
        grid=(nt,),
        in_specs=[row(GMLP_WIDTH), prow(GMLP_WIDTH), srow(GMLP_WIDTH), prow(D_MODEL), srow(D_MODEL),
                  full(w["w_out_a"]), full(w["w_out_b"]), variant(g1), variant(sc2), variant(sh2),
                  full(w["ln2"]), full(w["rw"]), full(w["rb"])],
        out_specs=[row(D_MODEL), row(H2E_COLS),
                   pl.BlockSpec((1, 1, TILE), lambda i: (i, 0, 0)),
                   pl.BlockSpec((1, 1, TILE), lambda i: (i, 0, 0)),
                   pl.BlockSpec((8, LANES), lambda i: (0, 0))],
        out_shape=[jax.ShapeDtypeStruct((r, D_MODEL), F32),
                   jax.ShapeDtypeStruct((r, H2E_COLS), F32),
                   jax.ShapeDtypeStruct((nt, 1, TILE), jnp.int32),
                   jax.ShapeDtypeStruct((nt, 1, TILE), jnp.int32),
                   jax.ShapeDtypeStruct((8, LANES), F32)],
        scratch_shapes=[pltpu.VMEM((LANES, TILE), F32), pltpu.VMEM((8, LANES), F32)],
        compiler_params=_cparams(1), name="mix_out",
    )(a, bp, bs, xp, xs, w["w_out_a"], w["w_out_b"], g1, sc2, sh2, w["ln2"], w["rw"], w["rb"])


def _silu(x):
    return x / (1.0 + jnp.exp(-x))


def _invert_kernel(pos_ref, pad_lo_ref, pad_hi_ref, src_ref):
    def clear(s, carry):
        src_ref[s] = 0
        return carry

    for g in range(N_GROUPS):
        lax.fori_loop(pad_lo_ref[g], pad_hi_ref[g], clear, 0)
    lax.fori_loop(pad_hi_ref[N_GROUPS - 1], src_ref.shape[0], clear, 0)

    def place(i, carry):
        src_ref[pos_ref[i]] = i
        return carry

    lax.fori_loop(0, pos_ref.shape[0], place, 0, unroll=ROW_DMA_UNROLL)


def _invert_call(pos, pad_lo, pad_hi, n_slots):
    return pl.pallas_call(
        _invert_kernel,
        in_specs=[pl.BlockSpec(memory_space=pltpu.SMEM)] * 3,
        out_specs=pl.BlockSpec(memory_space=pltpu.SMEM),
        out_shape=jax.ShapeDtypeStruct((n_slots,), jnp.int32),
        name="moe_invert",
    )(pos, pad_lo, pad_hi)


def _gather_rows(idx_ref, base, src_ref, dst_ref, sem):
    for r in range(TILE):
        pltpu.make_async_copy(src_ref.at[pl.ds(idx_ref[base + r], 1)], dst_ref.at[pl.ds(r, 1)], sem).start()


def _wait_rows(src_ref, dst_ref, sem):
    pltpu.make_async_copy(src_ref.at[pl.ds(0, TILE)], dst_ref, sem).wait()


def _routed_kernel(grp_ref, on_ref, src_ref, h_ref, wg_ref, wu_ref, wd_ref, o_ref, buf_ref, sem):
    t = pl.program_id(0)
    n = pl.num_programs(0)
    slot = t % 2
    on = on_ref[t] > 0

    @pl.when((t == 0) & on)
    def _():
        _gather_rows(src_ref, 0, h_ref, buf_ref.at[0], sem.at[0])

    @pl.when((on_ref[jnp.minimum(t + 1, n - 1)] > 0) & (t + 1 < n))
    def _():
        _gather_rows(src_ref, (t + 1) * TILE, h_ref, buf_ref.at[1 - slot], sem.at[1 - slot])

    @pl.when(jnp.logical_not(on))
    def _():
        o_ref[...] = jnp.zeros_like(o_ref)

    @pl.when(on)
    def _():
        _wait_rows(h_ref, buf_ref.at[slot], sem.at[slot])
        h = buf_ref[slot, :, :D_MODEL].astype(BF16)
        gates = buf_ref[slot, :, D_MODEL:]
        lane = lax.broadcasted_iota(jnp.int32, (TILE, LANES), 1)
        first = grp_ref[t] * EXPERTS_PER_GROUP
        acc = None
        for e in range(EXPERTS_PER_GROUP):
            hg = jnp.dot(h, wg_ref[0, e], preferred_element_type=F32)
            hu = jnp.dot(h, wu_ref[0, e], preferred_element_type=F32)
            gate = jnp.sum(jnp.where(lane == first + e, gates, 0.0), axis=-1, keepdims=True)
            out = jnp.dot((_silu(hg) * hu).astype(BF16), wd_ref[0, e], preferred_element_type=F32) * gate
            acc = out if acc is None else acc + out
        o_ref[...] = acc


def _routed_call(tile_grp, tile_on, src, h2e, wg, wu, wd):
    nts = src.shape[0] // TILE
    wspec = lambda arr: pl.BlockSpec((1,) + arr.shape[1:], lambda t, grp, on, src: (grp[t], 0, 0, 0))
    return pl.pallas_call(
        _routed_kernel,
        grid_spec=pltpu.PrefetchScalarGridSpec(
            num_scalar_prefetch=3, grid=(nts,),
            in_specs=[pl.BlockSpec(memory_space=pl.ANY), wspec(wg), wspec(wu), wspec(wd)],
            out_specs=pl.BlockSpec((TILE, D_MODEL), lambda t, grp, on, src: (t, 0)),
            scratch_shapes=[pltpu.VMEM((2, TILE, H2E_COLS), F32), pltpu.SemaphoreType.DMA((2,))]),
        out_shape=jax.ShapeDtypeStruct((nts * TILE, D_MODEL), F32),
        compiler_params=_cparams(1), name="moe_routed",
    )(tile_grp, tile_on, src, h2e, wg, wu, wd)


def _combine_kernel(n_prompt_tiles, pos_ref, h_ref, xm_ref, g2_ref, wg_ref, wu_ref, wd_ref, routed_ref,
                    yp_ref, ys_ref, buf_ref, sem):
    i = pl.program_id(0)
    n = pl.num_programs(0)
    slot = i % 2

    @pl.when(i == 0)
    def _():
        _gather_rows(pos_ref, 0, routed_ref, buf_ref.at[0], sem.at[0])

    @pl.when(i + 1 < n)
    def _():
        _gather_rows(pos_ref, (i + 1) * TILE, routed_ref, buf_ref.at[1 - slot], sem.at[1 - slot])

    h = h_ref[:, :D_MODEL].astype(BF16)
    act = _silu(jnp.dot(h, wg_ref[...], preferred_element_type=F32)) * jnp.dot(
        h, wu_ref[...], preferred_element_type=F32)
    shared = jnp.dot(act.astype(BF16), wd_ref[...], preferred_element_type=F32)
    _wait_rows(routed_ref, buf_ref.at[slot], sem.at[slot])
    y = _gated_add(xm_ref[...], g2_ref[0], shared + buf_ref[slot])

    @pl.when(i < n_prompt_tiles)
    def _():
        yp_ref[...] = y

    @pl.when(i >= n_prompt_tiles)
    def _():
        ys_ref[...] = y


def _combine_call(pos, h2e, xm, g2, wg, wu, wd, routed, rp):
    r = h2e.shape[0]
    rs = r - rp
    npt = rp // TILE
    last = npt - 1
    full = lambda arr: pl.BlockSpec(arr.shape, lambda i, pos: (0,) * arr.ndim)
    return pl.pallas_call(
        functools.partial(_combine_kernel, npt),
        grid_spec=pltpu.PrefetchScalarGridSpec(
            num_scalar_prefetch=1, grid=(r // TILE,),
            in_specs=[pl.BlockSpec((TILE, H2E_COLS), lambda i, pos: (i, 0)),
                      pl.BlockSpec((TILE, D_MODEL), lambda i, pos: (i, 0)),
                      pl.BlockSpec((1,) + g2.shape[1:], lambda i, pos: (i // npt, 0, 0)),
                      full(wg), full(wu), full(wd),
                      pl.BlockSpec(memory_space=pl.ANY)],
            out_specs=[pl.BlockSpec((TILE, D_MODEL), lambda i, pos: (jnp.minimum(i, last), 0)),
                       pl.BlockSpec((TILE, D_MODEL), lambda i, pos: (0, 0))],
            scratch_shapes=[pltpu.VMEM((2, TILE, D_MODEL), F32), pltpu.SemaphoreType.DMA((2,))]),
        out_shape=[jax.ShapeDtypeStruct((rp, D_MODEL), F32),
                   jax.ShapeDtypeStruct((rs, D_MODEL), F32)],
        compiler_params=_cparams(1), name="moe_combine",
    )(pos, h2e, xm, g2, wg, wu, wd, routed)


def _moe(h2e, grp, rank, counts, xm, g2, w, rp):
    r = h2e.shape[0]
    n_sorted_tiles = r // TILE + N_GROUPS
    cnt = counts[:N_GROUPS, 0].astype(jnp.int32)
    tiles_g = (cnt + TILE - 1) // TILE
    end_g = jnp.cumsum(tiles_g)
    off_g = (end_g - tiles_g) * TILE
    grp_flat = grp.reshape(r)
    pos = rank.reshape(r)
    for g in range(N_GROUPS):
        pos = pos + jnp.where(grp_flat == g, off_g[g], 0)
    t_idx = jnp.arange(n_sorted_tiles, dtype=jnp.int32)
    tile_grp = jnp.zeros_like(t_idx)
    for g in range(N_GROUPS - 1):
        tile_grp = tile_grp + (t_idx >= end_g[g]).astype(jnp.int32)
    tile_on = (t_idx < end_g[N_GROUPS - 1]).astype(jnp.int32)
    src = _invert_call(pos, off_g + cnt, end_g * TILE, n_sorted_tiles * TILE)
    routed = _routed_call(tile_grp, tile_on, src, h2e, w["wg"], w["wu"], w["wd"])
    return _combine_call(pos, h2e, xm, g2, w["sg"], w["su"], w["sd"], routed, rp)


def _rot_half_cols(wcols):
    half = QK_ROPE_DIM // 2
    return jnp.concatenate([-wcols[:, half:], wcols[:, :half]], axis=1)


def _pad_cols(wcols, width):
    return jnp.pad(wcols, ((0, 0), (0, width - wcols.shape[1])))


def _split_bf16(w):
    hi = w.astype(BF16)
    lo = (w - hi.astype(F32)).astype(BF16)
    return jnp.concatenate([hi, lo], axis=1)


def _rope_tables(seq, past, s_len, n_seq):
    half = QK_ROPE_DIM // 2
    inv = ROPE_THETA ** (-jnp.arange(half, dtype=F32) / half)

    def cos_sin(pos):
        ang = pos.astype(F32)[:, None] * inv[None, :]
        return jnp.cos(ang), jnp.sin(ang)

    cc, sc = cos_sin(jnp.arange(0, seq, GMLP_CHUNK, dtype=jnp.int32))
    cf, sf = cos_sin(jnp.arange(GMLP_CHUNK, dtype=jnp.int32))
    cos_p = (cc[:, None, :] * cf[None, :, :] - sc[:, None, :] * sf[None, :, :]).reshape(seq, half)
    sin_p = (sc[:, None, :] * cf[None, :, :] + cc[:, None, :] * sf[None, :, :]).reshape(seq, half)
    cos_s, sin_s = cos_sin(past + jnp.arange(s_len, dtype=jnp.int32))
    cos = jnp.concatenate([cos_p, jnp.tile(cos_s, (n_seq, 1))], axis=0)
    sin = jnp.concatenate([sin_p, jnp.tile(sin_s, (n_seq, 1))], axis=0)
    zeros = jnp.zeros((cos.shape[0], LANES - QK_ROPE_DIM), F32)
    return (jnp.concatenate([cos, cos, zeros], axis=1), jnp.concatenate([sin, sin, zeros], axis=1))


def _layer_weights(l, p, cos, sin):
    o1 = GMLP_WIDTH
    o2 = 2 * GMLP_WIDTH
    o3 = o2 + Q_LORA_RANK
    o4 = o3 + KV_LORA_RANK
    w_in = p["w_in"][l]
    kpe_cols = w_in[:, o4:]
    w_in2 = jnp.concatenate([w_in[:, :o4], _pad_cols(kpe_cols, LANES),
                             _pad_cols(_rot_half_cols(kpe_cols), LANES)], axis=1).astype(BF16)
    w_qb = p["w_qb"][l]
    q_parts = []
    for hd in range(MLA_HEADS):
        base = hd * QK_HEAD_DIM
        rope_cols = w_qb[:, base + QK_NOPE_DIM:base + QK_HEAD_DIM]
        q_parts += [w_qb[:, base:base + QK_NOPE_DIM], _pad_cols(rope_cols, LANES),
                    _pad_cols(_rot_half_cols(rope_cols), LANES)]
    w_qb2 = jnp.concatenate(q_parts, axis=1).astype(BF16)
    w_kvb = p["w_kvb"][l].reshape(KV_LORA_RANK, MLA_HEADS, QK_NOPE_DIM + V_HEAD_DIM)
    w_kvb2 = jnp.concatenate([w_kvb[:, :, :QK_NOPE_DIM].reshape(KV_LORA_RANK, -1),
                              w_kvb[:, :, QK_NOPE_DIM:].reshape(KV_LORA_RANK, -1)], axis=1).astype(BF16)
    qscale = LOG2E / math.sqrt(QK_HEAD_DIM)
    gq = _pad_cols(p["q_norm_g"][l][None, :] * qscale, QK_PAD)
    gk = _pad_cols(p["k_norm_g"][l][None, :], QK_PAD)
    bound = (QK_HEAD_DIM * qscale * SCORE_BOUND_MARGIN * jnp.max(jnp.abs(p["q_norm_g"][l]))
             * jnp.max(jnp.abs(p["k_norm_g"][l])))
    bounded = bound <= MAX_SCORE_BOUND
    pad_lane = jnp.arange(LANES) == QK_ROPE_DIM
    off = jnp.stack([jnp.where(pad_lane, 1.0, 0.0),
                     jnp.where(pad_lane & bounded, -bound, 0.0)]).astype(F32)

    ws = p["gmlp_ws"][l]
    tri = jnp.tril(jnp.ones((GMLP_CHUNK, GMLP_CHUNK), dtype=bool))
    wt = jnp.where(tri[None], ws, 0.0)
    hc = GMLP_CHUNK // 2
    top = wt[:, :hc, :hc]
    zero = jnp.zeros_like(top)
    wt_s = jnp.concatenate([jnp.concatenate([top, zero], axis=2), jnp.concatenate([zero, top], axis=2)], axis=1)
    gw = jnp.stack([wt, wt_s]).astype(BF16)
    b = p["gmlp_b"][l]
    b_s = jnp.concatenate([b[:, :hc], b[:, :hc]], axis=1)
    gb = jnp.broadcast_to(jnp.stack([b, b_s])[..., None], (2, GMLP_GROUPS, GMLP_CHUNK, LANES)).astype(F32)

    w_out = p["w_out"][l].astype(BF16)
    return dict(
        wg=p["exp_w_gate"][l].astype(BF16).reshape(N_GROUPS, EXPERTS_PER_GROUP, D_MODEL, EXPERT_FF),
        wu=p["exp_w_up"][l].astype(BF16).reshape(N_GROUPS, EXPERTS_PER_GROUP, D_MODEL, EXPERT_FF),
        wd=p["exp_w_down"][l].astype(BF16).reshape(N_GROUPS, EXPERTS_PER_GROUP, EXPERT_FF, D_MODEL),
        sg=p["sh_w_gate"][l].astype(BF16),
        su=p["sh_w_up"][l].astype(BF16), sd=p["sh_w_down"][l].astype(BF16),
        ln1=p["ln1_g"][l][None, :], w_in=w_in2, cos=cos, sin=sin,
        gv=p["gmlp_v_g"][l].reshape(1, GMLP_WIDTH), gw=gw, gb=gb,
        q_a_g=p["q_a_g"][l][None, :], w_qb=w_qb2, kv_a_g=p["kv_a_g"][l][None, :], w_kvb=w_kvb2,
        gq=gq, gk=gk, w_out_a=w_out[:GMLP_WIDTH], w_out_b=w_out[GMLP_WIDTH:],
        ln2=p["ln2_g"][l][None, :], rw=_split_bf16(_pad_cols(p["router_w"], LANES)),
        rb=p["router_bias"].reshape(N_EXPERTS, 1), off=off,
        bounded=bounded.astype(jnp.int32).reshape(1),
    )


def kernel(x_prompt, x_sample, cache_kv_latent, cache_k_rope, c_prompt, c_sample, w_ada, b_ada, ln1_g, w_in,
           gmlp_v_g, gmlp_ws, gmlp_b, q_a_g, w_qb, kv_a_g, w_kvb, q_norm_g, k_norm_g, w_out, ln2_g, router_w,
           router_bias, exp_w_gate, exp_w_up, exp_w_down, sh_w_gate, sh_w_up, sh_w_down):
    p = dict(w_in=w_in, gmlp_v_g=gmlp_v_g, gmlp_ws=gmlp_ws, gmlp_b=gmlp_b, q_a_g=q_a_g, w_qb=w_qb,
             kv_a_g=kv_a_g, w_kvb=w_kvb, q_norm_g=q_norm_g, k_norm_g=k_norm_g, w_out=w_out, ln1_g=ln1_g,
             ln2_g=ln2_g, router_w=router_w, router_bias=router_bias, exp_w_gate=exp_w_gate,
             exp_w_up=exp_w_up, exp_w_down=exp_w_down, sh_w_gate=sh_w_gate, sh_w_up=sh_w_up,
             sh_w_down=sh_w_down)
    batch, seq, _ = x_prompt.shape
    n_seq, s_len, _ = x_sample.shape
    depth, _, past, _ = cache_kv_latent.shape
    assert batch == 1 and s_len == SUB and n_seq == N_SUB and n_seq * s_len == TILE
    assert seq % TILE == 0 and seq % ATT_TQ == 0 and past % GMLP_CHUNK == 0 and past % CHUNK == 0
    rp = seq
    rs = n_seq * s_len

    cos, sin = _rope_tables(seq, past, s_len, n_seq)

    c_all = jnp.concatenate([jnp.broadcast_to(c_prompt, (N_SUB, D_MODEL)), c_sample], axis=0)
    mod = _ada_call(c_all, w_ada, b_ada)
    mod = mod.reshape(depth, 2, N_SUB, 6, D_MODEL)

    weights = [_layer_weights(l, p, cos, sin) for l in range(depth)]
    lat_all = cache_kv_latent.reshape(depth, n_seq * past, KV_LORA_RANK)
    pe_all = cache_k_rope.reshape(depth, n_seq * past, QK_ROPE_DIM)
    k_past, v_past = _cache_kv_call(lat_all, pe_all, jnp.stack([w["w_kvb"] for w in weights]),
                                    jnp.stack([w["gk"] for w in weights]),
                                    jnp.stack([w["off"] for w in weights]))

    xp = x_prompt.reshape(rp, D_MODEL)
    xs = x_sample.reshape(rs, D_MODEL)
    planes = tuple(jnp.zeros((depth, rows, width), F32) for rows, width in
                   ((rp, KV_LORA_RANK), (rp, QK_ROPE_DIM), (rs, KV_LORA_RANK), (rs, QK_ROPE_DIM),
                    (rs, GMLP_WIDTH)))
    for l in range(depth):
        w = weights[l]
        sh1, sc1, g1, sh2, sc2, g2 = [mod[l, :, :, j, :] for j in range(6)]
        a, q, k, v, *planes = _mix_in_call(xp, xs, sc1, sh1, w, l, depth, tuple(planes))
        bp = _prompt_attn_call(w["bounded"], q, k, v, rp)
        bs = _sample_attn_call(l, q, k, v, k_past, v_past, rp, n_seq, s_len, past)
        xm, h2e, grp, rank, counts = _mix_out_call(a, bp, bs, xp, xs, g1, sc2, sh2, w)
        xp, xs = _moe(h2e, grp, rank, counts, xm, g2, w, rp)
    lat_p, pe_p, lat_s, pe_s, v_s = planes
    return (xp.reshape(batch, seq, D_MODEL), xs.reshape(n_seq, s_len, D_MODEL),
            lat_p.reshape(depth, batch, seq, KV_LORA_RANK), pe_p.reshape(depth, batch, seq, QK_ROPE_DIM),
            lat_s.reshape(depth, n_seq, s_len, KV_LORA_RANK), pe_s.reshape(depth, n_seq, s_len, QK_ROPE_DIM),
            v_s.reshape(depth, n_seq, s_len, GMLP_WIDTH))
```

```python
import functools
import math

import jax
import jax.numpy as jnp
from jax import lax
from jax.experimental import pallas as pl
from jax.experimental.pallas import tpu as pltpu

F32 = jnp.float32
BF16 = jnp.bfloat16

D_MODEL = 1024
CHUNK = 64
GMLP_WIDTH = 512
GMLP_GROUPS = 4
GMLP_CHUNK = 128
MLA_HEADS = 4
QK_NOPE_DIM = 128
QK_ROPE_DIM = 64
QK_HEAD_DIM = 192
V_HEAD_DIM = 128
Q_LORA_RANK = 384
KV_LORA_RANK = 256
ROPE_THETA = 10000.0
N_EXPERTS = 16
N_GROUPS = 4
EXPERTS_PER_GROUP = 4
EXPERT_FF = 512
EPS = 1e-6

LANES = 128
TILE = 512
SUB = 64
N_SUB = TILE // SUB
QK_PAD = 256
IN_COLS = 2 * GMLP_WIDTH + Q_LORA_RANK + KV_LORA_RANK + 2 * LANES
Q_COLS = MLA_HEADS * 3 * LANES
H2E_COLS = D_MODEL + LANES
ROW_DMA_UNROLL = 8
ATT_TK = 512
ATT_TQ = 2 * ATT_TK
ATT_UNROLL = 4
ATT_HEADS = 2
VMEM_LIMIT = 56 * 1024 * 1024
LOG2E = 1.4426950408889634
SCORE_BOUND_MARGIN = 1.02
MAX_SCORE_BOUND = 48.0


def _cparams(n_axes):
    return pltpu.CompilerParams(dimension_semantics=("arbitrary",) * n_axes,
                                vmem_limit_bytes=VMEM_LIMIT)


def _rms(x, eps=EPS):
    return x * lax.rsqrt(jnp.mean(x * x, axis=-1, keepdims=True) + eps)


def _gelu(x):
    c = math.sqrt(2.0 / math.pi)
    return 0.5 * x * (1.0 + jnp.tanh(c * (x + 0.044715 * (x * x * x))))


def _modulate(h, scale, shift):
    h3 = h.reshape(N_SUB, SUB, h.shape[-1])
    h3 = h3 * (1.0 + scale[:, None, :]) + shift[:, None, :]
    return h3.reshape(h.shape)


def _gated_add(x, gate, y):
    y3 = y.reshape(N_SUB, SUB, y.shape[-1]) * gate[:, None, :]
    return x + y3.reshape(y.shape)


def _ada_kernel(c_ref, w_ref, b_ref, o_ref):
    c = c_ref[...]
    cs = (c / (1.0 + jnp.exp(-c))).astype(BF16)
    w = w_ref[0].astype(BF16)
    o_ref[0] = jnp.dot(cs, w, preferred_element_type=F32) + b_ref[0]


def _ada_call(c_all, w_ada, b_ada):
    depth = w_ada.shape[0]
    nblk = w_ada.shape[2] // D_MODEL
    return pl.pallas_call(
        _ada_kernel,
        grid=(depth, nblk),
        in_specs=[
            pl.BlockSpec((16, D_MODEL), lambda l, j: (0, 0)),
            pl.BlockSpec((1, D_MODEL, D_MODEL), lambda l, j: (l, 0, j)),
            pl.BlockSpec((1, 1, D_MODEL), lambda l, j: (l, 0, j)),
        ],
        out_specs=pl.BlockSpec((1, 16, D_MODEL), lambda l, j: (l, 0, j)),
        out_shape=jax.ShapeDtypeStruct((depth, 16, w_ada.shape[2]), F32),
        compiler_params=_cparams(2),
        name="ada_mod",
    )(c_all, w_ada, b_ada.reshape(depth, 1, -1))


def _mix_in_kernel(n_prompt_tiles, n_prev, xp_ref, xs_ref, sc_ref, sh_ref, ln_ref, win_ref, cos_ref, sin_ref,
                   gv_ref, gw_ref, gb_ref, qag_ref, wqb_ref, kvag_ref, wkvb_ref, gq_ref, gk_ref, off_ref,
                   *refs):
    a_ref, q_ref, k_ref, v_ref, latp_ref, pep_ref, lats_ref, pes_ref, vn_ref = refs[n_prev:]
    i = pl.program_id(0)
    is_sample = i >= n_prompt_tiles
    x = jnp.where(is_sample, xs_ref[...], xp_ref[...])
    h = _rms(x) * ln_ref[...]
    h = _modulate(h, sc_ref[0], sh_ref[0])
    z = jnp.dot(h.astype(BF16), win_ref[...], preferred_element_type=F32)

    u = _gelu(z[:, :GMLP_WIDTH])
    v = _gelu(z[:, GMLP_WIDTH:2 * GMLP_WIDTH])
    gv = gv_ref[...]
    vn_parts = []
    for g in range(GMLP_GROUPS):
        sl = slice(g * LANES, (g + 1) * LANES)
        vn_parts.append(_rms(v[:, sl]) * gv[:, sl])

    @pl.when(is_sample)
    def _():
        for g in range(GMLP_GROUPS):
            vn_ref[0, :, g * LANES:(g + 1) * LANES] = vn_parts[g]

    for g in range(GMLP_GROUPS):
        sl = slice(g * LANES, (g + 1) * LANES)
        vb = vn_parts[g].astype(BF16)
        wg = gw_ref[0, g]
        bg = gb_ref[0, g]
        for c in range(TILE // GMLP_CHUNK):
            rows = slice(c * GMLP_CHUNK, (c + 1) * GMLP_CHUNK)
            s = jnp.dot(wg, vb[rows], preferred_element_type=F32) + bg
            a_ref[rows, sl] = (u[rows, sl] * s).astype(BF16)

    cos = cos_ref[...]
    sin = sin_ref[...]

    o2 = 2 * GMLP_WIDTH
    o3 = o2 + Q_LORA_RANK
    o4 = o3 + KV_LORA_RANK
    ql = _rms(z[:, o2:o3]) * qag_ref[...]
    qq = jnp.dot(ql.astype(BF16), wqb_ref[...], preferred_element_type=F32)
    gq = gq_ref[...]
    for hd in range(MLA_HEADS):
        base = hd * 3 * LANES
        nope = qq[:, base:base + LANES]
        rope = qq[:, base + LANES:base + 2 * LANES] * cos + qq[:, base + 2 * LANES:base + 3 * LANES] * sin
        ss = jnp.sum(nope * nope, axis=-1, keepdims=True) + jnp.sum(rope * rope, axis=-1, keepdims=True)
        rinv = lax.rsqrt(ss * (1.0 / QK_HEAD_DIM) + EPS)
        q_ref[hd, :, :LANES] = (nope * rinv * gq[:, :LANES]).astype(BF16)
        q_ref[hd, :, LANES:] = (rope * rinv * gq[:, LANES:] + off_ref[0:1, :]).astype(BF16)

    ckv = _rms(z[:, o3:o4]) * kvag_ref[...]
    kpe = z[:, o4:o4 + LANES] * cos + z[:, o4 + LANES:o4 + 2 * LANES] * sin

    @pl.when(jnp.logical_not(is_sample))
    def _():
        latp_ref[0] = ckv
        pep_ref[0] = kpe[:, :QK_ROPE_DIM]

    @pl.when(is_sample)
    def _():
        lats_ref[0] = ckv
        pes_ref[0] = kpe[:, :QK_ROPE_DIM]

    kv = jnp.dot(ckv.astype(BF16), wkvb_ref[...], preferred_element_type=F32)
    gk = gk_ref[...]
    kpe_ss = jnp.sum(kpe * kpe, axis=-1, keepdims=True)
    for hd in range(MLA_HEADS):
        nope = kv[:, hd * LANES:(hd + 1) * LANES]
        ss = jnp.sum(nope * nope, axis=-1, keepdims=True) + kpe_ss
        rinv = lax.rsqrt(ss * (1.0 / QK_HEAD_DIM) + EPS)
        k_ref[hd, :, :LANES] = (nope * rinv * gk[:, :LANES]).astype(BF16)
        k_ref[hd, :, LANES:] = (kpe * rinv * gk[:, LANES:] + off_ref[1:2, :]).astype(BF16)
        v_ref[hd] = kv[:, (MLA_HEADS + hd) * LANES:(MLA_HEADS + hd + 1) * LANES].astype(BF16)


def _mix_in_call(xp, xs, sc1, sh1, w, layer, depth, prev):
    rp, rs = xp.shape[0], xs.shape[0]
    npt = rp // TILE
    nt = npt + rs // TILE
    r = rp + rs
    last = npt - 1

    def full(a):
        nd = a.ndim
        return pl.BlockSpec(a.shape, lambda i: (0,) * nd)

    def variant(a):
        nd = a.ndim
        return pl.BlockSpec((1,) + a.shape[1:], lambda i: (i // npt,) + (0,) * (nd - 1))

    row = lambda width: pl.BlockSpec((TILE, width), lambda i: (i, 0))
    head = lambda width: pl.BlockSpec((MLA_HEADS, TILE, width), lambda i: (0, i, 0))
    in_specs = [
        pl.BlockSpec((TILE, D_MODEL), lambda i: (jnp.minimum(i, last), 0)),
        pl.BlockSpec((TILE, D_MODEL), lambda i: (0, 0)),
        variant(sc1), variant(sh1), full(w["ln1"]), full(w["w_in"]),
        row(LANES), row(LANES),
        full(w["gv"]), variant(w["gw"]), variant(w["gb"]),
        full(w["q_a_g"]), full(w["w_qb"]), full(w["kv_a_g"]), full(w["w_kvb"]),
        full(w["gq"]), full(w["gk"]), full(w["off"]),
    ] + [pl.BlockSpec(memory_space=pl.ANY)] * len(prev)
    prompt_plane = lambda width: pl.BlockSpec((1, TILE, width), lambda i: (layer, jnp.minimum(i, last), 0))
    sample_plane = lambda width: pl.BlockSpec((1, TILE, width), lambda i: (layer, 0, 0))
    out_specs = [
        row(GMLP_WIDTH), head(QK_PAD), head(QK_PAD), head(V_HEAD_DIM),
        prompt_plane(KV_LORA_RANK), prompt_plane(QK_ROPE_DIM),
        sample_plane(KV_LORA_RANK), sample_plane(QK_ROPE_DIM), sample_plane(GMLP_WIDTH),
    ]
    out_shape = [
        jax.ShapeDtypeStruct((r, GMLP_WIDTH), BF16),
        jax.ShapeDtypeStruct((MLA_HEADS, r, QK_PAD), BF16),
        jax.ShapeDtypeStruct((MLA_HEADS, r, QK_PAD), BF16),
        jax.ShapeDtypeStruct((MLA_HEADS, r, V_HEAD_DIM), BF16),
        jax.ShapeDtypeStruct((depth, rp, KV_LORA_RANK), F32),
        jax.ShapeDtypeStruct((depth, rp, QK_ROPE_DIM), F32),
        jax.ShapeDtypeStruct((depth, rs, KV_LORA_RANK), F32),
        jax.ShapeDtypeStruct((depth, rs, QK_ROPE_DIM), F32),
        jax.ShapeDtypeStruct((depth, rs, GMLP_WIDTH), F32),
    ]
    n_fixed = len(in_specs) - len(prev)
    n_own = 4
    return pl.pallas_call(
        functools.partial(_mix_in_kernel, npt, len(prev)),
        grid=(nt,), in_specs=in_specs, out_specs=out_specs, out_shape=out_shape,
        input_output_aliases={n_fixed + j: n_own + j for j in range(len(prev))},
        compiler_params=_cparams(1), name="mix_in",
    )(xp, xs, sc1, sh1, w["ln1"], w["w_in"], w["cos"], w["sin"], w["gv"], w["gw"], w["gb"],
      w["q_a_g"], w["w_qb"], w["kv_a_g"], w["w_kvb"], w["gq"], w["gk"], w["off"], *prev)


def _cache_kv_kernel(lat_ref, pe_ref, wkvb_ref, gk_ref, off_ref, k_ref, v_ref):
    lat = lat_ref[0]
    pe_t = pe_ref[0, 0]
    kpe = jnp.concatenate([pe_t, jnp.zeros_like(pe_t)], axis=0).T
    kv = jnp.dot(lat.astype(BF16), wkvb_ref[0], preferred_element_type=F32)
    gk = gk_ref[0]
    kpe_ss = jnp.sum(kpe * kpe, axis=-1, keepdims=True)
    for hd in range(MLA_HEADS):
        nope = kv[:, hd * LANES:(hd + 1) * LANES]
        ss = jnp.sum(nope * nope, axis=-1, keepdims=True) + kpe_ss
        rinv = lax.rsqrt(ss * (1.0 / QK_HEAD_DIM) + EPS)
        k_ref[0, hd, :, :LANES] = (nope * rinv * gk[:, :LANES]).astype(BF16)
        k_ref[0, hd, :, LANES:] = (kpe * rinv * gk[:, LANES:] + off_ref[0, 1:2, :]).astype(BF16)
        v_ref[0, hd] = kv[:, (MLA_HEADS + hd) * LANES:(MLA_HEADS + hd + 1) * LANES].astype(BF16)


def _cache_kv_call(lat, pe_t, w_kvb, gk, off):
    depth, rows, _ = lat.shape
    t = min(1024, pe_t.shape[-1])
    per_stream = pe_t.shape[-1] // t
    return pl.pallas_call(
        _cache_kv_kernel,
        grid=(depth, rows // t),
        in_specs=[
            pl.BlockSpec((1, t, KV_LORA_RANK), lambda l, i: (l, i, 0)),
            pl.BlockSpec((1, 1, QK_ROPE_DIM, t), lambda l, i: (l, i // per_stream, 0, i % per_stream)),
            pl.BlockSpec((1,) + w_kvb.shape[1:], lambda l, i: (l, 0, 0)),
            pl.BlockSpec((1, 1, QK_PAD), lambda l, i: (l, 0, 0)),
            pl.BlockSpec((1, 2, LANES), lambda l, i: (l, 0, 0)),
        ],
        out_specs=[
            pl.BlockSpec((1, MLA_HEADS, t, QK_PAD), lambda l, i: (l, 0, i, 0)),
            pl.BlockSpec((1, MLA_HEADS, t, V_HEAD_DIM), lambda l, i: (l, 0, i, 0)),
        ],
        out_shape=[
            jax.ShapeDtypeStruct((depth, MLA_HEADS, rows, QK_PAD), BF16),
            jax.ShapeDtypeStruct((depth, MLA_HEADS, rows, V_HEAD_DIM), BF16),
        ],
        compiler_params=_cparams(2), name="cache_kv",
    )(lat, pe_t, w_kvb, gk, off)


def _attn_step(q, k, v, m, l, acc, mask):
    s = lax.dot_general(q, k, (((1,), (1,)), ((), ())), preferred_element_type=F32)
    if mask is not None:
        s = jnp.where(mask, s, -1e30)
    m_new = jnp.maximum(m, jnp.max(s, axis=-1, keepdims=True))
    alpha = jnp.exp2(m - m_new)
    p = jnp.exp2(s - m_new)
    l = alpha * l + jnp.sum(p, axis=-1, keepdims=True)
    acc = alpha * acc + jnp.dot(p.astype(BF16), v, preferred_element_type=F32)
    return m_new, l, acc


def _bounded_step(q, k, v_aug, acc, mask):
    s = lax.dot_general(q, k, (((1,), (1,)), ((), ())), preferred_element_type=F32)
    if mask is not None:
        s = jnp.where(mask, s, -1e30)
    return acc + jnp.dot(jnp.exp2(s).astype(BF16), v_aug, preferred_element_type=F32)


def _prompt_attn_kernel(bounded_ref, q_ref, k_ref, v_ref, o_ref):
    qi = pl.program_id(1)
    row = lax.broadcasted_iota(jnp.int32, (ATT_TK, ATT_TK), 0)
    col = lax.broadcasted_iota(jnp.int32, (ATT_TK, ATT_TK), 1)
    mask = (col // CHUNK) <= (row // CHUNK)

    def q_half(h, half):
        return q_ref[h, half * ATT_TK:(half + 1) * ATT_TK, :]

    def kv(h, j):
        off = pl.multiple_of(j * ATT_TK, ATT_TK)
        return k_ref[h, pl.ds(off, ATT_TK), :], v_ref[h, pl.ds(off, ATT_TK), :]

    def out(h, half):
        return o_ref.at[half * ATT_TK:(half + 1) * ATT_TK, h * V_HEAD_DIM:(h + 1) * V_HEAD_DIM]

    @pl.when(bounded_ref[0] > 0)
    def _():
        lane = lax.broadcasted_iota(jnp.int32, (ATT_TK, LANES), 1)
        ones_col = jnp.where(lane == 0, 1.0, 0.0).astype(BF16)

        def kv_aug(h, j):
            k, v = kv(h, j)
            return k, jnp.concatenate([v, ones_col], axis=1)

        def blocks(first, count, accs):
            accs = [list(a) for a in accs]
            for u in range(count):
                for h in range(ATT_HEADS):
                    k, v = kv_aug(h, first + u)
                    for half in range(2):
                        accs[h][half] = _bounded_step(q_half(h, half), k, v, accs[h][half], None)
            return tuple(tuple(a) for a in accs)

        zero = jnp.zeros((ATT_TK, 2 * LANES), F32)
        n_trips = (2 * qi) // ATT_UNROLL
        accs = lax.fori_loop(0, n_trips, lambda j, c: blocks(ATT_UNROLL * j, ATT_UNROLL, c),
                             ((zero, zero),) * ATT_HEADS)
        accs = lax.cond(qi % 2 == 1, lambda c: blocks(ATT_UNROLL * n_trips, 2, c), lambda c: c, accs)
        for h in range(ATT_HEADS):
            acc_a, acc_b = accs[h]
            k, v = kv_aug(h, 2 * qi)
            acc_a = _bounded_step(q_half(h, 0), k, v, acc_a, mask)
            acc_b = _bounded_step(q_half(h, 1), k, v, acc_b, None)
            k, v = kv_aug(h, 2 * qi + 1)
            acc_b = _bounded_step(q_half(h, 1), k, v, acc_b, mask)
            out(h, 0)[...] = (acc_a[:, :LANES] / acc_a[:, LANES:LANES + 1]).astype(BF16)
            out(h, 1)[...] = (acc_b[:, :LANES] / acc_b[:, LANES:LANES + 1]).astype(BF16)

    @pl.when(bounded_ref[0] == 0)
    def _():
        def init():
            return (jnp.full((ATT_TK, 1), -1e30, F32), jnp.zeros((ATT_TK, 1), F32),
                    jnp.zeros((ATT_TK, V_HEAD_DIM), F32))

        for h in range(ATT_HEADS):
            qa, qb = q_half(h, 0), q_half(h, 1)

            def body(j, carry, h=h, qa=qa, qb=qb):
                k, v = kv(h, j)
                return _attn_step(qa, k, v, *carry[0], None), _attn_step(qb, k, v, *carry[1], None)

            sa, sb = lax.fori_loop(0, 2 * qi, body, (init(), init()))
            k, v = kv(h, 2 * qi)
            sa = _attn_step(qa, k, v, *sa, mask)
            sb = _attn_step(qb, k, v, *sb, None)
            k, v = kv(h, 2 * qi + 1)
            sb = _attn_step(qb, k, v, *sb, mask)
            out(h, 0)[...] = (sa[2] / sa[1]).astype(BF16)
            out(h, 1)[...] = (sb[2] / sb[1]).astype(BF16)


def _prompt_attn_call(bounded, q, k, v, rp):
    r = q.shape[1]
    resident = lambda arr: pl.BlockSpec((ATT_HEADS,) + arr.shape[1:], lambda h, i, b: (h, 0, 0),
                                        pipeline_mode=pl.Buffered(1))
    return pl.pallas_call(
        _prompt_attn_kernel,
        grid_spec=pltpu.PrefetchScalarGridSpec(
            num_scalar_prefetch=1, grid=(MLA_HEADS // ATT_HEADS, rp // ATT_TQ),
            in_specs=[
                pl.BlockSpec((ATT_HEADS, ATT_TQ, QK_PAD), lambda h, i, b: (h, i, 0)),
                resident(k), resident(v),
            ],
            out_specs=pl.BlockSpec((ATT_TQ, ATT_HEADS * V_HEAD_DIM), lambda h, i, b: (i, h))),
        out_shape=jax.ShapeDtypeStruct((rp, MLA_HEADS * V_HEAD_DIM), BF16),
        compiler_params=_cparams(2), name="prompt_attn",
    )(bounded, q, k, v)


def _sample_attn_kernel(q_ref, kp_ref, vp_ref, kn_ref, vn_ref, o_ref):
    q = q_ref[0]
    nt = (((1,), (1,)), ((), ()))
    s1 = lax.dot_general(q, kp_ref[0, 0], nt, preferred_element_type=F32)
    s2 = lax.dot_general(q, kn_ref[0], nt, preferred_element_type=F32)
    m = jnp.maximum(jnp.max(s1, axis=-1, keepdims=True), jnp.max(s2, axis=-1, keepdims=True))
    p1 = jnp.exp2(s1 - m)
    p2 = jnp.exp2(s2 - m)
    l = jnp.sum(p1, axis=-1, keepdims=True) + jnp.sum(p2, axis=-1, keepdims=True)
    o = (jnp.dot(p1.astype(BF16), vp_ref[0, 0], preferred_element_type=F32)
         + jnp.dot(p2.astype(BF16), vn_ref[0], preferred_element_type=F32))
    o_ref[...] = (o / l).astype(BF16)


def _sample_attn_call(layer, q, k, v, k_past, v_past, rp, n_seq, s_len, past):
    first = rp // s_len
    return pl.pallas_call(
        _sample_attn_kernel,
        grid=(n_seq, MLA_HEADS),
        in_specs=[
            pl.BlockSpec((1, s_len, QK_PAD), lambda b, h: (h, first + b, 0)),
            pl.BlockSpec((1, 1, past, QK_PAD), lambda b, h: (layer, h, b, 0)),
            pl.BlockSpec((1, 1, past, V_HEAD_DIM), lambda b, h: (layer, h, b, 0)),
            pl.BlockSpec((1, s_len, QK_PAD), lambda b, h: (h, first + b, 0)),
            pl.BlockSpec((1, s_len, V_HEAD_DIM), lambda b, h: (h, first + b, 0)),
        ],
        out_specs=pl.BlockSpec((s_len, V_HEAD_DIM), lambda b, h: (b, h)),
        out_shape=jax.ShapeDtypeStruct((n_seq * s_len, MLA_HEADS * V_HEAD_DIM), BF16),
        compiler_params=_cparams(2), name="sample_attn",
    )(q, k_past, v_past, k, v)


def _route(logits_t, bias_col):
    scores = 1.0 / (1.0 + jnp.exp(-logits_t))
    biased = scores + bias_col
    s_rows = [scores[e:e + 1, :] for e in range(N_EXPERTS)]
    b_rows = [biased[e:e + 1, :] for e in range(N_EXPERTS)]
    group_score = []
    for g in range(N_GROUPS):
        rows = b_rows[g * EXPERTS_PER_GROUP:(g + 1) * EXPERTS_PER_GROUP]
        best = None
        for a in range(EXPERTS_PER_GROUP):
            for b in range(a + 1, EXPERTS_PER_GROUP):
                pair = rows[a] + rows[b]
                best = pair if best is None else jnp.maximum(best, pair)
        group_score.append(best)
    best_group = jnp.zeros_like(group_score[0], dtype=jnp.int32)
    best_val = group_score[0]
    for g in range(1, N_GROUPS):
        better = group_score[g] > best_val
        best_group = jnp.where(better, g, best_group)
        best_val = jnp.where(better, group_score[g], best_val)
    selected = []
    for e in range(N_EXPERTS):
        g = e // EXPERTS_PER_GROUP
        rank = jnp.zeros_like(best_group)
        for j in range(g * EXPERTS_PER_GROUP, (g + 1) * EXPERTS_PER_GROUP):
            if j == e:
                continue
            ahead = b_rows[j] > b_rows[e]
            if j < e:
                ahead = ahead | (b_rows[j] == b_rows[e])
            rank = rank + ahead.astype(jnp.int32)
        selected.append((best_group == g) & (rank < 2))
    denom = jnp.zeros_like(s_rows[0])
    for e in range(N_EXPERTS):
        denom = denom + jnp.where(selected[e], s_rows[e], 0.0)
    return [jnp.where(selected[e], s_rows[e] / denom, 0.0) for e in range(N_EXPERTS)], best_group


def _mix_out_kernel(n_prompt_tiles, a_ref, bp_ref, bs_ref, xp_ref, xs_ref, wa_ref, wb_ref, g1_ref, sc_ref,
                    sh_ref, g2_ref, ln_ref, rw_ref, rb_ref, sg_ref, su_ref, sd_ref,
                    ysh_ref, h2e_ref, grp_ref, rank_ref, cnt_ref, gt_ref, carry_ref):
    i = pl.program_id(0)
    is_sample = i >= n_prompt_tiles
    x = jnp.where(is_sample, xs_ref[...], xp_ref[...])
    b = jnp.where(is_sample, bs_ref[...], bp_ref[...])
    mix = (jnp.dot(a_ref[...], wa_ref[...], preferred_element_type=F32)
           + jnp.dot(b, wb_ref[...], preferred_element_type=F32))
    xm = _gated_add(x, g1_ref[0], mix)
    h2 = _modulate(_rms(xm) * ln_ref[...], sc_ref[0], sh_ref[0])
    h2e_ref[:, :D_MODEL] = h2
    h2_hi = h2.astype(BF16)
    act = _silu(jnp.dot(h2_hi, sg_ref[...], preferred_element_type=F32)) * jnp.dot(
        h2_hi, su_ref[...], preferred_element_type=F32)
    shared = jnp.dot(act.astype(BF16), sd_ref[...], preferred_element_type=F32)
    ysh_ref[...] = _gated_add(xm, g2_ref[0], shared)
    h2_lo = (h2 - h2_hi.astype(F32)).astype(BF16)
    by_hi = jnp.dot(h2_hi, rw_ref[...], preferred_element_type=F32)
    logits = (by_hi[:, :LANES] + by_hi[:, LANES:]
              + jnp.dot(h2_lo, rw_ref[:, :LANES], preferred_element_type=F32))
    gate_rows, best_group = _route(logits.T[:N_EXPERTS, :], rb_ref[...])
    gt_ref[...] = jnp.zeros_like(gt_ref)
    for e in range(N_EXPERTS):
        gt_ref[e:e + 1, :] = gate_rows[e]
    h2e_ref[:, D_MODEL:] = gt_ref[...].T

    @pl.when(i == 0)
    def _():
        carry_ref[...] = jnp.zeros_like(carry_ref)

    sub = lax.broadcasted_iota(jnp.int32, (8, TILE), 0)
    onehot = jnp.where(sub == best_group, 1.0, 0.0)
    r_idx = lax.broadcasted_iota(jnp.int32, (TILE, TILE), 0)
    c_idx = lax.broadcasted_iota(jnp.int32, (TILE, TILE), 1)
    upper = jnp.where(r_idx <= c_idx, 1.0, 0.0).astype(BF16)
    cum = jnp.dot(onehot.astype(BF16), upper, preferred_element_type=F32)
    carry = carry_ref[...]
    rank = jnp.sum(onehot * (cum - 1.0 + carry[:, :1]), axis=0, keepdims=True)
    grp_ref[0] = best_group
    rank_ref[0] = rank.astype(jnp.int32)
    carry = carry + jnp.sum(onehot, axis=1, keepdims=True)
    carry_ref[...] = carry
    cnt_ref[...] = carry


def _mix_out_call(a, bp, bs, xp, xs, g1, sc2, sh2, g2, w):
    rp, rs = xp.shape[0], xs.shape[0]
    npt = rp // TILE
    nt = npt + rs // TILE
    r = rp + rs
    last = npt - 1

    def full(arr):
        nd = arr.ndim
        return pl.BlockSpec(arr.shape, lambda i: (0,) * nd)

    def variant(arr):
        nd = arr.ndim
        return pl.BlockSpec((1,) + arr.shape[1:], lambda i: (i // npt,) + (0,) * (nd - 1))

    row = lambda width: pl.BlockSpec((TILE, width), lambda i: (i, 0))
    prow = lambda width: pl.BlockSpec((TILE, width), lambda i: (jnp.minimum(i, last), 0))
    srow = lambda width: pl.BlockSpec((TILE, width), lambda i: (0, 0))
    return pl.pallas_call(
        functools.partial(_mix_out_kernel, npt),
        grid=(nt,),
        in_specs=[row(GMLP_WIDTH), prow(GMLP_WIDTH), srow(GMLP_WIDTH), prow(D_MODEL), srow(D_MODEL),
                  full(w["w_out_a"]), full(w["w_out_b"]), variant(g1), variant(sc2), variant(sh2),
                  variant(g2), full(w["ln2"]), full(w["rw"]), full(w["rb"]),
                  full(w["sg"]), full(w["su"]), full(w["sd"])],
        out_specs=[row(D_MODEL), row(H2E_COLS),
                   pl.BlockSpec((1, 1, TILE), lambda i: (i, 0, 0)),
                   pl.BlockSpec((1, 1, TILE), lambda i: (i, 0, 0)),
                   pl.BlockSpec((8, LANES), lambda i: (0, 0))],
        out_shape=[jax.ShapeDtypeStruct((r, D_MODEL), F32),
                   jax.ShapeDtypeStruct((r, H2E_COLS), F32),
                   jax.ShapeDtypeStruct((nt, 1, TILE), jnp.int32),
                   jax.ShapeDtypeStruct((nt, 1, TILE), jnp.int32),
                   jax.ShapeDtypeStruct((8, LANES), F32)],
        scratch_shapes=[pltpu.VMEM((LANES, TILE), F32), pltpu.VMEM((8, LANES), F32)],
        compiler_params=_cparams(1), name="mix_out",
    )(a, bp, bs, xp, xs, w["w_out_a"], w["w_out_b"], g1, sc2, sh2, g2, w["ln2"], w["rw"], w["rb"],
      w["sg"], w["su"], w["sd"])


def _silu(x):
    return x / (1.0 + jnp.exp(-x))


def _invert_kernel(pos_ref, pad_lo_ref, pad_hi_ref, src_ref):
    def clear(s, carry):
        src_ref[s] = 0
        return carry

    for g in range(N_GROUPS):
        lax.fori_loop(pad_lo_ref[g], pad_hi_ref[g], clear, 0)
    lax.fori_loop(pad_hi_ref[N_GROUPS - 1], src_ref.shape[0], clear, 0)

    def place(i, carry):
        src_ref[pos_ref[i]] = i
        return carry

    lax.fori_loop(0, pos_ref.shape[0], place, 0, unroll=ROW_DMA_UNROLL)


def _invert_call(pos, pad_lo, pad_hi, n_slots):
    return pl.pallas_call(
        _invert_kernel,
        in_specs=[pl.BlockSpec(memory_space=pltpu.SMEM)] * 3,
        out_specs=pl.BlockSpec(memory_space=pltpu.SMEM),
        out_shape=jax.ShapeDtypeStruct((n_slots,), jnp.int32),
        name="moe_invert",
    )(pos, pad_lo, pad_hi)


def _gather_rows(idx_ref, base, src_ref, dst_ref, sem):
    for r in range(TILE):
        pltpu.make_async_copy(src_ref.at[pl.ds(idx_ref[base + r], 1)], dst_ref.at[pl.ds(r, 1)], sem).start()


def _wait_rows(src_ref, dst_ref, sem):
    pltpu.make_async_copy(src_ref.at[pl.ds(0, TILE)], dst_ref, sem).wait()


def _routed_kernel(grp_ref, on_ref, src_ref, h_ref, wg_ref, wu_ref, wd_ref, o_ref, buf_ref, sem):
    t = pl.program_id(0)
    n = pl.num_programs(0)
    slot = t % 2
    on = on_ref[t] > 0

    @pl.when((t == 0) & on)
    def _():
        _gather_rows(src_ref, 0, h_ref, buf_ref.at[0], sem.at[0])

    @pl.when((on_ref[jnp.minimum(t + 1, n - 1)] > 0) & (t + 1 < n))
    def _():
        _gather_rows(src_ref, (t + 1) * TILE, h_ref, buf_ref.at[1 - slot], sem.at[1 - slot])

    @pl.when(jnp.logical_not(on))
    def _():
        o_ref[...] = jnp.zeros_like(o_ref)

    @pl.when(on)
    def _():
        _wait_rows(h_ref, buf_ref.at[slot], sem.at[slot])
        h = buf_ref[slot, :, :D_MODEL].astype(BF16)
        gates = buf_ref[slot, :, D_MODEL:]
        lane = lax.broadcasted_iota(jnp.int32, (TILE, LANES), 1)
        first = grp_ref[t] * EXPERTS_PER_GROUP
        acc = None
        for e in range(EXPERTS_PER_GROUP):
            hg = jnp.dot(h, wg_ref[0, e], preferred_element_type=F32)
            hu = jnp.dot(h, wu_ref[0, e], preferred_element_type=F32)
            gate = jnp.sum(jnp.where(lane == first + e, gates, 0.0), axis=-1, keepdims=True)
            out = jnp.dot((_silu(hg) * hu).astype(BF16), wd_ref[0, e], preferred_element_type=F32) * gate
            acc = out if acc is None else acc + out
        o_ref[...] = acc


def _routed_call(tile_grp, tile_on, src, h2e, wg, wu, wd, layer):
    nts = src.shape[0] // TILE
    wspec = lambda arr: pl.BlockSpec((1,) + arr.shape[1:],
                                     lambda t, grp, on, src: (layer * N_GROUPS + grp[t], 0, 0, 0))
    return pl.pallas_call(
        _routed_kernel,
        grid_spec=pltpu.PrefetchScalarGridSpec(
            num_scalar_prefetch=3, grid=(nts,),
            in_specs=[pl.BlockSpec(memory_space=pl.ANY), wspec(wg), wspec(wu), wspec(wd)],
            out_specs=pl.BlockSpec((TILE, D_MODEL), lambda t, grp, on, src: (t, 0)),
            scratch_shapes=[pltpu.VMEM((2, TILE, H2E_COLS), F32), pltpu.SemaphoreType.DMA((2,))]),
        out_shape=jax.ShapeDtypeStruct((nts * TILE, D_MODEL), F32),
        compiler_params=_cparams(1), name="moe_routed",
    )(tile_grp, tile_on, src, h2e, wg, wu, wd)


def _combine_kernel(n_prompt_tiles, pos_ref, ysh_ref, g2_ref, routed_ref, yp_ref, ys_ref, buf_ref, sem):
    i = pl.program_id(0)
    n = pl.num_programs(0)
    slot = i % 2

    @pl.when(i == 0)
    def _():
        _gather_rows(pos_ref, 0, routed_ref, buf_ref.at[0], sem.at[0])

    @pl.when(i + 1 < n)
    def _():
        _gather_rows(pos_ref, (i + 1) * TILE, routed_ref, buf_ref.at[1 - slot], sem.at[1 - slot])

    _wait_rows(routed_ref, buf_ref.at[slot], sem.at[slot])
    y = _gated_add(ysh_ref[...], g2_ref[0], buf_ref[slot])

    @pl.when(i < n_prompt_tiles)
    def _():
        yp_ref[...] = y

    @pl.when(i >= n_prompt_tiles)
    def _():
        ys_ref[...] = y


def _combine_call(pos, ysh, g2, routed, rp):
    r = ysh.shape[0]
    rs = r - rp
    npt = rp // TILE
    last = npt - 1
    return pl.pallas_call(
        functools.partial(_combine_kernel, npt),
        grid_spec=pltpu.PrefetchScalarGridSpec(
            num_scalar_prefetch=1, grid=(r // TILE,),
            in_specs=[pl.BlockSpec((TILE, D_MODEL), lambda i, pos: (i, 0)),
                      pl.BlockSpec((1,) + g2.shape[1:], lambda i, pos: (i // npt, 0, 0)),
                      pl.BlockSpec(memory_space=pl.ANY)],
            out_specs=[pl.BlockSpec((TILE, D_MODEL), lambda i, pos: (jnp.minimum(i, last), 0)),
                       pl.BlockSpec((TILE, D_MODEL), lambda i, pos: (0, 0))],
            scratch_shapes=[pltpu.VMEM((2, TILE, D_MODEL), F32), pltpu.SemaphoreType.DMA((2,))]),
        out_shape=[jax.ShapeDtypeStruct((rp, D_MODEL), F32),
                   jax.ShapeDtypeStruct((rs, D_MODEL), F32)],
        compiler_params=_cparams(1), name="moe_combine",
    )(pos, ysh, g2, routed)


def _moe(h2e, grp, rank, counts, ysh, g2, experts, layer, rp):
    r = h2e.shape[0]
    n_sorted_tiles = r // TILE + N_GROUPS
    cnt = counts[:N_GROUPS, 0].astype(jnp.int32)
    tiles_g = (cnt + TILE - 1) // TILE
    end_g = jnp.cumsum(tiles_g)
    off_g = (end_g - tiles_g) * TILE
    grp_flat = grp.reshape(r)
    pos = rank.reshape(r)
    for g in range(N_GROUPS):
        pos = pos + jnp.where(grp_flat == g, off_g[g], 0)
    t_idx = jnp.arange(n_sorted_tiles, dtype=jnp.int32)
    tile_grp = jnp.zeros_like(t_idx)
    for g in range(N_GROUPS - 1):
        tile_grp = tile_grp + (t_idx >= end_g[g]).astype(jnp.int32)
    tile_on = (t_idx < end_g[N_GROUPS - 1]).astype(jnp.int32)
    src = _invert_call(pos, off_g + cnt, end_g * TILE, n_sorted_tiles * TILE)
    routed = _routed_call(tile_grp, tile_on, src, h2e, *experts, layer)
    return _combine_call(pos, ysh, g2, routed, rp)


def _rot_half_cols(wcols):
    half = QK_ROPE_DIM // 2
    return jnp.concatenate([-wcols[:, half:], wcols[:, :half]], axis=1)


def _pad_cols(wcols, width):
    return jnp.pad(wcols, ((0, 0), (0, width - wcols.shape[1])))


def _split_bf16(w):
    hi = w.astype(BF16)
    lo = (w - hi.astype(F32)).astype(BF16)
    return jnp.concatenate([hi, lo], axis=1)


def _rope_tables(seq, past, s_len, n_seq):
    half = QK_ROPE_DIM // 2
    inv = ROPE_THETA ** (-jnp.arange(half, dtype=F32) / half)

    def cos_sin(pos):
        ang = pos.astype(F32)[:, None] * inv[None, :]
        return jnp.cos(ang), jnp.sin(ang)

    cc, sc = cos_sin(jnp.arange(0, seq, GMLP_CHUNK, dtype=jnp.int32))
    cf, sf = cos_sin(jnp.arange(GMLP_CHUNK, dtype=jnp.int32))
    cos_p = (cc[:, None, :] * cf[None, :, :] - sc[:, None, :] * sf[None, :, :]).reshape(seq, half)
    sin_p = (sc[:, None, :] * cf[None, :, :] + cc[:, None, :] * sf[None, :, :]).reshape(seq, half)
    cos_s, sin_s = cos_sin(past + jnp.arange(s_len, dtype=jnp.int32))
    cos = jnp.concatenate([cos_p, jnp.tile(cos_s, (n_seq, 1))], axis=0)
    sin = jnp.concatenate([sin_p, jnp.tile(sin_s, (n_seq, 1))], axis=0)
    zeros = jnp.zeros((cos.shape[0], LANES - QK_ROPE_DIM), F32)
    return (jnp.concatenate([cos, cos, zeros], axis=1), jnp.concatenate([sin, sin, zeros], axis=1))


def _layer_weights(l, p, cos, sin):
    o1 = GMLP_WIDTH
    o2 = 2 * GMLP_WIDTH
    o3 = o2 + Q_LORA_RANK
    o4 = o3 + KV_LORA_RANK
    w_in = p["w_in"][l]
    kpe_cols = w_in[:, o4:]
    w_in2 = jnp.concatenate([w_in[:, :o4], _pad_cols(kpe_cols, LANES),
                             _pad_cols(_rot_half_cols(kpe_cols), LANES)], axis=1).astype(BF16)
    w_qb = p["w_qb"][l]
    q_parts = []
    for hd in range(MLA_HEADS):
        base = hd * QK_HEAD_DIM
        rope_cols = w_qb[:, base + QK_NOPE_DIM:base + QK_HEAD_DIM]
        q_parts += [w_qb[:, base:base + QK_NOPE_DIM], _pad_cols(rope_cols, LANES),
                    _pad_cols(_rot_half_cols(rope_cols), LANES)]
    w_qb2 = jnp.concatenate(q_parts, axis=1).astype(BF16)
    w_kvb = p["w_kvb"][l].reshape(KV_LORA_RANK, MLA_HEADS, QK_NOPE_DIM + V_HEAD_DIM)
    w_kvb2 = jnp.concatenate([w_kvb[:, :, :QK_NOPE_DIM].reshape(KV_LORA_RANK, -1),
                              w_kvb[:, :, QK_NOPE_DIM:].reshape(KV_LORA_RANK, -1)], axis=1).astype(BF16)
    qscale = LOG2E / math.sqrt(QK_HEAD_DIM)
    gq = _pad_cols(p["q_norm_g"][l][None, :] * qscale, QK_PAD)
    gk = _pad_cols(p["k_norm_g"][l][None, :], QK_PAD)
    bound = (QK_HEAD_DIM * qscale * SCORE_BOUND_MARGIN * jnp.max(jnp.abs(p["q_norm_g"][l]))
             * jnp.max(jnp.abs(p["k_norm_g"][l])))
    bounded = bound <= MAX_SCORE_BOUND
    pad_lane = jnp.arange(LANES) == QK_ROPE_DIM
    off = jnp.stack([jnp.where(pad_lane, 1.0, 0.0),
                     jnp.where(pad_lane & bounded, -bound, 0.0)]).astype(F32)

    ws = p["gmlp_ws"][l]
    tri = jnp.tril(jnp.ones((GMLP_CHUNK, GMLP_CHUNK), dtype=bool))
    wt = jnp.where(tri[None], ws, 0.0)
    hc = GMLP_CHUNK // 2
    top = wt[:, :hc, :hc]
    zero = jnp.zeros_like(top)
    wt_s = jnp.concatenate([jnp.concatenate([top, zero], axis=2), jnp.concatenate([zero, top], axis=2)], axis=1)
    gw = jnp.stack([wt, wt_s]).astype(BF16)
    b = p["gmlp_b"][l]
    b_s = jnp.concatenate([b[:, :hc], b[:, :hc]], axis=1)
    gb = jnp.broadcast_to(jnp.stack([b, b_s])[..., None], (2, GMLP_GROUPS, GMLP_CHUNK, LANES)).astype(F32)

    w_out = p["w_out"][l].astype(BF16)
    return dict(
        sg=p["sh_w_gate"][l].astype(BF16),
        su=p["sh_w_up"][l].astype(BF16), sd=p["sh_w_down"][l].astype(BF16),
        ln1=p["ln1_g"][l][None, :], w_in=w_in2, cos=cos, sin=sin,
        gv=p["gmlp_v_g"][l].reshape(1, GMLP_WIDTH), gw=gw, gb=gb,
        q_a_g=p["q_a_g"][l][None, :], w_qb=w_qb2, kv_a_g=p["kv_a_g"][l][None, :], w_kvb=w_kvb2,
        gq=gq, gk=gk, w_out_a=w_out[:GMLP_WIDTH], w_out_b=w_out[GMLP_WIDTH:],
        ln2=p["ln2_g"][l][None, :], rw=_split_bf16(_pad_cols(p["router_w"], LANES)),
        rb=p["router_bias"].reshape(N_EXPERTS, 1), off=off,
        bounded=bounded.astype(jnp.int32).reshape(1),
    )


def kernel(x_prompt, x_sample, cache_kv_latent, cache_k_rope, c_prompt, c_sample, w_ada, b_ada, ln1_g, w_in,
           gmlp_v_g, gmlp_ws, gmlp_b, q_a_g, w_qb, kv_a_g, w_kvb, q_norm_g, k_norm_g, w_out, ln2_g, router_w,
           router_bias, exp_w_gate, exp_w_up, exp_w_down, sh_w_gate, sh_w_up, sh_w_down):
    p = dict(w_in=w_in, gmlp_v_g=gmlp_v_g, gmlp_ws=gmlp_ws, gmlp_b=gmlp_b, q_a_g=q_a_g, w_qb=w_qb,
             kv_a_g=kv_a_g, w_kvb=w_kvb, q_norm_g=q_norm_g, k_norm_g=k_norm_g, w_out=w_out, ln1_g=ln1_g,
             ln2_g=ln2_g, router_w=router_w, router_bias=router_bias, exp_w_gate=exp_w_gate,
             exp_w_up=exp_w_up, exp_w_down=exp_w_down, sh_w_gate=sh_w_gate, sh_w_up=sh_w_up,
             sh_w_down=sh_w_down)
    batch, seq, _ = x_prompt.shape
    n_seq, s_len, _ = x_sample.shape
    depth, _, past, _ = cache_kv_latent.shape
    assert batch == 1 and s_len == SUB and n_seq == N_SUB and n_seq * s_len == TILE
    assert seq % TILE == 0 and seq % ATT_TQ == 0 and past % GMLP_CHUNK == 0 and past % CHUNK == 0
    rp = seq
    rs = n_seq * s_len

    cos, sin = _rope_tables(seq, past, s_len, n_seq)

    c_all = jnp.concatenate([jnp.broadcast_to(c_prompt, (N_SUB, D_MODEL)), c_sample], axis=0)
    mod = _ada_call(c_all, w_ada, b_ada)
    mod = mod.reshape(depth, 2, N_SUB, 6, D_MODEL)

    weights = [_layer_weights(l, p, cos, sin) for l in range(depth)]
    experts = (exp_w_gate.astype(BF16).reshape(depth * N_GROUPS, EXPERTS_PER_GROUP, D_MODEL, EXPERT_FF),
               exp_w_up.astype(BF16).reshape(depth * N_GROUPS, EXPERTS_PER_GROUP, D_MODEL, EXPERT_FF),
               exp_w_down.astype(BF16).reshape(depth * N_GROUPS, EXPERTS_PER_GROUP, EXPERT_FF, D_MODEL))
    lat_all = cache_kv_latent.reshape(depth, n_seq * past, KV_LORA_RANK)
    pe_all = jnp.swapaxes(cache_k_rope, 2, 3)
    k_past, v_past = _cache_kv_call(lat_all, pe_all, jnp.stack([w["w_kvb"] for w in weights]),
                                    jnp.stack([w["gk"] for w in weights]),
                                    jnp.stack([w["off"] for w in weights]))

    xp = x_prompt.reshape(rp, D_MODEL)
    xs = x_sample.reshape(rs, D_MODEL)
    planes = tuple(jnp.zeros((depth, rows, width), F32) for rows, width in
                   ((rp, KV_LORA_RANK), (rp, QK_ROPE_DIM), (rs, KV_LORA_RANK), (rs, QK_ROPE_DIM),
                    (rs, GMLP_WIDTH)))
    for l in range(depth):
        w = weights[l]
        sh1, sc1, g1, sh2, sc2, g2 = [mod[l, :, :, j, :] for j in range(6)]
        a, q, k, v, *planes = _mix_in_call(xp, xs, sc1, sh1, w, l, depth, tuple(planes))
        bp = _prompt_attn_call(w["bounded"], q, k, v, rp)
        bs = _sample_attn_call(l, q, k, v, k_past, v_past, rp, n_seq, s_len, past)
        ysh, h2e, grp, rank, counts = _mix_out_call(a, bp, bs, xp, xs, g1, sc2, sh2, g2, w)
        xp, xs = _moe(h2e, grp, rank, counts, ysh, g2, experts, l, rp)
    lat_p, pe_p, lat_s, pe_s, v_s = planes
    return (xp.reshape(batch, seq, D_MODEL), xs.reshape(n_seq, s_len, D_MODEL),
            lat_p.reshape(depth, batch, seq, KV_LORA_RANK), pe_p.reshape(depth, batch, seq, QK_ROPE_DIM),
            lat_s.reshape(depth, n_seq, s_len, KV_LORA_RANK), pe_s.reshape(depth, n_seq, s_len, QK_ROPE_DIM),
            v_s.reshape(depth, n_seq, s_len, GMLP_WIDTH))
```

```python
import functools
import math

import jax
import jax.numpy as jnp
from jax import lax
from jax.experimental import pallas as pl
from jax.experimental.pallas import tpu as pltpu

F32 = jnp.float32
BF16 = jnp.bfloat16

D_MODEL = 1024
CHUNK = 64
GMLP_WIDTH = 512
GMLP_GROUPS = 4
GMLP_CHUNK = 128
MLA_HEADS = 4
QK_NOPE_DIM = 128
QK_ROPE_DIM = 64
QK_HEAD_DIM = 192
V_HEAD_DIM = 128
Q_LORA_RANK = 384
KV_LORA_RANK = 256
ROPE_THETA = 10000.0
N_EXPERTS = 16
N_GROUPS = 4
EXPERTS_PER_GROUP = 4
EXPERT_FF = 512
EPS = 1e-6

LANES = 128
TILE = 512
SUB = 64
N_SUB = TILE // SUB
QK_PAD = 256
IN_COLS = 2 * GMLP_WIDTH + Q_LORA_RANK + KV_LORA_RANK + 2 * LANES
Q_COLS = MLA_HEADS * 3 * LANES
H2E_COLS = D_MODEL + LANES
ROW_DMA_UNROLL = 8
EXPERT_PAIRS = tuple((a, b) for a in range(EXPERTS_PER_GROUP) for b in range(a + 1, EXPERTS_PER_GROUP))
N_CLASSES = N_GROUPS * len(EXPERT_PAIRS)
CLASS_ROWS = 32
ATT_TK = 512
ATT_TQ = 2 * ATT_TK
ATT_UNROLL = 8
ATT_HEADS = 2
VMEM_LIMIT = 56 * 1024 * 1024
LOG2E = 1.4426950408889634
SCORE_BOUND_MARGIN = 1.02
MAX_SCORE_BOUND = 48.0


def _cparams(n_axes):
    return pltpu.CompilerParams(dimension_semantics=("arbitrary",) * n_axes,
                                vmem_limit_bytes=VMEM_LIMIT)


def _rms(x, eps=EPS):
    return x * lax.rsqrt(jnp.mean(x * x, axis=-1, keepdims=True) + eps)


def _gelu(x):
    c = math.sqrt(2.0 / math.pi)
    return 0.5 * x * (1.0 + jnp.tanh(c * (x + 0.044715 * (x * x * x))))


def _modulate(h, scale, shift):
    h3 = h.reshape(N_SUB, SUB, h.shape[-1])
    h3 = h3 * (1.0 + scale[:, None, :]) + shift[:, None, :]
    return h3.reshape(h.shape)


def _gated_add(x, gate, y):
    y3 = y.reshape(N_SUB, SUB, y.shape[-1]) * gate[:, None, :]
    return x + y3.reshape(y.shape)


def _ada_kernel(c_ref, w_ref, b_ref, o_ref):
    c = c_ref[...]
    cs = (c / (1.0 + jnp.exp(-c))).astype(BF16)
    w = w_ref[0].astype(BF16)
    o_ref[0] = jnp.dot(cs, w, preferred_element_type=F32) + b_ref[0]


def _ada_call(c_all, w_ada, b_ada):
    depth = w_ada.shape[0]
    nblk = w_ada.shape[2] // D_MODEL
    return pl.pallas_call(
        _ada_kernel,
        grid=(depth, nblk),
        in_specs=[
            pl.BlockSpec((16, D_MODEL), lambda l, j: (0, 0)),
            pl.BlockSpec((1, D_MODEL, D_MODEL), lambda l, j: (l, 0, j)),
            pl.BlockSpec((1, 1, D_MODEL), lambda l, j: (l, 0, j)),
        ],
        out_specs=pl.BlockSpec((1, 16, D_MODEL), lambda l, j: (l, 0, j)),
        out_shape=jax.ShapeDtypeStruct((depth, 16, w_ada.shape[2]), F32),
        compiler_params=_cparams(2),
        name="ada_mod",
    )(c_all, w_ada, b_ada.reshape(depth, 1, -1))


def _mix_in_kernel(n_prompt_tiles, n_prev, xp_ref, xs_ref, sc_ref, sh_ref, ln_ref, win_ref, cos_ref, sin_ref,
                   gv_ref, gw_ref, gb_ref, qag_ref, wqb_ref, kvag_ref, wkvb_ref, gq_ref, gk_ref, off_ref,
                   *refs):
    a_ref, q_ref, k_ref, v_ref, latp_ref, pep_ref, lats_ref, pes_ref, vn_ref = refs[n_prev:]
    i = pl.program_id(0)
    is_sample = i >= n_prompt_tiles
    x = jnp.where(is_sample, xs_ref[...], xp_ref[...])
    h = _rms(x) * ln_ref[...]
    h = _modulate(h, sc_ref[0], sh_ref[0])
    z = jnp.dot(h.astype(BF16), win_ref[...], preferred_element_type=F32)

    u = _gelu(z[:, :GMLP_WIDTH])
    v = _gelu(z[:, GMLP_WIDTH:2 * GMLP_WIDTH])
    gv = gv_ref[...]
    vn_parts = []
    for g in range(GMLP_GROUPS):
        sl = slice(g * LANES, (g + 1) * LANES)
        vn_parts.append(_rms(v[:, sl]) * gv[:, sl])

    @pl.when(is_sample)
    def _():
        for g in range(GMLP_GROUPS):
            vn_ref[0, :, g * LANES:(g + 1) * LANES] = vn_parts[g]

    for g in range(GMLP_GROUPS):
        sl = slice(g * LANES, (g + 1) * LANES)
        vb = vn_parts[g].astype(BF16)
        wg = gw_ref[0, g]
        bg = gb_ref[0, g]
        for c in range(TILE // GMLP_CHUNK):
            rows = slice(c * GMLP_CHUNK, (c + 1) * GMLP_CHUNK)
            s = jnp.dot(wg, vb[rows], preferred_element_type=F32) + bg
            a_ref[rows, sl] = (u[rows, sl] * s).astype(BF16)

    cos = cos_ref[...]
    sin = sin_ref[...]

    o2 = 2 * GMLP_WIDTH
    o3 = o2 + Q_LORA_RANK
    o4 = o3 + KV_LORA_RANK
    ql = _rms(z[:, o2:o3]) * qag_ref[...]
    qq = jnp.dot(ql.astype(BF16), wqb_ref[...], preferred_element_type=F32)
    gq = gq_ref[...]
    for hd in range(MLA_HEADS):
        base = hd * 3 * LANES
        nope = qq[:, base:base + LANES]
        rope = qq[:, base + LANES:base + 2 * LANES] * cos + qq[:, base + 2 * LANES:base + 3 * LANES] * sin
        ss = jnp.sum(nope * nope, axis=-1, keepdims=True) + jnp.sum(rope * rope, axis=-1, keepdims=True)
        rinv = lax.rsqrt(ss * (1.0 / QK_HEAD_DIM) + EPS)
        q_ref[hd, :, :LANES] = (nope * rinv * gq[:, :LANES]).astype(BF16)
        q_ref[hd, :, LANES:] = (rope * rinv * gq[:, LANES:] + off_ref[0:1, :]).astype(BF16)

    ckv = _rms(z[:, o3:o4]) * kvag_ref[...]
    kpe = z[:, o4:o4 + LANES] * cos + z[:, o4 + LANES:o4 + 2 * LANES] * sin

    @pl.when(jnp.logical_not(is_sample))
    def _():
        latp_ref[0] = ckv
        pep_ref[0] = kpe[:, :QK_ROPE_DIM]

    @pl.when(is_sample)
    def _():
        lats_ref[0] = ckv
        pes_ref[0] = kpe[:, :QK_ROPE_DIM]

    kv = jnp.dot(ckv.astype(BF16), wkvb_ref[...], preferred_element_type=F32)
    gk = gk_ref[...]
    kpe_ss = jnp.sum(kpe * kpe, axis=-1, keepdims=True)
    for hd in range(MLA_HEADS):
        nope = kv[:, hd * LANES:(hd + 1) * LANES]
        ss = jnp.sum(nope * nope, axis=-1, keepdims=True) + kpe_ss
        rinv = lax.rsqrt(ss * (1.0 / QK_HEAD_DIM) + EPS)
        k_ref[hd, :, :LANES] = (nope * rinv * gk[:, :LANES]).astype(BF16)
        k_ref[hd, :, LANES:] = (kpe * rinv * gk[:, LANES:] + off_ref[1:2, :]).astype(BF16)
        v_ref[hd] = kv[:, (MLA_HEADS + hd) * LANES:(MLA_HEADS + hd + 1) * LANES].astype(BF16)


def _mix_in_call(xp, xs, sc1, sh1, w, layer, depth, prev):
    rp, rs = xp.shape[0], xs.shape[0]
    npt = rp // TILE
    nt = npt + rs // TILE
    r = rp + rs
    last = npt - 1

    def full(a):
        nd = a.ndim
        return pl.BlockSpec(a.shape, lambda i: (0,) * nd)

    def variant(a):
        nd = a.ndim
        return pl.BlockSpec((1,) + a.shape[1:], lambda i: (i // npt,) + (0,) * (nd - 1))

    row = lambda width: pl.BlockSpec((TILE, width), lambda i: (i, 0))
    head = lambda width: pl.BlockSpec((MLA_HEADS, TILE, width), lambda i: (0, i, 0))
    in_specs = [
        pl.BlockSpec((TILE, D_MODEL), lambda i: (jnp.minimum(i, last), 0)),
        pl.BlockSpec((TILE, D_MODEL), lambda i: (0, 0)),
        variant(sc1), variant(sh1), full(w["ln1"]), full(w["w_in"]),
        row(LANES), row(LANES),
        full(w["gv"]), variant(w["gw"]), variant(w["gb"]),
        full(w["q_a_g"]), full(w["w_qb"]), full(w["kv_a_g"]), full(w["w_kvb"]),
        full(w["gq"]), full(w["gk"]), full(w["off"]),
    ] + [pl.BlockSpec(memory_space=pl.ANY)] * len(prev)
    prompt_plane = lambda width: pl.BlockSpec((1, TILE, width), lambda i: (layer, jnp.minimum(i, last), 0))
    sample_plane = lambda width: pl.BlockSpec((1, TILE, width), lambda i: (layer, 0, 0))
    out_specs = [
        row(GMLP_WIDTH), head(QK_PAD), head(QK_PAD), head(V_HEAD_DIM),
        prompt_plane(KV_LORA_RANK), prompt_plane(QK_ROPE_DIM),
        sample_plane(KV_LORA_RANK), sample_plane(QK_ROPE_DIM), sample_plane(GMLP_WIDTH),
    ]
    out_shape = [
        jax.ShapeDtypeStruct((r, GMLP_WIDTH), BF16),
        jax.ShapeDtypeStruct((MLA_HEADS, r, QK_PAD), BF16),
        jax.ShapeDtypeStruct((MLA_HEADS, r, QK_PAD), BF16),
        jax.ShapeDtypeStruct((MLA_HEADS, r, V_HEAD_DIM), BF16),
        jax.ShapeDtypeStruct((depth, rp, KV_LORA_RANK), F32),
        jax.ShapeDtypeStruct((depth, rp, QK_ROPE_DIM), F32),
        jax.ShapeDtypeStruct((depth, rs, KV_LORA_RANK), F32),
        jax.ShapeDtypeStruct((depth, rs, QK_ROPE_DIM), F32),
        jax.ShapeDtypeStruct((depth, rs, GMLP_WIDTH), F32),
    ]
    n_fixed = len(in_specs) - len(prev)
    n_own = 4
    return pl.pallas_call(
        functools.partial(_mix_in_kernel, npt, len(prev)),
        grid=(nt,), in_specs=in_specs, out_specs=out_specs, out_shape=out_shape,
        input_output_aliases={n_fixed + j: n_own + j for j in range(len(prev))},
        compiler_params=_cparams(1), name="mix_in",
    )(xp, xs, sc1, sh1, w["ln1"], w["w_in"], w["cos"], w["sin"], w["gv"], w["gw"], w["gb"],
      w["q_a_g"], w["w_qb"], w["kv_a_g"], w["w_kvb"], w["gq"], w["gk"], w["off"], *prev)


def _cache_kv_kernel(lat_ref, pe_ref, wkvb_ref, gk_ref, off_ref, k_ref, v_ref):
    lat = lat_ref[0]
    pe_t = pe_ref[0, 0]
    kpe = jnp.concatenate([pe_t, jnp.zeros_like(pe_t)], axis=0).T
    kv = jnp.dot(lat.astype(BF16), wkvb_ref[0], preferred_element_type=F32)
    gk = gk_ref[0]
    kpe_ss = jnp.sum(kpe * kpe, axis=-1, keepdims=True)
    for hd in range(MLA_HEADS):
        nope = kv[:, hd * LANES:(hd + 1) * LANES]
        ss = jnp.sum(nope * nope, axis=-1, keepdims=True) + kpe_ss
        rinv = lax.rsqrt(ss * (1.0 / QK_HEAD_DIM) + EPS)
        k_ref[0, hd, :, :LANES] = (nope * rinv * gk[:, :LANES]).astype(BF16)
        k_ref[0, hd, :, LANES:] = (kpe * rinv * gk[:, LANES:] + off_ref[0, 1:2, :]).astype(BF16)
        v_ref[0, hd] = kv[:, (MLA_HEADS + hd) * LANES:(MLA_HEADS + hd + 1) * LANES].astype(BF16)


def _cache_kv_call(lat, pe_t, w_kvb, gk, off):
    depth, rows, _ = lat.shape
    t = min(1024, pe_t.shape[-1])
    per_stream = pe_t.shape[-1] // t
    return pl.pallas_call(
        _cache_kv_kernel,
        grid=(depth, rows // t),
        in_specs=[
            pl.BlockSpec((1, t, KV_LORA_RANK), lambda l, i: (l, i, 0)),
            pl.BlockSpec((1, 1, QK_ROPE_DIM, t), lambda l, i: (l, i // per_stream, 0, i % per_stream)),
            pl.BlockSpec((1,) + w_kvb.shape[1:], lambda l, i: (l, 0, 0)),
            pl.BlockSpec((1, 1, QK_PAD), lambda l, i: (l, 0, 0)),
            pl.BlockSpec((1, 2, LANES), lambda l, i: (l, 0, 0)),
        ],
        out_specs=[
            pl.BlockSpec((1, MLA_HEADS, t, QK_PAD), lambda l, i: (l, 0, i, 0)),
            pl.BlockSpec((1, MLA_HEADS, t, V_HEAD_DIM), lambda l, i: (l, 0, i, 0)),
        ],
        out_shape=[
            jax.ShapeDtypeStruct((depth, MLA_HEADS, rows, QK_PAD), BF16),
            jax.ShapeDtypeStruct((depth, MLA_HEADS, rows, V_HEAD_DIM), BF16),
        ],
        compiler_params=_cparams(2), name="cache_kv",
    )(lat, pe_t, w_kvb, gk, off)


def _attn_step(q, k, v, m, l, acc, mask):
    s = lax.dot_general(q, k, (((1,), (1,)), ((), ())), preferred_element_type=F32)
    if mask is not None:
        s = jnp.where(mask, s, -1e30)
    m_new = jnp.maximum(m, jnp.max(s, axis=-1, keepdims=True))
    alpha = jnp.exp2(m - m_new)
    p = jnp.exp2(s - m_new)
    l = alpha * l + jnp.sum(p, axis=-1, keepdims=True)
    acc = alpha * acc + jnp.dot(p.astype(BF16), v, preferred_element_type=F32)
    return m_new, l, acc


def _bounded_step(q, k, v_aug, acc, mask):
    s = lax.dot_general(q, k, (((1,), (1,)), ((), ())), preferred_element_type=F32)
    if mask is not None:
        s = jnp.where(mask, s, -1e30)
    return acc + jnp.dot(jnp.exp2(s).astype(BF16), v_aug, preferred_element_type=F32)


def _prompt_attn_kernel(bounded_ref, q_ref, k_ref, v_ref, o_ref):
    qi = pl.program_id(1)
    row = lax.broadcasted_iota(jnp.int32, (ATT_TK, ATT_TK), 0)
    col = lax.broadcasted_iota(jnp.int32, (ATT_TK, ATT_TK), 1)
    mask = (col // CHUNK) <= (row // CHUNK)

    def q_half(h, half):
        return q_ref[h, half * ATT_TK:(half + 1) * ATT_TK, :]

    def kv(h, j):
        off = pl.multiple_of(j * ATT_TK, ATT_TK)
        return k_ref[h, pl.ds(off, ATT_TK), :], v_ref[h, pl.ds(off, ATT_TK), :]

    def out(h, half):
        return o_ref.at[half * ATT_TK:(half + 1) * ATT_TK, h * V_HEAD_DIM:(h + 1) * V_HEAD_DIM]

    @pl.when(bounded_ref[0] > 0)
    def _():
        lane = lax.broadcasted_iota(jnp.int32, (ATT_TK, LANES), 1)
        ones_col = jnp.where(lane == 0, 1.0, 0.0).astype(BF16)

        def kv_aug(h, j):
            k, v = kv(h, j)
            return k, jnp.concatenate([v, ones_col], axis=1)

        def blocks(first, count, accs):
            accs = [list(a) for a in accs]
            for u in range(count):
                for h in range(ATT_HEADS):
                    k, v = kv_aug(h, first + u)
                    for half in range(2):
                        accs[h][half] = _bounded_step(q_half(h, half), k, v, accs[h][half], None)
            return tuple(tuple(a) for a in accs)

        zero = jnp.zeros((ATT_TK, 2 * LANES), F32)
        n_full = 2 * qi
        n_trips = n_full // ATT_UNROLL
        accs = lax.fori_loop(0, n_trips, lambda j, c: blocks(ATT_UNROLL * j, ATT_UNROLL, c),
                             ((zero, zero),) * ATT_HEADS)
        done = ATT_UNROLL * n_trips
        part = ATT_UNROLL // 2
        while part >= 2:
            accs = lax.cond((n_full & part) != 0, lambda c, d=done, p=part: blocks(d, p, c), lambda c: c, accs)
            done = done + (n_full & part)
            part //= 2
        for h in range(ATT_HEADS):
            acc_a, acc_b = accs[h]
            k, v = kv_aug(h, 2 * qi)
            acc_a = _bounded_step(q_half(h, 0), k, v, acc_a, mask)
            acc_b = _bounded_step(q_half(h, 1), k, v, acc_b, None)
            k, v = kv_aug(h, 2 * qi + 1)
            acc_b = _bounded_step(q_half(h, 1), k, v, acc_b, mask)
            out(h, 0)[...] = (acc_a[:, :LANES] / acc_a[:, LANES:LANES + 1]).astype(BF16)
            out(h, 1)[...] = (acc_b[:, :LANES] / acc_b[:, LANES:LANES + 1]).astype(BF16)

    @pl.when(bounded_ref[0] == 0)
    def _():
        def init():
            return (jnp.full((ATT_TK, 1), -1e30, F32), jnp.zeros((ATT_TK, 1), F32),
                    jnp.zeros((ATT_TK, V_HEAD_DIM), F32))

        for h in range(ATT_HEADS):
            qa, qb = q_half(h, 0), q_half(h, 1)

            def body(j, carry, h=h, qa=qa, qb=qb):
                k, v = kv(h, j)
                return _attn_step(qa, k, v, *carry[0], None), _attn_step(qb, k, v, *carry[1], None)

            sa, sb = lax.fori_loop(0, 2 * qi, body, (init(), init()))
            k, v = kv(h, 2 * qi)
            sa = _attn_step(qa, k, v, *sa, mask)
            sb = _attn_step(qb, k, v, *sb, None)
            k, v = kv(h, 2 * qi + 1)
            sb = _attn_step(qb, k, v, *sb, mask)
            out(h, 0)[...] = (sa[2] / sa[1]).astype(BF16)
            out(h, 1)[...] = (sb[2] / sb[1]).astype(BF16)


def _prompt_attn_call(bounded, q, k, v, rp):
    r = q.shape[1]
    resident = lambda arr: pl.BlockSpec((ATT_HEADS,) + arr.shape[1:], lambda h, i, b: (h, 0, 0),
                                        pipeline_mode=pl.Buffered(1))
    return pl.pallas_call(
        _prompt_attn_kernel,
        grid_spec=pltpu.PrefetchScalarGridSpec(
            num_scalar_prefetch=1, grid=(MLA_HEADS // ATT_HEADS, rp // ATT_TQ),
            in_specs=[
                pl.BlockSpec((ATT_HEADS, ATT_TQ, QK_PAD), lambda h, i, b: (h, i, 0)),
                resident(k), resident(v),
            ],
            out_specs=pl.BlockSpec((ATT_TQ, ATT_HEADS * V_HEAD_DIM), lambda h, i, b: (i, h))),
        out_shape=jax.ShapeDtypeStruct((rp, MLA_HEADS * V_HEAD_DIM), BF16),
        compiler_params=_cparams(2), name="prompt_attn",
    )(bounded, q, k, v)


def _sample_attn_kernel(q_ref, kp_ref, vp_ref, kn_ref, vn_ref, o_ref):
    q = q_ref[0]
    nt = (((1,), (1,)), ((), ()))
    s1 = lax.dot_general(q, kp_ref[0, 0], nt, preferred_element_type=F32)
    s2 = lax.dot_general(q, kn_ref[0], nt, preferred_element_type=F32)
    m = jnp.maximum(jnp.max(s1, axis=-1, keepdims=True), jnp.max(s2, axis=-1, keepdims=True))
    p1 = jnp.exp2(s1 - m)
    p2 = jnp.exp2(s2 - m)
    l = jnp.sum(p1, axis=-1, keepdims=True) + jnp.sum(p2, axis=-1, keepdims=True)
    o = (jnp.dot(p1.astype(BF16), vp_ref[0, 0], preferred_element_type=F32)
         + jnp.dot(p2.astype(BF16), vn_ref[0], preferred_element_type=F32))
    o_ref[...] = (o / l).astype(BF16)


def _sample_attn_call(layer, q, k, v, k_past, v_past, rp, n_seq, s_len, past):
    first = rp // s_len
    return pl.pallas_call(
        _sample_attn_kernel,
        grid=(n_seq, MLA_HEADS),
        in_specs=[
            pl.BlockSpec((1, s_len, QK_PAD), lambda b, h: (h, first + b, 0)),
            pl.BlockSpec((1, 1, past, QK_PAD), lambda b, h: (layer, h, b, 0)),
            pl.BlockSpec((1, 1, past, V_HEAD_DIM), lambda b, h: (layer, h, b, 0)),
            pl.BlockSpec((1, s_len, QK_PAD), lambda b, h: (h, first + b, 0)),
            pl.BlockSpec((1, s_len, V_HEAD_DIM), lambda b, h: (h, first + b, 0)),
        ],
        out_specs=pl.BlockSpec((s_len, V_HEAD_DIM), lambda b, h: (b, h)),
        out_shape=jax.ShapeDtypeStruct((n_seq * s_len, MLA_HEADS * V_HEAD_DIM), BF16),
        compiler_params=_cparams(2), name="sample_attn",
    )(q, k_past, v_past, k, v)


def _route(logits_t, bias_col):
    scores = 1.0 / (1.0 + jnp.exp(-logits_t))
    biased = scores + bias_col
    s_rows = [scores[e:e + 1, :] for e in range(N_EXPERTS)]
    b_rows = [biased[e:e + 1, :] for e in range(N_EXPERTS)]
    group_score = []
    for g in range(N_GROUPS):
        rows = b_rows[g * EXPERTS_PER_GROUP:(g + 1) * EXPERTS_PER_GROUP]
        best = None
        for a in range(EXPERTS_PER_GROUP):
            for b in range(a + 1, EXPERTS_PER_GROUP):
                pair = rows[a] + rows[b]
                best = pair if best is None else jnp.maximum(best, pair)
        group_score.append(best)
    best_group = jnp.zeros_like(group_score[0], dtype=jnp.int32)
    best_val = group_score[0]
    for g in range(1, N_GROUPS):
        better = group_score[g] > best_val
        best_group = jnp.where(better, g, best_group)
        best_val = jnp.where(better, group_score[g], best_val)
    selected = []
    for e in range(N_EXPERTS):
        g = e // EXPERTS_PER_GROUP
        rank = jnp.zeros_like(best_group)
        for j in range(g * EXPERTS_PER_GROUP, (g + 1) * EXPERTS_PER_GROUP):
            if j == e:
                continue
            ahead = b_rows[j] > b_rows[e]
            if j < e:
                ahead = ahead | (b_rows[j] == b_rows[e])
            rank = rank + ahead.astype(jnp.int32)
        selected.append((best_group == g) & (rank < 2))
    denom = jnp.zeros_like(s_rows[0])
    for e in range(N_EXPERTS):
        denom = denom + jnp.where(selected[e], s_rows[e], 0.0)
    classes = [selected[g * EXPERTS_PER_GROUP + a] & selected[g * EXPERTS_PER_GROUP + b]
               for g in range(N_GROUPS) for a, b in EXPERT_PAIRS]
    return [jnp.where(selected[e], s_rows[e] / denom, 0.0) for e in range(N_EXPERTS)], classes


def _mix_out_kernel(n_prompt_tiles, a_ref, bp_ref, bs_ref, xp_ref, xs_ref, wa_ref, wb_ref, g1_ref, sc_ref,
                    sh_ref, g2_ref, ln_ref, rw_ref, rb_ref, sg_ref, su_ref, sd_ref,
                    ysh_ref, h2e_ref, grp_ref, rank_ref, cnt_ref, gt_ref, oh_ref, carry_ref):
    i = pl.program_id(0)
    is_sample = i >= n_prompt_tiles
    x = jnp.where(is_sample, xs_ref[...], xp_ref[...])
    b = jnp.where(is_sample, bs_ref[...], bp_ref[...])
    mix = (jnp.dot(a_ref[...], wa_ref[...], preferred_element_type=F32)
           + jnp.dot(b, wb_ref[...], preferred_element_type=F32))
    xm = _gated_add(x, g1_ref[0], mix)
    h2 = _modulate(_rms(xm) * ln_ref[...], sc_ref[0], sh_ref[0])
    h2e_ref[:, :D_MODEL] = h2
    h2_hi = h2.astype(BF16)
    act = _silu(jnp.dot(h2_hi, sg_ref[...], preferred_element_type=F32)) * jnp.dot(
        h2_hi, su_ref[...], preferred_element_type=F32)
    shared = jnp.dot(act.astype(BF16), sd_ref[...], preferred_element_type=F32)
    ysh_ref[...] = _gated_add(xm, g2_ref[0], shared)
    h2_lo = (h2 - h2_hi.astype(F32)).astype(BF16)
    by_hi = jnp.dot(h2_hi, rw_ref[...], preferred_element_type=F32)
    logits = (by_hi[:, :LANES] + by_hi[:, LANES:]
              + jnp.dot(h2_lo, rw_ref[:, :LANES], preferred_element_type=F32))
    gate_rows, classes = _route(logits.T[:N_EXPERTS, :], rb_ref[...])
    gt_ref[...] = jnp.zeros_like(gt_ref)
    for e in range(N_EXPERTS):
        gt_ref[e:e + 1, :] = gate_rows[e]
    h2e_ref[:, D_MODEL:] = gt_ref[...].T

    @pl.when(i == 0)
    def _():
        carry_ref[...] = jnp.zeros_like(carry_ref)

    oh_ref[...] = jnp.zeros_like(oh_ref)
    for c in range(N_CLASSES):
        oh_ref[c:c + 1, :] = jnp.where(classes[c], 1.0, 0.0)
    onehot = oh_ref[...]
    r_idx = lax.broadcasted_iota(jnp.int32, (TILE, TILE), 0)
    c_idx = lax.broadcasted_iota(jnp.int32, (TILE, TILE), 1)
    upper = jnp.where(r_idx <= c_idx, 1.0, 0.0).astype(BF16)
    cum = jnp.dot(onehot.astype(BF16), upper, preferred_element_type=F32)
    carry = carry_ref[...]
    rank = jnp.sum(onehot * (cum - 1.0 + carry[:, :1]), axis=0, keepdims=True)
    cls_id = lax.broadcasted_iota(jnp.int32, (CLASS_ROWS, TILE), 0).astype(F32)
    grp_ref[0] = jnp.sum(onehot * cls_id, axis=0, keepdims=True).astype(jnp.int32)
    rank_ref[0] = rank.astype(jnp.int32)
    carry = carry + jnp.sum(onehot, axis=1, keepdims=True)
    carry_ref[...] = carry
    cnt_ref[...] = carry


def _mix_out_call(a, bp, bs, xp, xs, g1, sc2, sh2, g2, w):
    rp, rs = xp.shape[0], xs.shape[0]
    npt = rp // TILE
    nt = npt + rs // TILE
    r = rp + rs
    last = npt - 1

    def full(arr):
        nd = arr.ndim
        return pl.BlockSpec(arr.shape, lambda i: (0,) * nd)

    def variant(arr):
        nd = arr.ndim
        return pl.BlockSpec((1,) + arr.shape[1:], lambda i: (i // npt,) + (0,) * (nd - 1))

    row = lambda width: pl.BlockSpec((TILE, width), lambda i: (i, 0))
    prow = lambda width: pl.BlockSpec((TILE, width), lambda i: (jnp.minimum(i, last), 0))
    srow = lambda width: pl.BlockSpec((TILE, width), lambda i: (0, 0))
    return pl.pallas_call(
        functools.partial(_mix_out_kernel, npt),
        grid=(nt,),
        in_specs=[row(GMLP_WIDTH), prow(GMLP_WIDTH), srow(GMLP_WIDTH), prow(D_MODEL), srow(D_MODEL),
                  full(w["w_out_a"]), full(w["w_out_b"]), variant(g1), variant(sc2), variant(sh2),
                  variant(g2), full(w["ln2"]), full(w["rw"]), full(w["rb"]),
                  full(w["sg"]), full(w["su"]), full(w["sd"])],
        out_specs=[row(D_MODEL), row(H2E_COLS),
                   pl.BlockSpec((1, 1, TILE), lambda i: (i, 0, 0)),
                   pl.BlockSpec((1, 1, TILE), lambda i: (i, 0, 0)),
                   pl.BlockSpec((CLASS_ROWS, LANES), lambda i: (0, 0))],
        out_shape=[jax.ShapeDtypeStruct((r, D_MODEL), F32),
                   jax.ShapeDtypeStruct((r, H2E_COLS), F32),
                   jax.ShapeDtypeStruct((nt, 1, TILE), jnp.int32),
                   jax.ShapeDtypeStruct((nt, 1, TILE), jnp.int32),
                   jax.ShapeDtypeStruct((CLASS_ROWS, LANES), F32)],
        scratch_shapes=[pltpu.VMEM((LANES, TILE), F32), pltpu.VMEM((CLASS_ROWS, TILE), F32),
                        pltpu.VMEM((CLASS_ROWS, LANES), F32)],
        compiler_params=_cparams(1), name="mix_out",
    )(a, bp, bs, xp, xs, w["w_out_a"], w["w_out_b"], g1, sc2, sh2, g2, w["ln2"], w["rw"], w["rb"],
      w["sg"], w["su"], w["sd"])


def _silu(x):
    return x / (1.0 + jnp.exp(-x))


def _invert_kernel(pos_ref, pad_lo_ref, pad_hi_ref, src_ref):
    def clear(s, carry):
        src_ref[s] = 0
        return carry

    for g in range(N_GROUPS):
        lax.fori_loop(pad_lo_ref[g], pad_hi_ref[g], clear, 0)
    lax.fori_loop(pad_hi_ref[N_GROUPS - 1], src_ref.shape[0], clear, 0)

    def place(i, carry):
        src_ref[pos_ref[i]] = i
        return carry

    lax.fori_loop(0, pos_ref.shape[0], place, 0, unroll=ROW_DMA_UNROLL)


def _invert_call(pos, pad_lo, pad_hi, n_slots):
    return pl.pallas_call(
        _invert_kernel,
        in_specs=[pl.BlockSpec(memory_space=pltpu.SMEM)] * 3,
        out_specs=pl.BlockSpec(memory_space=pltpu.SMEM),
        out_shape=jax.ShapeDtypeStruct((n_slots,), jnp.int32),
        name="moe_invert",
    )(pos, pad_lo, pad_hi)


def _gather_rows(idx_ref, base, src_ref, dst_ref, sem):
    for r in range(TILE):
        pltpu.make_async_copy(src_ref.at[pl.ds(idx_ref[base + r], 1)], dst_ref.at[pl.ds(r, 1)], sem).start()


def _wait_rows(src_ref, dst_ref, sem):
    pltpu.make_async_copy(src_ref.at[pl.ds(0, TILE)], dst_ref, sem).wait()


def _routed_kernel(grp_ref, on_ref, need_ref, src_ref, h_ref, wg_ref, wu_ref, wd_ref, o_ref, buf_ref, sem):
    t = pl.program_id(0)
    n = pl.num_programs(0)
    slot = t % 2
    on = on_ref[t] > 0

    @pl.when((t == 0) & on)
    def _():
        _gather_rows(src_ref, 0, h_ref, buf_ref.at[0], sem.at[0])

    @pl.when((on_ref[jnp.minimum(t + 1, n - 1)] > 0) & (t + 1 < n))
    def _():
        _gather_rows(src_ref, (t + 1) * TILE, h_ref, buf_ref.at[1 - slot], sem.at[1 - slot])

    @pl.when(jnp.logical_not(on))
    def _():
        o_ref[...] = jnp.zeros_like(o_ref)

    @pl.when(on)
    def _():
        _wait_rows(h_ref, buf_ref.at[slot], sem.at[slot])
        first = grp_ref[t] * EXPERTS_PER_GROUP
        o_ref[...] = jnp.zeros_like(o_ref)
        for e in range(EXPERTS_PER_GROUP):
            @pl.when(need_ref[t * EXPERTS_PER_GROUP + e] > 0)
            def _(e=e):
                h = buf_ref[slot, :, :D_MODEL].astype(BF16)
                hg = jnp.dot(h, wg_ref[0, e], preferred_element_type=F32)
                hu = jnp.dot(h, wu_ref[0, e], preferred_element_type=F32)
                lane = lax.broadcasted_iota(jnp.int32, (TILE, LANES), 1)
                gate = jnp.sum(jnp.where(lane == first + e, buf_ref[slot, :, D_MODEL:], 0.0),
                               axis=-1, keepdims=True)
                o_ref[...] += jnp.dot((_silu(hg) * hu).astype(BF16), wd_ref[0, e],
                                      preferred_element_type=F32) * gate


def _routed_call(tile_grp, tile_on, tile_need, src, h2e, wg, wu, wd, layer):
    nts = src.shape[0] // TILE
    wspec = lambda arr: pl.BlockSpec((1,) + arr.shape[1:],
                                     lambda t, grp, on, need, src: (layer * N_GROUPS + grp[t], 0, 0, 0))
    return pl.pallas_call(
        _routed_kernel,
        grid_spec=pltpu.PrefetchScalarGridSpec(
            num_scalar_prefetch=4, grid=(nts,),
            in_specs=[pl.BlockSpec(memory_space=pl.ANY), wspec(wg), wspec(wu), wspec(wd)],
            out_specs=pl.BlockSpec((TILE, D_MODEL), lambda t, grp, on, need, src: (t, 0)),
            scratch_shapes=[pltpu.VMEM((2, TILE, H2E_COLS), F32), pltpu.SemaphoreType.DMA((2,))]),
        out_shape=jax.ShapeDtypeStruct((nts * TILE, D_MODEL), F32),
        compiler_params=_cparams(1), name="moe_routed",
    )(tile_grp, tile_on, tile_need, src, h2e, wg, wu, wd)


def _combine_kernel(n_prompt_tiles, pos_ref, ysh_ref, g2_ref, routed_ref, yp_ref, ys_ref, buf_ref, sem):
    i = pl.program_id(0)
    n = pl.num_programs(0)
    slot = i % 2

    @pl.when(i == 0)
    def _():
        _gather_rows(pos_ref, 0, routed_ref, buf_ref.at[0], sem.at[0])

    @pl.when(i + 1 < n)
    def _():
        _gather_rows(pos_ref, (i + 1) * TILE, routed_ref, buf_ref.at[1 - slot], sem.at[1 - slot])

    _wait_rows(routed_ref, buf_ref.at[slot], sem.at[slot])
    y = _gated_add(ysh_ref[...], g2_ref[0], buf_ref[slot])

    @pl.when(i < n_prompt_tiles)
    def _():
        yp_ref[...] = y

    @pl.when(i >= n_prompt_tiles)
    def _():
        ys_ref[...] = y


def _combine_call(pos, ysh, g2, routed, rp):
    r = ysh.shape[0]
    rs = r - rp
    npt = rp // TILE
    last = npt - 1
    return pl.pallas_call(
        functools.partial(_combine_kernel, npt),
        grid_spec=pltpu.PrefetchScalarGridSpec(
            num_scalar_prefetch=1, grid=(r // TILE,),
            in_specs=[pl.BlockSpec((TILE, D_MODEL), lambda i, pos: (i, 0)),
                      pl.BlockSpec((1,) + g2.shape[1:], lambda i, pos: (i // npt, 0, 0)),
                      pl.BlockSpec(memory_space=pl.ANY)],
            out_specs=[pl.BlockSpec((TILE, D_MODEL), lambda i, pos: (jnp.minimum(i, last), 0)),
                       pl.BlockSpec((TILE, D_MODEL), lambda i, pos: (0, 0))],
            scratch_shapes=[pltpu.VMEM((2, TILE, D_MODEL), F32), pltpu.SemaphoreType.DMA((2,))]),
        out_shape=[jax.ShapeDtypeStruct((rp, D_MODEL), F32),
                   jax.ShapeDtypeStruct((rs, D_MODEL), F32)],
        compiler_params=_cparams(1), name="moe_combine",
    )(pos, ysh, g2, routed)


def _moe(h2e, cls, rank, counts, ysh, g2, experts, layer, rp):
    r = h2e.shape[0]
    n_pairs = len(EXPERT_PAIRS)
    n_sorted_tiles = r // TILE + N_GROUPS
    cnt_c = counts[:N_CLASSES, 0].astype(jnp.int32).reshape(N_GROUPS, n_pairs)
    cnt = jnp.sum(cnt_c, axis=1)
    tiles_g = (cnt + TILE - 1) // TILE
    end_g = jnp.cumsum(tiles_g)
    off_g = (end_g - tiles_g) * TILE
    start_c = (off_g[:, None] + jnp.cumsum(cnt_c, axis=1) - cnt_c).reshape(N_CLASSES)
    end_c = start_c + cnt_c.reshape(N_CLASSES)
    cls_flat = cls.reshape(r)
    pos = rank.reshape(r)
    for c in range(N_CLASSES):
        pos = pos + jnp.where(cls_flat == c, start_c[c], 0)
    t_idx = jnp.arange(n_sorted_tiles, dtype=jnp.int32)
    tile_grp = jnp.zeros_like(t_idx)
    for g in range(N_GROUPS - 1):
        tile_grp = tile_grp + (t_idx >= end_g[g]).astype(jnp.int32)
    tile_on = (t_idx < end_g[N_GROUPS - 1]).astype(jnp.int32)
    lo = t_idx[:, None] * TILE
    overlap = ((start_c[None, :] < lo + TILE) & (end_c[None, :] > lo)
               & (end_c > start_c)[None, :])
    member = jnp.array([[int(e in pair) for e in range(EXPERTS_PER_GROUP)] for pair in EXPERT_PAIRS] * N_GROUPS,
                       dtype=jnp.int32)
    tile_need = jnp.max(overlap[:, :, None].astype(jnp.int32) * member[None], axis=1).reshape(-1)
    src = _invert_call(pos, off_g + cnt, end_g * TILE, n_sorted_tiles * TILE)
    routed = _routed_call(tile_grp, tile_on, tile_need, src, h2e, *experts, layer)
    return _combine_call(pos, ysh, g2, routed, rp)


def _rot_half_cols(wcols):
    half = QK_ROPE_DIM // 2
    return jnp.concatenate([-wcols[:, half:], wcols[:, :half]], axis=1)


def _pad_cols(wcols, width):
    return jnp.pad(wcols, ((0, 0), (0, width - wcols.shape[1])))


def _split_bf16(w):
    hi = w.astype(BF16)
    lo = (w - hi.astype(F32)).astype(BF16)
    return jnp.concatenate([hi, lo], axis=1)


def _rope_tables(seq, past, s_len, n_seq):
    half = QK_ROPE_DIM // 2
    inv = ROPE_THETA ** (-jnp.arange(half, dtype=F32) / half)

    def cos_sin(pos):
        ang = pos.astype(F32)[:, None] * inv[None, :]
        return jnp.cos(ang), jnp.sin(ang)

    cc, sc = cos_sin(jnp.arange(0, seq, GMLP_CHUNK, dtype=jnp.int32))
    cf, sf = cos_sin(jnp.arange(GMLP_CHUNK, dtype=jnp.int32))
    cos_p = (cc[:, None, :] * cf[None, :, :] - sc[:, None, :] * sf[None, :, :]).reshape(seq, half)
    sin_p = (sc[:, None, :] * cf[None, :, :] + cc[:, None, :] * sf[None, :, :]).reshape(seq, half)
    cos_s, sin_s = cos_sin(past + jnp.arange(s_len, dtype=jnp.int32))
    cos = jnp.concatenate([cos_p, jnp.tile(cos_s, (n_seq, 1))], axis=0)
    sin = jnp.concatenate([sin_p, jnp.tile(sin_s, (n_seq, 1))], axis=0)
    zeros = jnp.zeros((cos.shape[0], LANES - QK_ROPE_DIM), F32)
    return (jnp.concatenate([cos, cos, zeros], axis=1), jnp.concatenate([sin, sin, zeros], axis=1))


def _layer_weights(l, p, cos, sin):
    o1 = GMLP_WIDTH
    o2 = 2 * GMLP_WIDTH
    o3 = o2 + Q_LORA_RANK
    o4 = o3 + KV_LORA_RANK
    w_in = p["w_in"][l]
    kpe_cols = w_in[:, o4:]
    w_in2 = jnp.concatenate([w_in[:, :o4], _pad_cols(kpe_cols, LANES),
                             _pad_cols(_rot_half_cols(kpe_cols), LANES)], axis=1).astype(BF16)
    w_qb = p["w_qb"][l]
    q_parts = []
    for hd in range(MLA_HEADS):
        base = hd * QK_HEAD_DIM
        rope_cols = w_qb[:, base + QK_NOPE_DIM:base + QK_HEAD_DIM]
        q_parts += [w_qb[:, base:base + QK_NOPE_DIM], _pad_cols(rope_cols, LANES),
                    _pad_cols(_rot_half_cols(rope_cols), LANES)]
    w_qb2 = jnp.concatenate(q_parts, axis=1).astype(BF16)
    w_kvb = p["w_kvb"][l].reshape(KV_LORA_RANK, MLA_HEADS, QK_NOPE_DIM + V_HEAD_DIM)
    w_kvb2 = jnp.concatenate([w_kvb[:, :, :QK_NOPE_DIM].reshape(KV_LORA_RANK, -1),
                              w_kvb[:, :, QK_NOPE_DIM:].reshape(KV_LORA_RANK, -1)], axis=1).astype(BF16)
    qscale = LOG2E / math.sqrt(QK_HEAD_DIM)
    gq = _pad_cols(p["q_norm_g"][l][None, :] * qscale, QK_PAD)
    gk = _pad_cols(p["k_norm_g"][l][None, :], QK_PAD)
    bound = (QK_HEAD_DIM * qscale * SCORE_BOUND_MARGIN * jnp.max(jnp.abs(p["q_norm_g"][l]))
             * jnp.max(jnp.abs(p["k_norm_g"][l])))
    bounded = bound <= MAX_SCORE_BOUND
    pad_lane = jnp.arange(LANES) == QK_ROPE_DIM
    off = jnp.stack([jnp.where(pad_lane, 1.0, 0.0),
                     jnp.where(pad_lane & bounded, -bound, 0.0)]).astype(F32)

    ws = p["gmlp_ws"][l]
    tri = jnp.tril(jnp.ones((GMLP_CHUNK, GMLP_CHUNK), dtype=bool))
    wt = jnp.where(tri[None], ws, 0.0)
    hc = GMLP_CHUNK // 2
    top = wt[:, :hc, :hc]
    zero = jnp.zeros_like(top)
    wt_s = jnp.concatenate([jnp.concatenate([top, zero], axis=2), jnp.concatenate([zero, top], axis=2)], axis=1)
    gw = jnp.stack([wt, wt_s]).astype(BF16)
    b = p["gmlp_b"][l]
    b_s = jnp.concatenate([b[:, :hc], b[:, :hc]], axis=1)
    gb = jnp.broadcast_to(jnp.stack([b, b_s])[..., None], (2, GMLP_GROUPS, GMLP_CHUNK, LANES)).astype(F32)

    w_out = p["w_out"][l].astype(BF16)
    return dict(
        sg=p["sh_w_gate"][l].astype(BF16),
        su=p["sh_w_up"][l].astype(BF16), sd=p["sh_w_down"][l].astype(BF16),
        ln1=p["ln1_g"][l][None, :], w_in=w_in2, cos=cos, sin=sin,
        gv=p["gmlp_v_g"][l].reshape(1, GMLP_WIDTH), gw=gw, gb=gb,
        q_a_g=p["q_a_g"][l][None, :], w_qb=w_qb2, kv_a_g=p["kv_a_g"][l][None, :], w_kvb=w_kvb2,
        gq=gq, gk=gk, w_out_a=w_out[:GMLP_WIDTH], w_out_b=w_out[GMLP_WIDTH:],
        ln2=p["ln2_g"][l][None, :], rw=_split_bf16(_pad_cols(p["router_w"], LANES)),
        rb=p["router_bias"].reshape(N_EXPERTS, 1), off=off,
        bounded=bounded.astype(jnp.int32).reshape(1),
    )


def kernel(x_prompt, x_sample, cache_kv_latent, cache_k_rope, c_prompt, c_sample, w_ada, b_ada, ln1_g, w_in,
           gmlp_v_g, gmlp_ws, gmlp_b, q_a_g, w_qb, kv_a_g, w_kvb, q_norm_g, k_norm_g, w_out, ln2_g, router_w,
           router_bias, exp_w_gate, exp_w_up, exp_w_down, sh_w_gate, sh_w_up, sh_w_down):
    p = dict(w_in=w_in, gmlp_v_g=gmlp_v_g, gmlp_ws=gmlp_ws, gmlp_b=gmlp_b, q_a_g=q_a_g, w_qb=w_qb,
             kv_a_g=kv_a_g, w_kvb=w_kvb, q_norm_g=q_norm_g, k_norm_g=k_norm_g, w_out=w_out, ln1_g=ln1_g,
             ln2_g=ln2_g, router_w=router_w, router_bias=router_bias, exp_w_gate=exp_w_gate,
             exp_w_up=exp_w_up, exp_w_down=exp_w_down, sh_w_gate=sh_w_gate, sh_w_up=sh_w_up,
             sh_w_down=sh_w_down)
    batch, seq, _ = x_prompt.shape
    n_seq, s_len, _ = x_sample.shape
    depth, _, past, _ = cache_kv_latent.shape
    assert batch == 1 and s_len == SUB and n_seq == N_SUB and n_seq * s_len == TILE
    assert seq % TILE == 0 and seq % ATT_TQ == 0 and past % GMLP_CHUNK == 0 and past % CHUNK == 0
    rp = seq
    rs = n_seq * s_len

    cos, sin = _rope_tables(seq, past, s_len, n_seq)

    c_all = jnp.concatenate([jnp.broadcast_to(c_prompt, (N_SUB, D_MODEL)), c_sample], axis=0)
    mod = _ada_call(c_all, w_ada, b_ada)
    mod = mod.reshape(depth, 2, N_SUB, 6, D_MODEL)

    weights = [_layer_weights(l, p, cos, sin) for l in range(depth)]
    experts = (exp_w_gate.astype(BF16).reshape(depth * N_GROUPS, EXPERTS_PER_GROUP, D_MODEL, EXPERT_FF),
               exp_w_up.astype(BF16).reshape(depth * N_GROUPS, EXPERTS_PER_GROUP, D_MODEL, EXPERT_FF),
               exp_w_down.astype(BF16).reshape(depth * N_GROUPS, EXPERTS_PER_GROUP, EXPERT_FF, D_MODEL))
    lat_all = cache_kv_latent.reshape(depth, n_seq * past, KV_LORA_RANK)
    pe_all = jnp.swapaxes(cache_k_rope, 2, 3)
    k_past, v_past = _cache_kv_call(lat_all, pe_all, jnp.stack([w["w_kvb"] for w in weights]),
                                    jnp.stack([w["gk"] for w in weights]),
                                    jnp.stack([w["off"] for w in weights]))

    xp = x_prompt.reshape(rp, D_MODEL)
    xs = x_sample.reshape(rs, D_MODEL)
    planes = tuple(jnp.zeros((depth, rows, width), F32) for rows, width in
                   ((rp, KV_LORA_RANK), (rp, QK_ROPE_DIM), (rs, KV_LORA_RANK), (rs, QK_ROPE_DIM),
                    (rs, GMLP_WIDTH)))
    for l in range(depth):
        w = weights[l]
        sh1, sc1, g1, sh2, sc2, g2 = [mod[l, :, :, j, :] for j in range(6)]
        a, q, k, v, *planes = _mix_in_call(xp, xs, sc1, sh1, w, l, depth, tuple(planes))
        bp = _prompt_attn_call(w["bounded"], q, k, v, rp)
        bs = _sample_attn_call(l, q, k, v, k_past, v_past, rp, n_seq, s_len, past)
        ysh, h2e, grp, rank, counts = _mix_out_call(a, bp, bs, xp, xs, g1, sc2, sh2, g2, w)
        xp, xs = _moe(h2e, grp, rank, counts, ysh, g2, experts, l, rp)
    lat_p, pe_p, lat_s, pe_s, v_s = planes
    return (xp.reshape(batch, seq, D_MODEL), xs.reshape(n_seq, s_len, D_MODEL),
            lat_p.reshape(depth, batch, seq, KV_LORA_RANK), pe_p.reshape(depth, batch, seq, QK_ROPE_DIM),
            lat_s.reshape(depth, n_seq, s_len, KV_LORA_RANK), pe_s.reshape(depth, n_seq, s_len, QK_ROPE_DIM),
            v_s.reshape(depth, n_seq, s_len, GMLP_WIDTH))
```

```python
import functools
import math

import jax
import jax.numpy as jnp
from jax import lax
from jax.experimental import pallas as pl
from jax.experimental.pallas import tpu as pltpu

F32 = jnp.float32
BF16 = jnp.bfloat16

D_MODEL = 1024
CHUNK = 64
GMLP_WIDTH = 512
GMLP_GROUPS = 4
GMLP_CHUNK = 128
MLA_HEADS = 4
QK_NOPE_DIM = 128
QK_ROPE_DIM = 64
QK_HEAD_DIM = 192
V_HEAD_DIM = 128
Q_LORA_RANK = 384
KV_LORA_RANK = 256
ROPE_THETA = 10000.0
N_EXPERTS = 16
N_GROUPS = 4
EXPERTS_PER_GROUP = 4
EXPERT_FF = 512
EPS = 1e-6

LANES = 128
TILE = 512
SUB = 64
N_SUB = TILE // SUB
QK_PAD = 256
IN_COLS = 2 * GMLP_WIDTH + Q_LORA_RANK + KV_LORA_RANK + 2 * LANES
Q_COLS = MLA_HEADS * 3 * LANES
H2E_COLS = D_MODEL + LANES
ROW_DMA_UNROLL = 8
EXPERT_PAIRS = tuple((a, b) for a in range(EXPERTS_PER_GROUP) for b in range(a + 1, EXPERTS_PER_GROUP))
N_CLASSES = N_GROUPS * len(EXPERT_PAIRS)
CLASS_ROWS = 32
ATT_TK = 512
ATT_TQ = 2 * ATT_TK
ATT_UNROLL = 8
ATT_HEADS = 2
VMEM_LIMIT = 56 * 1024 * 1024
LOG2E = 1.4426950408889634
SCORE_BOUND_MARGIN = 1.02
MAX_SCORE_BOUND = 48.0


def _cparams(n_axes):
    return pltpu.CompilerParams(dimension_semantics=("arbitrary",) * n_axes,
                                vmem_limit_bytes=VMEM_LIMIT)


def _rms(x, eps=EPS):
    return x * lax.rsqrt(jnp.mean(x * x, axis=-1, keepdims=True) + eps)


def _gelu(x):
    c = math.sqrt(2.0 / math.pi)
    return 0.5 * x * (1.0 + jnp.tanh(c * (x + 0.044715 * (x * x * x))))


def _modulate(h, scale, shift):
    h3 = h.reshape(N_SUB, SUB, h.shape[-1])
    h3 = h3 * (1.0 + scale[:, None, :]) + shift[:, None, :]
    return h3.reshape(h.shape)


def _gated_add(x, gate, y):
    y3 = y.reshape(N_SUB, SUB, y.shape[-1]) * gate[:, None, :]
    return x + y3.reshape(y.shape)


def _ada_kernel(c_ref, w_ref, b_ref, o_ref):
    c = c_ref[...]
    cs = (c / (1.0 + jnp.exp(-c))).astype(BF16)
    w = w_ref[0].astype(BF16)
    o_ref[0] = jnp.dot(cs, w, preferred_element_type=F32) + b_ref[0]


def _ada_call(c_all, w_ada, b_ada):
    depth = w_ada.shape[0]
    nblk = w_ada.shape[2] // D_MODEL
    return pl.pallas_call(
        _ada_kernel,
        grid=(depth, nblk),
        in_specs=[
            pl.BlockSpec((16, D_MODEL), lambda l, j: (0, 0)),
            pl.BlockSpec((1, D_MODEL, D_MODEL), lambda l, j: (l, 0, j)),
            pl.BlockSpec((1, 1, D_MODEL), lambda l, j: (l, 0, j)),
        ],
        out_specs=pl.BlockSpec((1, 16, D_MODEL), lambda l, j: (l, 0, j)),
        out_shape=jax.ShapeDtypeStruct((depth, 16, w_ada.shape[2]), F32),
        compiler_params=_cparams(2),
        name="ada_mod",
    )(c_all, w_ada, b_ada.reshape(depth, 1, -1))


def _mix_in_kernel(n_prompt_tiles, n_prev, xp_ref, xs_ref, sc_ref, sh_ref, ln_ref, win_ref, cos_ref, sin_ref,
                   gv_ref, gw_ref, gb_ref, qag_ref, wqb_ref, kvag_ref, wkvb_ref, gq_ref, gk_ref, off_ref,
                   *refs):
    a_ref, q_ref, k_ref, v_ref, latp_ref, pep_ref, lats_ref, pes_ref, vn_ref = refs[n_prev:]
    i = pl.program_id(0)
    is_sample = i >= n_prompt_tiles
    x = jnp.where(is_sample, xs_ref[...], xp_ref[...])
    h = _rms(x) * ln_ref[...]
    h = _modulate(h, sc_ref[0], sh_ref[0])
    z = jnp.dot(h.astype(BF16), win_ref[...], preferred_element_type=F32)

    u = _gelu(z[:, :GMLP_WIDTH])
    v = _gelu(z[:, GMLP_WIDTH:2 * GMLP_WIDTH])
    gv = gv_ref[...]
    vn_parts = []
    for g in range(GMLP_GROUPS):
        sl = slice(g * LANES, (g + 1) * LANES)
        vn_parts.append(_rms(v[:, sl]) * gv[:, sl])

    @pl.when(is_sample)
    def _():
        for g in range(GMLP_GROUPS):
            vn_ref[0, :, g * LANES:(g + 1) * LANES] = vn_parts[g]

    for g in range(GMLP_GROUPS):
        sl = slice(g * LANES, (g + 1) * LANES)
        vb = vn_parts[g].astype(BF16)
        wg = gw_ref[0, g]
        bg = gb_ref[0, g]
        for c in range(TILE // GMLP_CHUNK):
            rows = slice(c * GMLP_CHUNK, (c + 1) * GMLP_CHUNK)
            s = jnp.dot(wg, vb[rows], preferred_element_type=F32) + bg
            a_ref[rows, sl] = (u[rows, sl] * s).astype(BF16)

    cos = cos_ref[...]
    sin = sin_ref[...]

    o2 = 2 * GMLP_WIDTH
    o3 = o2 + Q_LORA_RANK
    o4 = o3 + KV_LORA_RANK
    ql = _rms(z[:, o2:o3]) * qag_ref[...]
    qq = jnp.dot(ql.astype(BF16), wqb_ref[...], preferred_element_type=F32)
    gq = gq_ref[...]
    for hd in range(MLA_HEADS):
        base = hd * 3 * LANES
        nope = qq[:, base:base + LANES]
        rope = qq[:, base + LANES:base + 2 * LANES] * cos + qq[:, base + 2 * LANES:base + 3 * LANES] * sin
        ss = jnp.sum(nope * nope, axis=-1, keepdims=True) + jnp.sum(rope * rope, axis=-1, keepdims=True)
        rinv = lax.rsqrt(ss * (1.0 / QK_HEAD_DIM) + EPS)
        q_ref[hd, :, :LANES] = (nope * rinv * gq[:, :LANES]).astype(BF16)
        q_ref[hd, :, LANES:] = (rope * rinv * gq[:, LANES:] + off_ref[0:1, :]).astype(BF16)

    ckv = _rms(z[:, o3:o4]) * kvag_ref[...]
    kpe = z[:, o4:o4 + LANES] * cos + z[:, o4 + LANES:o4 + 2 * LANES] * sin

    @pl.when(jnp.logical_not(is_sample))
    def _():
        latp_ref[0] = ckv
        pep_ref[0] = kpe[:, :QK_ROPE_DIM]

    @pl.when(is_sample)
    def _():
        lats_ref[0] = ckv
        pes_ref[0] = kpe[:, :QK_ROPE_DIM]

    kv = jnp.dot(ckv.astype(BF16), wkvb_ref[...], preferred_element_type=F32)
    gk = gk_ref[...]
    kpe_ss = jnp.sum(kpe * kpe, axis=-1, keepdims=True)
    for hd in range(MLA_HEADS):
        nope = kv[:, hd * LANES:(hd + 1) * LANES]
        ss = jnp.sum(nope * nope, axis=-1, keepdims=True) + kpe_ss
        rinv = lax.rsqrt(ss * (1.0 / QK_HEAD_DIM) + EPS)
        k_ref[hd, :, :LANES] = (nope * rinv * gk[:, :LANES]).astype(BF16)
        k_ref[hd, :, LANES:] = (kpe * rinv * gk[:, LANES:] + off_ref[1:2, :]).astype(BF16)
        v_ref[hd] = kv[:, (MLA_HEADS + hd) * LANES:(MLA_HEADS + hd + 1) * LANES].astype(BF16)


def _mix_in_call(xp, xs, sc1, sh1, w, layer, depth, prev):
    rp, rs = xp.shape[0], xs.shape[0]
    npt = rp // TILE
    nt = npt + rs // TILE
    r = rp + rs
    last = npt - 1

    def full(a):
        nd = a.ndim
        return pl.BlockSpec(a.shape, lambda i: (0,) * nd)

    def variant(a):
        nd = a.ndim
        return pl.BlockSpec((1,) + a.shape[1:], lambda i: (i // npt,) + (0,) * (nd - 1))

    row = lambda width: pl.BlockSpec((TILE, width), lambda i: (i, 0))
    head = lambda width: pl.BlockSpec((MLA_HEADS, TILE, width), lambda i: (0, i, 0))
    in_specs = [
        pl.BlockSpec((TILE, D_MODEL), lambda i: (jnp.minimum(i, last), 0)),
        pl.BlockSpec((TILE, D_MODEL), lambda i: (0, 0)),
        variant(sc1), variant(sh1), full(w["ln1"]), full(w["w_in"]),
        row(LANES), row(LANES),
        full(w["gv"]), variant(w["gw"]), variant(w["gb"]),
        full(w["q_a_g"]), full(w["w_qb"]), full(w["kv_a_g"]), full(w["w_kvb"]),
        full(w["gq"]), full(w["gk"]), full(w["off"]),
    ] + [pl.BlockSpec(memory_space=pl.ANY)] * len(prev)
    prompt_plane = lambda width: pl.BlockSpec((1, TILE, width), lambda i: (layer, jnp.minimum(i, last), 0))
    sample_plane = lambda width: pl.BlockSpec((1, TILE, width), lambda i: (layer, 0, 0))
    out_specs = [
        row(GMLP_WIDTH), head(QK_PAD), head(QK_PAD), head(V_HEAD_DIM),
        prompt_plane(KV_LORA_RANK), prompt_plane(QK_ROPE_DIM),
        sample_plane(KV_LORA_RANK), sample_plane(QK_ROPE_DIM), sample_plane(GMLP_WIDTH),
    ]
    out_shape = [
        jax.ShapeDtypeStruct((r, GMLP_WIDTH), BF16),
        jax.ShapeDtypeStruct((MLA_HEADS, r, QK_PAD), BF16),
        jax.ShapeDtypeStruct((MLA_HEADS, r, QK_PAD), BF16),
        jax.ShapeDtypeStruct((MLA_HEADS, r, V_HEAD_DIM), BF16),
        jax.ShapeDtypeStruct((depth, rp, KV_LORA_RANK), F32),
        jax.ShapeDtypeStruct((depth, rp, QK_ROPE_DIM), F32),
        jax.ShapeDtypeStruct((depth, rs, KV_LORA_RANK), F32),
        jax.ShapeDtypeStruct((depth, rs, QK_ROPE_DIM), F32),
        jax.ShapeDtypeStruct((depth, rs, GMLP_WIDTH), F32),
    ]
    n_fixed = len(in_specs) - len(prev)
    n_own = 4
    return pl.pallas_call(
        functools.partial(_mix_in_kernel, npt, len(prev)),
        grid=(nt,), in_specs=in_specs, out_specs=out_specs, out_shape=out_shape,
        input_output_aliases={n_fixed + j: n_own + j for j in range(len(prev))},
        compiler_params=_cparams(1), name="mix_in",
    )(xp, xs, sc1, sh1, w["ln1"], w["w_in"], w["cos"], w["sin"], w["gv"], w["gw"], w["gb"],
      w["q_a_g"], w["w_qb"], w["kv_a_g"], w["w_kvb"], w["gq"], w["gk"], w["off"], *prev)


def _cache_kv_kernel(lat_ref, pe_ref, wkvb_ref, gk_ref, off_ref, k_ref, v_ref):
    lat = lat_ref[0]
    pe_t = pe_ref[0, 0]
    kpe = jnp.concatenate([pe_t, jnp.zeros_like(pe_t)], axis=0).T
    kv = jnp.dot(lat.astype(BF16), wkvb_ref[0], preferred_element_type=F32)
    gk = gk_ref[0]
    kpe_ss = jnp.sum(kpe * kpe, axis=-1, keepdims=True)
    for hd in range(MLA_HEADS):
        nope = kv[:, hd * LANES:(hd + 1) * LANES]
        ss = jnp.sum(nope * nope, axis=-1, keepdims=True) + kpe_ss
        rinv = lax.rsqrt(ss * (1.0 / QK_HEAD_DIM) + EPS)
        k_ref[0, hd, :, :LANES] = (nope * rinv * gk[:, :LANES]).astype(BF16)
        k_ref[0, hd, :, LANES:] = (kpe * rinv * gk[:, LANES:] + off_ref[0, 1:2, :]).astype(BF16)
        v_ref[0, hd] = kv[:, (MLA_HEADS + hd) * LANES:(MLA_HEADS + hd + 1) * LANES].astype(BF16)


def _cache_kv_call(lat, pe_t, w_kvb, gk, off):
    depth, rows, _ = lat.shape
    t = min(1024, pe_t.shape[-1])
    per_stream = pe_t.shape[-1] // t
    return pl.pallas_call(
        _cache_kv_kernel,
        grid=(depth, rows // t),
        in_specs=[
            pl.BlockSpec((1, t, KV_LORA_RANK), lambda l, i: (l, i, 0)),
            pl.BlockSpec((1, 1, QK_ROPE_DIM, t), lambda l, i: (l, i // per_stream, 0, i % per_stream)),
            pl.BlockSpec((1,) + w_kvb.shape[1:], lambda l, i: (l, 0, 0)),
            pl.BlockSpec((1, 1, QK_PAD), lambda l, i: (l, 0, 0)),
            pl.BlockSpec((1, 2, LANES), lambda l, i: (l, 0, 0)),
        ],
        out_specs=[
            pl.BlockSpec((1, MLA_HEADS, t, QK_PAD), lambda l, i: (l, 0, i, 0)),
            pl.BlockSpec((1, MLA_HEADS, t, V_HEAD_DIM), lambda l, i: (l, 0, i, 0)),
        ],
        out_shape=[
            jax.ShapeDtypeStruct((depth, MLA_HEADS, rows, QK_PAD), BF16),
            jax.ShapeDtypeStruct((depth, MLA_HEADS, rows, V_HEAD_DIM), BF16),
        ],
        compiler_params=_cparams(2), name="cache_kv",
    )(lat, pe_t, w_kvb, gk, off)


def _attn_step(q, k, v, m, l, acc, mask):
    s = lax.dot_general(q, k, (((1,), (1,)), ((), ())), preferred_element_type=F32)
    if mask is not None:
        s = jnp.where(mask, s, -1e30)
    m_new = jnp.maximum(m, jnp.max(s, axis=-1, keepdims=True))
    alpha = jnp.exp2(m - m_new)
    p = jnp.exp2(s - m_new)
    l = alpha * l + jnp.sum(p, axis=-1, keepdims=True)
    acc = alpha * acc + jnp.dot(p.astype(BF16), v, preferred_element_type=F32)
    return m_new, l, acc


def _bounded_step(q, k, v_aug, acc, mask):
    s = lax.dot_general(q, k, (((1,), (1,)), ((), ())), preferred_element_type=F32)
    if mask is not None:
        s = jnp.where(mask, s, -1e30)
    return acc + jnp.dot(jnp.exp2(s).astype(BF16), v_aug, preferred_element_type=F32)


def _prompt_attn_kernel(bounded_ref, q_ref, k_ref, v_ref, o_ref):
    qi = pl.program_id(1)
    row = lax.broadcasted_iota(jnp.int32, (ATT_TK, ATT_TK), 0)
    col = lax.broadcasted_iota(jnp.int32, (ATT_TK, ATT_TK), 1)
    mask = (col // CHUNK) <= (row // CHUNK)

    def q_half(h, half):
        return q_ref[h, half * ATT_TK:(half + 1) * ATT_TK, :]

    def kv(h, j):
        off = pl.multiple_of(j * ATT_TK, ATT_TK)
        return k_ref[h, pl.ds(off, ATT_TK), :], v_ref[h, pl.ds(off, ATT_TK), :]

    def out(h, half):
        return o_ref.at[half * ATT_TK:(half + 1) * ATT_TK, h * V_HEAD_DIM:(h + 1) * V_HEAD_DIM]

    @pl.when(bounded_ref[0] > 0)
    def _():
        lane = lax.broadcasted_iota(jnp.int32, (ATT_TK, LANES), 1)
        ones_col = jnp.where(lane == 0, 1.0, 0.0).astype(BF16)

        def kv_aug(h, j):
            k, v = kv(h, j)
            return k, jnp.concatenate([v, ones_col], axis=1)

        def blocks(first, count, accs):
            accs = [list(a) for a in accs]
            for u in range(count):
                for h in range(ATT_HEADS):
                    k, v = kv_aug(h, first + u)
                    for half in range(2):
                        accs[h][half] = _bounded_step(q_half(h, half), k, v, accs[h][half], None)
            return tuple(tuple(a) for a in accs)

        zero = jnp.zeros((ATT_TK, 2 * LANES), F32)
        n_full = 2 * qi
        n_trips = n_full // ATT_UNROLL
        accs = lax.fori_loop(0, n_trips, lambda j, c: blocks(ATT_UNROLL * j, ATT_UNROLL, c),
                             ((zero, zero),) * ATT_HEADS)
        done = ATT_UNROLL * n_trips
        part = ATT_UNROLL // 2
        while part >= 2:
            accs = lax.cond((n_full & part) != 0, lambda c, d=done, p=part: blocks(d, p, c), lambda c: c, accs)
            done = done + (n_full & part)
            part //= 2
        for h in range(ATT_HEADS):
            acc_a, acc_b = accs[h]
            k, v = kv_aug(h, 2 * qi)
            acc_a = _bounded_step(q_half(h, 0), k, v, acc_a, mask)
            acc_b = _bounded_step(q_half(h, 1), k, v, acc_b, None)
            k, v = kv_aug(h, 2 * qi + 1)
            acc_b = _bounded_step(q_half(h, 1), k, v, acc_b, mask)
            out(h, 0)[...] = (acc_a[:, :LANES] / acc_a[:, LANES:LANES + 1]).astype(BF16)
            out(h, 1)[...] = (acc_b[:, :LANES] / acc_b[:, LANES:LANES + 1]).astype(BF16)

    @pl.when(bounded_ref[0] == 0)
    def _():
        def init():
            return (jnp.full((ATT_TK, 1), -1e30, F32), jnp.zeros((ATT_TK, 1), F32),
                    jnp.zeros((ATT_TK, V_HEAD_DIM), F32))

        for h in range(ATT_HEADS):
            qa, qb = q_half(h, 0), q_half(h, 1)

            def body(j, carry, h=h, qa=qa, qb=qb):
                k, v = kv(h, j)
                return _attn_step(qa, k, v, *carry[0], None), _attn_step(qb, k, v, *carry[1], None)

            sa, sb = lax.fori_loop(0, 2 * qi, body, (init(), init()))
            k, v = kv(h, 2 * qi)
            sa = _attn_step(qa, k, v, *sa, mask)
            sb = _attn_step(qb, k, v, *sb, None)
            k, v = kv(h, 2 * qi + 1)
            sb = _attn_step(qb, k, v, *sb, mask)
            out(h, 0)[...] = (sa[2] / sa[1]).astype(BF16)
            out(h, 1)[...] = (sb[2] / sb[1]).astype(BF16)


def _prompt_attn_call(bounded, q, k, v, rp):
    r = q.shape[1]
    resident = lambda arr: pl.BlockSpec((ATT_HEADS,) + arr.shape[1:], lambda h, i, b: (h, 0, 0),
                                        pipeline_mode=pl.Buffered(1))
    return pl.pallas_call(
        _prompt_attn_kernel,
        grid_spec=pltpu.PrefetchScalarGridSpec(
            num_scalar_prefetch=1, grid=(MLA_HEADS // ATT_HEADS, rp // ATT_TQ),
            in_specs=[
                pl.BlockSpec((ATT_HEADS, ATT_TQ, QK_PAD), lambda h, i, b: (h, i, 0)),
                resident(k), resident(v),
            ],
            out_specs=pl.BlockSpec((ATT_TQ, ATT_HEADS * V_HEAD_DIM), lambda h, i, b: (i, h))),
        out_shape=jax.ShapeDtypeStruct((rp, MLA_HEADS * V_HEAD_DIM), BF16),
        compiler_params=_cparams(2), name="prompt_attn",
    )(bounded, q, k, v)


def _sample_attn_kernel(q_ref, kp_ref, vp_ref, kn_ref, vn_ref, o_ref):
    q = q_ref[0]
    nt = (((1,), (1,)), ((), ()))
    s1 = lax.dot_general(q, kp_ref[0, 0], nt, preferred_element_type=F32)
    s2 = lax.dot_general(q, kn_ref[0], nt, preferred_element_type=F32)
    m = jnp.maximum(jnp.max(s1, axis=-1, keepdims=True), jnp.max(s2, axis=-1, keepdims=True))
    p1 = jnp.exp2(s1 - m)
    p2 = jnp.exp2(s2 - m)
    l = jnp.sum(p1, axis=-1, keepdims=True) + jnp.sum(p2, axis=-1, keepdims=True)
    o = (jnp.dot(p1.astype(BF16), vp_ref[0, 0], preferred_element_type=F32)
         + jnp.dot(p2.astype(BF16), vn_ref[0], preferred_element_type=F32))
    o_ref[...] = (o / l).astype(BF16)


def _sample_attn_call(layer, q, k, v, k_past, v_past, rp, n_seq, s_len, past):
    first = rp // s_len
    return pl.pallas_call(
        _sample_attn_kernel,
        grid=(n_seq, MLA_HEADS),
        in_specs=[
            pl.BlockSpec((1, s_len, QK_PAD), lambda b, h: (h, first + b, 0)),
            pl.BlockSpec((1, 1, past, QK_PAD), lambda b, h: (layer, h, b, 0)),
            pl.BlockSpec((1, 1, past, V_HEAD_DIM), lambda b, h: (layer, h, b, 0)),
            pl.BlockSpec((1, s_len, QK_PAD), lambda b, h: (h, first + b, 0)),
            pl.BlockSpec((1, s_len, V_HEAD_DIM), lambda b, h: (h, first + b, 0)),
        ],
        out_specs=pl.BlockSpec((s_len, V_HEAD_DIM), lambda b, h: (b, h)),
        out_shape=jax.ShapeDtypeStruct((n_seq * s_len, MLA_HEADS * V_HEAD_DIM), BF16),
        compiler_params=_cparams(2), name="sample_attn",
    )(q, k_past, v_past, k, v)


def _route(logits_t, bias_col):
    scores = 1.0 / (1.0 + jnp.exp(-logits_t))
    biased = scores + bias_col
    s_rows = [scores[e:e + 1, :] for e in range(N_EXPERTS)]
    b_rows = [biased[e:e + 1, :] for e in range(N_EXPERTS)]
    group_score = []
    for g in range(N_GROUPS):
        rows = b_rows[g * EXPERTS_PER_GROUP:(g + 1) * EXPERTS_PER_GROUP]
        best = None
        for a in range(EXPERTS_PER_GROUP):
            for b in range(a + 1, EXPERTS_PER_GROUP):
                pair = rows[a] + rows[b]
                best = pair if best is None else jnp.maximum(best, pair)
        group_score.append(best)
    best_group = jnp.zeros_like(group_score[0], dtype=jnp.int32)
    best_val = group_score[0]
    for g in range(1, N_GROUPS):
        better = group_score[g] > best_val
        best_group = jnp.where(better, g, best_group)
        best_val = jnp.where(better, group_score[g], best_val)
    selected = []
    for e in range(N_EXPERTS):
        g = e // EXPERTS_PER_GROUP
        rank = jnp.zeros_like(best_group)
        for j in range(g * EXPERTS_PER_GROUP, (g + 1) * EXPERTS_PER_GROUP):
            if j == e:
                continue
            ahead = b_rows[j] > b_rows[e]
            if j < e:
                ahead = ahead | (b_rows[j] == b_rows[e])
            rank = rank + ahead.astype(jnp.int32)
        selected.append((best_group == g) & (rank < 2))
    denom = jnp.zeros_like(s_rows[0])
    for e in range(N_EXPERTS):
        denom = denom + jnp.where(selected[e], s_rows[e], 0.0)
    classes = [selected[g * EXPERTS_PER_GROUP + a] & selected[g * EXPERTS_PER_GROUP + b]
               for g in range(N_GROUPS) for a, b in EXPERT_PAIRS]
    return [jnp.where(selected[e], s_rows[e] / denom, 0.0) for e in range(N_EXPERTS)], classes


def _mix_out_kernel(n_prompt_tiles, a_ref, bp_ref, bs_ref, xp_ref, xs_ref, wa_ref, wb_ref, g1_ref, sc_ref,
                    sh_ref, g2_ref, ln_ref, rw_ref, rb_ref, sg_ref, su_ref, sd_ref,
                    ysh_ref, h2e_ref, grp_ref, rank_ref, cnt_ref, gt_ref, oh_ref, carry_ref):
    i = pl.program_id(0)
    is_sample = i >= n_prompt_tiles
    x = jnp.where(is_sample, xs_ref[...], xp_ref[...])
    b = jnp.where(is_sample, bs_ref[...], bp_ref[...])
    mix = (jnp.dot(a_ref[...], wa_ref[...], preferred_element_type=F32)
           + jnp.dot(b, wb_ref[...], preferred_element_type=F32))
    xm = _gated_add(x, g1_ref[0], mix)
    h2 = _modulate(_rms(xm) * ln_ref[...], sc_ref[0], sh_ref[0])
    h2e_ref[:, :D_MODEL] = h2
    h2_hi = h2.astype(BF16)
    act = _silu(jnp.dot(h2_hi, sg_ref[...], preferred_element_type=F32)) * jnp.dot(
        h2_hi, su_ref[...], preferred_element_type=F32)
    shared = jnp.dot(act.astype(BF16), sd_ref[...], preferred_element_type=F32)
    ysh_ref[...] = _gated_add(xm, g2_ref[0], shared)
    h2_lo = (h2 - h2_hi.astype(F32)).astype(BF16)
    by_hi = jnp.dot(h2_hi, rw_ref[...], preferred_element_type=F32)
    logits = (by_hi[:, :LANES] + by_hi[:, LANES:]
              + jnp.dot(h2_lo, rw_ref[:, :LANES], preferred_element_type=F32))
    gate_rows, classes = _route(logits.T[:N_EXPERTS, :], rb_ref[...])
    gt_ref[...] = jnp.zeros_like(gt_ref)
    for e in range(N_EXPERTS):
        gt_ref[e:e + 1, :] = gate_rows[e]
    h2e_ref[:, D_MODEL:] = gt_ref[...].T

    @pl.when(i == 0)
    def _():
        carry_ref[...] = jnp.zeros_like(carry_ref)

    oh_ref[...] = jnp.zeros_like(oh_ref)
    for c in range(N_CLASSES):
        oh_ref[c:c + 1, :] = jnp.where(classes[c], 1.0, 0.0)
    onehot = oh_ref[...]
    r_idx = lax.broadcasted_iota(jnp.int32, (TILE, TILE), 0)
    c_idx = lax.broadcasted_iota(jnp.int32, (TILE, TILE), 1)
    upper = jnp.where(r_idx <= c_idx, 1.0, 0.0).astype(BF16)
    cum = jnp.dot(onehot.astype(BF16), upper, preferred_element_type=F32)
    carry = carry_ref[...]
    rank = jnp.sum(onehot * (cum - 1.0 + carry[:, :1]), axis=0, keepdims=True)
    cls_id = lax.broadcasted_iota(jnp.int32, (CLASS_ROWS, TILE), 0).astype(F32)
    cls_row = jnp.sum(onehot * cls_id, axis=0, keepdims=True).astype(jnp.int32)
    grp_ref[0] = jnp.broadcast_to(cls_row, (8, TILE))
    rank_ref[0] = jnp.broadcast_to(rank.astype(jnp.int32), (8, TILE))
    carry = carry + jnp.sum(onehot, axis=1, keepdims=True)
    carry_ref[...] = carry
    cnt_ref[...] = carry


def _mix_out_call(a, bp, bs, xp, xs, g1, sc2, sh2, g2, w):
    rp, rs = xp.shape[0], xs.shape[0]
    npt = rp // TILE
    nt = npt + rs // TILE
    r = rp + rs
    last = npt - 1

    def full(arr):
        nd = arr.ndim
        return pl.BlockSpec(arr.shape, lambda i: (0,) * nd)

    def variant(arr):
        nd = arr.ndim
        return pl.BlockSpec((1,) + arr.shape[1:], lambda i: (i // npt,) + (0,) * (nd - 1))

    row = lambda width: pl.BlockSpec((TILE, width), lambda i: (i, 0))
    prow = lambda width: pl.BlockSpec((TILE, width), lambda i: (jnp.minimum(i, last), 0))
    srow = lambda width: pl.BlockSpec((TILE, width), lambda i: (0, 0))
    return pl.pallas_call(
        functools.partial(_mix_out_kernel, npt),
        grid=(nt,),
        in_specs=[row(GMLP_WIDTH), prow(GMLP_WIDTH), srow(GMLP_WIDTH), prow(D_MODEL), srow(D_MODEL),
                  full(w["w_out_a"]), full(w["w_out_b"]), variant(g1), variant(sc2), variant(sh2),
                  variant(g2), full(w["ln2"]), full(w["rw"]), full(w["rb"]),
                  full(w["sg"]), full(w["su"]), full(w["sd"])],
        out_specs=[row(D_MODEL), row(H2E_COLS),
                   pl.BlockSpec((1, 8, TILE), lambda i: (i, 0, 0)),
                   pl.BlockSpec((1, 8, TILE), lambda i: (i, 0, 0)),
                   pl.BlockSpec((CLASS_ROWS, LANES), lambda i: (0, 0))],
        out_shape=[jax.ShapeDtypeStruct((r, D_MODEL), F32),
                   jax.ShapeDtypeStruct((r, H2E_COLS), F32),
                   jax.ShapeDtypeStruct((nt, 8, TILE), jnp.int32),
                   jax.ShapeDtypeStruct((nt, 8, TILE), jnp.int32),
                   jax.ShapeDtypeStruct((CLASS_ROWS, LANES), F32)],
        scratch_shapes=[pltpu.VMEM((LANES, TILE), F32), pltpu.VMEM((CLASS_ROWS, TILE), F32),
                        pltpu.VMEM((CLASS_ROWS, LANES), F32)],
        compiler_params=_cparams(1), name="mix_out",
    )(a, bp, bs, xp, xs, w["w_out_a"], w["w_out_b"], g1, sc2, sh2, g2, w["ln2"], w["rw"], w["rb"],
      w["sg"], w["su"], w["sd"])


def _silu(x):
    return x / (1.0 + jnp.exp(-x))


def _invert_positions(pos_ref, pad_lo_ref, pad_hi_ref, src_ref):
    def clear(s, carry):
        src_ref[s] = 0
        return carry

    for g in range(N_GROUPS):
        lax.fori_loop(pad_lo_ref[g], pad_hi_ref[g], clear, 0)
    lax.fori_loop(pad_hi_ref[N_GROUPS - 1], src_ref.shape[0], clear, 0)

    def place(i, carry):
        src_ref[pos_ref[i]] = i
        return carry

    lax.fori_loop(0, pos_ref.shape[0], place, 0, unroll=ROW_DMA_UNROLL)


def _gather_rows(idx_ref, base, src_ref, dst_ref, sem):
    for r in range(TILE):
        pltpu.make_async_copy(src_ref.at[pl.ds(idx_ref[base + r], 1)], dst_ref.at[pl.ds(r, 1)], sem).start()


def _wait_rows(src_ref, dst_ref, sem):
    pltpu.make_async_copy(src_ref.at[pl.ds(0, TILE)], dst_ref, sem).wait()


def _routed_kernel(grp_ref, on_ref, need_ref, pos_ref, pad_lo_ref, pad_hi_ref, h_ref, wg_ref, wu_ref, wd_ref,
                   o_ref, buf_ref, src_ref, sem):
    t = pl.program_id(0)
    n = pl.num_programs(0)
    slot = t % 2
    on = on_ref[t] > 0

    @pl.when(t == 0)
    def _():
        _invert_positions(pos_ref, pad_lo_ref, pad_hi_ref, src_ref)

    @pl.when((t == 0) & on)
    def _():
        _gather_rows(src_ref, 0, h_ref, buf_ref.at[0], sem.at[0])

    @pl.when((on_ref[jnp.minimum(t + 1, n - 1)] > 0) & (t + 1 < n))
    def _():
        _gather_rows(src_ref, (t + 1) * TILE, h_ref, buf_ref.at[1 - slot], sem.at[1 - slot])

    @pl.when(jnp.logical_not(on))
    def _():
        o_ref[...] = jnp.zeros_like(o_ref)

    @pl.when(on)
    def _():
        _wait_rows(h_ref, buf_ref.at[slot], sem.at[slot])
        first = grp_ref[t] * EXPERTS_PER_GROUP
        o_ref[...] = jnp.zeros_like(o_ref)
        for e in range(EXPERTS_PER_GROUP):
            @pl.when(need_ref[t * EXPERTS_PER_GROUP + e] > 0)
            def _(e=e):
                h = buf_ref[slot, :, :D_MODEL]
                hg = jnp.dot(h, wg_ref[0, e], preferred_element_type=F32)
                hu = jnp.dot(h, wu_ref[0, e], preferred_element_type=F32)
                lane = lax.broadcasted_iota(jnp.int32, (TILE, LANES), 1)
                gate = jnp.sum(jnp.where(lane == first + e, buf_ref[slot, :, D_MODEL:], 0.0),
                               axis=-1, keepdims=True)
                o_ref[...] += jnp.dot(_silu(hg) * hu, wd_ref[0, e], preferred_element_type=F32) * gate


def _routed_call(tile_grp, tile_on, tile_need, pos, pad_lo, pad_hi, n_slots, h2e, wg, wu, wd, layer):
    nts = n_slots // TILE
    wspec = lambda arr: pl.BlockSpec((1,) + arr.shape[1:],
                                     lambda t, grp, *_: (layer * N_GROUPS + grp[t], 0, 0, 0),
                                     pipeline_mode=pl.Buffered(1))
    return pl.pallas_call(
        _routed_kernel,
        grid_spec=pltpu.PrefetchScalarGridSpec(
            num_scalar_prefetch=6, grid=(nts,),
            in_specs=[pl.BlockSpec(memory_space=pl.ANY), wspec(wg), wspec(wu), wspec(wd)],
            out_specs=pl.BlockSpec((TILE, D_MODEL), lambda t, *_: (t, 0)),
            scratch_shapes=[pltpu.VMEM((2, TILE, H2E_COLS), F32), pltpu.SMEM((n_slots,), jnp.int32),
                            pltpu.SemaphoreType.DMA((2,))]),
        out_shape=jax.ShapeDtypeStruct((n_slots, D_MODEL), F32),
        compiler_params=_cparams(1), name="moe_routed",
    )(tile_grp, tile_on, tile_need, pos, pad_lo, pad_hi, h2e, wg, wu, wd)


def _combine_kernel(n_prompt_tiles, pos_ref, ysh_ref, g2_ref, routed_ref, yp_ref, ys_ref, buf_ref, sem):
    i = pl.program_id(0)
    n = pl.num_programs(0)
    slot = i % 2

    @pl.when(i == 0)
    def _():
        _gather_rows(pos_ref, 0, routed_ref, buf_ref.at[0], sem.at[0])

    @pl.when(i + 1 < n)
    def _():
        _gather_rows(pos_ref, (i + 1) * TILE, routed_ref, buf_ref.at[1 - slot], sem.at[1 - slot])

    _wait_rows(routed_ref, buf_ref.at[slot], sem.at[slot])
    y = _gated_add(ysh_ref[...], g2_ref[0], buf_ref[slot])

    @pl.when(i < n_prompt_tiles)
    def _():
        yp_ref[...] = y

    @pl.when(i >= n_prompt_tiles)
    def _():
        ys_ref[...] = y


def _combine_call(pos, ysh, g2, routed, rp):
    r = ysh.shape[0]
    rs = r - rp
    npt = rp // TILE
    last = npt - 1
    return pl.pallas_call(
        functools.partial(_combine_kernel, npt),
        grid_spec=pltpu.PrefetchScalarGridSpec(
            num_scalar_prefetch=1, grid=(r // TILE,),
            in_specs=[pl.BlockSpec((TILE, D_MODEL), lambda i, pos: (i, 0)),
                      pl.BlockSpec((1,) + g2.shape[1:], lambda i, pos: (i // npt, 0, 0)),
                      pl.BlockSpec(memory_space=pl.ANY)],
            out_specs=[pl.BlockSpec((TILE, D_MODEL), lambda i, pos: (jnp.minimum(i, last), 0)),
                       pl.BlockSpec((TILE, D_MODEL), lambda i, pos: (0, 0))],
            scratch_shapes=[pltpu.VMEM((2, TILE, D_MODEL), F32), pltpu.SemaphoreType.DMA((2,))]),
        out_shape=[jax.ShapeDtypeStruct((rp, D_MODEL), F32),
                   jax.ShapeDtypeStruct((rs, D_MODEL), F32)],
        compiler_params=_cparams(1), name="moe_combine",
    )(pos, ysh, g2, routed)


def _moe(h2e, cls, rank, counts, ysh, g2, experts, layer, rp):
    r = h2e.shape[0]
    n_pairs = len(EXPERT_PAIRS)
    n_sorted_tiles = r // TILE + N_GROUPS
    cnt_c = counts[:N_CLASSES, 0].astype(jnp.int32).reshape(N_GROUPS, n_pairs)
    cnt = jnp.sum(cnt_c, axis=1)
    tiles_g = (cnt + TILE - 1) // TILE
    end_g = jnp.cumsum(tiles_g)
    off_g = (end_g - tiles_g) * TILE
    start_c = (off_g[:, None] + jnp.cumsum(cnt_c, axis=1) - cnt_c).reshape(N_CLASSES)
    end_c = start_c + cnt_c.reshape(N_CLASSES)
    cls_flat = cls[:, 0, :].reshape(r)
    pos = rank[:, 0, :].reshape(r)
    for c in range(N_CLASSES):
        pos = pos + jnp.where(cls_flat == c, start_c[c], 0)
    t_idx = jnp.arange(n_sorted_tiles, dtype=jnp.int32)
    tile_grp = jnp.zeros_like(t_idx)
    for g in range(N_GROUPS - 1):
        tile_grp = tile_grp + (t_idx >= end_g[g]).astype(jnp.int32)
    tile_on = (t_idx < end_g[N_GROUPS - 1]).astype(jnp.int32)
    lo = t_idx[:, None] * TILE
    overlap = ((start_c[None, :] < lo + TILE) & (end_c[None, :] > lo)
               & (end_c > start_c)[None, :])
    member = jnp.array([[int(e in pair) for e in range(EXPERTS_PER_GROUP)] for pair in EXPERT_PAIRS] * N_GROUPS,
                       dtype=jnp.int32)
    tile_need = jnp.max(overlap[:, :, None].astype(jnp.int32) * member[None], axis=1).reshape(-1)
    routed = _routed_call(tile_grp, tile_on, tile_need, pos, off_g + cnt, end_g * TILE,
                          n_sorted_tiles * TILE, h2e, *experts, layer)
    return _combine_call(pos, ysh, g2, routed, rp)


def _rot_half_cols(wcols):
    half = QK_ROPE_DIM // 2
    return jnp.concatenate([-wcols[:, half:], wcols[:, :half]], axis=1)


def _pad_cols(wcols, width):
    return jnp.pad(wcols, ((0, 0), (0, width - wcols.shape[1])))


def _split_bf16(w):
    hi = w.astype(BF16)
    lo = (w - hi.astype(F32)).astype(BF16)
    return jnp.concatenate([hi, lo], axis=1)


def _rope_tables(seq, past, s_len, n_seq):
    half = QK_ROPE_DIM // 2
    inv = ROPE_THETA ** (-jnp.arange(half, dtype=F32) / half)
    inv_wide = jnp.concatenate([inv, inv, jnp.zeros((LANES - QK_ROPE_DIM,), F32)])
    live = (jnp.arange(LANES) < QK_ROPE_DIM).astype(F32)

    def cos_sin(pos):
        ang = pos.astype(F32)[:, None] * inv_wide[None, :]
        return jnp.cos(ang), jnp.sin(ang)

    cc, sc = cos_sin(jnp.arange(0, seq, GMLP_CHUNK, dtype=jnp.int32))
    cf, sf = cos_sin(jnp.arange(GMLP_CHUNK, dtype=jnp.int32))
    cos_p = ((cc[:, None, :] * cf[None, :, :] - sc[:, None, :] * sf[None, :, :]) * live).reshape(seq, LANES)
    sin_p = ((sc[:, None, :] * cf[None, :, :] + cc[:, None, :] * sf[None, :, :]) * live).reshape(seq, LANES)
    cos_s, sin_s = cos_sin(past + jnp.arange(s_len, dtype=jnp.int32))
    cos = jnp.concatenate([cos_p, jnp.tile(cos_s * live, (n_seq, 1))], axis=0)
    sin = jnp.concatenate([sin_p, jnp.tile(sin_s * live, (n_seq, 1))], axis=0)
    return cos, sin


def _layer_weights(l, p, cos, sin):
    o1 = GMLP_WIDTH
    o2 = 2 * GMLP_WIDTH
    o3 = o2 + Q_LORA_RANK
    o4 = o3 + KV_LORA_RANK
    w_in = p["w_in"][l]
    kpe_cols = w_in[:, o4:]
    w_in2 = jnp.concatenate([w_in[:, :o4], _pad_cols(kpe_cols, LANES),
                             _pad_cols(_rot_half_cols(kpe_cols), LANES)], axis=1).astype(BF16)
    w_qb = p["w_qb"][l]
    q_parts = []
    for hd in range(MLA_HEADS):
        base = hd * QK_HEAD_DIM
        rope_cols = w_qb[:, base + QK_NOPE_DIM:base + QK_HEAD_DIM]
        q_parts += [w_qb[:, base:base + QK_NOPE_DIM], _pad_cols(rope_cols, LANES),
                    _pad_cols(_rot_half_cols(rope_cols), LANES)]
    w_qb2 = jnp.concatenate(q_parts, axis=1).astype(BF16)
    w_kvb = p["w_kvb"][l].reshape(KV_LORA_RANK, MLA_HEADS, QK_NOPE_DIM + V_HEAD_DIM)
    w_kvb2 = jnp.concatenate([w_kvb[:, :, :QK_NOPE_DIM].reshape(KV_LORA_RANK, -1),
                              w_kvb[:, :, QK_NOPE_DIM:].reshape(KV_LORA_RANK, -1)], axis=1).astype(BF16)
    qscale = LOG2E / math.sqrt(QK_HEAD_DIM)
    gq = _pad_cols(p["q_norm_g"][l][None, :] * qscale, QK_PAD)
    gk = _pad_cols(p["k_norm_g"][l][None, :], QK_PAD)
    bound = (QK_HEAD_DIM * qscale * SCORE_BOUND_MARGIN * jnp.max(jnp.abs(p["q_norm_g"][l]))
             * jnp.max(jnp.abs(p["k_norm_g"][l])))
    bounded = bound <= MAX_SCORE_BOUND
    pad_lane = jnp.arange(LANES) == QK_ROPE_DIM
    off = jnp.stack([jnp.where(pad_lane, 1.0, 0.0),
                     jnp.where(pad_lane & bounded, -bound, 0.0)]).astype(F32)

    ws = p["gmlp_ws"][l]
    tri = jnp.tril(jnp.ones((GMLP_CHUNK, GMLP_CHUNK), dtype=bool))
    wt = jnp.where(tri[None], ws, 0.0)
    hc = GMLP_CHUNK // 2
    top = wt[:, :hc, :hc]
    zero = jnp.zeros_like(top)
    wt_s = jnp.concatenate([jnp.concatenate([top, zero], axis=2), jnp.concatenate([zero, top], axis=2)], axis=1)
    gw = jnp.stack([wt, wt_s]).astype(BF16)
    b = p["gmlp_b"][l]
    b_s = jnp.concatenate([b[:, :hc], b[:, :hc]], axis=1)
    gb = jnp.broadcast_to(jnp.stack([b, b_s])[..., None], (2, GMLP_GROUPS, GMLP_CHUNK, LANES)).astype(F32)

    w_out = p["w_out"][l].astype(BF16)
    return dict(
        sg=p["sh_w_gate"][l].astype(BF16),
        su=p["sh_w_up"][l].astype(BF16), sd=p["sh_w_down"][l].astype(BF16),
        ln1=p["ln1_g"][l][None, :], w_in=w_in2, cos=cos, sin=sin,
        gv=p["gmlp_v_g"][l].reshape(1, GMLP_WIDTH), gw=gw, gb=gb,
        q_a_g=p["q_a_g"][l][None, :], w_qb=w_qb2, kv_a_g=p["kv_a_g"][l][None, :], w_kvb=w_kvb2,
        gq=gq, gk=gk, w_out_a=w_out[:GMLP_WIDTH], w_out_b=w_out[GMLP_WIDTH:],
        ln2=p["ln2_g"][l][None, :], rw=_split_bf16(_pad_cols(p["router_w"], LANES)),
        rb=p["router_bias"].reshape(N_EXPERTS, 1), off=off,
        bounded=bounded.astype(jnp.int32).reshape(1),
    )


def kernel(x_prompt, x_sample, cache_kv_latent, cache_k_rope, c_prompt, c_sample, w_ada, b_ada, ln1_g, w_in,
           gmlp_v_g, gmlp_ws, gmlp_b, q_a_g, w_qb, kv_a_g, w_kvb, q_norm_g, k_norm_g, w_out, ln2_g, router_w,
           router_bias, exp_w_gate, exp_w_up, exp_w_down, sh_w_gate, sh_w_up, sh_w_down):
    p = dict(w_in=w_in, gmlp_v_g=gmlp_v_g, gmlp_ws=gmlp_ws, gmlp_b=gmlp_b, q_a_g=q_a_g, w_qb=w_qb,
             kv_a_g=kv_a_g, w_kvb=w_kvb, q_norm_g=q_norm_g, k_norm_g=k_norm_g, w_out=w_out, ln1_g=ln1_g,
             ln2_g=ln2_g, router_w=router_w, router_bias=router_bias, exp_w_gate=exp_w_gate,
             exp_w_up=exp_w_up, exp_w_down=exp_w_down, sh_w_gate=sh_w_gate, sh_w_up=sh_w_up,
             sh_w_down=sh_w_down)
    batch, seq, _ = x_prompt.shape
    n_seq, s_len, _ = x_sample.shape
    depth, _, past, _ = cache_kv_latent.shape
    assert batch == 1 and s_len == SUB and n_seq == N_SUB and n_seq * s_len == TILE
    assert seq % TILE == 0 and seq % ATT_TQ == 0 and past % GMLP_CHUNK == 0 and past % CHUNK == 0
    rp = seq
    rs = n_seq * s_len

    cos, sin = _rope_tables(seq, past, s_len, n_seq)

    c_all = jnp.concatenate([jnp.broadcast_to(c_prompt, (N_SUB, D_MODEL)), c_sample], axis=0)
    mod = _ada_call(c_all, w_ada, b_ada)
    mod = mod.reshape(depth, 2, N_SUB, 6, D_MODEL)

    weights = [_layer_weights(l, p, cos, sin) for l in range(depth)]
    experts = (exp_w_gate.reshape(depth * N_GROUPS, EXPERTS_PER_GROUP, D_MODEL, EXPERT_FF),
               exp_w_up.reshape(depth * N_GROUPS, EXPERTS_PER_GROUP, D_MODEL, EXPERT_FF),
               exp_w_down.reshape(depth * N_GROUPS, EXPERTS_PER_GROUP, EXPERT_FF, D_MODEL))
    lat_all = cache_kv_latent.reshape(depth, n_seq * past, KV_LORA_RANK)
    pe_all = jnp.swapaxes(cache_k_rope, 2, 3)
    k_past, v_past = _cache_kv_call(lat_all, pe_all, jnp.stack([w["w_kvb"] for w in weights]),
                                    jnp.stack([w["gk"] for w in weights]),
                                    jnp.stack([w["off"] for w in weights]))

    xp = x_prompt.reshape(rp, D_MODEL)
    xs = x_sample.reshape(rs, D_MODEL)
    planes = tuple(jnp.zeros((depth, rows, width), F32) for rows, width in
                   ((rp, KV_LORA_RANK), (rp, QK_ROPE_DIM), (rs, KV_LORA_RANK), (rs, QK_ROPE_DIM),
                    (rs, GMLP_WIDTH)))
    for l in range(depth):
        w = weights[l]
        sh1, sc1, g1, sh2, sc2, g2 = [mod[l, :, :, j, :] for j in range(6)]
        a, q, k, v, *planes = _mix_in_call(xp, xs, sc1, sh1, w, l, depth, tuple(planes))
        bp = _prompt_attn_call(w["bounded"], q, k, v, rp)
        bs = _sample_attn_call(l, q, k, v, k_past, v_past, rp, n_seq, s_len, past)
        ysh, h2e, grp, rank, counts = _mix_out_call(a, bp, bs, xp, xs, g1, sc2, sh2, g2, w)
        xp, xs = _moe(h2e, grp, rank, counts, ysh, g2, experts, l, rp)
    lat_p, pe_p, lat_s, pe_s, v_s = planes
    return (xp.reshape(batch, seq, D_MODEL), xs.reshape(n_seq, s_len, D_MODEL),
            lat_p.reshape(depth, batch, seq, KV_LORA_RANK), pe_p.reshape(depth, batch, seq, QK_ROPE_DIM),
            lat_s.reshape(depth, n_seq, s_len, KV_LORA_RANK), pe_s.reshape(depth, n_seq, s_len, QK_ROPE_DIM),
            v_s.reshape(depth, n_seq, s_len, GMLP_WIDTH))
```

```python
import functools
import math

import jax
import jax.numpy as jnp
from jax import lax
from jax.experimental import pallas as pl
from jax.experimental.pallas import tpu as pltpu

F32 = jnp.float32
BF16 = jnp.bfloat16

D_MODEL = 1024
CHUNK = 64
GMLP_WIDTH = 512
GMLP_GROUPS = 4
GMLP_CHUNK = 128
MLA_HEADS = 4
QK_NOPE_DIM = 128
QK_ROPE_DIM = 64
QK_HEAD_DIM = 192
V_HEAD_DIM = 128
Q_LORA_RANK = 384
KV_LORA_RANK = 256
ROPE_THETA = 10000.0
N_EXPERTS = 16
N_GROUPS = 4
EXPERTS_PER_GROUP = 4
EXPERT_FF = 512
EPS = 1e-6

LANES = 128
TILE = 512
SUB = 64
N_SUB = TILE // SUB
QK_PAD = 256
IN_COLS = 2 * GMLP_WIDTH + Q_LORA_RANK + KV_LORA_RANK + 2 * LANES
Q_COLS = MLA_HEADS * 3 * LANES
H2E_COLS = D_MODEL + LANES
ROW_DMA_UNROLL = 8
EXPERT_PAIRS = tuple((a, b) for a in range(EXPERTS_PER_GROUP) for b in range(a + 1, EXPERTS_PER_GROUP))
N_CLASSES = N_GROUPS * len(EXPERT_PAIRS)
CLASS_ROWS = 32
ATT_TK = 512
ATT_TQ = 2 * ATT_TK
ATT_UNROLL = 8
ATT_HEADS = 2
VMEM_LIMIT = 56 * 1024 * 1024
ROUTED_VMEM_LIMIT = 62 * 1024 * 1024
LOG2E = 1.4426950408889634
SCORE_BOUND_MARGIN = 1.02
MAX_SCORE_BOUND = 48.0


def _cparams(n_axes):
    return pltpu.CompilerParams(dimension_semantics=("arbitrary",) * n_axes,
                                vmem_limit_bytes=VMEM_LIMIT)


def _rms(x, eps=EPS):
    return x * lax.rsqrt(jnp.mean(x * x, axis=-1, keepdims=True) + eps)


def _gelu(x):
    c = math.sqrt(2.0 / math.pi)
    return 0.5 * x * (1.0 + jnp.tanh(c * (x + 0.044715 * (x * x * x))))


def _modulate(h, scale, shift):
    h3 = h.reshape(h.shape[0] // SUB, SUB, h.shape[-1])
    h3 = h3 * (1.0 + scale[:, None, :]) + shift[:, None, :]
    return h3.reshape(h.shape)


def _gated_add(x, gate, y):
    y3 = y.reshape(y.shape[0] // SUB, SUB, y.shape[-1]) * gate[:, None, :]
    return x + y3.reshape(y.shape)


def _ada_kernel(c_ref, w_ref, b_ref, o_ref):
    c = c_ref[...]
    cs = (c / (1.0 + jnp.exp(-c))).astype(BF16)
    w = w_ref[0].astype(BF16)
    o_ref[0] = jnp.dot(cs, w, preferred_element_type=F32) + b_ref[0]


def _ada_call(c_all, w_ada, b_ada):
    depth = w_ada.shape[0]
    nblk = w_ada.shape[2] // D_MODEL
    return pl.pallas_call(
        _ada_kernel,
        grid=(depth, nblk),
        in_specs=[
            pl.BlockSpec((16, D_MODEL), lambda l, j: (0, 0)),
            pl.BlockSpec((1, D_MODEL, D_MODEL), lambda l, j: (l, 0, j)),
            pl.BlockSpec((1, 1, D_MODEL), lambda l, j: (l, 0, j)),
        ],
        out_specs=pl.BlockSpec((1, 16, D_MODEL), lambda l, j: (l, 0, j)),
        out_shape=jax.ShapeDtypeStruct((depth, 16, w_ada.shape[2]), F32),
        compiler_params=_cparams(2),
        name="ada_mod",
    )(c_all, w_ada, b_ada.reshape(depth, 1, -1))


def _mix_in_kernel(n_prompt_tiles, n_prev, xp_ref, xs_ref, sc_ref, sh_ref, ln_ref, win_ref, cos_ref, sin_ref,
                   gv_ref, gw_ref, gb_ref, qag_ref, wqb_ref, kvag_ref, wkvb_ref, gq_ref, gk_ref, off_ref,
                   *refs):
    a_ref, q_ref, k_ref, v_ref, latp_ref, pep_ref, lats_ref, pes_ref, vn_ref = refs[n_prev:]
    i = pl.program_id(0)
    is_sample = i >= n_prompt_tiles
    o2 = 2 * GMLP_WIDTH
    o3 = o2 + Q_LORA_RANK
    o4 = o3 + KV_LORA_RANK
    part = TILE

    def rows_of(p):
        r0 = p * part
        rows = slice(r0, r0 + part)
        x = jnp.where(is_sample, xs_ref[rows, :], xp_ref[rows, :])
        h = _modulate(_rms(x) * ln_ref[...], sc_ref[0], sh_ref[0])
        z = jnp.dot(h.astype(BF16), win_ref[...], preferred_element_type=F32)

        u = _gelu(z[:, :GMLP_WIDTH])
        v = _gelu(z[:, GMLP_WIDTH:o2])
        gv = gv_ref[...]
        vn_parts = []
        for g in range(GMLP_GROUPS):
            sl = slice(g * LANES, (g + 1) * LANES)
            vn_parts.append(_rms(v[:, sl]) * gv[:, sl])
        for g in range(GMLP_GROUPS):
            sl = slice(g * LANES, (g + 1) * LANES)
            vb = vn_parts[g].astype(BF16)
            for c in range(part // GMLP_CHUNK):
                crow = slice(c * GMLP_CHUNK, (c + 1) * GMLP_CHUNK)
                s = jnp.dot(gw_ref[0, g], vb[crow], preferred_element_type=F32) + gb_ref[0, g]
                a_ref[r0 + c * GMLP_CHUNK:r0 + (c + 1) * GMLP_CHUNK, sl] = (u[crow, sl] * s).astype(BF16)

        cos = cos_ref[rows, :]
        sin = sin_ref[rows, :]

        ql = _rms(z[:, o2:o3]) * qag_ref[...]
        qq = jnp.dot(ql.astype(BF16), wqb_ref[...], preferred_element_type=F32)
        gq = gq_ref[...]
        for hd in range(MLA_HEADS):
            base = hd * 3 * LANES
            nope = qq[:, base:base + LANES]
            rope = qq[:, base + LANES:base + 2 * LANES] * cos + qq[:, base + 2 * LANES:base + 3 * LANES] * sin
            ss = jnp.sum(nope * nope, axis=-1, keepdims=True) + jnp.sum(rope * rope, axis=-1, keepdims=True)
            rinv = lax.rsqrt(ss * (1.0 / QK_HEAD_DIM) + EPS)
            q_ref[hd, rows, :LANES] = (nope * rinv * gq[:, :LANES]).astype(BF16)
            q_ref[hd, rows, LANES:] = (rope * rinv * gq[:, LANES:] + off_ref[0:1, :]).astype(BF16)

        ckv = _rms(z[:, o3:o4]) * kvag_ref[...]
        kpe = z[:, o4:o4 + LANES] * cos + z[:, o4 + LANES:o4 + 2 * LANES] * sin
        kv = jnp.dot(ckv.astype(BF16), wkvb_ref[...], preferred_element_type=F32)
        gk = gk_ref[...]
        kpe_ss = jnp.sum(kpe * kpe, axis=-1, keepdims=True)
        for hd in range(MLA_HEADS):
            nope = kv[:, hd * LANES:(hd + 1) * LANES]
            ss = jnp.sum(nope * nope, axis=-1, keepdims=True) + kpe_ss
            rinv = lax.rsqrt(ss * (1.0 / QK_HEAD_DIM) + EPS)
            k_ref[hd, rows, :LANES] = (nope * rinv * gk[:, :LANES]).astype(BF16)
            k_ref[hd, rows, LANES:] = (kpe * rinv * gk[:, LANES:] + off_ref[1:2, :]).astype(BF16)
            v_ref[hd, rows, :] = kv[:, (MLA_HEADS + hd) * LANES:(MLA_HEADS + hd + 1) * LANES].astype(BF16)
        return rows, vn_parts, ckv, kpe

    per_layer = [rows_of(0)]

    @pl.when(jnp.logical_not(is_sample))
    def _():
        for rows, _, ckv, kpe in per_layer:
            latp_ref[0, rows, :] = ckv
            pep_ref[0, rows, :] = kpe[:, :QK_ROPE_DIM]

    @pl.when(is_sample)
    def _():
        for rows, vn_parts, ckv, kpe in per_layer:
            lats_ref[0, rows, :] = ckv
            pes_ref[0, rows, :] = kpe[:, :QK_ROPE_DIM]
            for g in range(GMLP_GROUPS):
                vn_ref[0, rows, g * LANES:(g + 1) * LANES] = vn_parts[g]


def _mix_in_call(xp, xs, sc1, sh1, w, layer, depth, prev):
    rp, rs = xp.shape[0], xs.shape[0]
    npt = rp // TILE
    nt = npt + rs // TILE
    r = rp + rs
    last = npt - 1

    def full(a):
        nd = a.ndim
        return pl.BlockSpec(a.shape, lambda i: (0,) * nd)

    def variant(a):
        nd = a.ndim
        return pl.BlockSpec((1,) + a.shape[1:], lambda i: (i // npt,) + (0,) * (nd - 1))

    row = lambda width: pl.BlockSpec((TILE, width), lambda i: (i, 0))
    head = lambda width: pl.BlockSpec((MLA_HEADS, TILE, width), lambda i: (0, i, 0))
    in_specs = [
        pl.BlockSpec((TILE, D_MODEL), lambda i: (jnp.minimum(i, last), 0)),
        pl.BlockSpec((TILE, D_MODEL), lambda i: (0, 0)),
        variant(sc1), variant(sh1), full(w["ln1"]), full(w["w_in"]),
        row(LANES), row(LANES),
        full(w["gv"]), variant(w["gw"]), variant(w["gb"]),
        full(w["q_a_g"]), full(w["w_qb"]), full(w["kv_a_g"]), full(w["w_kvb"]),
        full(w["gq"]), full(w["gk"]), full(w["off"]),
    ] + [pl.BlockSpec(memory_space=pl.ANY)] * len(prev)
    prompt_plane = lambda width: pl.BlockSpec((1, TILE, width), lambda i: (layer, jnp.minimum(i, last), 0))
    sample_plane = lambda width: pl.BlockSpec((1, TILE, width), lambda i: (layer, 0, 0))
    out_specs = [
        row(GMLP_WIDTH), head(QK_PAD), head(QK_PAD), head(V_HEAD_DIM),
        prompt_plane(KV_LORA_RANK), prompt_plane(QK_ROPE_DIM),
        sample_plane(KV_LORA_RANK), sample_plane(QK_ROPE_DIM), sample_plane(GMLP_WIDTH),
    ]
    out_shape = [
        jax.ShapeDtypeStruct((r, GMLP_WIDTH), BF16),
        jax.ShapeDtypeStruct((MLA_HEADS, r, QK_PAD), BF16),
        jax.ShapeDtypeStruct((MLA_HEADS, r, QK_PAD), BF16),
        jax.ShapeDtypeStruct((MLA_HEADS, r, V_HEAD_DIM), BF16),
        jax.ShapeDtypeStruct((depth, rp, KV_LORA_RANK), F32),
        jax.ShapeDtypeStruct((depth, rp, QK_ROPE_DIM), F32),
        jax.ShapeDtypeStruct((depth, rs, KV_LORA_RANK), F32),
        jax.ShapeDtypeStruct((depth, rs, QK_ROPE_DIM), F32),
        jax.ShapeDtypeStruct((depth, rs, GMLP_WIDTH), F32),
    ]
    n_fixed = len(in_specs) - len(prev)
    n_own = 4
    return pl.pallas_call(
        functools.partial(_mix_in_kernel, npt, len(prev)),
        grid=(nt,), in_specs=in_specs, out_specs=out_specs, out_shape=out_shape,
        input_output_aliases={n_fixed + j: n_own + j for j in range(len(prev))},
        compiler_params=_cparams(1), name="mix_in",
    )(xp, xs, sc1, sh1, w["ln1"], w["w_in"], w["cos"], w["sin"], w["gv"], w["gw"], w["gb"],
      w["q_a_g"], w["w_qb"], w["kv_a_g"], w["w_kvb"], w["gq"], w["gk"], w["off"], *prev)


def _cache_kv_kernel(lat_ref, pe_ref, wkvb_ref, gk_ref, off_ref, k_ref, v_ref):
    lat = lat_ref[0]
    pe_t = pe_ref[0, 0]
    kpe = jnp.concatenate([pe_t, jnp.zeros_like(pe_t)], axis=0).T
    kv = jnp.dot(lat.astype(BF16), wkvb_ref[0], preferred_element_type=F32)
    gk = gk_ref[0]
    kpe_ss = jnp.sum(kpe * kpe, axis=-1, keepdims=True)
    for hd in range(MLA_HEADS):
        nope = kv[:, hd * LANES:(hd + 1) * LANES]
        ss = jnp.sum(nope * nope, axis=-1, keepdims=True) + kpe_ss
        rinv = lax.rsqrt(ss * (1.0 / QK_HEAD_DIM) + EPS)
        k_ref[0, hd, :, :LANES] = (nope * rinv * gk[:, :LANES]).astype(BF16)
        k_ref[0, hd, :, LANES:] = (kpe * rinv * gk[:, LANES:] + off_ref[0, 1:2, :]).astype(BF16)
        v_ref[0, hd] = kv[:, (MLA_HEADS + hd) * LANES:(MLA_HEADS + hd + 1) * LANES].astype(BF16)


def _cache_kv_call(lat, pe_t, w_kvb, gk, off):
    depth, rows, _ = lat.shape
    t = min(1024, pe_t.shape[-1])
    per_stream = pe_t.shape[-1] // t
    return pl.pallas_call(
        _cache_kv_kernel,
        grid=(depth, rows // t),
        in_specs=[
            pl.BlockSpec((1, t, KV_LORA_RANK), lambda l, i: (l, i, 0)),
            pl.BlockSpec((1, 1, QK_ROPE_DIM, t), lambda l, i: (l, i // per_stream, 0, i % per_stream)),
            pl.BlockSpec((1,) + w_kvb.shape[1:], lambda l, i: (l, 0, 0)),
            pl.BlockSpec((1, 1, QK_PAD), lambda l, i: (l, 0, 0)),
            pl.BlockSpec((1, 2, LANES), lambda l, i: (l, 0, 0)),
        ],
        out_specs=[
            pl.BlockSpec((1, MLA_HEADS, t, QK_PAD), lambda l, i: (l, 0, i, 0)),
            pl.BlockSpec((1, MLA_HEADS, t, V_HEAD_DIM), lambda l, i: (l, 0, i, 0)),
        ],
        out_shape=[
            jax.ShapeDtypeStruct((depth, MLA_HEADS, rows, QK_PAD), BF16),
            jax.ShapeDtypeStruct((depth, MLA_HEADS, rows, V_HEAD_DIM), BF16),
        ],
        compiler_params=_cparams(2), name="cache_kv",
    )(lat, pe_t, w_kvb, gk, off)


def _attn_step(q, k, v, m, l, acc, mask):
    s = lax.dot_general(q, k, (((1,), (1,)), ((), ())), preferred_element_type=F32)
    if mask is not None:
        s = jnp.where(mask, s, -1e30)
    m_new = jnp.maximum(m, jnp.max(s, axis=-1, keepdims=True))
    alpha = jnp.exp2(m - m_new)
    p = jnp.exp2(s - m_new)
    l = alpha * l + jnp.sum(p, axis=-1, keepdims=True)
    acc = alpha * acc + jnp.dot(p.astype(BF16), v, preferred_element_type=F32)
    return m_new, l, acc


def _bounded_step(q, k, v_aug, acc, mask):
    s = lax.dot_general(q, k, (((1,), (1,)), ((), ())), preferred_element_type=F32)
    if mask is not None:
        s = jnp.where(mask, s, -1e30)
    return acc + jnp.dot(jnp.exp2(s).astype(BF16), v_aug, preferred_element_type=F32)


def _prompt_attn_kernel(bounded_ref, q_ref, k_ref, v_ref, o_ref):
    qi = pl.program_id(1)
    row = lax.broadcasted_iota(jnp.int32, (ATT_TK, ATT_TK), 0)
    col = lax.broadcasted_iota(jnp.int32, (ATT_TK, ATT_TK), 1)
    mask = (col // CHUNK) <= (row // CHUNK)

    def q_half(h, half):
        return q_ref[h, half * ATT_TK:(half + 1) * ATT_TK, :]

    def kv(h, j):
        off = pl.multiple_of(j * ATT_TK, ATT_TK)
        return k_ref[h, pl.ds(off, ATT_TK), :], v_ref[h, pl.ds(off, ATT_TK), :]

    def out(h, half):
        return o_ref.at[half * ATT_TK:(half + 1) * ATT_TK, h * V_HEAD_DIM:(h + 1) * V_HEAD_DIM]

    @pl.when(bounded_ref[0] > 0)
    def _():
        lane = lax.broadcasted_iota(jnp.int32, (ATT_TK, LANES), 1)
        ones_col = jnp.where(lane == 0, 1.0, 0.0).astype(BF16)

        def kv_aug(h, j):
            k, v = kv(h, j)
            return k, jnp.concatenate([v, ones_col], axis=1)

        def blocks(first, count, accs):
            accs = [list(a) for a in accs]
            for u in range(count):
                for h in range(ATT_HEADS):
                    k, v = kv_aug(h, first + u)
                    for half in range(2):
                        accs[h][half] = _bounded_step(q_half(h, half), k, v, accs[h][half], None)
            return tuple(tuple(a) for a in accs)

        zero = jnp.zeros((ATT_TK, 2 * LANES), F32)
        n_full = 2 * qi
        n_trips = n_full // ATT_UNROLL
        accs = lax.fori_loop(0, n_trips, lambda j, c: blocks(ATT_UNROLL * j, ATT_UNROLL, c),
                             ((zero, zero),) * ATT_HEADS)
        done = ATT_UNROLL * n_trips
        part = ATT_UNROLL // 2
        while part >= 2:
            accs = lax.cond((n_full & part) != 0, lambda c, d=done, p=part: blocks(d, p, c), lambda c: c, accs)
            done = done + (n_full & part)
            part //= 2
        for h in range(ATT_HEADS):
            acc_a, acc_b = accs[h]
            k, v = kv_aug(h, 2 * qi)
            acc_a = _bounded_step(q_half(h, 0), k, v, acc_a, mask)
            acc_b = _bounded_step(q_half(h, 1), k, v, acc_b, None)
            k, v = kv_aug(h, 2 * qi + 1)
            acc_b = _bounded_step(q_half(h, 1), k, v, acc_b, mask)
            out(h, 0)[...] = (acc_a[:, :LANES] / acc_a[:, LANES:LANES + 1]).astype(BF16)
            out(h, 1)[...] = (acc_b[:, :LANES] / acc_b[:, LANES:LANES + 1]).astype(BF16)

    @pl.when(bounded_ref[0] == 0)
    def _():
        def init():
            return (jnp.full((ATT_TK, 1), -1e30, F32), jnp.zeros((ATT_TK, 1), F32),
                    jnp.zeros((ATT_TK, V_HEAD_DIM), F32))

        for h in range(ATT_HEADS):
            qa, qb = q_half(h, 0), q_half(h, 1)

            def body(j, carry, h=h, qa=qa, qb=qb):
                k, v = kv(h, j)
                return _attn_step(qa, k, v, *carry[0], None), _attn_step(qb, k, v, *carry[1], None)

            sa, sb = lax.fori_loop(0, 2 * qi, body, (init(), init()))
            k, v = kv(h, 2 * qi)
            sa = _attn_step(qa, k, v, *sa, mask)
            sb = _attn_step(qb, k, v, *sb, None)
            k, v = kv(h, 2 * qi + 1)
            sb = _attn_step(qb, k, v, *sb, mask)
            out(h, 0)[...] = (sa[2] / sa[1]).astype(BF16)
            out(h, 1)[...] = (sb[2] / sb[1]).astype(BF16)


def _prompt_attn_call(bounded, q, k, v, rp):
    r = q.shape[1]
    resident = lambda arr: pl.BlockSpec((ATT_HEADS,) + arr.shape[1:], lambda h, i, b: (h, 0, 0),
                                        pipeline_mode=pl.Buffered(1))
    return pl.pallas_call(
        _prompt_attn_kernel,
        grid_spec=pltpu.PrefetchScalarGridSpec(
            num_scalar_prefetch=1, grid=(MLA_HEADS // ATT_HEADS, rp // ATT_TQ),
            in_specs=[
                pl.BlockSpec((ATT_HEADS, ATT_TQ, QK_PAD), lambda h, i, b: (h, i, 0)),
                resident(k), resident(v),
            ],
            out_specs=pl.BlockSpec((ATT_TQ, ATT_HEADS * V_HEAD_DIM), lambda h, i, b: (i, h))),
        out_shape=jax.ShapeDtypeStruct((rp, MLA_HEADS * V_HEAD_DIM), BF16),
        compiler_params=_cparams(2), name="prompt_attn",
    )(bounded, q, k, v)


def _sample_attn_kernel(q_ref, kp_ref, vp_ref, kn_ref, vn_ref, o_ref):
    q = q_ref[0]
    nt = (((1,), (1,)), ((), ()))
    s1 = lax.dot_general(q, kp_ref[0, 0], nt, preferred_element_type=F32)
    s2 = lax.dot_general(q, kn_ref[0], nt, preferred_element_type=F32)
    m = jnp.maximum(jnp.max(s1, axis=-1, keepdims=True), jnp.max(s2, axis=-1, keepdims=True))
    p1 = jnp.exp2(s1 - m)
    p2 = jnp.exp2(s2 - m)
    l = jnp.sum(p1, axis=-1, keepdims=True) + jnp.sum(p2, axis=-1, keepdims=True)
    o = (jnp.dot(p1.astype(BF16), vp_ref[0, 0], preferred_element_type=F32)
         + jnp.dot(p2.astype(BF16), vn_ref[0], preferred_element_type=F32))
    o_ref[...] = (o / l).astype(BF16)


def _sample_attn_call(layer, q, k, v, k_past, v_past, rp, n_seq, s_len, past):
    first = rp // s_len
    return pl.pallas_call(
        _sample_attn_kernel,
        grid=(n_seq, MLA_HEADS),
        in_specs=[
            pl.BlockSpec((1, s_len, QK_PAD), lambda b, h: (h, first + b, 0)),
            pl.BlockSpec((1, 1, past, QK_PAD), lambda b, h: (layer, h, b, 0)),
            pl.BlockSpec((1, 1, past, V_HEAD_DIM), lambda b, h: (layer, h, b, 0)),
            pl.BlockSpec((1, s_len, QK_PAD), lambda b, h: (h, first + b, 0)),
            pl.BlockSpec((1, s_len, V_HEAD_DIM), lambda b, h: (h, first + b, 0)),
        ],
        out_specs=pl.BlockSpec((s_len, V_HEAD_DIM), lambda b, h: (b, h)),
        out_shape=jax.ShapeDtypeStruct((n_seq * s_len, MLA_HEADS * V_HEAD_DIM), BF16),
        compiler_params=_cparams(2), name="sample_attn",
    )(q, k_past, v_past, k, v)


def _route(logits_t, bias_col):
    scores = 1.0 / (1.0 + jnp.exp(-logits_t))
    biased = scores + bias_col
    s_rows = [scores[e:e + 1, :] for e in range(N_EXPERTS)]
    b_rows = [biased[e:e + 1, :] for e in range(N_EXPERTS)]
    group_score = []
    for g in range(N_GROUPS):
        rows = b_rows[g * EXPERTS_PER_GROUP:(g + 1) * EXPERTS_PER_GROUP]
        best = None
        for a in range(EXPERTS_PER_GROUP):
            for b in range(a + 1, EXPERTS_PER_GROUP):
                pair = rows[a] + rows[b]
                best = pair if best is None else jnp.maximum(best, pair)
        group_score.append(best)
    best_group = jnp.zeros_like(group_score[0], dtype=jnp.int32)
    best_val = group_score[0]
    for g in range(1, N_GROUPS):
        better = group_score[g] > best_val
        best_group = jnp.where(better, g, best_group)
        best_val = jnp.where(better, group_score[g], best_val)
    selected = []
    for e in range(N_EXPERTS):
        g = e // EXPERTS_PER_GROUP
        rank = jnp.zeros_like(best_group)
        for j in range(g * EXPERTS_PER_GROUP, (g + 1) * EXPERTS_PER_GROUP):
            if j == e:
                continue
            ahead = b_rows[j] > b_rows[e]
            if j < e:
                ahead = ahead | (b_rows[j] == b_rows[e])
            rank = rank + ahead.astype(jnp.int32)
        selected.append((best_group == g) & (rank < 2))
    denom = jnp.zeros_like(s_rows[0])
    for e in range(N_EXPERTS):
        denom = denom + jnp.where(selected[e], s_rows[e], 0.0)
    classes = [selected[g * EXPERTS_PER_GROUP + a] & selected[g * EXPERTS_PER_GROUP + b]
               for g in range(N_GROUPS) for a, b in EXPERT_PAIRS]
    return [jnp.where(selected[e], s_rows[e] / denom, 0.0) for e in range(N_EXPERTS)], classes


def _mix_out_kernel(n_prompt_tiles, a_ref, bp_ref, bs_ref, xp_ref, xs_ref, wa_ref, wb_ref, g1_ref, sc_ref,
                    sh_ref, g2_ref, ln_ref, rw_ref, rb_ref, sg_ref, su_ref, sd_ref,
                    ysh_ref, h2e_ref, grp_ref, rank_ref, cnt_ref, gt_ref, oh_ref, carry_ref):
    i = pl.program_id(0)
    is_sample = i >= n_prompt_tiles

    @pl.when(i == 0)
    def _():
        carry_ref[...] = jnp.zeros_like(carry_ref)

    x = jnp.where(is_sample, xs_ref[...], xp_ref[...])
    b = jnp.where(is_sample, bs_ref[...], bp_ref[...])
    mix = (jnp.dot(a_ref[...], wa_ref[...], preferred_element_type=F32)
           + jnp.dot(b, wb_ref[...], preferred_element_type=F32))
    xm = _gated_add(x, g1_ref[0], mix)
    h2 = _modulate(_rms(xm) * ln_ref[...], sc_ref[0], sh_ref[0])
    h2e_ref[:, :D_MODEL] = h2
    h2_hi = h2.astype(BF16)
    act = _silu(jnp.dot(h2_hi, sg_ref[...], preferred_element_type=F32)) * jnp.dot(
        h2_hi, su_ref[...], preferred_element_type=F32)
    shared = jnp.dot(act.astype(BF16), sd_ref[...], preferred_element_type=F32)
    ysh_ref[...] = _gated_add(xm, g2_ref[0], shared)
    h2_lo = (h2 - h2_hi.astype(F32)).astype(BF16)
    by_hi = jnp.dot(h2_hi, rw_ref[...], preferred_element_type=F32)
    logits = (by_hi[:, :LANES] + by_hi[:, LANES:]
              + jnp.dot(h2_lo, rw_ref[:, :LANES], preferred_element_type=F32))
    gate_rows, classes = _route(logits.T[:N_EXPERTS, :], rb_ref[...])
    gt_ref[...] = jnp.zeros_like(gt_ref)
    for e in range(N_EXPERTS):
        gt_ref[e:e + 1, :] = gate_rows[e]
    h2e_ref[:, D_MODEL:] = gt_ref[...].T

    oh_ref[...] = jnp.zeros_like(oh_ref)
    for c in range(N_CLASSES):
        oh_ref[c:c + 1, :] = jnp.where(classes[c], 1.0, 0.0)
    onehot = oh_ref[...]
    r_idx = lax.broadcasted_iota(jnp.int32, (TILE, TILE), 0)
    c_idx = lax.broadcasted_iota(jnp.int32, (TILE, TILE), 1)
    upper = jnp.where(r_idx <= c_idx, 1.0, 0.0).astype(BF16)
    cum = jnp.dot(onehot.astype(BF16), upper, preferred_element_type=F32)
    carry = carry_ref[...]
    rank = jnp.sum(onehot * (cum - 1.0 + carry[:, :1]), axis=0, keepdims=True)
    cls_id = lax.broadcasted_iota(jnp.int32, (CLASS_ROWS, TILE), 0).astype(F32)
    cls_row = jnp.sum(onehot * cls_id, axis=0, keepdims=True).astype(jnp.int32)
    grp_ref[0] = jnp.broadcast_to(cls_row, (8, TILE))
    rank_ref[0] = jnp.broadcast_to(rank.astype(jnp.int32), (8, TILE))
    carry = carry + jnp.sum(onehot, axis=1, keepdims=True)
    carry_ref[...] = carry
    cnt_ref[...] = carry


def _mix_out_call(a, bp, bs, xp, xs, g1, sc2, sh2, g2, w):
    rp, rs = xp.shape[0], xs.shape[0]
    npt = rp // TILE
    nt = npt + rs // TILE
    r = rp + rs
    last = npt - 1

    def full(arr):
        nd = arr.ndim
        return pl.BlockSpec(arr.shape, lambda i: (0,) * nd)

    def variant(arr):
        nd = arr.ndim
        return pl.BlockSpec((1,) + arr.shape[1:], lambda i: (i // npt,) + (0,) * (nd - 1))

    row = lambda width: pl.BlockSpec((TILE, width), lambda i: (i, 0))
    prow = lambda width: pl.BlockSpec((TILE, width), lambda i: (jnp.minimum(i, last), 0))
    srow = lambda width: pl.BlockSpec((TILE, width), lambda i: (0, 0))
    return pl.pallas_call(
        functools.partial(_mix_out_kernel, npt),
        grid=(nt,),
        in_specs=[row(GMLP_WIDTH), prow(GMLP_WIDTH), srow(GMLP_WIDTH), prow(D_MODEL), srow(D_MODEL),
                  full(w["w_out_a"]), full(w["w_out_b"]), variant(g1), variant(sc2), variant(sh2),
                  variant(g2), full(w["ln2"]), full(w["rw"]), full(w["rb"]),
                  full(w["sg"]), full(w["su"]), full(w["sd"])],
        out_specs=[row(D_MODEL), row(H2E_COLS),
                   pl.BlockSpec((1, 8, TILE), lambda i: (i, 0, 0)),
                   pl.BlockSpec((1, 8, TILE), lambda i: (i, 0, 0)),
                   pl.BlockSpec((CLASS_ROWS, LANES), lambda i: (0, 0))],
        out_shape=[jax.ShapeDtypeStruct((r, D_MODEL), F32),
                   jax.ShapeDtypeStruct((r, H2E_COLS), F32),
                   jax.ShapeDtypeStruct((nt, 8, TILE), jnp.int32),
                   jax.ShapeDtypeStruct((nt, 8, TILE), jnp.int32),
                   jax.ShapeDtypeStruct((CLASS_ROWS, LANES), F32)],
        scratch_shapes=[pltpu.VMEM((LANES, TILE), F32), pltpu.VMEM((CLASS_ROWS, TILE), F32),
                        pltpu.VMEM((CLASS_ROWS, LANES), F32)],
        compiler_params=_cparams(1), name="mix_out",
    )(a, bp, bs, xp, xs, w["w_out_a"], w["w_out_b"], g1, sc2, sh2, g2, w["ln2"], w["rw"], w["rb"],
      w["sg"], w["su"], w["sd"])


def _silu(x):
    return x / (1.0 + jnp.exp(-x))


def _invert_positions(pos_ref, pad_lo_ref, pad_hi_ref, src_ref):
    def clear(s, carry):
        src_ref[s] = 0
        return carry

    for g in range(N_GROUPS):
        lax.fori_loop(pad_lo_ref[g], pad_hi_ref[g], clear, 0)
    lax.fori_loop(pad_hi_ref[N_GROUPS - 1], src_ref.shape[0], clear, 0)

    def place(i, carry):
        src_ref[pos_ref[i]] = i
        return carry

    lax.fori_loop(0, pos_ref.shape[0], place, 0, unroll=ROW_DMA_UNROLL)


def _gather_rows(idx_ref, base, src_ref, dst_ref, sem):
    for r in range(TILE):
        pltpu.make_async_copy(src_ref.at[pl.ds(idx_ref[base + r], 1)], dst_ref.at[pl.ds(r, 1)], sem).start()


def _wait_rows(src_ref, dst_ref, sem):
    pltpu.make_async_copy(src_ref.at[pl.ds(0, TILE)], dst_ref, sem).wait()


def _routed_kernel(grp_ref, on_ref, need_ref, pos_ref, pad_lo_ref, pad_hi_ref, h_ref, wg_ref, wu_ref, wd_ref,
                   o_ref, buf_ref, src_ref, sem):
    t = pl.program_id(0)
    n = pl.num_programs(0)
    slot = t % 2
    on = on_ref[t] > 0

    @pl.when(t == 0)
    def _():
        _invert_positions(pos_ref, pad_lo_ref, pad_hi_ref, src_ref)

    @pl.when((t == 0) & on)
    def _():
        _gather_rows(src_ref, 0, h_ref, buf_ref.at[0], sem.at[0])

    @pl.when((on_ref[jnp.minimum(t + 1, n - 1)] > 0) & (t + 1 < n))
    def _():
        _gather_rows(src_ref, (t + 1) * TILE, h_ref, buf_ref.at[1 - slot], sem.at[1 - slot])

    @pl.when(jnp.logical_not(on))
    def _():
        o_ref[...] = jnp.zeros_like(o_ref)

    @pl.when(on)
    def _():
        _wait_rows(h_ref, buf_ref.at[slot], sem.at[slot])
        first = grp_ref[t] * EXPERTS_PER_GROUP
        o_ref[...] = jnp.zeros_like(o_ref)
        for e in range(EXPERTS_PER_GROUP):
            @pl.when(need_ref[t * EXPERTS_PER_GROUP + e] > 0)
            def _(e=e):
                h = buf_ref[slot, :, :D_MODEL]
                hg = jnp.dot(h, wg_ref[0, e], preferred_element_type=F32)
                hu = jnp.dot(h, wu_ref[0, e], preferred_element_type=F32)
                lane = lax.broadcasted_iota(jnp.int32, (TILE, LANES), 1)
                gate = jnp.sum(jnp.where(lane == first + e, buf_ref[slot, :, D_MODEL:], 0.0),
                               axis=-1, keepdims=True)
                o_ref[...] += jnp.dot(_silu(hg) * hu, wd_ref[0, e], preferred_element_type=F32) * gate


def _routed_call(tile_grp, tile_on, tile_need, pos, pad_lo, pad_hi, n_slots, h2e, wg, wu, wd, layer):
    nts = n_slots // TILE
    wspec = lambda arr: pl.BlockSpec((1,) + arr.shape[1:],
                                     lambda t, grp, *_: (layer * N_GROUPS + grp[t], 0, 0, 0))
    return pl.pallas_call(
        _routed_kernel,
        grid_spec=pltpu.PrefetchScalarGridSpec(
            num_scalar_prefetch=6, grid=(nts,),
            in_specs=[pl.BlockSpec(memory_space=pl.ANY), wspec(wg), wspec(wu), wspec(wd)],
            out_specs=pl.BlockSpec((TILE, D_MODEL), lambda t, *_: (t, 0)),
            scratch_shapes=[pltpu.VMEM((2, TILE, H2E_COLS), F32), pltpu.SMEM((n_slots,), jnp.int32),
                            pltpu.SemaphoreType.DMA((2,))]),
        out_shape=jax.ShapeDtypeStruct((n_slots, D_MODEL), F32),
        compiler_params=pltpu.CompilerParams(dimension_semantics=("arbitrary",),
                                             vmem_limit_bytes=ROUTED_VMEM_LIMIT), name="moe_routed",
    )(tile_grp, tile_on, tile_need, pos, pad_lo, pad_hi, h2e, wg, wu, wd)


def _combine_kernel(n_prompt_tiles, pos_ref, ysh_ref, g2_ref, routed_ref, yp_ref, ys_ref, buf_ref, sem):
    i = pl.program_id(0)
    n = pl.num_programs(0)
    slot = i % 2

    @pl.when(i == 0)
    def _():
        _gather_rows(pos_ref, 0, routed_ref, buf_ref.at[0], sem.at[0])

    @pl.when(i + 1 < n)
    def _():
        _gather_rows(pos_ref, (i + 1) * TILE, routed_ref, buf_ref.at[1 - slot], sem.at[1 - slot])

    _wait_rows(routed_ref, buf_ref.at[slot], sem.at[slot])
    y = _gated_add(ysh_ref[...], g2_ref[0], buf_ref[slot])

    @pl.when(i < n_prompt_tiles)
    def _():
        yp_ref[...] = y

    @pl.when(i >= n_prompt_tiles)
    def _():
        ys_ref[...] = y


def _combine_call(pos, ysh, g2, routed, rp):
    r = ysh.shape[0]
    rs = r - rp
    npt = rp // TILE
    last = npt - 1
    return pl.pallas_call(
        functools.partial(_combine_kernel, npt),
        grid_spec=pltpu.PrefetchScalarGridSpec(
            num_scalar_prefetch=1, grid=(r // TILE,),
            in_specs=[pl.BlockSpec((TILE, D_MODEL), lambda i, pos: (i, 0)),
                      pl.BlockSpec((1,) + g2.shape[1:], lambda i, pos: (i // npt, 0, 0)),
                      pl.BlockSpec(memory_space=pl.ANY)],
            out_specs=[pl.BlockSpec((TILE, D_MODEL), lambda i, pos: (jnp.minimum(i, last), 0)),
                       pl.BlockSpec((TILE, D_MODEL), lambda i, pos: (0, 0))],
            scratch_shapes=[pltpu.VMEM((2, TILE, D_MODEL), F32), pltpu.SemaphoreType.DMA((2,))]),
        out_shape=[jax.ShapeDtypeStruct((rp, D_MODEL), F32),
                   jax.ShapeDtypeStruct((rs, D_MODEL), F32)],
        compiler_params=_cparams(1), name="moe_combine",
    )(pos, ysh, g2, routed)


def _moe(h2e, cls, rank, counts, ysh, g2, experts, layer, rp):
    r = h2e.shape[0]
    n_pairs = len(EXPERT_PAIRS)
    n_sorted_tiles = r // TILE + N_GROUPS
    cnt_c = counts[:N_CLASSES, 0].astype(jnp.int32).reshape(N_GROUPS, n_pairs)
    cnt = jnp.sum(cnt_c, axis=1)
    tiles_g = (cnt + TILE - 1) // TILE
    end_g = jnp.cumsum(tiles_g)
    off_g = (end_g - tiles_g) * TILE
    start_c = (off_g[:, None] + jnp.cumsum(cnt_c, axis=1) - cnt_c).reshape(N_CLASSES)
    end_c = start_c + cnt_c.reshape(N_CLASSES)
    cls_flat = cls[:, 0, :].reshape(r)
    in_class = cls_flat[:, None] == jnp.arange(N_CLASSES, dtype=jnp.int32)[None, :]
    pos = rank[:, 0, :].reshape(r) + jnp.sum(jnp.where(in_class, start_c[None, :], 0), axis=1)
    t_idx = jnp.arange(n_sorted_tiles, dtype=jnp.int32)
    tile_grp = jnp.zeros_like(t_idx)
    for g in range(N_GROUPS - 1):
        tile_grp = tile_grp + (t_idx >= end_g[g]).astype(jnp.int32)
    tile_on = (t_idx < end_g[N_GROUPS - 1]).astype(jnp.int32)
    lo = t_idx[:, None] * TILE
    overlap = ((start_c[None, :] < lo + TILE) & (end_c[None, :] > lo)
               & (end_c > start_c)[None, :])
    member = jnp.array([[int(e in pair) for e in range(EXPERTS_PER_GROUP)] for pair in EXPERT_PAIRS] * N_GROUPS,
                       dtype=jnp.int32)
    tile_need = jnp.max(overlap[:, :, None].astype(jnp.int32) * member[None], axis=1).reshape(-1)
    routed = _routed_call(tile_grp, tile_on, tile_need, pos, off_g + cnt, end_g * TILE,
                          n_sorted_tiles * TILE, h2e, *experts, layer)
    return _combine_call(pos, ysh, g2, routed, rp)


def _rot_half_cols(wcols):
    half = QK_ROPE_DIM // 2
    return jnp.concatenate([-wcols[:, half:], wcols[:, :half]], axis=1)


def _pad_cols(wcols, width):
    return jnp.pad(wcols, ((0, 0), (0, width - wcols.shape[1])))


def _split_bf16(w):
    hi = w.astype(BF16)
    lo = (w - hi.astype(F32)).astype(BF16)
    return jnp.concatenate([hi, lo], axis=1)


def _rope_tables(seq, past, s_len, n_seq):
    half = QK_ROPE_DIM // 2
    inv = ROPE_THETA ** (-jnp.arange(half, dtype=F32) / half)
    inv_wide = jnp.concatenate([inv, inv, jnp.zeros((LANES - QK_ROPE_DIM,), F32)])
    live = (jnp.arange(LANES) < QK_ROPE_DIM).astype(F32)

    def cos_sin(pos):
        ang = pos.astype(F32)[:, None] * inv_wide[None, :]
        return jnp.cos(ang), jnp.sin(ang)

    cc, sc = cos_sin(jnp.arange(0, seq, GMLP_CHUNK, dtype=jnp.int32))
    cf, sf = cos_sin(jnp.arange(GMLP_CHUNK, dtype=jnp.int32))
    cos_p = ((cc[:, None, :] * cf[None, :, :] - sc[:, None, :] * sf[None, :, :]) * live).reshape(seq, LANES)
    sin_p = ((sc[:, None, :] * cf[None, :, :] + cc[:, None, :] * sf[None, :, :]) * live).reshape(seq, LANES)
    cos_s, sin_s = cos_sin(past + jnp.arange(s_len, dtype=jnp.int32))
    cos = jnp.concatenate([cos_p, jnp.tile(cos_s * live, (n_seq, 1))], axis=0)
    sin = jnp.concatenate([sin_p, jnp.tile(sin_s * live, (n_seq, 1))], axis=0)
    return cos, sin


def _layer_weights(l, p, cos, sin):
    o1 = GMLP_WIDTH
    o2 = 2 * GMLP_WIDTH
    o3 = o2 + Q_LORA_RANK
    o4 = o3 + KV_LORA_RANK
    w_in = p["w_in"][l]
    kpe_cols = w_in[:, o4:]
    w_in2 = jnp.concatenate([w_in[:, :o4], _pad_cols(kpe_cols, LANES),
                             _pad_cols(_rot_half_cols(kpe_cols), LANES)], axis=1).astype(BF16)
    w_qb = p["w_qb"][l]
    q_parts = []
    for hd in range(MLA_HEADS):
        base = hd * QK_HEAD_DIM
        rope_cols = w_qb[:, base + QK_NOPE_DIM:base + QK_HEAD_DIM]
        q_parts += [w_qb[:, base:base + QK_NOPE_DIM], _pad_cols(rope_cols, LANES),
                    _pad_cols(_rot_half_cols(rope_cols), LANES)]
    w_qb2 = jnp.concatenate(q_parts, axis=1).astype(BF16)
    w_kvb = p["w_kvb"][l].reshape(KV_LORA_RANK, MLA_HEADS, QK_NOPE_DIM + V_HEAD_DIM)
    w_kvb2 = jnp.concatenate([w_kvb[:, :, :QK_NOPE_DIM].reshape(KV_LORA_RANK, -1),
                              w_kvb[:, :, QK_NOPE_DIM:].reshape(KV_LORA_RANK, -1)], axis=1).astype(BF16)
    qscale = LOG2E / math.sqrt(QK_HEAD_DIM)
    gq = _pad_cols(p["q_norm_g"][l][None, :] * qscale, QK_PAD)
    gk = _pad_cols(p["k_norm_g"][l][None, :], QK_PAD)
    bound = (QK_HEAD_DIM * qscale * SCORE_BOUND_MARGIN * jnp.max(jnp.abs(p["q_norm_g"][l]))
             * jnp.max(jnp.abs(p["k_norm_g"][l])))
    bounded = bound <= MAX_SCORE_BOUND
    pad_lane = jnp.arange(LANES) == QK_ROPE_DIM
    off = jnp.stack([jnp.where(pad_lane, 1.0, 0.0),
                     jnp.where(pad_lane & bounded, -bound, 0.0)]).astype(F32)

    ws = p["gmlp_ws"][l]
    tri = jnp.tril(jnp.ones((GMLP_CHUNK, GMLP_CHUNK), dtype=bool))
    wt = jnp.where(tri[None], ws, 0.0)
    hc = GMLP_CHUNK // 2
    top = wt[:, :hc, :hc]
    zero = jnp.zeros_like(top)
    wt_s = jnp.concatenate([jnp.concatenate([top, zero], axis=2), jnp.concatenate([zero, top], axis=2)], axis=1)
    gw = jnp.stack([wt, wt_s]).astype(BF16)
    b = p["gmlp_b"][l]
    b_s = jnp.concatenate([b[:, :hc], b[:, :hc]], axis=1)
    gb = jnp.broadcast_to(jnp.stack([b, b_s])[..., None], (2, GMLP_GROUPS, GMLP_CHUNK, LANES)).astype(F32)

    w_out = p["w_out"][l].astype(BF16)
    return dict(
        sg=p["sh_w_gate"][l].astype(BF16),
        su=p["sh_w_up"][l].astype(BF16), sd=p["sh_w_down"][l].astype(BF16),
        ln1=p["ln1_g"][l][None, :], w_in=w_in2, cos=cos, sin=sin,
        gv=p["gmlp_v_g"][l].reshape(1, GMLP_WIDTH), gw=gw, gb=gb,
        q_a_g=p["q_a_g"][l][None, :], w_qb=w_qb2, kv_a_g=p["kv_a_g"][l][None, :], w_kvb=w_kvb2,
        gq=gq, gk=gk, w_out_a=w_out[:GMLP_WIDTH], w_out_b=w_out[GMLP_WIDTH:],
        ln2=p["ln2_g"][l][None, :], rw=_split_bf16(_pad_cols(p["router_w"], LANES)),
        rb=p["router_bias"].reshape(N_EXPERTS, 1), off=off,
        bounded=bounded.astype(jnp.int32).reshape(1),
    )


def kernel(x_prompt, x_sample, cache_kv_latent, cache_k_rope, c_prompt, c_sample, w_ada, b_ada, ln1_g, w_in,
           gmlp_v_g, gmlp_ws, gmlp_b, q_a_g, w_qb, kv_a_g, w_kvb, q_norm_g, k_norm_g, w_out, ln2_g, router_w,
           router_bias, exp_w_gate, exp_w_up, exp_w_down, sh_w_gate, sh_w_up, sh_w_down):
    p = dict(w_in=w_in, gmlp_v_g=gmlp_v_g, gmlp_ws=gmlp_ws, gmlp_b=gmlp_b, q_a_g=q_a_g, w_qb=w_qb,
             kv_a_g=kv_a_g, w_kvb=w_kvb, q_norm_g=q_norm_g, k_norm_g=k_norm_g, w_out=w_out, ln1_g=ln1_g,
             ln2_g=ln2_g, router_w=router_w, router_bias=router_bias, exp_w_gate=exp_w_gate,
             exp_w_up=exp_w_up, exp_w_down=exp_w_down, sh_w_gate=sh_w_gate, sh_w_up=sh_w_up,
             sh_w_down=sh_w_down)
    batch, seq, _ = x_prompt.shape
    n_seq, s_len, _ = x_sample.shape
    depth, _, past, _ = cache_kv_latent.shape
    assert batch == 1 and s_len == SUB and n_seq == N_SUB and n_seq * s_len == TILE
    assert seq % TILE == 0 and seq % ATT_TQ == 0 and past % GMLP_CHUNK == 0 and past % CHUNK == 0
    rp = seq
    rs = n_seq * s_len

    cos, sin = _rope_tables(seq, past, s_len, n_seq)

    c_all = jnp.concatenate([jnp.broadcast_to(c_prompt, (N_SUB, D_MODEL)), c_sample], axis=0)
    mod = _ada_call(c_all, w_ada, b_ada)
    mod = mod.reshape(depth, 2, N_SUB, 6, D_MODEL)

    weights = [_layer_weights(l, p, cos, sin) for l in range(depth)]
    experts = (exp_w_gate.reshape(depth * N_GROUPS, EXPERTS_PER_GROUP, D_MODEL, EXPERT_FF),
               exp_w_up.reshape(depth * N_GROUPS, EXPERTS_PER_GROUP, D_MODEL, EXPERT_FF),
               exp_w_down.reshape(depth * N_GROUPS, EXPERTS_PER_GROUP, EXPERT_FF, D_MODEL))
    lat_all = cache_kv_latent.reshape(depth, n_seq * past, KV_LORA_RANK)
    pe_all = jnp.swapaxes(cache_k_rope, 2, 3)
    k_past, v_past = _cache_kv_call(lat_all, pe_all, jnp.stack([w["w_kvb"] for w in weights]),
                                    jnp.stack([w["gk"] for w in weights]),
                                    jnp.stack([w["off"] for w in weights]))

    xp = x_prompt.reshape(rp, D_MODEL)
    xs = x_sample.reshape(rs, D_MODEL)
    planes = tuple(jnp.zeros((depth, rows, width), F32) for rows, width in
                   ((rp, KV_LORA_RANK), (rp, QK_ROPE_DIM), (rs, KV_LORA_RANK), (rs, QK_ROPE_DIM),
                    (rs, GMLP_WIDTH)))
    for l in range(depth):
        w = weights[l]
        sh1, sc1, g1, sh2, sc2, g2 = [mod[l, :, :, j, :] for j in range(6)]
        a, q, k, v, *planes = _mix_in_call(xp, xs, sc1, sh1, w, l, depth, tuple(planes))
        bp = _prompt_attn_call(w["bounded"], q, k, v, rp)
        bs = _sample_attn_call(l, q, k, v, k_past, v_past, rp, n_seq, s_len, past)
        ysh, h2e, grp, rank, counts = _mix_out_call(a, bp, bs, xp, xs, g1, sc2, sh2, g2, w)
        xp, xs = _moe(h2e, grp, rank, counts, ysh, g2, experts, l, rp)
    lat_p, pe_p, lat_s, pe_s, v_s = planes
    return (xp.reshape(batch, seq, D_MODEL), xs.reshape(n_seq, s_len, D_MODEL),
            lat_p.reshape(depth, batch, seq, KV_LORA_RANK), pe_p.reshape(depth, batch, seq, QK_ROPE_DIM),
            lat_s.reshape(depth, n_seq, s_len, KV_LORA_RANK), pe_s.reshape(depth, n_seq, s_len, QK_ROPE_DIM),
            v_s.reshape(depth, n_seq, s_len, GMLP_WIDTH))
```

```python
import functools
import math

import jax
import jax.numpy as jnp
from jax import lax
from jax.experimental import pallas as pl
from jax.experimental.pallas import tpu as pltpu

F32 = jnp.float32
BF16 = jnp.bfloat16

D_MODEL = 1024
CHUNK = 64
GMLP_WIDTH = 512
GMLP_GROUPS = 4
GMLP_CHUNK = 128
MLA_HEADS = 4
QK_NOPE_DIM = 128
QK_ROPE_DIM = 64
QK_HEAD_DIM = 192
V_HEAD_DIM = 128
Q_LORA_RANK = 384
KV_LORA_RANK = 256
ROPE_THETA = 10000.0
N_EXPERTS = 16
N_GROUPS = 4
EXPERTS_PER_GROUP = 4
EXPERT_FF = 512
EPS = 1e-6

LANES = 128
TILE = 512
SUB = 64
N_SUB = TILE // SUB
QK_PAD = 256
IN_COLS = 2 * GMLP_WIDTH + Q_LORA_RANK + KV_LORA_RANK + 2 * LANES
Q_COLS = MLA_HEADS * 3 * LANES
H2E_COLS = D_MODEL + LANES
ROW_DMA_UNROLL = 8
EXPERT_PAIRS = tuple((a, b) for a in range(EXPERTS_PER_GROUP) for b in range(a + 1, EXPERTS_PER_GROUP))
N_CLASSES = N_GROUPS * len(EXPERT_PAIRS)
CLASS_ROWS = 32
ATT_TK = 512
ATT_TQ = 2 * ATT_TK
ATT_UNROLL = 8
ATT_HEADS = 2
VMEM_LIMIT = 56 * 1024 * 1024
ROUTED_VMEM_LIMIT = 62 * 1024 * 1024
LOG2E = 1.4426950408889634
SCORE_BOUND_MARGIN = 1.02
MAX_SCORE_BOUND = 48.0


def _cparams(n_axes):
    return pltpu.CompilerParams(dimension_semantics=("arbitrary",) * n_axes,
                                vmem_limit_bytes=VMEM_LIMIT)


def _rms(x, eps=EPS):
    return x * lax.rsqrt(jnp.mean(x * x, axis=-1, keepdims=True) + eps)


def _gelu(x):
    c = math.sqrt(2.0 / math.pi)
    return 0.5 * x * (1.0 + jnp.tanh(c * (x + 0.044715 * (x * x * x))))


def _modulate(h, scale, shift):
    h3 = h.reshape(h.shape[0] // SUB, SUB, h.shape[-1])
    h3 = h3 * (1.0 + scale[:, None, :]) + shift[:, None, :]
    return h3.reshape(h.shape)


def _gated_add(x, gate, y):
    y3 = y.reshape(y.shape[0] // SUB, SUB, y.shape[-1]) * gate[:, None, :]
    return x + y3.reshape(y.shape)


def _ada_kernel(c_ref, w_ref, b_ref, o_ref):
    c = c_ref[...]
    cs = (c / (1.0 + jnp.exp(-c))).astype(BF16)
    w = w_ref[0].astype(BF16)
    o_ref[0] = jnp.dot(cs, w, preferred_element_type=F32) + b_ref[0]


def _ada_call(c_all, w_ada, b_ada):
    depth = w_ada.shape[0]
    nblk = w_ada.shape[2] // D_MODEL
    return pl.pallas_call(
        _ada_kernel,
        grid=(depth, nblk),
        in_specs=[
            pl.BlockSpec((16, D_MODEL), lambda l, j: (0, 0)),
            pl.BlockSpec((1, D_MODEL, D_MODEL), lambda l, j: (l, 0, j)),
            pl.BlockSpec((1, 1, D_MODEL), lambda l, j: (l, 0, j)),
        ],
        out_specs=pl.BlockSpec((1, 16, D_MODEL), lambda l, j: (l, 0, j)),
        out_shape=jax.ShapeDtypeStruct((depth, 16, w_ada.shape[2]), F32),
        compiler_params=_cparams(2),
        name="ada_mod",
    )(c_all, w_ada, b_ada.reshape(depth, 1, -1))


def _mix_in_kernel(n_prompt_tiles, n_prev, xp_ref, xs_ref, sc_ref, sh_ref, ln_ref, win_ref, cos_ref, sin_ref,
                   gv_ref, gw_ref, gb_ref, qag_ref, wqb_ref, kvag_ref, wkvb_ref, gq_ref, gk_ref, off_ref,
                   *refs):
    a_ref, q_ref, k_ref, v_ref, latp_ref, pep_ref, lats_ref, pes_ref, vn_ref = refs[n_prev:]
    i = pl.program_id(0)
    is_sample = i >= n_prompt_tiles
    o2 = 2 * GMLP_WIDTH
    o3 = o2 + Q_LORA_RANK
    o4 = o3 + KV_LORA_RANK
    part = TILE

    def rows_of(p):
        r0 = p * part
        rows = slice(r0, r0 + part)
        x = jnp.where(is_sample, xs_ref[rows, :], xp_ref[rows, :])
        h = _modulate(_rms(x) * ln_ref[...], sc_ref[0], sh_ref[0])
        z = jnp.dot(h.astype(BF16), win_ref[...], preferred_element_type=F32)

        u = _gelu(z[:, :GMLP_WIDTH])
        v = _gelu(z[:, GMLP_WIDTH:o2])
        gv = gv_ref[...]
        vn_parts = []
        for g in range(GMLP_GROUPS):
            sl = slice(g * LANES, (g + 1) * LANES)
            vn_parts.append(_rms(v[:, sl]) * gv[:, sl])
        for g in range(GMLP_GROUPS):
            sl = slice(g * LANES, (g + 1) * LANES)
            vb = vn_parts[g].astype(BF16)
            for c in range(part // GMLP_CHUNK):
                crow = slice(c * GMLP_CHUNK, (c + 1) * GMLP_CHUNK)
                s = jnp.dot(gw_ref[0, g], vb[crow], preferred_element_type=F32) + gb_ref[0, g]
                a_ref[r0 + c * GMLP_CHUNK:r0 + (c + 1) * GMLP_CHUNK, sl] = (u[crow, sl] * s).astype(BF16)

        cos = cos_ref[rows, :]
        sin = sin_ref[rows, :]

        ql = _rms(z[:, o2:o3]) * qag_ref[...]
        qq = jnp.dot(ql.astype(BF16), wqb_ref[...], preferred_element_type=F32)
        gq = gq_ref[...]
        for hd in range(MLA_HEADS):
            base = hd * 3 * LANES
            nope = qq[:, base:base + LANES]
            rope = qq[:, base + LANES:base + 2 * LANES] * cos + qq[:, base + 2 * LANES:base + 3 * LANES] * sin
            ss = jnp.sum(nope * nope, axis=-1, keepdims=True) + jnp.sum(rope * rope, axis=-1, keepdims=True)
            rinv = lax.rsqrt(ss * (1.0 / QK_HEAD_DIM) + EPS)
            q_ref[hd, rows, :LANES] = (nope * rinv * gq[:, :LANES]).astype(BF16)
            q_ref[hd, rows, LANES:] = (rope * rinv * gq[:, LANES:] + off_ref[0:1, :]).astype(BF16)

        ckv = _rms(z[:, o3:o4]) * kvag_ref[...]
        kpe = z[:, o4:o4 + LANES] * cos + z[:, o4 + LANES:o4 + 2 * LANES] * sin
        kv = jnp.dot(ckv.astype(BF16), wkvb_ref[...], preferred_element_type=F32)
        gk = gk_ref[...]
        kpe_ss = jnp.sum(kpe * kpe, axis=-1, keepdims=True)
        for hd in range(MLA_HEADS):
            nope = kv[:, hd * LANES:(hd + 1) * LANES]
            ss = jnp.sum(nope * nope, axis=-1, keepdims=True) + kpe_ss
            rinv = lax.rsqrt(ss * (1.0 / QK_HEAD_DIM) + EPS)
            k_ref[hd, rows, :LANES] = (nope * rinv * gk[:, :LANES]).astype(BF16)
            k_ref[hd, rows, LANES:] = (kpe * rinv * gk[:, LANES:] + off_ref[1:2, :]).astype(BF16)
            v_ref[hd, rows, :] = kv[:, (MLA_HEADS + hd) * LANES:(MLA_HEADS + hd + 1) * LANES].astype(BF16)
        return rows, vn_parts, ckv, kpe

    per_layer = [rows_of(0)]

    @pl.when(jnp.logical_not(is_sample))
    def _():
        for rows, _, ckv, kpe in per_layer:
            latp_ref[0, rows, :] = ckv
            pep_ref[0, rows, :] = kpe[:, :QK_ROPE_DIM]

    @pl.when(is_sample)
    def _():
        for rows, vn_parts, ckv, kpe in per_layer:
            lats_ref[0, rows, :] = ckv
            pes_ref[0, rows, :] = kpe[:, :QK_ROPE_DIM]
            for g in range(GMLP_GROUPS):
                vn_ref[0, rows, g * LANES:(g + 1) * LANES] = vn_parts[g]


def _mix_in_call(xp, xs, sc1, sh1, w, layer, depth, prev):
    rp, rs = xp.shape[0], xs.shape[0]
    npt = rp // TILE
    nt = npt + rs // TILE
    r = rp + rs
    last = npt - 1

    def full(a):
        nd = a.ndim
        return pl.BlockSpec(a.shape, lambda i: (0,) * nd)

    def variant(a):
        nd = a.ndim
        return pl.BlockSpec((1,) + a.shape[1:], lambda i: (i // npt,) + (0,) * (nd - 1))

    row = lambda width: pl.BlockSpec((TILE, width), lambda i: (i, 0))
    head = lambda width: pl.BlockSpec((MLA_HEADS, TILE, width), lambda i: (0, i, 0))
    in_specs = [
        pl.BlockSpec((TILE, D_MODEL), lambda i: (jnp.minimum(i, last), 0)),
        pl.BlockSpec((TILE, D_MODEL), lambda i: (0, 0)),
        variant(sc1), variant(sh1), full(w["ln1"]), full(w["w_in"]),
        row(LANES), row(LANES),
        full(w["gv"]), variant(w["gw"]), variant(w["gb"]),
        full(w["q_a_g"]), full(w["w_qb"]), full(w["kv_a_g"]), full(w["w_kvb"]),
        full(w["gq"]), full(w["gk"]), full(w["off"]),
    ] + [pl.BlockSpec(memory_space=pl.ANY)] * len(prev)
    prompt_plane = lambda width: pl.BlockSpec((1, TILE, width), lambda i: (layer, jnp.minimum(i, last), 0))
    sample_plane = lambda width: pl.BlockSpec((1, TILE, width), lambda i: (layer, 0, 0))
    out_specs = [
        row(GMLP_WIDTH), head(QK_PAD), head(QK_PAD), head(V_HEAD_DIM),
        prompt_plane(KV_LORA_RANK), prompt_plane(QK_ROPE_DIM),
        sample_plane(KV_LORA_RANK), sample_plane(QK_ROPE_DIM), sample_plane(GMLP_WIDTH),
    ]
    out_shape = [
        jax.ShapeDtypeStruct((r, GMLP_WIDTH), BF16),
        jax.ShapeDtypeStruct((MLA_HEADS, r, QK_PAD), BF16),
        jax.ShapeDtypeStruct((MLA_HEADS, r, QK_PAD), BF16),
        jax.ShapeDtypeStruct((MLA_HEADS, r, V_HEAD_DIM), BF16),
        jax.ShapeDtypeStruct((depth, rp, KV_LORA_RANK), F32),
        jax.ShapeDtypeStruct((depth, rp, QK_ROPE_DIM), F32),
        jax.ShapeDtypeStruct((depth, rs, KV_LORA_RANK), F32),
        jax.ShapeDtypeStruct((depth, rs, QK_ROPE_DIM), F32),
        jax.ShapeDtypeStruct((depth, rs, GMLP_WIDTH), F32),
    ]
    n_fixed = len(in_specs) - len(prev)
    n_own = 4
    return pl.pallas_call(
        functools.partial(_mix_in_kernel, npt, len(prev)),
        grid=(nt,), in_specs=in_specs, out_specs=out_specs, out_shape=out_shape,
        input_output_aliases={n_fixed + j: n_own + j for j in range(len(prev))},
        compiler_params=_cparams(1), name="mix_in",
    )(xp, xs, sc1, sh1, w["ln1"], w["w_in"], w["cos"], w["sin"], w["gv"], w["gw"], w["gb"],
      w["q_a_g"], w["w_qb"], w["kv_a_g"], w["w_kvb"], w["gq"], w["gk"], w["off"], *prev)


def _cache_kv_kernel(lat_ref, pe_ref, wkvb_ref, gk_ref, off_ref, k_ref, v_ref):
    lat = lat_ref[0]
    pe_t = pe_ref[0, 0]
    kpe = jnp.concatenate([pe_t, jnp.zeros_like(pe_t)], axis=0).T
    kv = jnp.dot(lat.astype(BF16), wkvb_ref[0], preferred_element_type=F32)
    gk = gk_ref[0]
    kpe_ss = jnp.sum(kpe * kpe, axis=-1, keepdims=True)
    for hd in range(MLA_HEADS):
        nope = kv[:, hd * LANES:(hd + 1) * LANES]
        ss = jnp.sum(nope * nope, axis=-1, keepdims=True) + kpe_ss
        rinv = lax.rsqrt(ss * (1.0 / QK_HEAD_DIM) + EPS)
        k_ref[0, hd, :, :LANES] = (nope * rinv * gk[:, :LANES]).astype(BF16)
        k_ref[0, hd, :, LANES:] = (kpe * rinv * gk[:, LANES:] + off_ref[0, 1:2, :]).astype(BF16)
        v_ref[0, hd] = kv[:, (MLA_HEADS + hd) * LANES:(MLA_HEADS + hd + 1) * LANES].astype(BF16)


def _cache_kv_call(lat, pe_t, w_kvb, gk, off):
    depth, rows, _ = lat.shape
    t = min(1024, pe_t.shape[-1])
    per_stream = pe_t.shape[-1] // t
    return pl.pallas_call(
        _cache_kv_kernel,
        grid=(depth, rows // t),
        in_specs=[
            pl.BlockSpec((1, t, KV_LORA_RANK), lambda l, i: (l, i, 0)),
            pl.BlockSpec((1, 1, QK_ROPE_DIM, t), lambda l, i: (l, i // per_stream, 0, i % per_stream)),
            pl.BlockSpec((1,) + w_kvb.shape[1:], lambda l, i: (l, 0, 0)),
            pl.BlockSpec((1, 1, QK_PAD), lambda l, i: (l, 0, 0)),
            pl.BlockSpec((1, 2, LANES), lambda l, i: (l, 0, 0)),
        ],
        out_specs=[
            pl.BlockSpec((1, MLA_HEADS, t, QK_PAD), lambda l, i: (l, 0, i, 0)),
            pl.BlockSpec((1, MLA_HEADS, t, V_HEAD_DIM), lambda l, i: (l, 0, i, 0)),
        ],
        out_shape=[
            jax.ShapeDtypeStruct((depth, MLA_HEADS, rows, QK_PAD), BF16),
            jax.ShapeDtypeStruct((depth, MLA_HEADS, rows, V_HEAD_DIM), BF16),
        ],
        compiler_params=_cparams(2), name="cache_kv",
    )(lat, pe_t, w_kvb, gk, off)


def _attn_step(q, k, v, m, l, acc, mask):
    s = lax.dot_general(q, k, (((1,), (1,)), ((), ())), preferred_element_type=F32)
    if mask is not None:
        s = jnp.where(mask, s, -1e30)
    m_new = jnp.maximum(m, jnp.max(s, axis=-1, keepdims=True))
    alpha = jnp.exp2(m - m_new)
    p = jnp.exp2(s - m_new)
    l = alpha * l + jnp.sum(p, axis=-1, keepdims=True)
    acc = alpha * acc + jnp.dot(p.astype(BF16), v, preferred_element_type=F32)
    return m_new, l, acc


def _bounded_step(q, k, v_aug, acc, mask):
    s = lax.dot_general(q, k, (((1,), (1,)), ((), ())), preferred_element_type=F32)
    if mask is not None:
        s = jnp.where(mask, s, -1e30)
    return acc + jnp.dot(jnp.exp2(s).astype(BF16), v_aug, preferred_element_type=F32)


def _prompt_attn_kernel(bounded_ref, q_ref, k_ref, v_ref, o_ref):
    qi = pl.program_id(1)
    row = lax.broadcasted_iota(jnp.int32, (ATT_TK, ATT_TK), 0)
    col = lax.broadcasted_iota(jnp.int32, (ATT_TK, ATT_TK), 1)
    mask = (col // CHUNK) <= (row // CHUNK)

    def q_half(h, half):
        return q_ref[h, half * ATT_TK:(half + 1) * ATT_TK, :]

    def kv(h, j):
        off = pl.multiple_of(j * ATT_TK, ATT_TK)
        return k_ref[h, pl.ds(off, ATT_TK), :], v_ref[h, pl.ds(off, ATT_TK), :]

    def out(h, half):
        return o_ref.at[half * ATT_TK:(half + 1) * ATT_TK, h * V_HEAD_DIM:(h + 1) * V_HEAD_DIM]

    @pl.when(bounded_ref[0] > 0)
    def _():
        lane = lax.broadcasted_iota(jnp.int32, (ATT_TK, LANES), 1)
        ones_col = jnp.where(lane == 0, 1.0, 0.0).astype(BF16)

        def kv_aug(h, j):
            k, v = kv(h, j)
            return k, jnp.concatenate([v, ones_col], axis=1)

        def blocks(first, count, accs):
            accs = [list(a) for a in accs]
            for u in range(count):
                for h in range(ATT_HEADS):
                    k, v = kv_aug(h, first + u)
                    for half in range(2):
                        accs[h][half] = _bounded_step(q_half(h, half), k, v, accs[h][half], None)
            return tuple(tuple(a) for a in accs)

        zero = jnp.zeros((ATT_TK, 2 * LANES), F32)
        n_full = 2 * qi
        n_trips = n_full // ATT_UNROLL
        accs = lax.fori_loop(0, n_trips, lambda j, c: blocks(ATT_UNROLL * j, ATT_UNROLL, c),
                             ((zero, zero),) * ATT_HEADS)
        done = ATT_UNROLL * n_trips
        part = ATT_UNROLL // 2
        while part >= 2:
            accs = lax.cond((n_full & part) != 0, lambda c, d=done, p=part: blocks(d, p, c), lambda c: c, accs)
            done = done + (n_full & part)
            part //= 2
        for h in range(ATT_HEADS):
            acc_a, acc_b = accs[h]
            k, v = kv_aug(h, 2 * qi)
            acc_a = _bounded_step(q_half(h, 0), k, v, acc_a, mask)
            acc_b = _bounded_step(q_half(h, 1), k, v, acc_b, None)
            k, v = kv_aug(h, 2 * qi + 1)
            acc_b = _bounded_step(q_half(h, 1), k, v, acc_b, mask)
            out(h, 0)[...] = (acc_a[:, :LANES] / acc_a[:, LANES:LANES + 1]).astype(BF16)
            out(h, 1)[...] = (acc_b[:, :LANES] / acc_b[:, LANES:LANES + 1]).astype(BF16)

    @pl.when(bounded_ref[0] == 0)
    def _():
        def init():
            return (jnp.full((ATT_TK, 1), -1e30, F32), jnp.zeros((ATT_TK, 1), F32),
                    jnp.zeros((ATT_TK, V_HEAD_DIM), F32))

        for h in range(ATT_HEADS):
            qa, qb = q_half(h, 0), q_half(h, 1)

            def body(j, carry, h=h, qa=qa, qb=qb):
                k, v = kv(h, j)
                return _attn_step(qa, k, v, *carry[0], None), _attn_step(qb, k, v, *carry[1], None)

            sa, sb = lax.fori_loop(0, 2 * qi, body, (init(), init()))
            k, v = kv(h, 2 * qi)
            sa = _attn_step(qa, k, v, *sa, mask)
            sb = _attn_step(qb, k, v, *sb, None)
            k, v = kv(h, 2 * qi + 1)
            sb = _attn_step(qb, k, v, *sb, mask)
            out(h, 0)[...] = (sa[2] / sa[1]).astype(BF16)
            out(h, 1)[...] = (sb[2] / sb[1]).astype(BF16)


def _prompt_attn_call(bounded, q, k, v, rp):
    r = q.shape[1]
    resident = lambda arr: pl.BlockSpec((ATT_HEADS,) + arr.shape[1:], lambda h, i, b: (h, 0, 0),
                                        pipeline_mode=pl.Buffered(1))
    return pl.pallas_call(
        _prompt_attn_kernel,
        grid_spec=pltpu.PrefetchScalarGridSpec(
            num_scalar_prefetch=1, grid=(MLA_HEADS // ATT_HEADS, rp // ATT_TQ),
            in_specs=[
                pl.BlockSpec((ATT_HEADS, ATT_TQ, QK_PAD), lambda h, i, b: (h, i, 0)),
                resident(k), resident(v),
            ],
            out_specs=pl.BlockSpec((ATT_TQ, ATT_HEADS * V_HEAD_DIM), lambda h, i, b: (i, h))),
        out_shape=jax.ShapeDtypeStruct((rp, MLA_HEADS * V_HEAD_DIM), BF16),
        compiler_params=_cparams(2), name="prompt_attn",
    )(bounded, q, k, v)


def _sample_attn_kernel(q_ref, kp_ref, vp_ref, kn_ref, vn_ref, o_ref):
    nt = (((1,), (1,)), ((), ()))
    for h in range(MLA_HEADS):
        q = q_ref[h]
        s1 = lax.dot_general(q, kp_ref[0, h], nt, preferred_element_type=F32)
        s2 = lax.dot_general(q, kn_ref[h], nt, preferred_element_type=F32)
        m = jnp.maximum(jnp.max(s1, axis=-1, keepdims=True), jnp.max(s2, axis=-1, keepdims=True))
        p1 = jnp.exp2(s1 - m)
        p2 = jnp.exp2(s2 - m)
        l = jnp.sum(p1, axis=-1, keepdims=True) + jnp.sum(p2, axis=-1, keepdims=True)
        o = (jnp.dot(p1.astype(BF16), vp_ref[0, h], preferred_element_type=F32)
             + jnp.dot(p2.astype(BF16), vn_ref[h], preferred_element_type=F32))
        o_ref[:, h * V_HEAD_DIM:(h + 1) * V_HEAD_DIM] = (o / l).astype(BF16)


def _sample_attn_call(layer, q, k, v, k_past, v_past, rp, n_seq, s_len, past):
    first = rp // s_len
    new = lambda width: pl.BlockSpec((MLA_HEADS, s_len, width), lambda b: (0, first + b, 0))
    cached = lambda width: pl.BlockSpec((1, MLA_HEADS, past, width), lambda b: (layer, 0, b, 0))
    return pl.pallas_call(
        _sample_attn_kernel,
        grid=(n_seq,),
        in_specs=[new(QK_PAD), cached(QK_PAD), cached(V_HEAD_DIM), new(QK_PAD), new(V_HEAD_DIM)],
        out_specs=pl.BlockSpec((s_len, MLA_HEADS * V_HEAD_DIM), lambda b: (b, 0)),
        out_shape=jax.ShapeDtypeStruct((n_seq * s_len, MLA_HEADS * V_HEAD_DIM), BF16),
        compiler_params=_cparams(1), name="sample_attn",
    )(q, k_past, v_past, k, v)


def _route(logits_t, bias_col):
    scores = 1.0 / (1.0 + jnp.exp(-logits_t))
    biased = scores + bias_col
    s_rows = [scores[e:e + 1, :] for e in range(N_EXPERTS)]
    b_rows = [biased[e:e + 1, :] for e in range(N_EXPERTS)]
    group_score = []
    for g in range(N_GROUPS):
        rows = b_rows[g * EXPERTS_PER_GROUP:(g + 1) * EXPERTS_PER_GROUP]
        best = None
        for a in range(EXPERTS_PER_GROUP):
            for b in range(a + 1, EXPERTS_PER_GROUP):
                pair = rows[a] + rows[b]
                best = pair if best is None else jnp.maximum(best, pair)
        group_score.append(best)
    best_group = jnp.zeros_like(group_score[0], dtype=jnp.int32)
    best_val = group_score[0]
    for g in range(1, N_GROUPS):
        better = group_score[g] > best_val
        best_group = jnp.where(better, g, best_group)
        best_val = jnp.where(better, group_score[g], best_val)
    selected = []
    for e in range(N_EXPERTS):
        g = e // EXPERTS_PER_GROUP
        rank = jnp.zeros_like(best_group)
        for j in range(g * EXPERTS_PER_GROUP, (g + 1) * EXPERTS_PER_GROUP):
            if j == e:
                continue
            ahead = b_rows[j] > b_rows[e]
            if j < e:
                ahead = ahead | (b_rows[j] == b_rows[e])
            rank = rank + ahead.astype(jnp.int32)
        selected.append((best_group == g) & (rank < 2))
    denom = jnp.zeros_like(s_rows[0])
    for e in range(N_EXPERTS):
        denom = denom + jnp.where(selected[e], s_rows[e], 0.0)
    classes = [selected[g * EXPERTS_PER_GROUP + a] & selected[g * EXPERTS_PER_GROUP + b]
               for g in range(N_GROUPS) for a, b in EXPERT_PAIRS]
    return [jnp.where(selected[e], s_rows[e] / denom, 0.0) for e in range(N_EXPERTS)], classes


def _mix_out_kernel(n_prompt_tiles, a_ref, bp_ref, bs_ref, xp_ref, xs_ref, wa_ref, wb_ref, g1_ref, sc_ref,
                    sh_ref, g2_ref, ln_ref, rw_ref, rb_ref, sg_ref, su_ref, sd_ref,
                    ysh_ref, h2e_ref, grp_ref, rank_ref, cnt_ref, gt_ref, oh_ref, carry_ref):
    i = pl.program_id(0)
    is_sample = i >= n_prompt_tiles

    @pl.when(i == 0)
    def _():
        carry_ref[...] = jnp.zeros_like(carry_ref)

    x = jnp.where(is_sample, xs_ref[...], xp_ref[...])
    b = jnp.where(is_sample, bs_ref[...], bp_ref[...])
    mix = (jnp.dot(a_ref[...], wa_ref[...], preferred_element_type=F32)
           + jnp.dot(b, wb_ref[...], preferred_element_type=F32))
    xm = _gated_add(x, g1_ref[0], mix)
    h2 = _modulate(_rms(xm) * ln_ref[...], sc_ref[0], sh_ref[0])
    h2e_ref[:, :D_MODEL] = h2
    h2_hi = h2.astype(BF16)
    act = _silu(jnp.dot(h2_hi, sg_ref[...], preferred_element_type=F32)) * jnp.dot(
        h2_hi, su_ref[...], preferred_element_type=F32)
    shared = jnp.dot(act.astype(BF16), sd_ref[...], preferred_element_type=F32)
    ysh_ref[...] = _gated_add(xm, g2_ref[0], shared)
    h2_lo = (h2 - h2_hi.astype(F32)).astype(BF16)
    by_hi = jnp.dot(h2_hi, rw_ref[...], preferred_element_type=F32)
    logits = (by_hi[:, :LANES] + by_hi[:, LANES:]
              + jnp.dot(h2_lo, rw_ref[:, :LANES], preferred_element_type=F32))
    gate_rows, classes = _route(logits.T[:N_EXPERTS, :], rb_ref[...])
    gt_ref[...] = jnp.zeros_like(gt_ref)
    for e in range(N_EXPERTS):
        gt_ref[e:e + 1, :] = gate_rows[e]
    h2e_ref[:, D_MODEL:] = gt_ref[...].T

    oh_ref[...] = jnp.zeros_like(oh_ref)
    for c in range(N_CLASSES):
        oh_ref[c:c + 1, :] = jnp.where(classes[c], 1.0, 0.0)
    onehot = oh_ref[...]
    r_idx = lax.broadcasted_iota(jnp.int32, (TILE, TILE), 0)
    c_idx = lax.broadcasted_iota(jnp.int32, (TILE, TILE), 1)
    upper = jnp.where(r_idx <= c_idx, 1.0, 0.0).astype(BF16)
    cum = jnp.dot(onehot.astype(BF16), upper, preferred_element_type=F32)
    carry = carry_ref[...]
    rank = jnp.sum(onehot * (cum - 1.0 + carry[:, :1]), axis=0, keepdims=True)
    cls_id = lax.broadcasted_iota(jnp.int32, (CLASS_ROWS, TILE), 0).astype(F32)
    cls_row = jnp.sum(onehot * cls_id, axis=0, keepdims=True).astype(jnp.int32)
    grp_ref[0] = jnp.broadcast_to(cls_row, (8, TILE))
    rank_ref[0] = jnp.broadcast_to(rank.astype(jnp.int32), (8, TILE))
    carry = carry + jnp.sum(onehot, axis=1, keepdims=True)
    carry_ref[...] = carry
    cnt_ref[...] = carry


def _mix_out_call(a, bp, bs, xp, xs, g1, sc2, sh2, g2, w):
    rp, rs = xp.shape[0], xs.shape[0]
    npt = rp // TILE
    nt = npt + rs // TILE
    r = rp + rs
    last = npt - 1

    def full(arr):
        nd = arr.ndim
        return pl.BlockSpec(arr.shape, lambda i: (0,) * nd)

    def variant(arr):
        nd = arr.ndim
        return pl.BlockSpec((1,) + arr.shape[1:], lambda i: (i // npt,) + (0,) * (nd - 1))

    row = lambda width: pl.BlockSpec((TILE, width), lambda i: (i, 0))
    prow = lambda width: pl.BlockSpec((TILE, width), lambda i: (jnp.minimum(i, last), 0))
    srow = lambda width: pl.BlockSpec((TILE, width), lambda i: (0, 0))
    return pl.pallas_call(
        functools.partial(_mix_out_kernel, npt),
        grid=(nt,),
        in_specs=[row(GMLP_WIDTH), prow(GMLP_WIDTH), srow(GMLP_WIDTH), prow(D_MODEL), srow(D_MODEL),
                  full(w["w_out_a"]), full(w["w_out_b"]), variant(g1), variant(sc2), variant(sh2),
                  variant(g2), full(w["ln2"]), full(w["rw"]), full(w["rb"]),
                  full(w["sg"]), full(w["su"]), full(w["sd"])],
        out_specs=[row(D_MODEL), row(H2E_COLS),
                   pl.BlockSpec((1, 8, TILE), lambda i: (i, 0, 0)),
                   pl.BlockSpec((1, 8, TILE), lambda i: (i, 0, 0)),
                   pl.BlockSpec((CLASS_ROWS, LANES), lambda i: (0, 0))],
        out_shape=[jax.ShapeDtypeStruct((r, D_MODEL), F32),
                   jax.ShapeDtypeStruct((r, H2E_COLS), F32),
                   jax.ShapeDtypeStruct((nt, 8, TILE), jnp.int32),
                   jax.ShapeDtypeStruct((nt, 8, TILE), jnp.int32),
                   jax.ShapeDtypeStruct((CLASS_ROWS, LANES), F32)],
        scratch_shapes=[pltpu.VMEM((LANES, TILE), F32), pltpu.VMEM((CLASS_ROWS, TILE), F32),
                        pltpu.VMEM((CLASS_ROWS, LANES), F32)],
        compiler_params=_cparams(1), name="mix_out",
    )(a, bp, bs, xp, xs, w["w_out_a"], w["w_out_b"], g1, sc2, sh2, g2, w["ln2"], w["rw"], w["rb"],
      w["sg"], w["su"], w["sd"])


def _silu(x):
    return x / (1.0 + jnp.exp(-x))


def _invert_positions(pos_ref, pad_lo_ref, pad_hi_ref, src_ref):
    def clear(s, carry):
        src_ref[s] = 0
        return carry

    for g in range(N_GROUPS):
        lax.fori_loop(pad_lo_ref[g], pad_hi_ref[g], clear, 0)
    lax.fori_loop(pad_hi_ref[N_GROUPS - 1], src_ref.shape[0], clear, 0)

    def place(i, carry):
        src_ref[pos_ref[i]] = i
        return carry

    lax.fori_loop(0, pos_ref.shape[0], place, 0, unroll=ROW_DMA_UNROLL)


def _gather_rows(idx_ref, base, src_ref, dst_ref, sem):
    for r in range(TILE):
        pltpu.make_async_copy(src_ref.at[pl.ds(idx_ref[base + r], 1)], dst_ref.at[pl.ds(r, 1)], sem).start()


def _wait_rows(src_ref, dst_ref, sem):
    pltpu.make_async_copy(src_ref.at[pl.ds(0, TILE)], dst_ref, sem).wait()


def _routed_kernel(grp_ref, on_ref, need_ref, pos_ref, pad_lo_ref, pad_hi_ref, h_ref, wg_ref, wu_ref, wd_ref,
                   o_ref, buf_ref, src_ref, sem):
    t = pl.program_id(0)
    n = pl.num_programs(0)
    slot = t % 2
    on = on_ref[t] > 0

    @pl.when(t == 0)
    def _():
        _invert_positions(pos_ref, pad_lo_ref, pad_hi_ref, src_ref)

    @pl.when((t == 0) & on)
    def _():
        _gather_rows(src_ref, 0, h_ref, buf_ref.at[0], sem.at[0])

    @pl.when((on_ref[jnp.minimum(t + 1, n - 1)] > 0) & (t + 1 < n))
    def _():
        _gather_rows(src_ref, (t + 1) * TILE, h_ref, buf_ref.at[1 - slot], sem.at[1 - slot])

    @pl.when(jnp.logical_not(on))
    def _():
        o_ref[...] = jnp.zeros_like(o_ref)

    @pl.when(on)
    def _():
        _wait_rows(h_ref, buf_ref.at[slot], sem.at[slot])
        first = grp_ref[t] * EXPERTS_PER_GROUP
        o_ref[...] = jnp.zeros_like(o_ref)
        for e in range(EXPERTS_PER_GROUP):
            @pl.when(need_ref[t * EXPERTS_PER_GROUP + e] > 0)
            def _(e=e):
                h = buf_ref[slot, :, :D_MODEL]
                hg = jnp.dot(h, wg_ref[0, e], preferred_element_type=F32)
                hu = jnp.dot(h, wu_ref[0, e], preferred_element_type=F32)
                lane = lax.broadcasted_iota(jnp.int32, (TILE, LANES), 1)
                gate = jnp.sum(jnp.where(lane == first + e, buf_ref[slot, :, D_MODEL:], 0.0),
                               axis=-1, keepdims=True)
                o_ref[...] += jnp.dot(_silu(hg) * hu, wd_ref[0, e], preferred_element_type=F32) * gate


def _routed_call(tile_grp, tile_on, tile_need, pos, pad_lo, pad_hi, n_slots, h2e, wg, wu, wd, layer):
    nts = n_slots // TILE
    wspec = lambda arr: pl.BlockSpec((1,) + arr.shape[1:],
                                     lambda t, grp, *_: (layer * N_GROUPS + grp[t], 0, 0, 0))
    return pl.pallas_call(
        _routed_kernel,
        grid_spec=pltpu.PrefetchScalarGridSpec(
            num_scalar_prefetch=6, grid=(nts,),
            in_specs=[pl.BlockSpec(memory_space=pl.ANY), wspec(wg), wspec(wu), wspec(wd)],
            out_specs=pl.BlockSpec((TILE, D_MODEL), lambda t, *_: (t, 0)),
            scratch_shapes=[pltpu.VMEM((2, TILE, H2E_COLS), F32), pltpu.SMEM((n_slots,), jnp.int32),
                            pltpu.SemaphoreType.DMA((2,))]),
        out_shape=jax.ShapeDtypeStruct((n_slots, D_MODEL), F32),
        compiler_params=pltpu.CompilerParams(dimension_semantics=("arbitrary",),
                                             vmem_limit_bytes=ROUTED_VMEM_LIMIT), name="moe_routed",
    )(tile_grp, tile_on, tile_need, pos, pad_lo, pad_hi, h2e, wg, wu, wd)


def _combine_kernel(n_prompt_tiles, pos_ref, ysh_ref, g2_ref, routed_ref, yp_ref, ys_ref, buf_ref, sem):
    i = pl.program_id(0)
    n = pl.num_programs(0)
    slot = i % 2

    @pl.when(i == 0)
    def _():
        _gather_rows(pos_ref, 0, routed_ref, buf_ref.at[0], sem.at[0])

    @pl.when(i + 1 < n)
    def _():
        _gather_rows(pos_ref, (i + 1) * TILE, routed_ref, buf_ref.at[1 - slot], sem.at[1 - slot])

    _wait_rows(routed_ref, buf_ref.at[slot], sem.at[slot])
    y = _gated_add(ysh_ref[...], g2_ref[0], buf_ref[slot])

    @pl.when(i < n_prompt_tiles)
    def _():
        yp_ref[...] = y

    @pl.when(i >= n_prompt_tiles)
    def _():
        ys_ref[...] = y


def _combine_call(pos, ysh, g2, routed, rp):
    r = ysh.shape[0]
    rs = r - rp
    npt = rp // TILE
    last = npt - 1
    return pl.pallas_call(
        functools.partial(_combine_kernel, npt),
        grid_spec=pltpu.PrefetchScalarGridSpec(
            num_scalar_prefetch=1, grid=(r // TILE,),
            in_specs=[pl.BlockSpec((TILE, D_MODEL), lambda i, pos: (i, 0)),
                      pl.BlockSpec((1,) + g2.shape[1:], lambda i, pos: (i // npt, 0, 0)),
                      pl.BlockSpec(memory_space=pl.ANY)],
            out_specs=[pl.BlockSpec((TILE, D_MODEL), lambda i, pos: (jnp.minimum(i, last), 0)),
                       pl.BlockSpec((TILE, D_MODEL), lambda i, pos: (0, 0))],
            scratch_shapes=[pltpu.VMEM((2, TILE, D_MODEL), F32), pltpu.SemaphoreType.DMA((2,))]),
        out_shape=[jax.ShapeDtypeStruct((rp, D_MODEL), F32),
                   jax.ShapeDtypeStruct((rs, D_MODEL), F32)],
        compiler_params=_cparams(1), name="moe_combine",
    )(pos, ysh, g2, routed)


def _moe(h2e, cls, rank, counts, ysh, g2, experts, layer, rp):
    r = h2e.shape[0]
    n_pairs = len(EXPERT_PAIRS)
    n_sorted_tiles = r // TILE + N_GROUPS
    cnt_c = counts[:N_CLASSES, 0].astype(jnp.int32).reshape(N_GROUPS, n_pairs)
    cnt = jnp.sum(cnt_c, axis=1)
    tiles_g = (cnt + TILE - 1) // TILE
    end_g = jnp.cumsum(tiles_g)
    off_g = (end_g - tiles_g) * TILE
    start_c = (off_g[:, None] + jnp.cumsum(cnt_c, axis=1) - cnt_c).reshape(N_CLASSES)
    end_c = start_c + cnt_c.reshape(N_CLASSES)
    cls_flat = cls[:, 0, :].reshape(r)
    in_class = cls_flat[:, None] == jnp.arange(N_CLASSES, dtype=jnp.int32)[None, :]
    pos = rank[:, 0, :].reshape(r) + jnp.sum(jnp.where(in_class, start_c[None, :], 0), axis=1)
    t_idx = jnp.arange(n_sorted_tiles, dtype=jnp.int32)
    tile_grp = jnp.zeros_like(t_idx)
    for g in range(N_GROUPS - 1):
        tile_grp = tile_grp + (t_idx >= end_g[g]).astype(jnp.int32)
    tile_on = (t_idx < end_g[N_GROUPS - 1]).astype(jnp.int32)
    lo = t_idx[:, None] * TILE
    overlap = ((start_c[None, :] < lo + TILE) & (end_c[None, :] > lo)
               & (end_c > start_c)[None, :])
    member = jnp.array([[int(e in pair) for e in range(EXPERTS_PER_GROUP)] for pair in EXPERT_PAIRS] * N_GROUPS,
                       dtype=jnp.int32)
    tile_need = jnp.max(overlap[:, :, None].astype(jnp.int32) * member[None], axis=1).reshape(-1)
    routed = _routed_call(tile_grp, tile_on, tile_need, pos, off_g + cnt, end_g * TILE,
                          n_sorted_tiles * TILE, h2e, *experts, layer)
    return _combine_call(pos, ysh, g2, routed, rp)


def _rot_half_cols(wcols):
    half = QK_ROPE_DIM // 2
    return jnp.concatenate([-wcols[:, half:], wcols[:, :half]], axis=1)


def _pad_cols(wcols, width):
    return jnp.pad(wcols, ((0, 0), (0, width - wcols.shape[1])))


def _split_bf16(w):
    hi = w.astype(BF16)
    lo = (w - hi.astype(F32)).astype(BF16)
    return jnp.concatenate([hi, lo], axis=1)


def _rope_tables(seq, past, s_len, n_seq):
    half = QK_ROPE_DIM // 2
    inv = ROPE_THETA ** (-jnp.arange(half, dtype=F32) / half)
    inv_wide = jnp.concatenate([inv, inv, jnp.zeros((LANES - QK_ROPE_DIM,), F32)])
    live = (jnp.arange(LANES) < QK_ROPE_DIM).astype(F32)

    def cos_sin(pos):
        ang = pos.astype(F32)[:, None] * inv_wide[None, :]
        return jnp.cos(ang), jnp.sin(ang)

    cc, sc = cos_sin(jnp.arange(0, seq, GMLP_CHUNK, dtype=jnp.int32))
    cf, sf = cos_sin(jnp.arange(GMLP_CHUNK, dtype=jnp.int32))
    cos_p = ((cc[:, None, :] * cf[None, :, :] - sc[:, None, :] * sf[None, :, :]) * live).reshape(seq, LANES)
    sin_p = ((sc[:, None, :] * cf[None, :, :] + cc[:, None, :] * sf[None, :, :]) * live).reshape(seq, LANES)
    cos_s, sin_s = cos_sin(past + jnp.arange(s_len, dtype=jnp.int32))
    cos = jnp.concatenate([cos_p, jnp.tile(cos_s * live, (n_seq, 1))], axis=0)
    sin = jnp.concatenate([sin_p, jnp.tile(sin_s * live, (n_seq, 1))], axis=0)
    return cos, sin


def _layer_weights(l, p, cos, sin):
    o1 = GMLP_WIDTH
    o2 = 2 * GMLP_WIDTH
    o3 = o2 + Q_LORA_RANK
    o4 = o3 + KV_LORA_RANK
    w_in = p["w_in"][l]
    kpe_cols = w_in[:, o4:]
    w_in2 = jnp.concatenate([w_in[:, :o4], _pad_cols(kpe_cols, LANES),
                             _pad_cols(_rot_half_cols(kpe_cols), LANES)], axis=1).astype(BF16)
    w_qb = p["w_qb"][l]
    q_parts = []
    for hd in range(MLA_HEADS):
        base = hd * QK_HEAD_DIM
        rope_cols = w_qb[:, base + QK_NOPE_DIM:base + QK_HEAD_DIM]
        q_parts += [w_qb[:, base:base + QK_NOPE_DIM], _pad_cols(rope_cols, LANES),
                    _pad_cols(_rot_half_cols(rope_cols), LANES)]
    w_qb2 = jnp.concatenate(q_parts, axis=1).astype(BF16)
    w_kvb = p["w_kvb"][l].reshape(KV_LORA_RANK, MLA_HEADS, QK_NOPE_DIM + V_HEAD_DIM)
    w_kvb2 = jnp.concatenate([w_kvb[:, :, :QK_NOPE_DIM].reshape(KV_LORA_RANK, -1),
                              w_kvb[:, :, QK_NOPE_DIM:].reshape(KV_LORA_RANK, -1)], axis=1).astype(BF16)
    qscale = LOG2E / math.sqrt(QK_HEAD_DIM)
    gq = _pad_cols(p["q_norm_g"][l][None, :] * qscale, QK_PAD)
    gk = _pad_cols(p["k_norm_g"][l][None, :], QK_PAD)
    bound = (QK_HEAD_DIM * qscale * SCORE_BOUND_MARGIN * jnp.max(jnp.abs(p["q_norm_g"][l]))
             * jnp.max(jnp.abs(p["k_norm_g"][l])))
    bounded = bound <= MAX_SCORE_BOUND
    pad_lane = jnp.arange(LANES) == QK_ROPE_DIM
    off = jnp.stack([jnp.where(pad_lane, 1.0, 0.0),
                     jnp.where(pad_lane & bounded, -bound, 0.0)]).astype(F32)

    ws = p["gmlp_ws"][l]
    tri = jnp.tril(jnp.ones((GMLP_CHUNK, GMLP_CHUNK), dtype=bool))
    wt = jnp.where(tri[None], ws, 0.0)
    hc = GMLP_CHUNK // 2
    top = wt[:, :hc, :hc]
    zero = jnp.zeros_like(top)
    wt_s = jnp.concatenate([jnp.concatenate([top, zero], axis=2), jnp.concatenate([zero, top], axis=2)], axis=1)
    gw = jnp.stack([wt, wt_s]).astype(BF16)
    b = p["gmlp_b"][l]
    b_s = jnp.concatenate([b[:, :hc], b[:, :hc]], axis=1)
    gb = jnp.broadcast_to(jnp.stack([b, b_s])[..., None], (2, GMLP_GROUPS, GMLP_CHUNK, LANES)).astype(F32)

    w_out = p["w_out"][l].astype(BF16)
    return dict(
        sg=p["sh_w_gate"][l].astype(BF16),
        su=p["sh_w_up"][l].astype(BF16), sd=p["sh_w_down"][l].astype(BF16),
        ln1=p["ln1_g"][l][None, :], w_in=w_in2, cos=cos, sin=sin,
        gv=p["gmlp_v_g"][l].reshape(1, GMLP_WIDTH), gw=gw, gb=gb,
        q_a_g=p["q_a_g"][l][None, :], w_qb=w_qb2, kv_a_g=p["kv_a_g"][l][None, :], w_kvb=w_kvb2,
        gq=gq, gk=gk, w_out_a=w_out[:GMLP_WIDTH], w_out_b=w_out[GMLP_WIDTH:],
        ln2=p["ln2_g"][l][None, :], rw=_split_bf16(_pad_cols(p["router_w"], LANES)),
        rb=p["router_bias"].reshape(N_EXPERTS, 1), off=off,
        bounded=bounded.astype(jnp.int32).reshape(1),
    )


def kernel(x_prompt, x_sample, cache_kv_latent, cache_k_rope, c_prompt, c_sample, w_ada, b_ada, ln1_g, w_in,
           gmlp_v_g, gmlp_ws, gmlp_b, q_a_g, w_qb, kv_a_g, w_kvb, q_norm_g, k_norm_g, w_out, ln2_g, router_w,
           router_bias, exp_w_gate, exp_w_up, exp_w_down, sh_w_gate, sh_w_up, sh_w_down):
    p = dict(w_in=w_in, gmlp_v_g=gmlp_v_g, gmlp_ws=gmlp_ws, gmlp_b=gmlp_b, q_a_g=q_a_g, w_qb=w_qb,
             kv_a_g=kv_a_g, w_kvb=w_kvb, q_norm_g=q_norm_g, k_norm_g=k_norm_g, w_out=w_out, ln1_g=ln1_g,
             ln2_g=ln2_g, router_w=router_w, router_bias=router_bias, exp_w_gate=exp_w_gate,
             exp_w_up=exp_w_up, exp_w_down=exp_w_down, sh_w_gate=sh_w_gate, sh_w_up=sh_w_up,
             sh_w_down=sh_w_down)
    batch, seq, _ = x_prompt.shape
    n_seq, s_len, _ = x_sample.shape
    depth, _, past, _ = cache_kv_latent.shape
    assert batch == 1 and s_len == SUB and n_seq == N_SUB and n_seq * s_len == TILE
    assert seq % TILE == 0 and seq % ATT_TQ == 0 and past % GMLP_CHUNK == 0 and past % CHUNK == 0
    rp = seq
    rs = n_seq * s_len

    cos, sin = _rope_tables(seq, past, s_len, n_seq)

    c_all = jnp.concatenate([jnp.broadcast_to(c_prompt, (N_SUB, D_MODEL)), c_sample], axis=0)
    mod = _ada_call(c_all, w_ada, b_ada)
    mod = mod.reshape(depth, 2, N_SUB, 6, D_MODEL)

    weights = [_layer_weights(l, p, cos, sin) for l in range(depth)]
    experts = (exp_w_gate.reshape(depth * N_GROUPS, EXPERTS_PER_GROUP, D_MODEL, EXPERT_FF),
               exp_w_up.reshape(depth * N_GROUPS, EXPERTS_PER_GROUP, D_MODEL, EXPERT_FF),
               exp_w_down.reshape(depth * N_GROUPS, EXPERTS_PER_GROUP, EXPERT_FF, D_MODEL))
    lat_all = cache_kv_latent.reshape(depth, n_seq * past, KV_LORA_RANK)
    pe_all = jnp.swapaxes(cache_k_rope, 2, 3)
    k_past, v_past = _cache_kv_call(lat_all, pe_all, jnp.stack([w["w_kvb"] for w in weights]),
                                    jnp.stack([w["gk"] for w in weights]),
                                    jnp.stack([w["off"] for w in weights]))

    xp = x_prompt.reshape(rp, D_MODEL)
    xs = x_sample.reshape(rs, D_MODEL)
    planes = tuple(jnp.zeros((depth, rows, width), F32) for rows, width in
                   ((rp, KV_LORA_RANK), (rp, QK_ROPE_DIM), (rs, KV_LORA_RANK), (rs, QK_ROPE_DIM),
                    (rs, GMLP_WIDTH)))
    for l in range(depth):
        w = weights[l]
        sh1, sc1, g1, sh2, sc2, g2 = [mod[l, :, :, j, :] for j in range(6)]
        a, q, k, v, *planes = _mix_in_call(xp, xs, sc1, sh1, w, l, depth, tuple(planes))
        bp = _prompt_attn_call(w["bounded"], q, k, v, rp)
        bs = _sample_attn_call(l, q, k, v, k_past, v_past, rp, n_seq, s_len, past)
        ysh, h2e, grp, rank, counts = _mix_out_call(a, bp, bs, xp, xs, g1, sc2, sh2, g2, w)
        xp, xs = _moe(h2e, grp, rank, counts, ysh, g2, experts, l, rp)
    lat_p, pe_p, lat_s, pe_s, v_s = planes
    return (xp.reshape(batch, seq, D_MODEL), xs.reshape(n_seq, s_len, D_MODEL),
            lat_p.reshape(depth, batch, seq, KV_LORA_RANK), pe_p.reshape(depth, batch, seq, QK_ROPE_DIM),
            lat_s.reshape(depth, n_seq, s_len, KV_LORA_RANK), pe_s.reshape(depth, n_seq, s_len, QK_ROPE_DIM),
            v_s.reshape(depth, n_seq, s_len, GMLP_WIDTH))
```

```python
import functools
import math

import jax
import jax.numpy as jnp
from jax import lax
from jax.experimental import pallas as pl
from jax.experimental.pallas import tpu as pltpu

F32 = jnp.float32
BF16 = jnp.bfloat16

D_MODEL = 1024
CHUNK = 64
GMLP_WIDTH = 512
GMLP_GROUPS = 4
GMLP_CHUNK = 128
MLA_HEADS = 4
QK_NOPE_DIM = 128
QK_ROPE_DIM = 64
QK_HEAD_DIM = 192
V_HEAD_DIM = 128
Q_LORA_RANK = 384
KV_LORA_RANK = 256
ROPE_THETA = 10000.0
N_EXPERTS = 16
N_GROUPS = 4
EXPERTS_PER_GROUP = 4
EXPERT_FF = 512
EPS = 1e-6

LANES = 128
TILE = 512
SUB = 64
N_SUB = TILE // SUB
QK_PAD = 256
IN_COLS = 2 * GMLP_WIDTH + Q_LORA_RANK + KV_LORA_RANK + 2 * LANES
Q_COLS = MLA_HEADS * 3 * LANES
H2E_COLS = D_MODEL + LANES
ROW_DMA_UNROLL = 8
EXPERT_PAIRS = ((0, 1), (0, 2), (1, 2), (1, 3), (0, 3), (2, 3))
assert sorted(EXPERT_PAIRS) == [(a, b) for a in range(EXPERTS_PER_GROUP) for b in range(a + 1, EXPERTS_PER_GROUP)]
N_CLASSES = N_GROUPS * len(EXPERT_PAIRS)
CLASS_ROWS = 32
ATT_TK = 512
ATT_TQ = 2 * ATT_TK
ATT_UNROLL = 8
ATT_HEADS = 2
VMEM_LIMIT = 56 * 1024 * 1024
ROUTED_VMEM_LIMIT = 62 * 1024 * 1024
LOG2E = 1.4426950408889634
SCORE_BOUND_MARGIN = 1.02
MAX_SCORE_BOUND = 48.0


def _cparams(n_axes):
    return pltpu.CompilerParams(dimension_semantics=("arbitrary",) * n_axes,
                                vmem_limit_bytes=VMEM_LIMIT)


def _rms(x, eps=EPS):
    return x * lax.rsqrt(jnp.mean(x * x, axis=-1, keepdims=True) + eps)


def _gelu(x):
    c = math.sqrt(2.0 / math.pi)
    return 0.5 * x * (1.0 + jnp.tanh(c * (x + 0.044715 * (x * x * x))))


def _modulate(h, scale, shift):
    h3 = h.reshape(h.shape[0] // SUB, SUB, h.shape[-1])
    h3 = h3 * (1.0 + scale[:, None, :]) + shift[:, None, :]
    return h3.reshape(h.shape)


def _gated_add(x, gate, y):
    y3 = y.reshape(y.shape[0] // SUB, SUB, y.shape[-1]) * gate[:, None, :]
    return x + y3.reshape(y.shape)


def _ada_kernel(c_ref, w_ref, b_ref, o_ref):
    c = c_ref[...]
    cs = (c / (1.0 + jnp.exp(-c))).astype(BF16)
    w = w_ref[0].astype(BF16)
    o_ref[0] = jnp.dot(cs, w, preferred_element_type=F32) + b_ref[0]


def _ada_call(c_all, w_ada, b_ada):
    depth = w_ada.shape[0]
    nblk = w_ada.shape[2] // D_MODEL
    return pl.pallas_call(
        _ada_kernel,
        grid=(depth, nblk),
        in_specs=[
            pl.BlockSpec((16, D_MODEL), lambda l, j: (0, 0)),
            pl.BlockSpec((1, D_MODEL, D_MODEL), lambda l, j: (l, 0, j)),
            pl.BlockSpec((1, 1, D_MODEL), lambda l, j: (l, 0, j)),
        ],
        out_specs=pl.BlockSpec((1, 16, D_MODEL), lambda l, j: (l, 0, j)),
        out_shape=jax.ShapeDtypeStruct((depth, 16, w_ada.shape[2]), F32),
        compiler_params=_cparams(2),
        name="ada_mod",
    )(c_all, w_ada, b_ada.reshape(depth, 1, -1))


def _mix_in_kernel(n_prompt_tiles, n_prev, xp_ref, xs_ref, sc_ref, sh_ref, ln_ref, win_ref, cos_ref, sin_ref,
                   gv_ref, gw_ref, gb_ref, qag_ref, wqb_ref, kvag_ref, wkvb_ref, gq_ref, gk_ref, off_ref,
                   *refs):
    a_ref, q_ref, k_ref, v_ref, latp_ref, pep_ref, lats_ref, pes_ref, vn_ref = refs[n_prev:]
    i = pl.program_id(0)
    is_sample = i >= n_prompt_tiles
    o2 = 2 * GMLP_WIDTH
    o3 = o2 + Q_LORA_RANK
    o4 = o3 + KV_LORA_RANK
    part = TILE

    def rows_of(p):
        r0 = p * part
        rows = slice(r0, r0 + part)
        x = jnp.where(is_sample, xs_ref[rows, :], xp_ref[rows, :])
        h = _modulate(_rms(x) * ln_ref[...], sc_ref[0], sh_ref[0])
        z = jnp.dot(h.astype(BF16), win_ref[...], preferred_element_type=F32)

        u = _gelu(z[:, :GMLP_WIDTH])
        v = _gelu(z[:, GMLP_WIDTH:o2])
        gv = gv_ref[...]
        vn_parts = []
        for g in range(GMLP_GROUPS):
            sl = slice(g * LANES, (g + 1) * LANES)
            vn_parts.append(_rms(v[:, sl]) * gv[:, sl])
        for g in range(GMLP_GROUPS):
            sl = slice(g * LANES, (g + 1) * LANES)
            vb = vn_parts[g].astype(BF16)
            for c in range(part // GMLP_CHUNK):
                crow = slice(c * GMLP_CHUNK, (c + 1) * GMLP_CHUNK)
                s = jnp.dot(gw_ref[0, g], vb[crow], preferred_element_type=F32) + gb_ref[0, g]
                a_ref[r0 + c * GMLP_CHUNK:r0 + (c + 1) * GMLP_CHUNK, sl] = (u[crow, sl] * s).astype(BF16)

        cos = cos_ref[rows, :]
        sin = sin_ref[rows, :]

        ql = _rms(z[:, o2:o3]) * qag_ref[...]
        qq = jnp.dot(ql.astype(BF16), wqb_ref[...], preferred_element_type=F32)
        gq = gq_ref[...]
        for hd in range(MLA_HEADS):
            base = hd * 3 * LANES
            nope = qq[:, base:base + LANES]
            rope = qq[:, base + LANES:base + 2 * LANES] * cos + qq[:, base + 2 * LANES:base + 3 * LANES] * sin
            ss = jnp.sum(nope * nope, axis=-1, keepdims=True) + jnp.sum(rope * rope, axis=-1, keepdims=True)
            rinv = lax.rsqrt(ss * (1.0 / QK_HEAD_DIM) + EPS)
            q_ref[hd, rows, :LANES] = (nope * rinv * gq[:, :LANES]).astype(BF16)
            q_ref[hd, rows, LANES:] = (rope * rinv * gq[:, LANES:] + off_ref[0:1, :]).astype(BF16)

        ckv = _rms(z[:, o3:o4]) * kvag_ref[...]
        kpe = z[:, o4:o4 + LANES] * cos + z[:, o4 + LANES:o4 + 2 * LANES] * sin
        kv = jnp.dot(ckv.astype(BF16), wkvb_ref[...], preferred_element_type=F32)
        gk = gk_ref[...]
        kpe_ss = jnp.sum(kpe * kpe, axis=-1, keepdims=True)
        kpe_g = kpe * gk[:, LANES:]
        for hd in range(MLA_HEADS):
            nope = kv[:, hd * LANES:(hd + 1) * LANES]
            ss = jnp.sum(nope * nope, axis=-1, keepdims=True) + kpe_ss
            rinv = lax.rsqrt(ss * (1.0 / QK_HEAD_DIM) + EPS)
            k_ref[hd, rows, :LANES] = (nope * rinv * gk[:, :LANES]).astype(BF16)
            k_ref[hd, rows, LANES:] = (kpe_g * rinv + off_ref[1:2, :]).astype(BF16)
            v_ref[hd, rows, :] = kv[:, (MLA_HEADS + hd) * LANES:(MLA_HEADS + hd + 1) * LANES].astype(BF16)
        return rows, vn_parts, ckv, kpe

    per_layer = [rows_of(0)]

    @pl.when(jnp.logical_not(is_sample))
    def _():
        for rows, _, ckv, kpe in per_layer:
            latp_ref[0, rows, :] = ckv
            pep_ref[0, rows, :] = kpe[:, :QK_ROPE_DIM]

    @pl.when(is_sample)
    def _():
        for rows, vn_parts, ckv, kpe in per_layer:
            lats_ref[0, rows, :] = ckv
            pes_ref[0, rows, :] = kpe[:, :QK_ROPE_DIM]
            for g in range(GMLP_GROUPS):
                vn_ref[0, rows, g * LANES:(g + 1) * LANES] = vn_parts[g]


def _mix_in_call(xp, xs, sc1, sh1, w, layer, depth, prev):
    rp, rs = xp.shape[0], xs.shape[0]
    npt = rp // TILE
    nt = npt + rs // TILE
    r = rp + rs
    last = npt - 1

    def full(a):
        nd = a.ndim
        return pl.BlockSpec(a.shape, lambda i: (0,) * nd)

    def variant(a):
        nd = a.ndim
        return pl.BlockSpec((1,) + a.shape[1:], lambda i: (i // npt,) + (0,) * (nd - 1))

    row = lambda width: pl.BlockSpec((TILE, width), lambda i: (i, 0))
    head = lambda width: pl.BlockSpec((MLA_HEADS, TILE, width), lambda i: (0, i, 0))
    in_specs = [
        pl.BlockSpec((TILE, D_MODEL), lambda i: (jnp.minimum(i, last), 0)),
        pl.BlockSpec((TILE, D_MODEL), lambda i: (0, 0)),
        variant(sc1), variant(sh1), full(w["ln1"]), full(w["w_in"]),
        row(LANES), row(LANES),
        full(w["gv"]), variant(w["gw"]), variant(w["gb"]),
        full(w["q_a_g"]), full(w["w_qb"]), full(w["kv_a_g"]), full(w["w_kvb"]),
        full(w["gq"]), full(w["gk"]), full(w["off"]),
    ] + [pl.BlockSpec(memory_space=pl.ANY)] * len(prev)
    prompt_plane = lambda width: pl.BlockSpec((1, TILE, width), lambda i: (layer, jnp.minimum(i, last), 0))
    sample_plane = lambda width: pl.BlockSpec((1, TILE, width), lambda i: (layer, 0, 0))
    out_specs = [
        row(GMLP_WIDTH), head(QK_PAD), head(QK_PAD), head(V_HEAD_DIM),
        prompt_plane(KV_LORA_RANK), prompt_plane(QK_ROPE_DIM),
        sample_plane(KV_LORA_RANK), sample_plane(QK_ROPE_DIM), sample_plane(GMLP_WIDTH),
    ]
    out_shape = [
        jax.ShapeDtypeStruct((r, GMLP_WIDTH), BF16),
        jax.ShapeDtypeStruct((MLA_HEADS, r, QK_PAD), BF16),
        jax.ShapeDtypeStruct((MLA_HEADS, r, QK_PAD), BF16),
        jax.ShapeDtypeStruct((MLA_HEADS, r, V_HEAD_DIM), BF16),
        jax.ShapeDtypeStruct((depth, rp, KV_LORA_RANK), F32),
        jax.ShapeDtypeStruct((depth, rp, QK_ROPE_DIM), F32),
        jax.ShapeDtypeStruct((depth, rs, KV_LORA_RANK), F32),
        jax.ShapeDtypeStruct((depth, rs, QK_ROPE_DIM), F32),
        jax.ShapeDtypeStruct((depth, rs, GMLP_WIDTH), F32),
    ]
    n_fixed = len(in_specs) - len(prev)
    n_own = 4
    return pl.pallas_call(
        functools.partial(_mix_in_kernel, npt, len(prev)),
        grid=(nt,), in_specs=in_specs, out_specs=out_specs, out_shape=out_shape,
        input_output_aliases={n_fixed + j: n_own + j for j in range(len(prev))},
        compiler_params=_cparams(1), name="mix_in",
    )(xp, xs, sc1, sh1, w["ln1"], w["w_in"], w["cos"], w["sin"], w["gv"], w["gw"], w["gb"],
      w["q_a_g"], w["w_qb"], w["kv_a_g"], w["w_kvb"], w["gq"], w["gk"], w["off"], *prev)


def _cache_kv_kernel(lat_ref, pe_ref, wkvb_ref, gk_ref, off_ref, k_ref, v_ref):
    lat = lat_ref[0]
    pe_t = pe_ref[0, 0]
    kpe = jnp.concatenate([pe_t, jnp.zeros_like(pe_t)], axis=0).T
    kv = jnp.dot(lat.astype(BF16), wkvb_ref[0], preferred_element_type=F32)
    gk = gk_ref[0]
    kpe_ss = jnp.sum(kpe * kpe, axis=-1, keepdims=True)
    kpe_g = kpe * gk[:, LANES:]
    for hd in range(MLA_HEADS):
        nope = kv[:, hd * LANES:(hd + 1) * LANES]
        ss = jnp.sum(nope * nope, axis=-1, keepdims=True) + kpe_ss
        rinv = lax.rsqrt(ss * (1.0 / QK_HEAD_DIM) + EPS)
        k_ref[0, hd, :, :LANES] = (nope * rinv * gk[:, :LANES]).astype(BF16)
        k_ref[0, hd, :, LANES:] = (kpe_g * rinv + off_ref[0, 1:2, :]).astype(BF16)
        v_ref[0, hd] = kv[:, (MLA_HEADS + hd) * LANES:(MLA_HEADS + hd + 1) * LANES].astype(BF16)


def _cache_kv_call(lat, pe_t, w_kvb, gk, off):
    depth, rows, _ = lat.shape
    t = min(1024, pe_t.shape[-1])
    per_stream = pe_t.shape[-1] // t
    return pl.pallas_call(
        _cache_kv_kernel,
        grid=(depth, rows // t),
        in_specs=[
            pl.BlockSpec((1, t, KV_LORA_RANK), lambda l, i: (l, i, 0)),
            pl.BlockSpec((1, 1, QK_ROPE_DIM, t), lambda l, i: (l, i // per_stream, 0, i % per_stream)),
            pl.BlockSpec((1,) + w_kvb.shape[1:], lambda l, i: (l, 0, 0)),
            pl.BlockSpec((1, 1, QK_PAD), lambda l, i: (l, 0, 0)),
            pl.BlockSpec((1, 2, LANES), lambda l, i: (l, 0, 0)),
        ],
        out_specs=[
            pl.BlockSpec((1, MLA_HEADS, t, QK_PAD), lambda l, i: (l, 0, i, 0)),
            pl.BlockSpec((1, MLA_HEADS, t, V_HEAD_DIM), lambda l, i: (l, 0, i, 0)),
        ],
        out_shape=[
            jax.ShapeDtypeStruct((depth, MLA_HEADS, rows, QK_PAD), BF16),
            jax.ShapeDtypeStruct((depth, MLA_HEADS, rows, V_HEAD_DIM), BF16),
        ],
        compiler_params=_cparams(2), name="cache_kv",
    )(lat, pe_t, w_kvb, gk, off)


def _attn_step(q, k, v, m, l, acc, mask):
    s = lax.dot_general(q, k, (((1,), (1,)), ((), ())), preferred_element_type=F32)
    if mask is not None:
        s = jnp.where(mask, s, -1e30)
    m_new = jnp.maximum(m, jnp.max(s, axis=-1, keepdims=True))
    alpha = jnp.exp2(m - m_new)
    p = jnp.exp2(s - m_new)
    l = alpha * l + jnp.sum(p, axis=-1, keepdims=True)
    acc = alpha * acc + jnp.dot(p.astype(BF16), v, preferred_element_type=F32)
    return m_new, l, acc


def _bounded_step(q, k, v_aug, acc, mask):
    s = lax.dot_general(q, k, (((1,), (1,)), ((), ())), preferred_element_type=F32)
    if mask is not None:
        s = jnp.where(mask, s, -1e30)
    return acc + jnp.dot(jnp.exp2(s).astype(BF16), v_aug, preferred_element_type=F32)


def _prompt_attn_kernel(bounded_ref, q_ref, k_ref, v_ref, o_ref):
    qi = pl.program_id(1)
    row = lax.broadcasted_iota(jnp.int32, (ATT_TK, ATT_TK), 0)
    col = lax.broadcasted_iota(jnp.int32, (ATT_TK, ATT_TK), 1)
    mask = (col // CHUNK) <= (row // CHUNK)

    def q_half(h, half):
        return q_ref[h, half * ATT_TK:(half + 1) * ATT_TK, :]

    def kv(h, j):
        off = pl.multiple_of(j * ATT_TK, ATT_TK)
        return k_ref[h, pl.ds(off, ATT_TK), :], v_ref[h, pl.ds(off, ATT_TK), :]

    def out(h, half):
        return o_ref.at[half * ATT_TK:(half + 1) * ATT_TK, h * V_HEAD_DIM:(h + 1) * V_HEAD_DIM]

    @pl.when(bounded_ref[0] > 0)
    def _():
        lane = lax.broadcasted_iota(jnp.int32, (ATT_TK, LANES), 1)
        ones_col = jnp.where(lane == 0, 1.0, 0.0).astype(BF16)

        def kv_aug(h, j):
            k, v = kv(h, j)
            return k, jnp.concatenate([v, ones_col], axis=1)

        def blocks(first, count, accs):
            accs = [list(a) for a in accs]
            for u in range(count):
                for h in range(ATT_HEADS):
                    k, v = kv_aug(h, first + u)
                    for half in range(2):
                        accs[h][half] = _bounded_step(q_half(h, half), k, v, accs[h][half], None)
            return tuple(tuple(a) for a in accs)

        zero = jnp.zeros((ATT_TK, 2 * LANES), F32)
        n_full = 2 * qi
        n_trips = n_full // ATT_UNROLL
        accs = lax.fori_loop(0, n_trips, lambda j, c: blocks(ATT_UNROLL * j, ATT_UNROLL, c),
                             ((zero, zero),) * ATT_HEADS)
        done = ATT_UNROLL * n_trips
        part = ATT_UNROLL // 2
        while part >= 2:
            accs = lax.cond((n_full & part) != 0, lambda c, d=done, p=part: blocks(d, p, c), lambda c: c, accs)
            done = done + (n_full & part)
            part //= 2
        for h in range(ATT_HEADS):
            acc_a, acc_b = accs[h]
            k, v = kv_aug(h, 2 * qi)
            acc_a = _bounded_step(q_half(h, 0), k, v, acc_a, mask)
            acc_b = _bounded_step(q_half(h, 1), k, v, acc_b, None)
            k, v = kv_aug(h, 2 * qi + 1)
            acc_b = _bounded_step(q_half(h, 1), k, v, acc_b, mask)
            out(h, 0)[...] = (acc_a[:, :LANES] / acc_a[:, LANES:LANES + 1]).astype(BF16)
            out(h, 1)[...] = (acc_b[:, :LANES] / acc_b[:, LANES:LANES + 1]).astype(BF16)

    @pl.when(bounded_ref[0] == 0)
    def _():
        def init():
            return (jnp.full((ATT_TK, 1), -1e30, F32), jnp.zeros((ATT_TK, 1), F32),
                    jnp.zeros((ATT_TK, V_HEAD_DIM), F32))

        for h in range(ATT_HEADS):
            qa, qb = q_half(h, 0), q_half(h, 1)

            def body(j, carry, h=h, qa=qa, qb=qb):
                k, v = kv(h, j)
                return _attn_step(qa, k, v, *carry[0], None), _attn_step(qb, k, v, *carry[1], None)

            sa, sb = lax.fori_loop(0, 2 * qi, body, (init(), init()))
            k, v = kv(h, 2 * qi)
            sa = _attn_step(qa, k, v, *sa, mask)
            sb = _attn_step(qb, k, v, *sb, None)
            k, v = kv(h, 2 * qi + 1)
            sb = _attn_step(qb, k, v, *sb, mask)
            out(h, 0)[...] = (sa[2] / sa[1]).astype(BF16)
            out(h, 1)[...] = (sb[2] / sb[1]).astype(BF16)


def _prompt_attn_call(bounded, q, k, v, rp):
    r = q.shape[1]
    resident = lambda arr: pl.BlockSpec((ATT_HEADS,) + arr.shape[1:], lambda h, i, b: (h, 0, 0),
                                        pipeline_mode=pl.Buffered(1))
    return pl.pallas_call(
        _prompt_attn_kernel,
        grid_spec=pltpu.PrefetchScalarGridSpec(
            num_scalar_prefetch=1, grid=(MLA_HEADS // ATT_HEADS, rp // ATT_TQ),
            in_specs=[
                pl.BlockSpec((ATT_HEADS, ATT_TQ, QK_PAD), lambda h, i, b: (h, i, 0)),
                resident(k), resident(v),
            ],
            out_specs=pl.BlockSpec((ATT_TQ, ATT_HEADS * V_HEAD_DIM), lambda h, i, b: (i, h))),
        out_shape=jax.ShapeDtypeStruct((rp, MLA_HEADS * V_HEAD_DIM), BF16),
        compiler_params=_cparams(2), name="prompt_attn",
    )(bounded, q, k, v)


def _sample_attn_kernel(q_ref, kp_ref, vp_ref, kn_ref, vn_ref, o_ref):
    nt = (((1,), (1,)), ((), ()))
    for h in range(MLA_HEADS):
        q = q_ref[h]
        s1 = lax.dot_general(q, kp_ref[0, h], nt, preferred_element_type=F32)
        s2 = lax.dot_general(q, kn_ref[h], nt, preferred_element_type=F32)
        m = jnp.maximum(jnp.max(s1, axis=-1, keepdims=True), jnp.max(s2, axis=-1, keepdims=True))
        p1 = jnp.exp2(s1 - m)
        p2 = jnp.exp2(s2 - m)
        l = jnp.sum(p1, axis=-1, keepdims=True) + jnp.sum(p2, axis=-1, keepdims=True)
        o = (jnp.dot(p1.astype(BF16), vp_ref[0, h], preferred_element_type=F32)
             + jnp.dot(p2.astype(BF16), vn_ref[h], preferred_element_type=F32))
        o_ref[:, h * V_HEAD_DIM:(h + 1) * V_HEAD_DIM] = (o / l).astype(BF16)


def _sample_attn_call(layer, q, k, v, k_past, v_past, rp, n_seq, s_len, past):
    first = rp // s_len
    new = lambda width: pl.BlockSpec((MLA_HEADS, s_len, width), lambda b: (0, first + b, 0))
    cached = lambda width: pl.BlockSpec((1, MLA_HEADS, past, width), lambda b: (layer, 0, b, 0))
    return pl.pallas_call(
        _sample_attn_kernel,
        grid=(n_seq,),
        in_specs=[new(QK_PAD), cached(QK_PAD), cached(V_HEAD_DIM), new(QK_PAD), new(V_HEAD_DIM)],
        out_specs=pl.BlockSpec((s_len, MLA_HEADS * V_HEAD_DIM), lambda b: (b, 0)),
        out_shape=jax.ShapeDtypeStruct((n_seq * s_len, MLA_HEADS * V_HEAD_DIM), BF16),
        compiler_params=_cparams(1), name="sample_attn",
    )(q, k_past, v_past, k, v)


def _route(logits_t, bias_col):
    scores = 1.0 / (1.0 + jnp.exp(-logits_t))
    biased = scores + bias_col
    s_rows = [scores[e:e + 1, :] for e in range(N_EXPERTS)]
    b_rows = [biased[e:e + 1, :] for e in range(N_EXPERTS)]
    group_score = []
    for g in range(N_GROUPS):
        rows = b_rows[g * EXPERTS_PER_GROUP:(g + 1) * EXPERTS_PER_GROUP]
        best = None
        for a in range(EXPERTS_PER_GROUP):
            for b in range(a + 1, EXPERTS_PER_GROUP):
                pair = rows[a] + rows[b]
                best = pair if best is None else jnp.maximum(best, pair)
        group_score.append(best)
    best_group = jnp.zeros_like(group_score[0], dtype=jnp.int32)
    best_val = group_score[0]
    for g in range(1, N_GROUPS):
        better = group_score[g] > best_val
        best_group = jnp.where(better, g, best_group)
        best_val = jnp.where(better, group_score[g], best_val)
    selected = []
    for e in range(N_EXPERTS):
        g = e // EXPERTS_PER_GROUP
        rank = jnp.zeros_like(best_group)
        for j in range(g * EXPERTS_PER_GROUP, (g + 1) * EXPERTS_PER_GROUP):
            if j == e:
                continue
            ahead = b_rows[j] > b_rows[e]
            if j < e:
                ahead = ahead | (b_rows[j] == b_rows[e])
            rank = rank + ahead.astype(jnp.int32)
        selected.append((best_group == g) & (rank < 2))
    denom = jnp.zeros_like(s_rows[0])
    for e in range(N_EXPERTS):
        denom = denom + jnp.where(selected[e], s_rows[e], 0.0)
    classes = [selected[g * EXPERTS_PER_GROUP + a] & selected[g * EXPERTS_PER_GROUP + b]
               for g in range(N_GROUPS) for a, b in EXPERT_PAIRS]
    return [jnp.where(selected[e], s_rows[e] / denom, 0.0) for e in range(N_EXPERTS)], classes


def _mix_out_kernel(n_prompt_tiles, a_ref, bp_ref, bs_ref, xp_ref, xs_ref, wa_ref, wb_ref, g1_ref, sc_ref,
                    sh_ref, g2_ref, ln_ref, rw_ref, rb_ref, sg_ref, su_ref, sd_ref, upper_ref,
                    ysh_ref, h2e_ref, grp_ref, rank_ref, cnt_ref, gt_ref, oh_ref, carry_ref):
    i = pl.program_id(0)
    is_sample = i >= n_prompt_tiles

    @pl.when(i == 0)
    def _():
        carry_ref[...] = jnp.zeros_like(carry_ref)

    x = jnp.where(is_sample, xs_ref[...], xp_ref[...])
    b = jnp.where(is_sample, bs_ref[...], bp_ref[...])
    mix = (jnp.dot(a_ref[...], wa_ref[...], preferred_element_type=F32)
           + jnp.dot(b, wb_ref[...], preferred_element_type=F32))
    xm = _gated_add(x, g1_ref[0], mix)
    h2 = _modulate(_rms(xm) * ln_ref[...], sc_ref[0], sh_ref[0])
    h2e_ref[:, :D_MODEL] = h2
    h2_hi = h2.astype(BF16)
    act = _silu(jnp.dot(h2_hi, sg_ref[...], preferred_element_type=F32)) * jnp.dot(
        h2_hi, su_ref[...], preferred_element_type=F32)
    shared = jnp.dot(act.astype(BF16), sd_ref[...], preferred_element_type=F32)
    ysh_ref[...] = _gated_add(xm, g2_ref[0], shared)
    h2_lo = (h2 - h2_hi.astype(F32)).astype(BF16)
    by_hi = jnp.dot(h2_hi, rw_ref[...], preferred_element_type=F32)
    logits = (by_hi[:, :LANES] + by_hi[:, LANES:]
              + jnp.dot(h2_lo, rw_ref[:, :LANES], preferred_element_type=F32))
    gate_rows, classes = _route(logits.T[:N_EXPERTS, :], rb_ref[...])
    gt_ref[...] = jnp.zeros_like(gt_ref)
    for e in range(N_EXPERTS):
        gt_ref[e:e + 1, :] = gate_rows[e]
    h2e_ref[:, D_MODEL:] = gt_ref[...].T

    oh_ref[...] = jnp.zeros_like(oh_ref)
    for c in range(N_CLASSES):
        oh_ref[c:c + 1, :] = jnp.where(classes[c], 1.0, 0.0)
    onehot = oh_ref[...]
    cum = jnp.dot(onehot.astype(BF16), upper_ref[...], preferred_element_type=F32)
    carry = carry_ref[...]
    rank = jnp.sum(onehot * (cum - 1.0 + carry[:, :1]), axis=0, keepdims=True)
    cls_id = lax.broadcasted_iota(jnp.int32, (CLASS_ROWS, TILE), 0).astype(F32)
    cls_row = jnp.sum(onehot * cls_id, axis=0, keepdims=True).astype(jnp.int32)
    grp_ref[0] = jnp.broadcast_to(cls_row, (8, TILE))
    rank_ref[0] = jnp.broadcast_to(rank.astype(jnp.int32), (8, TILE))
    carry = carry + jnp.sum(onehot, axis=1, keepdims=True)
    carry_ref[...] = carry
    cnt_ref[...] = carry


def _mix_out_call(a, bp, bs, xp, xs, g1, sc2, sh2, g2, w):
    rp, rs = xp.shape[0], xs.shape[0]
    upper = jnp.triu(jnp.ones((TILE, TILE), BF16))
    npt = rp // TILE
    nt = npt + rs // TILE
    r = rp + rs
    last = npt - 1

    def full(arr):
        nd = arr.ndim
        return pl.BlockSpec(arr.shape, lambda i: (0,) * nd)

    def variant(arr):
        nd = arr.ndim
        return pl.BlockSpec((1,) + arr.shape[1:], lambda i: (i // npt,) + (0,) * (nd - 1))

    row = lambda width: pl.BlockSpec((TILE, width), lambda i: (i, 0))
    prow = lambda width: pl.BlockSpec((TILE, width), lambda i: (jnp.minimum(i, last), 0))
    srow = lambda width: pl.BlockSpec((TILE, width), lambda i: (0, 0))
    return pl.pallas_call(
        functools.partial(_mix_out_kernel, npt),
        grid=(nt,),
        in_specs=[row(GMLP_WIDTH), prow(GMLP_WIDTH), srow(GMLP_WIDTH), prow(D_MODEL), srow(D_MODEL),
                  full(w["w_out_a"]), full(w["w_out_b"]), variant(g1), variant(sc2), variant(sh2),
                  variant(g2), full(w["ln2"]), full(w["rw"]), full(w["rb"]),
                  full(w["sg"]), full(w["su"]), full(w["sd"]), full(upper)],
        out_specs=[row(D_MODEL), row(H2E_COLS),
                   pl.BlockSpec((1, 8, TILE), lambda i: (i, 0, 0)),
                   pl.BlockSpec((1, 8, TILE), lambda i: (i, 0, 0)),
                   pl.BlockSpec((CLASS_ROWS, LANES), lambda i: (0, 0))],
        out_shape=[jax.ShapeDtypeStruct((r, D_MODEL), F32),
                   jax.ShapeDtypeStruct((r, H2E_COLS), F32),
                   jax.ShapeDtypeStruct((nt, 8, TILE), jnp.int32),
                   jax.ShapeDtypeStruct((nt, 8, TILE), jnp.int32),
                   jax.ShapeDtypeStruct((CLASS_ROWS, LANES), F32)],
        scratch_shapes=[pltpu.VMEM((LANES, TILE), F32), pltpu.VMEM((CLASS_ROWS, TILE), F32),
                        pltpu.VMEM((CLASS_ROWS, LANES), F32)],
        compiler_params=_cparams(1), name="mix_out",
    )(a, bp, bs, xp, xs, w["w_out_a"], w["w_out_b"], g1, sc2, sh2, g2, w["ln2"], w["rw"], w["rb"],
      w["sg"], w["su"], w["sd"], upper)


def _silu(x):
    return x / (1.0 + jnp.exp(-x))


def _invert_positions(pos_ref, pad_lo_ref, pad_hi_ref, src_ref):
    def clear(s, carry):
        src_ref[s] = 0
        return carry

    for g in range(N_GROUPS):
        lax.fori_loop(pad_lo_ref[g], pad_hi_ref[g], clear, 0)
    lax.fori_loop(pad_hi_ref[N_GROUPS - 1], src_ref.shape[0], clear, 0)

    def place(i, carry):
        src_ref[pos_ref[i]] = i
        return carry

    lax.fori_loop(0, pos_ref.shape[0], place, 0, unroll=ROW_DMA_UNROLL)


def _gather_rows(idx_ref, base, src_ref, dst_ref, sem):
    for r in range(TILE):
        pltpu.make_async_copy(src_ref.at[pl.ds(idx_ref[base + r], 1)], dst_ref.at[pl.ds(r, 1)], sem).start()


def _wait_rows(src_ref, dst_ref, sem):
    pltpu.make_async_copy(src_ref.at[pl.ds(0, TILE)], dst_ref, sem).wait()


def _routed_kernel(grp_ref, on_ref, need_ref, pos_ref, pad_lo_ref, pad_hi_ref, h_ref, wg_ref, wu_ref, wd_ref,
                   o_ref, buf_ref, src_ref, sem):
    t = pl.program_id(0)
    n = pl.num_programs(0)
    slot = t % 2
    on = on_ref[t] > 0

    @pl.when(t == 0)
    def _():
        _invert_positions(pos_ref, pad_lo_ref, pad_hi_ref, src_ref)

    @pl.when((t == 0) & on)
    def _():
        _gather_rows(src_ref, 0, h_ref, buf_ref.at[0], sem.at[0])

    @pl.when((on_ref[jnp.minimum(t + 1, n - 1)] > 0) & (t + 1 < n))
    def _():
        _gather_rows(src_ref, (t + 1) * TILE, h_ref, buf_ref.at[1 - slot], sem.at[1 - slot])

    @pl.when(jnp.logical_not(on))
    def _():
        o_ref[...] = jnp.zeros_like(o_ref)

    @pl.when(on)
    def _():
        _wait_rows(h_ref, buf_ref.at[slot], sem.at[slot])
        first = grp_ref[t] * EXPERTS_PER_GROUP
        o_ref[...] = jnp.zeros_like(o_ref)
        for e in range(EXPERTS_PER_GROUP):
            @pl.when(need_ref[t * EXPERTS_PER_GROUP + e] > 0)
            def _(e=e):
                h = buf_ref[slot, :, :D_MODEL]
                hg = jnp.dot(h, wg_ref[0, e], preferred_element_type=F32)
                hu = jnp.dot(h, wu_ref[0, e], preferred_element_type=F32)
                lane = lax.broadcasted_iota(jnp.int32, (TILE, LANES), 1)
                gate = jnp.sum(jnp.where(lane == first + e, buf_ref[slot, :, D_MODEL:], 0.0),
                               axis=-1, keepdims=True)
                o_ref[...] += jnp.dot(_silu(hg) * hu, wd_ref[0, e], preferred_element_type=F32) * gate


def _routed_call(tile_grp, tile_on, tile_need, pos, pad_lo, pad_hi, n_slots, h2e, wg, wu, wd, layer):
    nts = n_slots // TILE
    wspec = lambda arr: pl.BlockSpec((1,) + arr.shape[1:],
                                     lambda t, grp, *_: (layer * N_GROUPS + grp[t], 0, 0, 0))
    return pl.pallas_call(
        _routed_kernel,
        grid_spec=pltpu.PrefetchScalarGridSpec(
            num_scalar_prefetch=6, grid=(nts,),
            in_specs=[pl.BlockSpec(memory_space=pl.ANY), wspec(wg), wspec(wu), wspec(wd)],
            out_specs=pl.BlockSpec((TILE, D_MODEL), lambda t, *_: (t, 0)),
            scratch_shapes=[pltpu.VMEM((2, TILE, H2E_COLS), F32), pltpu.SMEM((n_slots,), jnp.int32),
                            pltpu.SemaphoreType.DMA((2,))]),
        out_shape=jax.ShapeDtypeStruct((n_slots, D_MODEL), F32),
        compiler_params=pltpu.CompilerParams(dimension_semantics=("arbitrary",),
                                             vmem_limit_bytes=ROUTED_VMEM_LIMIT), name="moe_routed",
    )(tile_grp, tile_on, tile_need, pos, pad_lo, pad_hi, h2e, wg, wu, wd)


def _combine_kernel(n_prompt_tiles, pos_ref, ysh_ref, g2_ref, routed_ref, yp_ref, ys_ref, buf_ref, sem):
    i = pl.program_id(0)
    n = pl.num_programs(0)
    slot = i % 2

    @pl.when(i == 0)
    def _():
        _gather_rows(pos_ref, 0, routed_ref, buf_ref.at[0], sem.at[0])

    @pl.when(i + 1 < n)
    def _():
        _gather_rows(pos_ref, (i + 1) * TILE, routed_ref, buf_ref.at[1 - slot], sem.at[1 - slot])

    _wait_rows(routed_ref, buf_ref.at[slot], sem.at[slot])
    y = _gated_add(ysh_ref[...], g2_ref[0], buf_ref[slot])

    @pl.when(i < n_prompt_tiles)
    def _():
        yp_ref[...] = y

    @pl.when(i >= n_prompt_tiles)
    def _():
        ys_ref[...] = y


def _combine_call(pos, ysh, g2, routed, rp):
    r = ysh.shape[0]
    rs = r - rp
    npt = rp // TILE
    last = npt - 1
    return pl.pallas_call(
        functools.partial(_combine_kernel, npt),
        grid_spec=pltpu.PrefetchScalarGridSpec(
            num_scalar_prefetch=1, grid=(r // TILE,),
            in_specs=[pl.BlockSpec((TILE, D_MODEL), lambda i, pos: (i, 0)),
                      pl.BlockSpec((1,) + g2.shape[1:], lambda i, pos: (i // npt, 0, 0)),
                      pl.BlockSpec(memory_space=pl.ANY)],
            out_specs=[pl.BlockSpec((TILE, D_MODEL), lambda i, pos: (jnp.minimum(i, last), 0)),
                       pl.BlockSpec((TILE, D_MODEL), lambda i, pos: (0, 0))],
            scratch_shapes=[pltpu.VMEM((2, TILE, D_MODEL), F32), pltpu.SemaphoreType.DMA((2,))]),
        out_shape=[jax.ShapeDtypeStruct((rp, D_MODEL), F32),
                   jax.ShapeDtypeStruct((rs, D_MODEL), F32)],
        compiler_params=_cparams(1), name="moe_combine",
    )(pos, ysh, g2, routed)


def _moe(h2e, cls, rank, counts, ysh, g2, experts, layer, rp):
    r = h2e.shape[0]
    n_pairs = len(EXPERT_PAIRS)
    n_sorted_tiles = r // TILE + N_GROUPS
    cnt_c = counts[:N_CLASSES, 0].astype(jnp.int32).reshape(N_GROUPS, n_pairs)
    cnt = jnp.sum(cnt_c, axis=1)
    tiles_g = (cnt + TILE - 1) // TILE
    end_g = jnp.cumsum(tiles_g)
    off_g = (end_g - tiles_g) * TILE
    start_c = (off_g[:, None] + jnp.cumsum(cnt_c, axis=1) - cnt_c).reshape(N_CLASSES)
    end_c = start_c + cnt_c.reshape(N_CLASSES)
    cls_flat = cls[:, 0, :].reshape(r)
    in_class = cls_flat[:, None] == jnp.arange(N_CLASSES, dtype=jnp.int32)[None, :]
    pos = rank[:, 0, :].reshape(r) + jnp.sum(jnp.where(in_class, start_c[None, :], 0), axis=1)
    t_idx = jnp.arange(n_sorted_tiles, dtype=jnp.int32)
    tile_grp = jnp.zeros_like(t_idx)
    for g in range(N_GROUPS - 1):
        tile_grp = tile_grp + (t_idx >= end_g[g]).astype(jnp.int32)
    tile_on = (t_idx < end_g[N_GROUPS - 1]).astype(jnp.int32)
    lo = t_idx[:, None] * TILE
    overlap = ((start_c[None, :] < lo + TILE) & (end_c[None, :] > lo)
               & (end_c > start_c)[None, :])
    member = jnp.array([[int(e in pair) for e in range(EXPERTS_PER_GROUP)] for pair in EXPERT_PAIRS] * N_GROUPS,
                       dtype=jnp.int32)
    tile_need = jnp.max(overlap[:, :, None].astype(jnp.int32) * member[None], axis=1).reshape(-1)
    routed = _routed_call(tile_grp, tile_on, tile_need, pos, off_g + cnt, end_g * TILE,
                          n_sorted_tiles * TILE, h2e, *experts, layer)
    return _combine_call(pos, ysh, g2, routed, rp)


def _rot_half_cols(wcols):
    half = QK_ROPE_DIM // 2
    return jnp.concatenate([-wcols[:, half:], wcols[:, :half]], axis=1)


def _pad_cols(wcols, width):
    return jnp.pad(wcols, ((0, 0), (0, width - wcols.shape[1])))


def _split_bf16(w):
    hi = w.astype(BF16)
    lo = (w - hi.astype(F32)).astype(BF16)
    return jnp.concatenate([hi, lo], axis=1)


def _rope_tables(seq, past, s_len, n_seq):
    half = QK_ROPE_DIM // 2
    inv = ROPE_THETA ** (-jnp.arange(half, dtype=F32) / half)
    inv_wide = jnp.concatenate([inv, inv, jnp.zeros((LANES - QK_ROPE_DIM,), F32)])
    live = (jnp.arange(LANES) < QK_ROPE_DIM).astype(F32)

    def cos_sin(pos):
        ang = pos.astype(F32)[:, None] * inv_wide[None, :]
        return jnp.cos(ang), jnp.sin(ang)

    cc, sc = cos_sin(jnp.arange(0, seq, GMLP_CHUNK, dtype=jnp.int32))
    cf, sf = cos_sin(jnp.arange(GMLP_CHUNK, dtype=jnp.int32))
    cos_p = ((cc[:, None, :] * cf[None, :, :] - sc[:, None, :] * sf[None, :, :]) * live).reshape(seq, LANES)
    sin_p = ((sc[:, None, :] * cf[None, :, :] + cc[:, None, :] * sf[None, :, :]) * live).reshape(seq, LANES)
    cos_s, sin_s = cos_sin(past + jnp.arange(s_len, dtype=jnp.int32))
    cos = jnp.concatenate([cos_p, jnp.tile(cos_s * live, (n_seq, 1))], axis=0)
    sin = jnp.concatenate([sin_p, jnp.tile(sin_s * live, (n_seq, 1))], axis=0)
    return cos, sin


def _layer_weights(l, p, cos, sin):
    o1 = GMLP_WIDTH
    o2 = 2 * GMLP_WIDTH
    o3 = o2 + Q_LORA_RANK
    o4 = o3 + KV_LORA_RANK
    w_in = p["w_in"][l]
    kpe_cols = w_in[:, o4:]
    w_in2 = jnp.concatenate([w_in[:, :o4], _pad_cols(kpe_cols, LANES),
                             _pad_cols(_rot_half_cols(kpe_cols), LANES)], axis=1).astype(BF16)
    w_qb = p["w_qb"][l]
    q_parts = []
    for hd in range(MLA_HEADS):
        base = hd * QK_HEAD_DIM
        rope_cols = w_qb[:, base + QK_NOPE_DIM:base + QK_HEAD_DIM]
        q_parts += [w_qb[:, base:base + QK_NOPE_DIM], _pad_cols(rope_cols, LANES),
                    _pad_cols(_rot_half_cols(rope_cols), LANES)]
    w_qb2 = jnp.concatenate(q_parts, axis=1).astype(BF16)
    w_kvb = p["w_kvb"][l].reshape(KV_LORA_RANK, MLA_HEADS, QK_NOPE_DIM + V_HEAD_DIM)
    w_kvb2 = jnp.concatenate([w_kvb[:, :, :QK_NOPE_DIM].reshape(KV_LORA_RANK, -1),
                              w_kvb[:, :, QK_NOPE_DIM:].reshape(KV_LORA_RANK, -1)], axis=1).astype(BF16)
    qscale = LOG2E / math.sqrt(QK_HEAD_DIM)
    gq = _pad_cols(p["q_norm_g"][l][None, :] * qscale, QK_PAD)
    gk = _pad_cols(p["k_norm_g"][l][None, :], QK_PAD)
    bound = (QK_HEAD_DIM * qscale * SCORE_BOUND_MARGIN * jnp.max(jnp.abs(p["q_norm_g"][l]))
             * jnp.max(jnp.abs(p["k_norm_g"][l])))
    bounded = bound <= MAX_SCORE_BOUND
    pad_lane = jnp.arange(LANES) == QK_ROPE_DIM
    off = jnp.stack([jnp.where(pad_lane, 1.0, 0.0),
                     jnp.where(pad_lane & bounded, -bound, 0.0)]).astype(F32)

    ws = p["gmlp_ws"][l]
    tri = jnp.tril(jnp.ones((GMLP_CHUNK, GMLP_CHUNK), dtype=bool))
    wt = jnp.where(tri[None], ws, 0.0)
    hc = GMLP_CHUNK // 2
    top = wt[:, :hc, :hc]
    zero = jnp.zeros_like(top)
    wt_s = jnp.concatenate([jnp.concatenate([top, zero], axis=2), jnp.concatenate([zero, top], axis=2)], axis=1)
    gw = jnp.stack([wt, wt_s]).astype(BF16)
    b = p["gmlp_b"][l]
    b_s = jnp.concatenate([b[:, :hc], b[:, :hc]], axis=1)
    gb = jnp.broadcast_to(jnp.stack([b, b_s])[..., None], (2, GMLP_GROUPS, GMLP_CHUNK, LANES)).astype(F32)

    w_out = p["w_out"][l].astype(BF16)
    return dict(
        sg=p["sh_w_gate"][l].astype(BF16),
        su=p["sh_w_up"][l].astype(BF16), sd=p["sh_w_down"][l].astype(BF16),
        ln1=p["ln1_g"][l][None, :], w_in=w_in2, cos=cos, sin=sin,
        gv=p["gmlp_v_g"][l].reshape(1, GMLP_WIDTH), gw=gw, gb=gb,
        q_a_g=p["q_a_g"][l][None, :], w_qb=w_qb2, kv_a_g=p["kv_a_g"][l][None, :], w_kvb=w_kvb2,
        gq=gq, gk=gk, w_out_a=w_out[:GMLP_WIDTH], w_out_b=w_out[GMLP_WIDTH:],
        ln2=p["ln2_g"][l][None, :], rw=_split_bf16(_pad_cols(p["router_w"], LANES)),
        rb=p["router_bias"].reshape(N_EXPERTS, 1), off=off,
        bounded=bounded.astype(jnp.int32).reshape(1),
    )


def kernel(x_prompt, x_sample, cache_kv_latent, cache_k_rope, c_prompt, c_sample, w_ada, b_ada, ln1_g, w_in,
           gmlp_v_g, gmlp_ws, gmlp_b, q_a_g, w_qb, kv_a_g, w_kvb, q_norm_g, k_norm_g, w_out, ln2_g, router_w,
           router_bias, exp_w_gate, exp_w_up, exp_w_down, sh_w_gate, sh_w_up, sh_w_down):
    p = dict(w_in=w_in, gmlp_v_g=gmlp_v_g, gmlp_ws=gmlp_ws, gmlp_b=gmlp_b, q_a_g=q_a_g, w_qb=w_qb,
             kv_a_g=kv_a_g, w_kvb=w_kvb, q_norm_g=q_norm_g, k_norm_g=k_norm_g, w_out=w_out, ln1_g=ln1_g,
             ln2_g=ln2_g, router_w=router_w, router_bias=router_bias, exp_w_gate=exp_w_gate,
             exp_w_up=exp_w_up, exp_w_down=exp_w_down, sh_w_gate=sh_w_gate, sh_w_up=sh_w_up,
             sh_w_down=sh_w_down)
    batch, seq, _ = x_prompt.shape
    n_seq, s_len, _ = x_sample.shape
    depth, _, past, _ = cache_kv_latent.shape
    assert batch == 1 and s_len == SUB and n_seq == N_SUB and n_seq * s_len == TILE
    assert seq % TILE == 0 and seq % ATT_TQ == 0 and past % GMLP_CHUNK == 0 and past % CHUNK == 0
    rp = seq
    rs = n_seq * s_len

    cos, sin = _rope_tables(seq, past, s_len, n_seq)

    c_all = jnp.concatenate([jnp.broadcast_to(c_prompt, (N_SUB, D_MODEL)), c_sample], axis=0)
    mod = _ada_call(c_all, w_ada, b_ada)
    mod = mod.reshape(depth, 2, N_SUB, 6, D_MODEL)

    weights = [_layer_weights(l, p, cos, sin) for l in range(depth)]
    experts = (exp_w_gate.reshape(depth * N_GROUPS, EXPERTS_PER_GROUP, D_MODEL, EXPERT_FF),
               exp_w_up.reshape(depth * N_GROUPS, EXPERTS_PER_GROUP, D_MODEL, EXPERT_FF),
               exp_w_down.reshape(depth * N_GROUPS, EXPERTS_PER_GROUP, EXPERT_FF, D_MODEL))
    lat_all = cache_kv_latent.reshape(depth, n_seq * past, KV_LORA_RANK)
    pe_all = jnp.swapaxes(cache_k_rope, 2, 3)
    k_past, v_past = _cache_kv_call(lat_all, pe_all, jnp.stack([w["w_kvb"] for w in weights]),
                                    jnp.stack([w["gk"] for w in weights]),
                                    jnp.stack([w["off"] for w in weights]))

    xp = x_prompt.reshape(rp, D_MODEL)
    xs = x_sample.reshape(rs, D_MODEL)
    planes = tuple(jnp.zeros((depth, rows, width), F32) for rows, width in
                   ((rp, KV_LORA_RANK), (rp, QK_ROPE_DIM), (rs, KV_LORA_RANK), (rs, QK_ROPE_DIM),
                    (rs, GMLP_WIDTH)))
    for l in range(depth):
        w = weights[l]
        sh1, sc1, g1, sh2, sc2, g2 = [mod[l, :, :, j, :] for j in range(6)]
        a, q, k, v, *planes = _mix_in_call(xp, xs, sc1, sh1, w, l, depth, tuple(planes))
        bp = _prompt_attn_call(w["bounded"], q, k, v, rp)
        bs = _sample_attn_call(l, q, k, v, k_past, v_past, rp, n_seq, s_len, past)
        ysh, h2e, grp, rank, counts = _mix_out_call(a, bp, bs, xp, xs, g1, sc2, sh2, g2, w)
        xp, xs = _moe(h2e, grp, rank, counts, ysh, g2, experts, l, rp)
    lat_p, pe_p, lat_s, pe_s, v_s = planes
    return (xp.reshape(batch, seq, D_MODEL), xs.reshape(n_seq, s_len, D_MODEL),
            lat_p.reshape(depth, batch, seq, KV_LORA_RANK), pe_p.reshape(depth, batch, seq, QK_ROPE_DIM),
            lat_s.reshape(depth, n_seq, s_len, KV_LORA_RANK), pe_s.reshape(depth, n_seq, s_len, QK_ROPE_DIM),
            v_s.reshape(depth, n_seq, s_len, GMLP_WIDTH))
```

```python
import functools
import math

import jax
import jax.numpy as jnp
from jax import lax
from jax.experimental import pallas as pl
from jax.experimental.pallas import tpu as pltpu

F32 = jnp.float32
BF16 = jnp.bfloat16

D_MODEL = 1024
CHUNK = 64
GMLP_WIDTH = 512
GMLP_GROUPS = 4
GMLP_CHUNK = 128
MLA_HEADS = 4
QK_NOPE_DIM = 128
QK_ROPE_DIM = 64
QK_HEAD_DIM = 192
V_HEAD_DIM = 128
Q_LORA_RANK = 384
KV_LORA_RANK = 256
ROPE_THETA = 10000.0
N_EXPERTS = 16
N_GROUPS = 4
EXPERTS_PER_GROUP = 4
EXPERT_FF = 512
EPS = 1e-6

LANES = 128
TILE = 512
SUB = 64
N_SUB = TILE // SUB
QK_PAD = 256
IN_COLS = 2 * GMLP_WIDTH + Q_LORA_RANK + KV_LORA_RANK + 2 * LANES
Q_COLS = MLA_HEADS * 3 * LANES
H2E_COLS = D_MODEL + LANES
ROW_DMA_UNROLL = 8
CACHE_TILE = 2048
EXPERT_PAIRS = ((0, 1), (0, 2), (1, 2), (1, 3), (0, 3), (2, 3))
assert sorted(EXPERT_PAIRS) == [(a, b) for a in range(EXPERTS_PER_GROUP) for b in range(a + 1, EXPERTS_PER_GROUP)]
N_CLASSES = N_GROUPS * len(EXPERT_PAIRS)
CLASS_ROWS = 32
ATT_TK = 512
ATT_TQ = 2 * ATT_TK
ATT_UNROLL = 8
ATT_HEADS = 2
VMEM_LIMIT = 56 * 1024 * 1024
ROUTED_VMEM_LIMIT = 62 * 1024 * 1024
LOG2E = 1.4426950408889634
SCORE_BOUND_MARGIN = 1.02
MAX_SCORE_BOUND = 48.0


def _cparams(n_axes):
    return pltpu.CompilerParams(dimension_semantics=("arbitrary",) * n_axes,
                                vmem_limit_bytes=VMEM_LIMIT)


def _rms(x, eps=EPS):
    return x * lax.rsqrt(jnp.mean(x * x, axis=-1, keepdims=True) + eps)


def _gelu(x):
    c = math.sqrt(2.0 / math.pi)
    return 0.5 * x * (1.0 + jnp.tanh(c * (x + 0.044715 * (x * x * x))))


def _modulate(h, scale, shift):
    h3 = h.reshape(h.shape[0] // SUB, SUB, h.shape[-1])
    h3 = h3 * (1.0 + scale[:, None, :]) + shift[:, None, :]
    return h3.reshape(h.shape)


def _gated_add(x, gate, y):
    y3 = y.reshape(y.shape[0] // SUB, SUB, y.shape[-1]) * gate[:, None, :]
    return x + y3.reshape(y.shape)


def _ada_kernel(c_ref, w_ref, b_ref, o_ref):
    c = c_ref[...]
    cs = (c / (1.0 + jnp.exp(-c))).astype(BF16)
    w = w_ref[0].astype(BF16)
    o_ref[0] = jnp.dot(cs, w, preferred_element_type=F32) + b_ref[0]


def _ada_call(c_all, w_ada, b_ada):
    depth = w_ada.shape[0]
    nblk = w_ada.shape[2] // D_MODEL
    return pl.pallas_call(
        _ada_kernel,
        grid=(depth, nblk),
        in_specs=[
            pl.BlockSpec((16, D_MODEL), lambda l, j: (0, 0)),
            pl.BlockSpec((1, D_MODEL, D_MODEL), lambda l, j: (l, 0, j)),
            pl.BlockSpec((1, 1, D_MODEL), lambda l, j: (l, 0, j)),
        ],
        out_specs=pl.BlockSpec((1, 16, D_MODEL), lambda l, j: (l, 0, j)),
        out_shape=jax.ShapeDtypeStruct((depth, 16, w_ada.shape[2]), F32),
        compiler_params=_cparams(2),
        name="ada_mod",
    )(c_all, w_ada, b_ada.reshape(depth, 1, -1))


def _mix_in_kernel(n_prompt_tiles, n_prev, xp_ref, xs_ref, sc_ref, sh_ref, ln_ref, win_ref,
                   cosp_ref, coss_ref, sinp_ref, sins_ref,
                   gv_ref, gw_ref, gb_ref, qag_ref, wqb_ref, kvag_ref, wkvb_ref, gq_ref, gk_ref, off_ref,
                   *refs):
    a_ref, q_ref, k_ref, v_ref, latp_ref, pep_ref, lats_ref, pes_ref, vn_ref = refs[n_prev:]
    i = pl.program_id(0)
    is_sample = i >= n_prompt_tiles
    o2 = 2 * GMLP_WIDTH
    o3 = o2 + Q_LORA_RANK
    o4 = o3 + KV_LORA_RANK
    part = TILE

    def rows_of(p):
        r0 = p * part
        rows = slice(r0, r0 + part)
        x = jnp.where(is_sample, xs_ref[rows, :], xp_ref[rows, :])
        h = _modulate(_rms(x) * ln_ref[...], sc_ref[0], sh_ref[0])
        z = jnp.dot(h.astype(BF16), win_ref[...], preferred_element_type=F32)

        u = _gelu(z[:, :GMLP_WIDTH])
        v = _gelu(z[:, GMLP_WIDTH:o2])
        gv = gv_ref[...]
        vn_parts = []
        for g in range(GMLP_GROUPS):
            sl = slice(g * LANES, (g + 1) * LANES)
            vn_parts.append(_rms(v[:, sl]) * gv[:, sl])
        for g in range(GMLP_GROUPS):
            sl = slice(g * LANES, (g + 1) * LANES)
            vb = vn_parts[g].astype(BF16)
            for c in range(part // GMLP_CHUNK):
                crow = slice(c * GMLP_CHUNK, (c + 1) * GMLP_CHUNK)
                s = jnp.dot(gw_ref[0, g], vb[crow], preferred_element_type=F32) + gb_ref[0, g]
                a_ref[r0 + c * GMLP_CHUNK:r0 + (c + 1) * GMLP_CHUNK, sl] = (u[crow, sl] * s).astype(BF16)

        cos = jnp.where(is_sample, coss_ref[rows, :], cosp_ref[rows, :])
        sin = jnp.where(is_sample, sins_ref[rows, :], sinp_ref[rows, :])

        ql = _rms(z[:, o2:o3]) * qag_ref[...]
        qq = jnp.dot(ql.astype(BF16), wqb_ref[...], preferred_element_type=F32)
        gq = gq_ref[...]
        for hd in range(MLA_HEADS):
            base = hd * 3 * LANES
            nope = qq[:, base:base + LANES]
            rope = qq[:, base + LANES:base + 2 * LANES] * cos + qq[:, base + 2 * LANES:base + 3 * LANES] * sin
            ss = jnp.sum(nope * nope, axis=-1, keepdims=True) + jnp.sum(rope * rope, axis=-1, keepdims=True)
            rinv = lax.rsqrt(ss * (1.0 / QK_HEAD_DIM) + EPS)
            q_ref[hd, rows, :LANES] = (nope * rinv * gq[:, :LANES]).astype(BF16)
            q_ref[hd, rows, LANES:] = (rope * rinv * gq[:, LANES:] + off_ref[0:1, :]).astype(BF16)

        ckv = _rms(z[:, o3:o4]) * kvag_ref[...]
        kpe = z[:, o4:o4 + LANES] * cos + z[:, o4 + LANES:o4 + 2 * LANES] * sin
        kv = jnp.dot(ckv.astype(BF16), wkvb_ref[...], preferred_element_type=F32)
        gk = gk_ref[...]
        kpe_ss = jnp.sum(kpe * kpe, axis=-1, keepdims=True)
        kpe_g = kpe * gk[:, LANES:]
        for hd in range(MLA_HEADS):
            nope = kv[:, hd * LANES:(hd + 1) * LANES]
            ss = jnp.sum(nope * nope, axis=-1, keepdims=True) + kpe_ss
            rinv = lax.rsqrt(ss * (1.0 / QK_HEAD_DIM) + EPS)
            k_ref[hd, rows, :LANES] = (nope * rinv * gk[:, :LANES]).astype(BF16)
            k_ref[hd, rows, LANES:] = (kpe_g * rinv + off_ref[1:2, :]).astype(BF16)
            v_ref[hd, rows, :] = kv[:, (MLA_HEADS + hd) * LANES:(MLA_HEADS + hd + 1) * LANES].astype(BF16)
        return rows, vn_parts, ckv, kpe

    per_layer = [rows_of(0)]

    @pl.when(jnp.logical_not(is_sample))
    def _():
        for rows, _, ckv, kpe in per_layer:
            latp_ref[0, rows, :] = ckv
            pep_ref[0, rows, :] = kpe[:, :QK_ROPE_DIM]

    @pl.when(is_sample)
    def _():
        for rows, vn_parts, ckv, kpe in per_layer:
            lats_ref[0, rows, :] = ckv
            pes_ref[0, rows, :] = kpe[:, :QK_ROPE_DIM]
            for g in range(GMLP_GROUPS):
                vn_ref[0, rows, g * LANES:(g + 1) * LANES] = vn_parts[g]


def _mix_in_call(xp, xs, sc1, sh1, w, layer, depth, prev):
    rp, rs = xp.shape[0], xs.shape[0]
    npt = rp // TILE
    nt = npt + rs // TILE
    r = rp + rs
    last = npt - 1

    def full(a):
        nd = a.ndim
        return pl.BlockSpec(a.shape, lambda i: (0,) * nd)

    def variant(a):
        nd = a.ndim
        return pl.BlockSpec((1,) + a.shape[1:], lambda i: (i // npt,) + (0,) * (nd - 1))

    row = lambda width: pl.BlockSpec((TILE, width), lambda i: (i, 0))
    head = lambda width: pl.BlockSpec((MLA_HEADS, TILE, width), lambda i: (0, i, 0))
    in_specs = [
        pl.BlockSpec((TILE, D_MODEL), lambda i: (jnp.minimum(i, last), 0)),
        pl.BlockSpec((TILE, D_MODEL), lambda i: (0, 0)),
        variant(sc1), variant(sh1), full(w["ln1"]), full(w["w_in"]),
        pl.BlockSpec((TILE, LANES), lambda i: (jnp.minimum(i, last), 0)),
        pl.BlockSpec((TILE, LANES), lambda i: (0, 0)),
        pl.BlockSpec((TILE, LANES), lambda i: (jnp.minimum(i, last), 0)),
        pl.BlockSpec((TILE, LANES), lambda i: (0, 0)),
        full(w["gv"]), variant(w["gw"]), variant(w["gb"]),
        full(w["q_a_g"]), full(w["w_qb"]), full(w["kv_a_g"]), full(w["w_kvb"]),
        full(w["gq"]), full(w["gk"]), full(w["off"]),
    ] + [pl.BlockSpec(memory_space=pl.ANY)] * len(prev)
    prompt_plane = lambda width: pl.BlockSpec((1, TILE, width), lambda i: (layer, jnp.minimum(i, last), 0))
    sample_plane = lambda width: pl.BlockSpec((1, TILE, width), lambda i: (layer, 0, 0))
    out_specs = [
        row(GMLP_WIDTH), head(QK_PAD), head(QK_PAD), head(V_HEAD_DIM),
        prompt_plane(KV_LORA_RANK), prompt_plane(QK_ROPE_DIM),
        sample_plane(KV_LORA_RANK), sample_plane(QK_ROPE_DIM), sample_plane(GMLP_WIDTH),
    ]
    out_shape = [
        jax.ShapeDtypeStruct((r, GMLP_WIDTH), BF16),
        jax.ShapeDtypeStruct((MLA_HEADS, r, QK_PAD), BF16),
        jax.ShapeDtypeStruct((MLA_HEADS, r, QK_PAD), BF16),
        jax.ShapeDtypeStruct((MLA_HEADS, r, V_HEAD_DIM), BF16),
        jax.ShapeDtypeStruct((depth, rp, KV_LORA_RANK), F32),
        jax.ShapeDtypeStruct((depth, rp, QK_ROPE_DIM), F32),
        jax.ShapeDtypeStruct((depth, rs, KV_LORA_RANK), F32),
        jax.ShapeDtypeStruct((depth, rs, QK_ROPE_DIM), F32),
        jax.ShapeDtypeStruct((depth, rs, GMLP_WIDTH), F32),
    ]
    n_fixed = len(in_specs) - len(prev)
    n_own = 4
    return pl.pallas_call(
        functools.partial(_mix_in_kernel, npt, len(prev)),
        grid=(nt,), in_specs=in_specs, out_specs=out_specs, out_shape=out_shape,
        input_output_aliases={n_fixed + j: n_own + j for j in range(len(prev))},
        compiler_params=_cparams(1), name="mix_in",
    )(xp, xs, sc1, sh1, w["ln1"], w["w_in"], *w["rope"], w["gv"], w["gw"], w["gb"],
      w["q_a_g"], w["w_qb"], w["kv_a_g"], w["w_kvb"], w["gq"], w["gk"], w["off"], *prev)


def _cache_kv_kernel(lat_ref, pe_ref, wkvb_ref, gk_ref, off_ref, k_ref, v_ref):
    lat = lat_ref[0]
    pe_t = pe_ref[0, 0]
    kpe = jnp.concatenate([pe_t, jnp.zeros_like(pe_t)], axis=0).T
    kv = jnp.dot(lat.astype(BF16), wkvb_ref[0], preferred_element_type=F32)
    gk = gk_ref[0]
    kpe_ss = jnp.sum(kpe * kpe, axis=-1, keepdims=True)
    kpe_g = kpe * gk[:, LANES:]
    for hd in range(MLA_HEADS):
        nope = kv[:, hd * LANES:(hd + 1) * LANES]
        ss = jnp.sum(nope * nope, axis=-1, keepdims=True) + kpe_ss
        rinv = lax.rsqrt(ss * (1.0 / QK_HEAD_DIM) + EPS)
        k_ref[0, hd, :, :LANES] = (nope * rinv * gk[:, :LANES]).astype(BF16)
        k_ref[0, hd, :, LANES:] = (kpe_g * rinv + off_ref[0, 1:2, :]).astype(BF16)
        v_ref[0, hd] = kv[:, (MLA_HEADS + hd) * LANES:(MLA_HEADS + hd + 1) * LANES].astype(BF16)


def _cache_kv_call(lat, pe_t, w_kvb, gk, off):
    depth, rows, _ = lat.shape
    t = min(CACHE_TILE, pe_t.shape[-1])
    per_stream = pe_t.shape[-1] // t
    return pl.pallas_call(
        _cache_kv_kernel,
        grid=(depth, rows // t),
        in_specs=[
            pl.BlockSpec((1, t, KV_LORA_RANK), lambda l, i: (l, i, 0)),
            pl.BlockSpec((1, 1, QK_ROPE_DIM, t), lambda l, i: (l, i // per_stream, 0, i % per_stream)),
            pl.BlockSpec((1,) + w_kvb.shape[1:], lambda l, i: (l, 0, 0)),
            pl.BlockSpec((1, 1, QK_PAD), lambda l, i: (l, 0, 0)),
            pl.BlockSpec((1, 2, LANES), lambda l, i: (l, 0, 0)),
        ],
        out_specs=[
            pl.BlockSpec((1, MLA_HEADS, t, QK_PAD), lambda l, i: (l, 0, i, 0)),
            pl.BlockSpec((1, MLA_HEADS, t, V_HEAD_DIM), lambda l, i: (l, 0, i, 0)),
        ],
        out_shape=[
            jax.ShapeDtypeStruct((depth, MLA_HEADS, rows, QK_PAD), BF16),
            jax.ShapeDtypeStruct((depth, MLA_HEADS, rows, V_HEAD_DIM), BF16),
        ],
        compiler_params=_cparams(2), name="cache_kv",
    )(lat, pe_t, w_kvb, gk, off)


def _attn_step(q, k, v, m, l, acc, mask):
    s = lax.dot_general(q, k, (((1,), (1,)), ((), ())), preferred_element_type=F32)
    if mask is not None:
        s = jnp.where(mask, s, -1e30)
    m_new = jnp.maximum(m, jnp.max(s, axis=-1, keepdims=True))
    alpha = jnp.exp2(m - m_new)
    p = jnp.exp2(s - m_new)
    l = alpha * l + jnp.sum(p, axis=-1, keepdims=True)
    acc = alpha * acc + jnp.dot(p.astype(BF16), v, preferred_element_type=F32)
    return m_new, l, acc


def _bounded_step(q, k, v_aug, acc, mask):
    s = lax.dot_general(q, k, (((1,), (1,)), ((), ())), preferred_element_type=F32)
    if mask is not None:
        s = jnp.where(mask, s, -1e30)
    return acc + jnp.dot(jnp.exp2(s).astype(BF16), v_aug, preferred_element_type=F32)


def _prompt_attn_kernel(bounded_ref, q_ref, k_ref, v_ref, o_ref):
    qi = pl.program_id(1)
    row = lax.broadcasted_iota(jnp.int32, (ATT_TK, ATT_TK), 0)
    col = lax.broadcasted_iota(jnp.int32, (ATT_TK, ATT_TK), 1)
    mask = (col // CHUNK) <= (row // CHUNK)

    def q_half(h, half):
        return q_ref[h, half * ATT_TK:(half + 1) * ATT_TK, :]

    def kv(h, j):
        off = pl.multiple_of(j * ATT_TK, ATT_TK)
        return k_ref[h, pl.ds(off, ATT_TK), :], v_ref[h, pl.ds(off, ATT_TK), :]

    def out(h, half):
        return o_ref.at[half * ATT_TK:(half + 1) * ATT_TK, h * V_HEAD_DIM:(h + 1) * V_HEAD_DIM]

    @pl.when(bounded_ref[0] > 0)
    def _():
        lane = lax.broadcasted_iota(jnp.int32, (ATT_TK, LANES), 1)
        ones_col = jnp.where(lane == 0, 1.0, 0.0).astype(BF16)

        def kv_aug(h, j):
            k, v = kv(h, j)
            return k, jnp.concatenate([v, ones_col], axis=1)

        def blocks(first, count, accs):
            accs = [list(a) for a in accs]
            for u in range(count):
                for h in range(ATT_HEADS):
                    k, v = kv_aug(h, first + u)
                    for half in range(2):
                        accs[h][half] = _bounded_step(q_half(h, half), k, v, accs[h][half], None)
            return tuple(tuple(a) for a in accs)

        zero = jnp.zeros((ATT_TK, 2 * LANES), F32)
        n_full = 2 * qi
        n_trips = n_full // ATT_UNROLL
        accs = lax.fori_loop(0, n_trips, lambda j, c: blocks(ATT_UNROLL * j, ATT_UNROLL, c),
                             ((zero, zero),) * ATT_HEADS)
        done = ATT_UNROLL * n_trips
        part = ATT_UNROLL // 2
        while part >= 2:
            accs = lax.cond((n_full & part) != 0, lambda c, d=done, p=part: blocks(d, p, c), lambda c: c, accs)
            done = done + (n_full & part)
            part //= 2
        for h in range(ATT_HEADS):
            acc_a, acc_b = accs[h]
            k, v = kv_aug(h, 2 * qi)
            acc_a = _bounded_step(q_half(h, 0), k, v, acc_a, mask)
            acc_b = _bounded_step(q_half(h, 1), k, v, acc_b, None)
            k, v = kv_aug(h, 2 * qi + 1)
            acc_b = _bounded_step(q_half(h, 1), k, v, acc_b, mask)
            out(h, 0)[...] = (acc_a[:, :LANES] / acc_a[:, LANES:LANES + 1]).astype(BF16)
            out(h, 1)[...] = (acc_b[:, :LANES] / acc_b[:, LANES:LANES + 1]).astype(BF16)

    @pl.when(bounded_ref[0] == 0)
    def _():
        def init():
            return (jnp.full((ATT_TK, 1), -1e30, F32), jnp.zeros((ATT_TK, 1), F32),
                    jnp.zeros((ATT_TK, V_HEAD_DIM), F32))

        for h in range(ATT_HEADS):
            qa, qb = q_half(h, 0), q_half(h, 1)

            def body(j, carry, h=h, qa=qa, qb=qb):
                k, v = kv(h, j)
                return _attn_step(qa, k, v, *carry[0], None), _attn_step(qb, k, v, *carry[1], None)

            sa, sb = lax.fori_loop(0, 2 * qi, body, (init(), init()))
            k, v = kv(h, 2 * qi)
            sa = _attn_step(qa, k, v, *sa, mask)
            sb = _attn_step(qb, k, v, *sb, None)
            k, v = kv(h, 2 * qi + 1)
            sb = _attn_step(qb, k, v, *sb, mask)
            out(h, 0)[...] = (sa[2] / sa[1]).astype(BF16)
            out(h, 1)[...] = (sb[2] / sb[1]).astype(BF16)


def _prompt_attn_call(bounded, q, k, v, rp):
    r = q.shape[1]
    resident = lambda arr: pl.BlockSpec((ATT_HEADS,) + arr.shape[1:], lambda h, i, b: (h, 0, 0),
                                        pipeline_mode=pl.Buffered(1))
    return pl.pallas_call(
        _prompt_attn_kernel,
        grid_spec=pltpu.PrefetchScalarGridSpec(
            num_scalar_prefetch=1, grid=(MLA_HEADS // ATT_HEADS, rp // ATT_TQ),
            in_specs=[
                pl.BlockSpec((ATT_HEADS, ATT_TQ, QK_PAD), lambda h, i, b: (h, i, 0)),
                resident(k), resident(v),
            ],
            out_specs=pl.BlockSpec((ATT_TQ, ATT_HEADS * V_HEAD_DIM), lambda h, i, b: (i, h))),
        out_shape=jax.ShapeDtypeStruct((rp, MLA_HEADS * V_HEAD_DIM), BF16),
        compiler_params=_cparams(2), name="prompt_attn",
    )(bounded, q, k, v)


def _sample_attn_kernel(q_ref, kp_ref, vp_ref, kn_ref, vn_ref, o_ref):
    nt = (((1,), (1,)), ((), ()))
    for h in range(MLA_HEADS):
        q = q_ref[h]
        s1 = lax.dot_general(q, kp_ref[0, h], nt, preferred_element_type=F32)
        s2 = lax.dot_general(q, kn_ref[h], nt, preferred_element_type=F32)
        m = jnp.maximum(jnp.max(s1, axis=-1, keepdims=True), jnp.max(s2, axis=-1, keepdims=True))
        p1 = jnp.exp2(s1 - m)
        p2 = jnp.exp2(s2 - m)
        l = jnp.sum(p1, axis=-1, keepdims=True) + jnp.sum(p2, axis=-1, keepdims=True)
        o = (jnp.dot(p1.astype(BF16), vp_ref[0, h], preferred_element_type=F32)
             + jnp.dot(p2.astype(BF16), vn_ref[h], preferred_element_type=F32))
        o_ref[:, h * V_HEAD_DIM:(h + 1) * V_HEAD_DIM] = (o / l).astype(BF16)


def _sample_attn_call(layer, q, k, v, k_past, v_past, rp, n_seq, s_len, past):
    first = rp // s_len
    new = lambda width: pl.BlockSpec((MLA_HEADS, s_len, width), lambda b: (0, first + b, 0))
    cached = lambda width: pl.BlockSpec((1, MLA_HEADS, past, width), lambda b: (layer, 0, b, 0))
    return pl.pallas_call(
        _sample_attn_kernel,
        grid=(n_seq,),
        in_specs=[new(QK_PAD), cached(QK_PAD), cached(V_HEAD_DIM), new(QK_PAD), new(V_HEAD_DIM)],
        out_specs=pl.BlockSpec((s_len, MLA_HEADS * V_HEAD_DIM), lambda b: (b, 0)),
        out_shape=jax.ShapeDtypeStruct((n_seq * s_len, MLA_HEADS * V_HEAD_DIM), BF16),
        compiler_params=_cparams(1), name="sample_attn",
    )(q, k_past, v_past, k, v)


def _route(logits_t, bias_col):
    scores = 1.0 / (1.0 + jnp.exp(-logits_t))
    biased = scores + bias_col
    s_rows = [scores[e:e + 1, :] for e in range(N_EXPERTS)]
    b_rows = [biased[e:e + 1, :] for e in range(N_EXPERTS)]
    group_score = []
    for g in range(N_GROUPS):
        rows = b_rows[g * EXPERTS_PER_GROUP:(g + 1) * EXPERTS_PER_GROUP]
        best = None
        for a in range(EXPERTS_PER_GROUP):
            for b in range(a + 1, EXPERTS_PER_GROUP):
                pair = rows[a] + rows[b]
                best = pair if best is None else jnp.maximum(best, pair)
        group_score.append(best)
    best_group = jnp.zeros_like(group_score[0], dtype=jnp.int32)
    best_val = group_score[0]
    for g in range(1, N_GROUPS):
        better = group_score[g] > best_val
        best_group = jnp.where(better, g, best_group)
        best_val = jnp.where(better, group_score[g], best_val)
    selected = []
    for e in range(N_EXPERTS):
        g = e // EXPERTS_PER_GROUP
        rank = jnp.zeros_like(best_group)
        for j in range(g * EXPERTS_PER_GROUP, (g + 1) * EXPERTS_PER_GROUP):
            if j == e:
                continue
            ahead = b_rows[j] > b_rows[e]
            if j < e:
                ahead = ahead | (b_rows[j] == b_rows[e])
            rank = rank + ahead.astype(jnp.int32)
        selected.append((best_group == g) & (rank < 2))
    denom = jnp.zeros_like(s_rows[0])
    for e in range(N_EXPERTS):
        denom = denom + jnp.where(selected[e], s_rows[e], 0.0)
    classes = [selected[g * EXPERTS_PER_GROUP + a] & selected[g * EXPERTS_PER_GROUP + b]
               for g in range(N_GROUPS) for a, b in EXPERT_PAIRS]
    return [jnp.where(selected[e], s_rows[e] / denom, 0.0) for e in range(N_EXPERTS)], classes


def _mix_out_kernel(n_prompt_tiles, a_ref, bp_ref, bs_ref, xp_ref, xs_ref, wa_ref, wb_ref, g1_ref, sc_ref,
                    sh_ref, g2_ref, ln_ref, rw_ref, rb_ref, sg_ref, su_ref, sd_ref, upper_ref,
                    ysh_ref, h2e_ref, grp_ref, rank_ref, cnt_ref, gt_ref, oh_ref, carry_ref):
    i = pl.program_id(0)
    is_sample = i >= n_prompt_tiles

    @pl.when(i == 0)
    def _():
        carry_ref[...] = jnp.zeros_like(carry_ref)

    x = jnp.where(is_sample, xs_ref[...], xp_ref[...])
    b = jnp.where(is_sample, bs_ref[...], bp_ref[...])
    mix = (jnp.dot(a_ref[...], wa_ref[...], preferred_element_type=F32)
           + jnp.dot(b, wb_ref[...], preferred_element_type=F32))
    xm = _gated_add(x, g1_ref[0], mix)
    h2 = _modulate(_rms(xm) * ln_ref[...], sc_ref[0], sh_ref[0])
    h2e_ref[:, :D_MODEL] = h2
    h2_hi = h2.astype(BF16)
    act = _silu(jnp.dot(h2_hi, sg_ref[...], preferred_element_type=F32)) * jnp.dot(
        h2_hi, su_ref[...], preferred_element_type=F32)
    shared = jnp.dot(act.astype(BF16), sd_ref[...], preferred_element_type=F32)
    ysh_ref[...] = _gated_add(xm, g2_ref[0], shared)
    h2_lo = (h2 - h2_hi.astype(F32)).astype(BF16)
    by_hi = jnp.dot(h2_hi, rw_ref[...], preferred_element_type=F32)
    logits = (by_hi[:, :LANES] + by_hi[:, LANES:]
              + jnp.dot(h2_lo, rw_ref[:, :LANES], preferred_element_type=F32))
    gate_rows, classes = _route(logits.T[:N_EXPERTS, :], rb_ref[...])
    gt_ref[...] = jnp.zeros_like(gt_ref)
    for e in range(N_EXPERTS):
        gt_ref[e:e + 1, :] = gate_rows[e]
    h2e_ref[:, D_MODEL:] = gt_ref[...].T

    oh_ref[...] = jnp.zeros_like(oh_ref)
    for c in range(N_CLASSES):
        oh_ref[c:c + 1, :] = jnp.where(classes[c], 1.0, 0.0)
    onehot = oh_ref[...]
    cum = jnp.dot(onehot.astype(BF16), upper_ref[...], preferred_element_type=F32)
    carry = carry_ref[...]
    rank = jnp.sum(onehot * (cum - 1.0 + carry[:, :1]), axis=0, keepdims=True)
    cls_id = lax.broadcasted_iota(jnp.int32, (CLASS_ROWS, TILE), 0).astype(F32)
    cls_row = jnp.sum(onehot * cls_id, axis=0, keepdims=True).astype(jnp.int32)
    grp_ref[0] = jnp.broadcast_to(cls_row, (8, TILE))
    rank_ref[0] = jnp.broadcast_to(rank.astype(jnp.int32), (8, TILE))
    carry = carry + jnp.sum(onehot, axis=1, keepdims=True)
    carry_ref[...] = carry
    cnt_ref[...] = carry


def _mix_out_call(a, bp, bs, xp, xs, g1, sc2, sh2, g2, w):
    rp, rs = xp.shape[0], xs.shape[0]
    upper = jnp.triu(jnp.ones((TILE, TILE), BF16))
    npt = rp // TILE
    nt = npt + rs // TILE
    r = rp + rs
    last = npt - 1

    def full(arr):
        nd = arr.ndim
        return pl.BlockSpec(arr.shape, lambda i: (0,) * nd)

    def variant(arr):
        nd = arr.ndim
        return pl.BlockSpec((1,) + arr.shape[1:], lambda i: (i // npt,) + (0,) * (nd - 1))

    row = lambda width: pl.BlockSpec((TILE, width), lambda i: (i, 0))
    prow = lambda width: pl.BlockSpec((TILE, width), lambda i: (jnp.minimum(i, last), 0))
    srow = lambda width: pl.BlockSpec((TILE, width), lambda i: (0, 0))
    return pl.pallas_call(
        functools.partial(_mix_out_kernel, npt),
        grid=(nt,),
        in_specs=[row(GMLP_WIDTH), prow(GMLP_WIDTH), srow(GMLP_WIDTH), prow(D_MODEL), srow(D_MODEL),
                  full(w["w_out_a"]), full(w["w_out_b"]), variant(g1), variant(sc2), variant(sh2),
                  variant(g2), full(w["ln2"]), full(w["rw"]), full(w["rb"]),
                  full(w["sg"]), full(w["su"]), full(w["sd"]), full(upper)],
        out_specs=[row(D_MODEL), row(H2E_COLS),
                   pl.BlockSpec((1, 8, TILE), lambda i: (i, 0, 0)),
                   pl.BlockSpec((1, 8, TILE), lambda i: (i, 0, 0)),
                   pl.BlockSpec((CLASS_ROWS, LANES), lambda i: (0, 0))],
        out_shape=[jax.ShapeDtypeStruct((r, D_MODEL), F32),
                   jax.ShapeDtypeStruct((r, H2E_COLS), F32),
                   jax.ShapeDtypeStruct((nt, 8, TILE), jnp.int32),
                   jax.ShapeDtypeStruct((nt, 8, TILE), jnp.int32),
                   jax.ShapeDtypeStruct((CLASS_ROWS, LANES), F32)],
        scratch_shapes=[pltpu.VMEM((LANES, TILE), F32), pltpu.VMEM((CLASS_ROWS, TILE), F32),
                        pltpu.VMEM((CLASS_ROWS, LANES), F32)],
        compiler_params=_cparams(1), name="mix_out",
    )(a, bp, bs, xp, xs, w["w_out_a"], w["w_out_b"], g1, sc2, sh2, g2, w["ln2"], w["rw"], w["rb"],
      w["sg"], w["su"], w["sd"], upper)


def _silu(x):
    return x / (1.0 + jnp.exp(-x))


def _invert_positions(pos_ref, pad_lo_ref, pad_hi_ref, src_ref):
    def clear(s, carry):
        src_ref[s] = 0
        return carry

    for g in range(N_GROUPS):
        lax.fori_loop(pad_lo_ref[g], pad_hi_ref[g], clear, 0)
    lax.fori_loop(pad_hi_ref[N_GROUPS - 1], src_ref.shape[0], clear, 0)

    def place(i, carry):
        src_ref[pos_ref[i]] = i
        return carry

    lax.fori_loop(0, pos_ref.shape[0], place, 0, unroll=ROW_DMA_UNROLL)


def _gather_rows(idx_ref, base, src_ref, dst_ref, sem):
    for r in range(TILE):
        pltpu.make_async_copy(src_ref.at[pl.ds(idx_ref[base + r], 1)], dst_ref.at[pl.ds(r, 1)], sem).start()


def _wait_rows(src_ref, dst_ref, sem):
    pltpu.make_async_copy(src_ref.at[pl.ds(0, TILE)], dst_ref, sem).wait()


def _routed_kernel(grp_ref, on_ref, need_ref, pos_ref, pad_lo_ref, pad_hi_ref, h_ref, wg_ref, wu_ref, wd_ref,
                   o_ref, buf_ref, src_ref, sem):
    t = pl.program_id(0)
    n = pl.num_programs(0)
    slot = t % 2
    on = on_ref[t] > 0

    @pl.when(t == 0)
    def _():
        _invert_positions(pos_ref, pad_lo_ref, pad_hi_ref, src_ref)

    @pl.when((t == 0) & on)
    def _():
        _gather_rows(src_ref, 0, h_ref, buf_ref.at[0], sem.at[0])

    @pl.when((on_ref[jnp.minimum(t + 1, n - 1)] > 0) & (t + 1 < n))
    def _():
        _gather_rows(src_ref, (t + 1) * TILE, h_ref, buf_ref.at[1 - slot], sem.at[1 - slot])

    @pl.when(jnp.logical_not(on))
    def _():
        o_ref[...] = jnp.zeros_like(o_ref)

    @pl.when(on)
    def _():
        _wait_rows(h_ref, buf_ref.at[slot], sem.at[slot])
        first = grp_ref[t] * EXPERTS_PER_GROUP
        o_ref[...] = jnp.zeros_like(o_ref)
        for e in range(EXPERTS_PER_GROUP):
            @pl.when(need_ref[t * EXPERTS_PER_GROUP + e] > 0)
            def _(e=e):
                h = buf_ref[slot, :, :D_MODEL]
                hg = jnp.dot(h, wg_ref[0, e], preferred_element_type=F32)
                hu = jnp.dot(h, wu_ref[0, e], preferred_element_type=F32)
                lane = lax.broadcasted_iota(jnp.int32, (TILE, LANES), 1)
                gate = jnp.sum(jnp.where(lane == first + e, buf_ref[slot, :, D_MODEL:], 0.0),
                               axis=-1, keepdims=True)
                o_ref[...] += jnp.dot(_silu(hg) * hu, wd_ref[0, e], preferred_element_type=F32) * gate


def _routed_call(tile_grp, tile_on, tile_need, pos, pad_lo, pad_hi, n_slots, h2e, wg, wu, wd, layer):
    nts = n_slots // TILE
    wspec = lambda arr: pl.BlockSpec((1,) + arr.shape[1:],
                                     lambda t, grp, *_: (layer * N_GROUPS + grp[t], 0, 0, 0))
    return pl.pallas_call(
        _routed_kernel,
        grid_spec=pltpu.PrefetchScalarGridSpec(
            num_scalar_prefetch=6, grid=(nts,),
            in_specs=[pl.BlockSpec(memory_space=pl.ANY), wspec(wg), wspec(wu), wspec(wd)],
            out_specs=pl.BlockSpec((TILE, D_MODEL), lambda t, *_: (t, 0)),
            scratch_shapes=[pltpu.VMEM((2, TILE, H2E_COLS), F32), pltpu.SMEM((n_slots,), jnp.int32),
                            pltpu.SemaphoreType.DMA((2,))]),
        out_shape=jax.ShapeDtypeStruct((n_slots, D_MODEL), F32),
        compiler_params=pltpu.CompilerParams(dimension_semantics=("arbitrary",),
                                             vmem_limit_bytes=ROUTED_VMEM_LIMIT), name="moe_routed",
    )(tile_grp, tile_on, tile_need, pos, pad_lo, pad_hi, h2e, wg, wu, wd)


def _combine_kernel(n_prompt_tiles, pos_ref, ysh_ref, g2_ref, routed_ref, yp_ref, ys_ref, buf_ref, sem):
    i = pl.program_id(0)
    n = pl.num_programs(0)
    slot = i % 2

    @pl.when(i == 0)
    def _():
        _gather_rows(pos_ref, 0, routed_ref, buf_ref.at[0], sem.at[0])

    @pl.when(i + 1 < n)
    def _():
        _gather_rows(pos_ref, (i + 1) * TILE, routed_ref, buf_ref.at[1 - slot], sem.at[1 - slot])

    _wait_rows(routed_ref, buf_ref.at[slot], sem.at[slot])
    y = _gated_add(ysh_ref[...], g2_ref[0], buf_ref[slot])

    @pl.when(i < n_prompt_tiles)
    def _():
        yp_ref[...] = y

    @pl.when(i >= n_prompt_tiles)
    def _():
        ys_ref[...] = y


def _combine_call(pos, ysh, g2, routed, rp):
    r = ysh.shape[0]
    rs = r - rp
    npt = rp // TILE
    last = npt - 1
    return pl.pallas_call(
        functools.partial(_combine_kernel, npt),
        grid_spec=pltpu.PrefetchScalarGridSpec(
            num_scalar_prefetch=1, grid=(r // TILE,),
            in_specs=[pl.BlockSpec((TILE, D_MODEL), lambda i, pos: (i, 0)),
                      pl.BlockSpec((1,) + g2.shape[1:], lambda i, pos: (i // npt, 0, 0)),
                      pl.BlockSpec(memory_space=pl.ANY)],
            out_specs=[pl.BlockSpec((TILE, D_MODEL), lambda i, pos: (jnp.minimum(i, last), 0)),
                       pl.BlockSpec((TILE, D_MODEL), lambda i, pos: (0, 0))],
            scratch_shapes=[pltpu.VMEM((2, TILE, D_MODEL), F32), pltpu.SemaphoreType.DMA((2,))]),
        out_shape=[jax.ShapeDtypeStruct((rp, D_MODEL), F32),
                   jax.ShapeDtypeStruct((rs, D_MODEL), F32)],
        compiler_params=_cparams(1), name="moe_combine",
    )(pos, ysh, g2, routed)


def _moe(h2e, cls, rank, counts, ysh, g2, experts, layer, rp):
    r = h2e.shape[0]
    n_pairs = len(EXPERT_PAIRS)
    n_sorted_tiles = r // TILE + N_GROUPS
    cnt_c = counts[:N_CLASSES, 0].astype(jnp.int32).reshape(N_GROUPS, n_pairs)
    cnt = jnp.sum(cnt_c, axis=1)
    tiles_g = (cnt + TILE - 1) // TILE
    end_g = jnp.cumsum(tiles_g)
    off_g = (end_g - tiles_g) * TILE
    start_c = (off_g[:, None] + jnp.cumsum(cnt_c, axis=1) - cnt_c).reshape(N_CLASSES)
    end_c = start_c + cnt_c.reshape(N_CLASSES)
    cls_flat = cls[:, 0, :].reshape(r)
    in_class = cls_flat[:, None] == jnp.arange(N_CLASSES, dtype=jnp.int32)[None, :]
    pos = rank[:, 0, :].reshape(r) + jnp.sum(jnp.where(in_class, start_c[None, :], 0), axis=1)
    t_idx = jnp.arange(n_sorted_tiles, dtype=jnp.int32)
    tile_grp = jnp.zeros_like(t_idx)
    for g in range(N_GROUPS - 1):
        tile_grp = tile_grp + (t_idx >= end_g[g]).astype(jnp.int32)
    tile_on = (t_idx < end_g[N_GROUPS - 1]).astype(jnp.int32)
    lo = t_idx[:, None] * TILE
    overlap = ((start_c[None, :] < lo + TILE) & (end_c[None, :] > lo)
               & (end_c > start_c)[None, :])
    member = jnp.array([[int(e in pair) for e in range(EXPERTS_PER_GROUP)] for pair in EXPERT_PAIRS] * N_GROUPS,
                       dtype=jnp.int32)
    tile_need = jnp.max(overlap[:, :, None].astype(jnp.int32) * member[None], axis=1).reshape(-1)
    routed = _routed_call(tile_grp, tile_on, tile_need, pos, off_g + cnt, end_g * TILE,
                          n_sorted_tiles * TILE, h2e, *experts, layer)
    return _combine_call(pos, ysh, g2, routed, rp)


def _rot_half_cols(wcols):
    half = QK_ROPE_DIM // 2
    return jnp.concatenate([-wcols[:, half:], wcols[:, :half]], axis=1)


def _pad_cols(wcols, width):
    return jnp.pad(wcols, ((0, 0), (0, width - wcols.shape[1])))


def _split_bf16(w):
    hi = w.astype(BF16)
    lo = (w - hi.astype(F32)).astype(BF16)
    return jnp.concatenate([hi, lo], axis=1)


def _rope_tables(seq, past, s_len, n_seq):
    half = QK_ROPE_DIM // 2
    inv = ROPE_THETA ** (-jnp.arange(half, dtype=F32) / half)
    inv_wide = jnp.concatenate([inv, inv, jnp.zeros((LANES - QK_ROPE_DIM,), F32)])
    live = (jnp.arange(LANES) < QK_ROPE_DIM).astype(F32)

    def cos_sin(pos):
        ang = pos.astype(F32)[:, None] * inv_wide[None, :]
        return jnp.cos(ang), jnp.sin(ang)

    cc, sc = cos_sin(jnp.arange(0, seq, GMLP_CHUNK, dtype=jnp.int32))
    cf, sf = cos_sin(jnp.arange(GMLP_CHUNK, dtype=jnp.int32))
    cos_p = ((cc[:, None, :] * cf[None, :, :] - sc[:, None, :] * sf[None, :, :]) * live).reshape(seq, LANES)
    sin_p = ((sc[:, None, :] * cf[None, :, :] + cc[:, None, :] * sf[None, :, :]) * live).reshape(seq, LANES)
    cos_s, sin_s = cos_sin(past + jnp.arange(s_len, dtype=jnp.int32))
    return cos_p, jnp.tile(cos_s * live, (n_seq, 1)), sin_p, jnp.tile(sin_s * live, (n_seq, 1))


def _layer_weights(l, p, rope):
    o1 = GMLP_WIDTH
    o2 = 2 * GMLP_WIDTH
    o3 = o2 + Q_LORA_RANK
    o4 = o3 + KV_LORA_RANK
    w_in = p["w_in"][l]
    kpe_cols = w_in[:, o4:]
    w_in2 = jnp.concatenate([w_in[:, :o4], _pad_cols(kpe_cols, LANES),
                             _pad_cols(_rot_half_cols(kpe_cols), LANES)], axis=1).astype(BF16)
    w_qb = p["w_qb"][l]
    q_parts = []
    for hd in range(MLA_HEADS):
        base = hd * QK_HEAD_DIM
        rope_cols = w_qb[:, base + QK_NOPE_DIM:base + QK_HEAD_DIM]
        q_parts += [w_qb[:, base:base + QK_NOPE_DIM], _pad_cols(rope_cols, LANES),
                    _pad_cols(_rot_half_cols(rope_cols), LANES)]
    w_qb2 = jnp.concatenate(q_parts, axis=1).astype(BF16)
    w_kvb = p["w_kvb"][l].reshape(KV_LORA_RANK, MLA_HEADS, QK_NOPE_DIM + V_HEAD_DIM)
    w_kvb2 = jnp.concatenate([w_kvb[:, :, :QK_NOPE_DIM].reshape(KV_LORA_RANK, -1),
                              w_kvb[:, :, QK_NOPE_DIM:].reshape(KV_LORA_RANK, -1)], axis=1).astype(BF16)
    qscale = LOG2E / math.sqrt(QK_HEAD_DIM)
    gq = _pad_cols(p["q_norm_g"][l][None, :] * qscale, QK_PAD)
    gk = _pad_cols(p["k_norm_g"][l][None, :], QK_PAD)
    bound = (QK_HEAD_DIM * qscale * SCORE_BOUND_MARGIN * jnp.max(jnp.abs(p["q_norm_g"][l]))
             * jnp.max(jnp.abs(p["k_norm_g"][l])))
    bounded = bound <= MAX_SCORE_BOUND
    pad_lane = jnp.arange(LANES) == QK_ROPE_DIM
    off = jnp.stack([jnp.where(pad_lane, 1.0, 0.0),
                     jnp.where(pad_lane & bounded, -bound, 0.0)]).astype(F32)

    ws = p["gmlp_ws"][l]
    tri = jnp.tril(jnp.ones((GMLP_CHUNK, GMLP_CHUNK), dtype=bool))
    wt = jnp.where(tri[None], ws, 0.0)
    hc = GMLP_CHUNK // 2
    top = wt[:, :hc, :hc]
    zero = jnp.zeros_like(top)
    wt_s = jnp.concatenate([jnp.concatenate([top, zero], axis=2), jnp.concatenate([zero, top], axis=2)], axis=1)
    gw = jnp.stack([wt, wt_s]).astype(BF16)
    b = p["gmlp_b"][l]
    b_s = jnp.concatenate([b[:, :hc], b[:, :hc]], axis=1)
    gb = jnp.broadcast_to(jnp.stack([b, b_s])[..., None], (2, GMLP_GROUPS, GMLP_CHUNK, LANES)).astype(F32)

    w_out = p["w_out"][l].astype(BF16)
    return dict(
        sg=p["sh_w_gate"][l].astype(BF16),
        su=p["sh_w_up"][l].astype(BF16), sd=p["sh_w_down"][l].astype(BF16),
        ln1=p["ln1_g"][l][None, :], w_in=w_in2, rope=rope,
        gv=p["gmlp_v_g"][l].reshape(1, GMLP_WIDTH), gw=gw, gb=gb,
        q_a_g=p["q_a_g"][l][None, :], w_qb=w_qb2, kv_a_g=p["kv_a_g"][l][None, :], w_kvb=w_kvb2,
        gq=gq, gk=gk, w_out_a=w_out[:GMLP_WIDTH], w_out_b=w_out[GMLP_WIDTH:],
        ln2=p["ln2_g"][l][None, :], rw=_split_bf16(_pad_cols(p["router_w"], LANES)),
        rb=p["router_bias"].reshape(N_EXPERTS, 1), off=off,
        bounded=bounded.astype(jnp.int32).reshape(1),
    )


def kernel(x_prompt, x_sample, cache_kv_latent, cache_k_rope, c_prompt, c_sample, w_ada, b_ada, ln1_g, w_in,
           gmlp_v_g, gmlp_ws, gmlp_b, q_a_g, w_qb, kv_a_g, w_kvb, q_norm_g, k_norm_g, w_out, ln2_g, router_w,
           router_bias, exp_w_gate, exp_w_up, exp_w_down, sh_w_gate, sh_w_up, sh_w_down):
    p = dict(w_in=w_in, gmlp_v_g=gmlp_v_g, gmlp_ws=gmlp_ws, gmlp_b=gmlp_b, q_a_g=q_a_g, w_qb=w_qb,
             kv_a_g=kv_a_g, w_kvb=w_kvb, q_norm_g=q_norm_g, k_norm_g=k_norm_g, w_out=w_out, ln1_g=ln1_g,
             ln2_g=ln2_g, router_w=router_w, router_bias=router_bias, exp_w_gate=exp_w_gate,
             exp_w_up=exp_w_up, exp_w_down=exp_w_down, sh_w_gate=sh_w_gate, sh_w_up=sh_w_up,
             sh_w_down=sh_w_down)
    batch, seq, _ = x_prompt.shape
    n_seq, s_len, _ = x_sample.shape
    depth, _, past, _ = cache_kv_latent.shape
    assert batch == 1 and s_len == SUB and n_seq == N_SUB and n_seq * s_len == TILE
    assert seq % TILE == 0 and seq % ATT_TQ == 0 and past % GMLP_CHUNK == 0 and past % CHUNK == 0
    rp = seq
    rs = n_seq * s_len

    rope = _rope_tables(seq, past, s_len, n_seq)

    c_all = jnp.concatenate([jnp.broadcast_to(c_prompt, (N_SUB, D_MODEL)), c_sample], axis=0)
    mod = _ada_call(c_all, w_ada, b_ada)
    mod = mod.reshape(depth, 2, N_SUB, 6, D_MODEL)

    weights = [_layer_weights(l, p, rope) for l in range(depth)]
    experts = (exp_w_gate.reshape(depth * N_GROUPS, EXPERTS_PER_GROUP, D_MODEL, EXPERT_FF),
               exp_w_up.reshape(depth * N_GROUPS, EXPERTS_PER_GROUP, D_MODEL, EXPERT_FF),
               exp_w_down.reshape(depth * N_GROUPS, EXPERTS_PER_GROUP, EXPERT_FF, D_MODEL))
    lat_all = cache_kv_latent.reshape(depth, n_seq * past, KV_LORA_RANK)
    pe_all = jnp.swapaxes(cache_k_rope, 2, 3)
    k_past, v_past = _cache_kv_call(lat_all, pe_all, jnp.stack([w["w_kvb"] for w in weights]),
                                    jnp.stack([w["gk"] for w in weights]),
                                    jnp.stack([w["off"] for w in weights]))

    xp = x_prompt.reshape(rp, D_MODEL)
    xs = x_sample.reshape(rs, D_MODEL)
    planes = tuple(jnp.zeros((depth, rows, width), F32) for rows, width in
                   ((rp, KV_LORA_RANK), (rp, QK_ROPE_DIM), (rs, KV_LORA_RANK), (rs, QK_ROPE_DIM),
                    (rs, GMLP_WIDTH)))
    for l in range(depth):
        w = weights[l]
        sh1, sc1, g1, sh2, sc2, g2 = [mod[l, :, :, j, :] for j in range(6)]
        a, q, k, v, *planes = _mix_in_call(xp, xs, sc1, sh1, w, l, depth, tuple(planes))
        bp = _prompt_attn_call(w["bounded"], q, k, v, rp)
        bs = _sample_attn_call(l, q, k, v, k_past, v_past, rp, n_seq, s_len, past)
        ysh, h2e, grp, rank, counts = _mix_out_call(a, bp, bs, xp, xs, g1, sc2, sh2, g2, w)
        xp, xs = _moe(h2e, grp, rank, counts, ysh, g2, experts, l, rp)
    lat_p, pe_p, lat_s, pe_s, v_s = planes
    return (xp.reshape(batch, seq, D_MODEL), xs.reshape(n_seq, s_len, D_MODEL),
            lat_p.reshape(depth, batch, seq, KV_LORA_RANK), pe_p.reshape(depth, batch, seq, QK_ROPE_DIM),
            lat_s.reshape(depth, n_seq, s_len, KV_LORA_RANK), pe_s.reshape(depth, n_seq, s_len, QK_ROPE_DIM),
            v_s.reshape(depth, n_seq, s_len, GMLP_WIDTH))
```

```python
import functools
import math

import jax
import jax.numpy as jnp
from jax import lax
from jax.experimental import pallas as pl
from jax.experimental.pallas import tpu as pltpu

F32 = jnp.float32
BF16 = jnp.bfloat16

D_MODEL = 1024
CHUNK = 64
GMLP_WIDTH = 512
GMLP_GROUPS = 4
GMLP_CHUNK = 128
MLA_HEADS = 4
QK_NOPE_DIM = 128
QK_ROPE_DIM = 64
QK_HEAD_DIM = 192
V_HEAD_DIM = 128
Q_LORA_RANK = 384
KV_LORA_RANK = 256
ROPE_THETA = 10000.0
N_EXPERTS = 16
N_GROUPS = 4
EXPERTS_PER_GROUP = 4
EXPERT_FF = 512
EPS = 1e-6

LANES = 128
TILE = 512
SUB = 64
N_SUB = TILE // SUB
QK_PAD = 256
IN_COLS = 2 * GMLP_WIDTH + Q_LORA_RANK + KV_LORA_RANK + 2 * LANES
Q_COLS = MLA_HEADS * 3 * LANES
H2E_COLS = D_MODEL + LANES
ROW_DMA_UNROLL = 8
CACHE_TILE = 4096
EXPERT_PAIRS = ((0, 1), (0, 2), (1, 2), (1, 3), (0, 3), (2, 3))
assert sorted(EXPERT_PAIRS) == [(a, b) for a in range(EXPERTS_PER_GROUP) for b in range(a + 1, EXPERTS_PER_GROUP)]
N_CLASSES = N_GROUPS * len(EXPERT_PAIRS)
CLASS_ROWS = 32
ATT_TK = 512
ATT_TQ = 2 * ATT_TK
ATT_UNROLL = 8
ATT_HEADS = 2
VMEM_LIMIT = 56 * 1024 * 1024
ROUTED_VMEM_LIMIT = 62 * 1024 * 1024
LOG2E = 1.4426950408889634
SCORE_BOUND_MARGIN = 1.02
MAX_SCORE_BOUND = 48.0


def _cparams(n_axes):
    return pltpu.CompilerParams(dimension_semantics=("arbitrary",) * n_axes,
                                vmem_limit_bytes=VMEM_LIMIT)


def _rms(x, eps=EPS):
    return x * lax.rsqrt(jnp.mean(x * x, axis=-1, keepdims=True) + eps)


def _gelu(x):
    c = math.sqrt(2.0 / math.pi)
    return 0.5 * x * (1.0 + jnp.tanh(c * (x + 0.044715 * (x * x * x))))


def _modulate(h, scale, shift):
    h3 = h.reshape(h.shape[0] // SUB, SUB, h.shape[-1])
    h3 = h3 * (1.0 + scale[:, None, :]) + shift[:, None, :]
    return h3.reshape(h.shape)


def _gated_add(x, gate, y):
    y3 = y.reshape(y.shape[0] // SUB, SUB, y.shape[-1]) * gate[:, None, :]
    return x + y3.reshape(y.shape)


def _ada_kernel(c_ref, w_ref, b_ref, o_ref):
    c = c_ref[...]
    cs = (c / (1.0 + jnp.exp(-c))).astype(BF16)
    w = w_ref[0].astype(BF16)
    o_ref[0] = jnp.dot(cs, w, preferred_element_type=F32) + b_ref[0]


def _ada_call(c_all, w_ada, b_ada):
    depth = w_ada.shape[0]
    nblk = w_ada.shape[2] // D_MODEL
    return pl.pallas_call(
        _ada_kernel,
        grid=(depth, nblk),
        in_specs=[
            pl.BlockSpec((16, D_MODEL), lambda l, j: (0, 0)),
            pl.BlockSpec((1, D_MODEL, D_MODEL), lambda l, j: (l, 0, j)),
            pl.BlockSpec((1, 1, D_MODEL), lambda l, j: (l, 0, j)),
        ],
        out_specs=pl.BlockSpec((1, 16, D_MODEL), lambda l, j: (l, 0, j)),
        out_shape=jax.ShapeDtypeStruct((depth, 16, w_ada.shape[2]), F32),
        compiler_params=_cparams(2),
        name="ada_mod",
    )(c_all, w_ada, b_ada.reshape(depth, 1, -1))


def _mix_in_kernel(n_prompt_tiles, n_prev, xp_ref, xs_ref, sc_ref, sh_ref, ln_ref, win_ref,
                   cosp_ref, coss_ref, sinp_ref, sins_ref,
                   gv_ref, gw_ref, gb_ref, qag_ref, wqb_ref, kvag_ref, wkvb_ref, gq_ref, gk_ref, off_ref,
                   *refs):
    a_ref, q_ref, k_ref, v_ref, latp_ref, pep_ref, lats_ref, pes_ref, vn_ref = refs[n_prev:]
    i = pl.program_id(0)
    is_sample = i >= n_prompt_tiles
    o2 = 2 * GMLP_WIDTH
    o3 = o2 + Q_LORA_RANK
    o4 = o3 + KV_LORA_RANK
    part = TILE

    def rows_of(p):
        r0 = p * part
        rows = slice(r0, r0 + part)
        x = jnp.where(is_sample, xs_ref[rows, :], xp_ref[rows, :])
        h = _modulate(_rms(x) * ln_ref[...], sc_ref[0], sh_ref[0])
        z = jnp.dot(h.astype(BF16), win_ref[...], preferred_element_type=F32)

        u = _gelu(z[:, :GMLP_WIDTH])
        v = _gelu(z[:, GMLP_WIDTH:o2])
        gv = gv_ref[...]
        vn_parts = []
        for g in range(GMLP_GROUPS):
            sl = slice(g * LANES, (g + 1) * LANES)
            vn_parts.append(_rms(v[:, sl]) * gv[:, sl])
        for g in range(GMLP_GROUPS):
            sl = slice(g * LANES, (g + 1) * LANES)
            vb = vn_parts[g].astype(BF16)
            for c in range(part // GMLP_CHUNK):
                crow = slice(c * GMLP_CHUNK, (c + 1) * GMLP_CHUNK)
                s = jnp.dot(gw_ref[0, g], vb[crow], preferred_element_type=F32) + gb_ref[0, g]
                a_ref[r0 + c * GMLP_CHUNK:r0 + (c + 1) * GMLP_CHUNK, sl] = (u[crow, sl] * s).astype(BF16)

        cos = jnp.where(is_sample, coss_ref[rows, :], cosp_ref[rows, :])
        sin = jnp.where(is_sample, sins_ref[rows, :], sinp_ref[rows, :])

        ql = _rms(z[:, o2:o3]) * qag_ref[...]
        qq = jnp.dot(ql.astype(BF16), wqb_ref[...], preferred_element_type=F32)
        gq = gq_ref[...]
        for hd in range(MLA_HEADS):
            base = hd * 3 * LANES
            nope = qq[:, base:base + LANES]
            rope = qq[:, base + LANES:base + 2 * LANES] * cos + qq[:, base + 2 * LANES:base + 3 * LANES] * sin
            ss = jnp.sum(nope * nope, axis=-1, keepdims=True) + jnp.sum(rope * rope, axis=-1, keepdims=True)
            rinv = lax.rsqrt(ss * (1.0 / QK_HEAD_DIM) + EPS)
            q_ref[hd, rows, :LANES] = (nope * rinv * gq[:, :LANES]).astype(BF16)
            q_ref[hd, rows, LANES:] = (rope * rinv * gq[:, LANES:] + off_ref[0:1, :]).astype(BF16)

        ckv = _rms(z[:, o3:o4]) * kvag_ref[...]
        kpe = z[:, o4:o4 + LANES] * cos + z[:, o4 + LANES:o4 + 2 * LANES] * sin
        kv = jnp.dot(ckv.astype(BF16), wkvb_ref[...], preferred_element_type=F32)
        gk = gk_ref[...]
        kpe_ss = jnp.sum(kpe * kpe, axis=-1, keepdims=True)
        kpe_g = kpe * gk[:, LANES:]
        for hd in range(MLA_HEADS):
            nope = kv[:, hd * LANES:(hd + 1) * LANES]
            ss = jnp.sum(nope * nope, axis=-1, keepdims=True) + kpe_ss
            rinv = lax.rsqrt(ss * (1.0 / QK_HEAD_DIM) + EPS)
            k_ref[hd, rows, :LANES] = (nope * rinv * gk[:, :LANES]).astype(BF16)
            k_ref[hd, rows, LANES:] = (kpe_g * rinv + off_ref[1:2, :]).astype(BF16)
            v_ref[hd, rows, :] = kv[:, (MLA_HEADS + hd) * LANES:(MLA_HEADS + hd + 1) * LANES].astype(BF16)
        return rows, vn_parts, ckv, kpe

    per_layer = [rows_of(0)]

    @pl.when(jnp.logical_not(is_sample))
    def _():
        for rows, _, ckv, kpe in per_layer:
            latp_ref[0, rows, :] = ckv
            pep_ref[0, rows, :] = kpe[:, :QK_ROPE_DIM]

    @pl.when(is_sample)
    def _():
        for rows, vn_parts, ckv, kpe in per_layer:
            lats_ref[0, rows, :] = ckv
            pes_ref[0, rows, :] = kpe[:, :QK_ROPE_DIM]
            for g in range(GMLP_GROUPS):
                vn_ref[0, rows, g * LANES:(g + 1) * LANES] = vn_parts[g]


def _mix_in_call(xp, xs, sc1, sh1, w, layer, depth, prev):
    rp, rs = xp.shape[0], xs.shape[0]
    npt = rp // TILE
    nt = npt + rs // TILE
    r = rp + rs
    last = npt - 1

    def full(a):
        nd = a.ndim
        return pl.BlockSpec(a.shape, lambda i: (0,) * nd)

    def variant(a):
        nd = a.ndim
        return pl.BlockSpec((1,) + a.shape[1:], lambda i: (i // npt,) + (0,) * (nd - 1))

    row = lambda width: pl.BlockSpec((TILE, width), lambda i: (i, 0))
    head = lambda width: pl.BlockSpec((MLA_HEADS, TILE, width), lambda i: (0, i, 0))
    in_specs = [
        pl.BlockSpec((TILE, D_MODEL), lambda i: (jnp.minimum(i, last), 0)),
        pl.BlockSpec((TILE, D_MODEL), lambda i: (0, 0)),
        variant(sc1), variant(sh1), full(w["ln1"]), full(w["w_in"]),
        pl.BlockSpec((TILE, LANES), lambda i: (jnp.minimum(i, last), 0)),
        pl.BlockSpec((TILE, LANES), lambda i: (0, 0)),
        pl.BlockSpec((TILE, LANES), lambda i: (jnp.minimum(i, last), 0)),
        pl.BlockSpec((TILE, LANES), lambda i: (0, 0)),
        full(w["gv"]), variant(w["gw"]), variant(w["gb"]),
        full(w["q_a_g"]), full(w["w_qb"]), full(w["kv_a_g"]), full(w["w_kvb"]),
        full(w["gq"]), full(w["gk"]), full(w["off"]),
    ] + [pl.BlockSpec(memory_space=pl.ANY)] * len(prev)
    prompt_plane = lambda width: pl.BlockSpec((1, TILE, width), lambda i: (layer, jnp.minimum(i, last), 0))
    sample_plane = lambda width: pl.BlockSpec((1, TILE, width), lambda i: (layer, 0, 0))
    out_specs = [
        row(GMLP_WIDTH), head(QK_PAD), head(QK_PAD), head(V_HEAD_DIM),
        prompt_plane(KV_LORA_RANK), prompt_plane(QK_ROPE_DIM),
        sample_plane(KV_LORA_RANK), sample_plane(QK_ROPE_DIM), sample_plane(GMLP_WIDTH),
    ]
    out_shape = [
        jax.ShapeDtypeStruct((r, GMLP_WIDTH), BF16),
        jax.ShapeDtypeStruct((MLA_HEADS, r, QK_PAD), BF16),
        jax.ShapeDtypeStruct((MLA_HEADS, r, QK_PAD), BF16),
        jax.ShapeDtypeStruct((MLA_HEADS, r, V_HEAD_DIM), BF16),
        jax.ShapeDtypeStruct((depth, rp, KV_LORA_RANK), F32),
        jax.ShapeDtypeStruct((depth, rp, QK_ROPE_DIM), F32),
        jax.ShapeDtypeStruct((depth, rs, KV_LORA_RANK), F32),
        jax.ShapeDtypeStruct((depth, rs, QK_ROPE_DIM), F32),
        jax.ShapeDtypeStruct((depth, rs, GMLP_WIDTH), F32),
    ]
    n_fixed = len(in_specs) - len(prev)
    n_own = 4
    return pl.pallas_call(
        functools.partial(_mix_in_kernel, npt, len(prev)),
        grid=(nt,), in_specs=in_specs, out_specs=out_specs, out_shape=out_shape,
        input_output_aliases={n_fixed + j: n_own + j for j in range(len(prev))},
        compiler_params=_cparams(1), name="mix_in",
    )(xp, xs, sc1, sh1, w["ln1"], w["w_in"], *w["rope"], w["gv"], w["gw"], w["gb"],
      w["q_a_g"], w["w_qb"], w["kv_a_g"], w["w_kvb"], w["gq"], w["gk"], w["off"], *prev)


def _cache_kv_kernel(lat_ref, pe_ref, wkvb_ref, gk_ref, off_ref, k_ref, v_ref):
    lat = lat_ref[0]
    pe_t = pe_ref[0, 0]
    kpe = jnp.concatenate([pe_t, jnp.zeros_like(pe_t)], axis=0).T
    kv = jnp.dot(lat.astype(BF16), wkvb_ref[0], preferred_element_type=F32)
    gk = gk_ref[0]
    kpe_ss = jnp.sum(kpe * kpe, axis=-1, keepdims=True)
    kpe_g = kpe * gk[:, LANES:]
    for hd in range(MLA_HEADS):
        nope = kv[:, hd * LANES:(hd + 1) * LANES]
        ss = jnp.sum(nope * nope, axis=-1, keepdims=True) + kpe_ss
        rinv = lax.rsqrt(ss * (1.0 / QK_HEAD_DIM) + EPS)
        k_ref[0, hd, :, :LANES] = (nope * rinv * gk[:, :LANES]).astype(BF16)
        k_ref[0, hd, :, LANES:] = (kpe_g * rinv + off_ref[0, 1:2, :]).astype(BF16)
        v_ref[0, hd] = kv[:, (MLA_HEADS + hd) * LANES:(MLA_HEADS + hd + 1) * LANES].astype(BF16)


def _cache_kv_call(lat, pe_t, w_kvb, gk, off):
    depth, rows, _ = lat.shape
    t = min(CACHE_TILE, pe_t.shape[-1])
    per_stream = pe_t.shape[-1] // t
    return pl.pallas_call(
        _cache_kv_kernel,
        grid=(depth, rows // t),
        in_specs=[
            pl.BlockSpec((1, t, KV_LORA_RANK), lambda l, i: (l, i, 0)),
            pl.BlockSpec((1, 1, QK_ROPE_DIM, t), lambda l, i: (l, i // per_stream, 0, i % per_stream)),
            pl.BlockSpec((1,) + w_kvb.shape[1:], lambda l, i: (l, 0, 0)),
            pl.BlockSpec((1, 1, QK_PAD), lambda l, i: (l, 0, 0)),
            pl.BlockSpec((1, 2, LANES), lambda l, i: (l, 0, 0)),
        ],
        out_specs=[
            pl.BlockSpec((1, MLA_HEADS, t, QK_PAD), lambda l, i: (l, 0, i, 0)),
            pl.BlockSpec((1, MLA_HEADS, t, V_HEAD_DIM), lambda l, i: (l, 0, i, 0)),
        ],
        out_shape=[
            jax.ShapeDtypeStruct((depth, MLA_HEADS, rows, QK_PAD), BF16),
            jax.ShapeDtypeStruct((depth, MLA_HEADS, rows, V_HEAD_DIM), BF16),
        ],
        compiler_params=_cparams(2), name="cache_kv",
    )(lat, pe_t, w_kvb, gk, off)


def _attn_step(q, k, v, m, l, acc, mask):
    s = lax.dot_general(q, k, (((1,), (1,)), ((), ())), preferred_element_type=F32)
    if mask is not None:
        s = jnp.where(mask, s, -1e30)
    m_new = jnp.maximum(m, jnp.max(s, axis=-1, keepdims=True))
    alpha = jnp.exp2(m - m_new)
    p = jnp.exp2(s - m_new)
    l = alpha * l + jnp.sum(p, axis=-1, keepdims=True)
    acc = alpha * acc + jnp.dot(p.astype(BF16), v, preferred_element_type=F32)
    return m_new, l, acc


def _bounded_step(q, k, v_aug, acc, mask):
    s = lax.dot_general(q, k, (((1,), (1,)), ((), ())), preferred_element_type=F32)
    if mask is not None:
        s = jnp.where(mask, s, -1e30)
    return acc + jnp.dot(jnp.exp2(s).astype(BF16), v_aug, preferred_element_type=F32)


def _prompt_attn_kernel(bounded_ref, q_ref, k_ref, v_ref, o_ref):
    qi = pl.program_id(1)
    row = lax.broadcasted_iota(jnp.int32, (ATT_TK, ATT_TK), 0)
    col = lax.broadcasted_iota(jnp.int32, (ATT_TK, ATT_TK), 1)
    mask = (col // CHUNK) <= (row // CHUNK)

    def q_half(h, half):
        return q_ref[h, half * ATT_TK:(half + 1) * ATT_TK, :]

    def kv(h, j):
        off = pl.multiple_of(j * ATT_TK, ATT_TK)
        return k_ref[h, pl.ds(off, ATT_TK), :], v_ref[h, pl.ds(off, ATT_TK), :]

    def out(h, half):
        return o_ref.at[half * ATT_TK:(half + 1) * ATT_TK, h * V_HEAD_DIM:(h + 1) * V_HEAD_DIM]

    @pl.when(bounded_ref[0] > 0)
    def _():
        lane = lax.broadcasted_iota(jnp.int32, (ATT_TK, LANES), 1)
        ones_col = jnp.where(lane == 0, 1.0, 0.0).astype(BF16)

        def kv_aug(h, j):
            k, v = kv(h, j)
            return k, jnp.concatenate([v, ones_col], axis=1)

        def blocks(first, count, accs):
            accs = [list(a) for a in accs]
            for u in range(count):
                for h in range(ATT_HEADS):
                    k, v = kv_aug(h, first + u)
                    for half in range(2):
                        accs[h][half] = _bounded_step(q_half(h, half), k, v, accs[h][half], None)
            return tuple(tuple(a) for a in accs)

        zero = jnp.zeros((ATT_TK, 2 * LANES), F32)
        n_full = 2 * qi
        n_trips = n_full // ATT_UNROLL
        accs = lax.fori_loop(0, n_trips, lambda j, c: blocks(ATT_UNROLL * j, ATT_UNROLL, c),
                             ((zero, zero),) * ATT_HEADS)
        done = ATT_UNROLL * n_trips
        part = ATT_UNROLL // 2
        while part >= 2:
            accs = lax.cond((n_full & part) != 0, lambda c, d=done, p=part: blocks(d, p, c), lambda c: c, accs)
            done = done + (n_full & part)
            part //= 2
        for h in range(ATT_HEADS):
            acc_a, acc_b = accs[h]
            k, v = kv_aug(h, 2 * qi)
            acc_a = _bounded_step(q_half(h, 0), k, v, acc_a, mask)
            acc_b = _bounded_step(q_half(h, 1), k, v, acc_b, None)
            k, v = kv_aug(h, 2 * qi + 1)
            acc_b = _bounded_step(q_half(h, 1), k, v, acc_b, mask)
            out(h, 0)[...] = (acc_a[:, :LANES] / acc_a[:, LANES:LANES + 1]).astype(BF16)
            out(h, 1)[...] = (acc_b[:, :LANES] / acc_b[:, LANES:LANES + 1]).astype(BF16)

    @pl.when(bounded_ref[0] == 0)
    def _():
        def init():
            return (jnp.full((ATT_TK, 1), -1e30, F32), jnp.zeros((ATT_TK, 1), F32),
                    jnp.zeros((ATT_TK, V_HEAD_DIM), F32))

        for h in range(ATT_HEADS):
            qa, qb = q_half(h, 0), q_half(h, 1)

            def body(j, carry, h=h, qa=qa, qb=qb):
                k, v = kv(h, j)
                return _attn_step(qa, k, v, *carry[0], None), _attn_step(qb, k, v, *carry[1], None)

            sa, sb = lax.fori_loop(0, 2 * qi, body, (init(), init()))
            k, v = kv(h, 2 * qi)
            sa = _attn_step(qa, k, v, *sa, mask)
            sb = _attn_step(qb, k, v, *sb, None)
            k, v = kv(h, 2 * qi + 1)
            sb = _attn_step(qb, k, v, *sb, mask)
            out(h, 0)[...] = (sa[2] / sa[1]).astype(BF16)
            out(h, 1)[...] = (sb[2] / sb[1]).astype(BF16)


def _prompt_attn_call(bounded, q, k, v, rp):
    r = q.shape[1]
    resident = lambda arr: pl.BlockSpec((ATT_HEADS,) + arr.shape[1:], lambda h, i, b: (h, 0, 0),
                                        pipeline_mode=pl.Buffered(1))
    return pl.pallas_call(
        _prompt_attn_kernel,
        grid_spec=pltpu.PrefetchScalarGridSpec(
            num_scalar_prefetch=1, grid=(MLA_HEADS // ATT_HEADS, rp // ATT_TQ),
            in_specs=[
                pl.BlockSpec((ATT_HEADS, ATT_TQ, QK_PAD), lambda h, i, b: (h, i, 0)),
                resident(k), resident(v),
            ],
            out_specs=pl.BlockSpec((ATT_TQ, ATT_HEADS * V_HEAD_DIM), lambda h, i, b: (i, h))),
        out_shape=jax.ShapeDtypeStruct((rp, MLA_HEADS * V_HEAD_DIM), BF16),
        compiler_params=_cparams(2), name="prompt_attn",
    )(bounded, q, k, v)


def _sample_attn_kernel(q_ref, kp_ref, vp_ref, kn_ref, vn_ref, o_ref):
    nt = (((1,), (1,)), ((), ()))
    for h in range(MLA_HEADS):
        q = q_ref[h]
        s1 = lax.dot_general(q, kp_ref[0, h], nt, preferred_element_type=F32)
        s2 = lax.dot_general(q, kn_ref[h], nt, preferred_element_type=F32)
        m = jnp.maximum(jnp.max(s1, axis=-1, keepdims=True), jnp.max(s2, axis=-1, keepdims=True))
        p1 = jnp.exp2(s1 - m)
        p2 = jnp.exp2(s2 - m)
        l = jnp.sum(p1, axis=-1, keepdims=True) + jnp.sum(p2, axis=-1, keepdims=True)
        o = (jnp.dot(p1.astype(BF16), vp_ref[0, h], preferred_element_type=F32)
             + jnp.dot(p2.astype(BF16), vn_ref[h], preferred_element_type=F32))
        o_ref[:, h * V_HEAD_DIM:(h + 1) * V_HEAD_DIM] = (o / l).astype(BF16)


def _sample_attn_call(layer, q, k, v, k_past, v_past, rp, n_seq, s_len, past):
    first = rp // s_len
    new = lambda width: pl.BlockSpec((MLA_HEADS, s_len, width), lambda b: (0, first + b, 0))
    cached = lambda width: pl.BlockSpec((1, MLA_HEADS, past, width), lambda b: (layer, 0, b, 0))
    return pl.pallas_call(
        _sample_attn_kernel,
        grid=(n_seq,),
        in_specs=[new(QK_PAD), cached(QK_PAD), cached(V_HEAD_DIM), new(QK_PAD), new(V_HEAD_DIM)],
        out_specs=pl.BlockSpec((s_len, MLA_HEADS * V_HEAD_DIM), lambda b: (b, 0)),
        out_shape=jax.ShapeDtypeStruct((n_seq * s_len, MLA_HEADS * V_HEAD_DIM), BF16),
        compiler_params=_cparams(1), name="sample_attn",
    )(q, k_past, v_past, k, v)


def _route(logits_t, bias_col):
    scores = 1.0 / (1.0 + jnp.exp(-logits_t))
    biased = scores + bias_col
    s_rows = [scores[e:e + 1, :] for e in range(N_EXPERTS)]
    b_rows = [biased[e:e + 1, :] for e in range(N_EXPERTS)]
    group_score = []
    for g in range(N_GROUPS):
        rows = b_rows[g * EXPERTS_PER_GROUP:(g + 1) * EXPERTS_PER_GROUP]
        best = None
        for a in range(EXPERTS_PER_GROUP):
            for b in range(a + 1, EXPERTS_PER_GROUP):
                pair = rows[a] + rows[b]
                best = pair if best is None else jnp.maximum(best, pair)
        group_score.append(best)
    best_group = jnp.zeros_like(group_score[0], dtype=jnp.int32)
    best_val = group_score[0]
    for g in range(1, N_GROUPS):
        better = group_score[g] > best_val
        best_group = jnp.where(better, g, best_group)
        best_val = jnp.where(better, group_score[g], best_val)
    selected = []
    for e in range(N_EXPERTS):
        g = e // EXPERTS_PER_GROUP
        rank = jnp.zeros_like(best_group)
        for j in range(g * EXPERTS_PER_GROUP, (g + 1) * EXPERTS_PER_GROUP):
            if j == e:
                continue
            ahead = b_rows[j] > b_rows[e]
            if j < e:
                ahead = ahead | (b_rows[j] == b_rows[e])
            rank = rank + ahead.astype(jnp.int32)
        selected.append((best_group == g) & (rank < 2))
    denom = jnp.zeros_like(s_rows[0])
    for e in range(N_EXPERTS):
        denom = denom + jnp.where(selected[e], s_rows[e], 0.0)
    classes = [selected[g * EXPERTS_PER_GROUP + a] & selected[g * EXPERTS_PER_GROUP + b]
               for g in range(N_GROUPS) for a, b in EXPERT_PAIRS]
    return [jnp.where(selected[e], s_rows[e] / denom, 0.0) for e in range(N_EXPERTS)], classes


def _mix_out_kernel(n_prompt_tiles, a_ref, bp_ref, bs_ref, xp_ref, xs_ref, wa_ref, wb_ref, g1_ref, sc_ref,
                    sh_ref, g2_ref, ln_ref, rw_ref, rb_ref, sg_ref, su_ref, sd_ref, upper_ref,
                    ysh_ref, h2e_ref, grp_ref, rank_ref, cnt_ref, gt_ref, oh_ref, carry_ref):
    i = pl.program_id(0)
    is_sample = i >= n_prompt_tiles

    @pl.when(i == 0)
    def _():
        carry_ref[...] = jnp.zeros_like(carry_ref)

    x = jnp.where(is_sample, xs_ref[...], xp_ref[...])
    b = jnp.where(is_sample, bs_ref[...], bp_ref[...])
    mix = (jnp.dot(a_ref[...], wa_ref[...], preferred_element_type=F32)
           + jnp.dot(b, wb_ref[...], preferred_element_type=F32))
    xm = _gated_add(x, g1_ref[0], mix)
    h2 = _modulate(_rms(xm) * ln_ref[...], sc_ref[0], sh_ref[0])
    h2e_ref[:, :D_MODEL] = h2
    h2_hi = h2.astype(BF16)
    act = _silu(jnp.dot(h2_hi, sg_ref[...], preferred_element_type=F32)) * jnp.dot(
        h2_hi, su_ref[...], preferred_element_type=F32)
    shared = jnp.dot(act.astype(BF16), sd_ref[...], preferred_element_type=F32)
    ysh_ref[...] = _gated_add(xm, g2_ref[0], shared)
    h2_lo = (h2 - h2_hi.astype(F32)).astype(BF16)
    by_hi = jnp.dot(h2_hi, rw_ref[...], preferred_element_type=F32)
    logits = (by_hi[:, :LANES] + by_hi[:, LANES:]
              + jnp.dot(h2_lo, rw_ref[:, :LANES], preferred_element_type=F32))
    gate_rows, classes = _route(logits.T[:N_EXPERTS, :], rb_ref[...])
    gt_ref[...] = jnp.zeros_like(gt_ref)
    for e in range(N_EXPERTS):
        gt_ref[e:e + 1, :] = gate_rows[e]
    h2e_ref[:, D_MODEL:] = gt_ref[...].T

    oh_ref[...] = jnp.zeros_like(oh_ref)
    for c in range(N_CLASSES):
        oh_ref[c:c + 1, :] = jnp.where(classes[c], 1.0, 0.0)
    onehot = oh_ref[...]
    cum = jnp.dot(onehot.astype(BF16), upper_ref[...], preferred_element_type=F32)
    carry = carry_ref[...]
    rank = jnp.sum(onehot * (cum - 1.0 + carry[:, :1]), axis=0, keepdims=True)
    cls_id = lax.broadcasted_iota(jnp.int32, (CLASS_ROWS, TILE), 0).astype(F32)
    cls_row = jnp.sum(onehot * cls_id, axis=0, keepdims=True).astype(jnp.int32)
    grp_ref[0] = jnp.broadcast_to(cls_row, (8, TILE))
    rank_ref[0] = jnp.broadcast_to(rank.astype(jnp.int32), (8, TILE))
    carry = carry + jnp.sum(onehot, axis=1, keepdims=True)
    carry_ref[...] = carry
    cnt_ref[...] = carry


def _mix_out_call(a, bp, bs, xp, xs, g1, sc2, sh2, g2, w):
    rp, rs = xp.shape[0], xs.shape[0]
    upper = jnp.triu(jnp.ones((TILE, TILE), BF16))
    npt = rp // TILE
    nt = npt + rs // TILE
    r = rp + rs
    last = npt - 1

    def full(arr):
        nd = arr.ndim
        return pl.BlockSpec(arr.shape, lambda i: (0,) * nd)

    def variant(arr):
        nd = arr.ndim
        return pl.BlockSpec((1,) + arr.shape[1:], lambda i: (i // npt,) + (0,) * (nd - 1))

    row = lambda width: pl.BlockSpec((TILE, width), lambda i: (i, 0))
    prow = lambda width: pl.BlockSpec((TILE, width), lambda i: (jnp.minimum(i, last), 0))
    srow = lambda width: pl.BlockSpec((TILE, width), lambda i: (0, 0))
    return pl.pallas_call(
        functools.partial(_mix_out_kernel, npt),
        grid=(nt,),
        in_specs=[row(GMLP_WIDTH), prow(GMLP_WIDTH), srow(GMLP_WIDTH), prow(D_MODEL), srow(D_MODEL),
                  full(w["w_out_a"]), full(w["w_out_b"]), variant(g1), variant(sc2), variant(sh2),
                  variant(g2), full(w["ln2"]), full(w["rw"]), full(w["rb"]),
                  full(w["sg"]), full(w["su"]), full(w["sd"]), full(upper)],
        out_specs=[row(D_MODEL), row(H2E_COLS),
                   pl.BlockSpec((1, 8, TILE), lambda i: (i, 0, 0)),
                   pl.BlockSpec((1, 8, TILE), lambda i: (i, 0, 0)),
                   pl.BlockSpec((CLASS_ROWS, LANES), lambda i: (0, 0))],
        out_shape=[jax.ShapeDtypeStruct((r, D_MODEL), F32),
                   jax.ShapeDtypeStruct((r, H2E_COLS), F32),
                   jax.ShapeDtypeStruct((nt, 8, TILE), jnp.int32),
                   jax.ShapeDtypeStruct((nt, 8, TILE), jnp.int32),
                   jax.ShapeDtypeStruct((CLASS_ROWS, LANES), F32)],
        scratch_shapes=[pltpu.VMEM((LANES, TILE), F32), pltpu.VMEM((CLASS_ROWS, TILE), F32),
                        pltpu.VMEM((CLASS_ROWS, LANES), F32)],
        compiler_params=_cparams(1), name="mix_out",
    )(a, bp, bs, xp, xs, w["w_out_a"], w["w_out_b"], g1, sc2, sh2, g2, w["ln2"], w["rw"], w["rb"],
      w["sg"], w["su"], w["sd"], upper)


def _silu(x):
    return x / (1.0 + jnp.exp(-x))


def _invert_positions(pos_ref, pad_lo_ref, pad_hi_ref, src_ref):
    def clear(s, carry):
        src_ref[s] = 0
        return carry

    for g in range(N_GROUPS):
        lax.fori_loop(pad_lo_ref[g], pad_hi_ref[g], clear, 0)
    lax.fori_loop(pad_hi_ref[N_GROUPS - 1], src_ref.shape[0], clear, 0)

    def place(i, carry):
        src_ref[pos_ref[i]] = i
        return carry

    lax.fori_loop(0, pos_ref.shape[0], place, 0, unroll=ROW_DMA_UNROLL)


def _gather_rows(idx_ref, base, src_ref, dst_ref, sem):
    for r in range(TILE):
        pltpu.make_async_copy(src_ref.at[pl.ds(idx_ref[base + r], 1)], dst_ref.at[pl.ds(r, 1)], sem).start()


def _wait_rows(src_ref, dst_ref, sem):
    pltpu.make_async_copy(src_ref.at[pl.ds(0, TILE)], dst_ref, sem).wait()


def _routed_kernel(grp_ref, on_ref, need_ref, pos_ref, pad_lo_ref, pad_hi_ref, h_ref, wg_ref, wu_ref, wd_ref,
                   o_ref, buf_ref, src_ref, sem):
    t = pl.program_id(0)
    n = pl.num_programs(0)
    slot = t % 2
    on = on_ref[t] > 0

    @pl.when(t == 0)
    def _():
        _invert_positions(pos_ref, pad_lo_ref, pad_hi_ref, src_ref)

    @pl.when((t == 0) & on)
    def _():
        _gather_rows(src_ref, 0, h_ref, buf_ref.at[0], sem.at[0])

    @pl.when((on_ref[jnp.minimum(t + 1, n - 1)] > 0) & (t + 1 < n))
    def _():
        _gather_rows(src_ref, (t + 1) * TILE, h_ref, buf_ref.at[1 - slot], sem.at[1 - slot])

    @pl.when(jnp.logical_not(on))
    def _():
        o_ref[...] = jnp.zeros_like(o_ref)

    @pl.when(on)
    def _():
        _wait_rows(h_ref, buf_ref.at[slot], sem.at[slot])
        first = grp_ref[t] * EXPERTS_PER_GROUP
        o_ref[...] = jnp.zeros_like(o_ref)
        for e in range(EXPERTS_PER_GROUP):
            @pl.when(need_ref[t * EXPERTS_PER_GROUP + e] > 0)
            def _(e=e):
                h = buf_ref[slot, :, :D_MODEL]
                hg = jnp.dot(h, wg_ref[0, e], preferred_element_type=F32)
                hu = jnp.dot(h, wu_ref[0, e], preferred_element_type=F32)
                lane = lax.broadcasted_iota(jnp.int32, (TILE, LANES), 1)
                gate = jnp.sum(jnp.where(lane == first + e, buf_ref[slot, :, D_MODEL:], 0.0),
                               axis=-1, keepdims=True)
                o_ref[...] += jnp.dot(_silu(hg) * hu, wd_ref[0, e], preferred_element_type=F32) * gate


def _routed_call(tile_grp, tile_on, tile_need, pos, pad_lo, pad_hi, n_slots, h2e, wg, wu, wd, layer):
    nts = n_slots // TILE
    wspec = lambda arr: pl.BlockSpec((1,) + arr.shape[1:],
                                     lambda t, grp, *_: (layer * N_GROUPS + grp[t], 0, 0, 0))
    return pl.pallas_call(
        _routed_kernel,
        grid_spec=pltpu.PrefetchScalarGridSpec(
            num_scalar_prefetch=6, grid=(nts,),
            in_specs=[pl.BlockSpec(memory_space=pl.ANY), wspec(wg), wspec(wu), wspec(wd)],
            out_specs=pl.BlockSpec((TILE, D_MODEL), lambda t, *_: (t, 0)),
            scratch_shapes=[pltpu.VMEM((2, TILE, H2E_COLS), F32), pltpu.SMEM((n_slots,), jnp.int32),
                            pltpu.SemaphoreType.DMA((2,))]),
        out_shape=jax.ShapeDtypeStruct((n_slots, D_MODEL), F32),
        compiler_params=pltpu.CompilerParams(dimension_semantics=("arbitrary",),
                                             vmem_limit_bytes=ROUTED_VMEM_LIMIT), name="moe_routed",
    )(tile_grp, tile_on, tile_need, pos, pad_lo, pad_hi, h2e, wg, wu, wd)


def _combine_kernel(n_prompt_tiles, pos_ref, ysh_ref, g2_ref, routed_ref, yp_ref, ys_ref, buf_ref, sem):
    i = pl.program_id(0)
    n = pl.num_programs(0)
    slot = i % 2

    @pl.when(i == 0)
    def _():
        _gather_rows(pos_ref, 0, routed_ref, buf_ref.at[0], sem.at[0])

    @pl.when(i + 1 < n)
    def _():
        _gather_rows(pos_ref, (i + 1) * TILE, routed_ref, buf_ref.at[1 - slot], sem.at[1 - slot])

    _wait_rows(routed_ref, buf_ref.at[slot], sem.at[slot])
    y = _gated_add(ysh_ref[...], g2_ref[0], buf_ref[slot])

    @pl.when(i < n_prompt_tiles)
    def _():
        yp_ref[...] = y

    @pl.when(i >= n_prompt_tiles)
    def _():
        ys_ref[...] = y


def _combine_call(pos, ysh, g2, routed, rp):
    r = ysh.shape[0]
    rs = r - rp
    npt = rp // TILE
    last = npt - 1
    return pl.pallas_call(
        functools.partial(_combine_kernel, npt),
        grid_spec=pltpu.PrefetchScalarGridSpec(
            num_scalar_prefetch=1, grid=(r // TILE,),
            in_specs=[pl.BlockSpec((TILE, D_MODEL), lambda i, pos: (i, 0)),
                      pl.BlockSpec((1,) + g2.shape[1:], lambda i, pos: (i // npt, 0, 0)),
                      pl.BlockSpec(memory_space=pl.ANY)],
            out_specs=[pl.BlockSpec((TILE, D_MODEL), lambda i, pos: (jnp.minimum(i, last), 0)),
                       pl.BlockSpec((TILE, D_MODEL), lambda i, pos: (0, 0))],
            scratch_shapes=[pltpu.VMEM((2, TILE, D_MODEL), F32), pltpu.SemaphoreType.DMA((2,))]),
        out_shape=[jax.ShapeDtypeStruct((rp, D_MODEL), F32),
                   jax.ShapeDtypeStruct((rs, D_MODEL), F32)],
        compiler_params=_cparams(1), name="moe_combine",
    )(pos, ysh, g2, routed)


def _moe(h2e, cls, rank, counts, ysh, g2, experts, layer, rp):
    r = h2e.shape[0]
    n_pairs = len(EXPERT_PAIRS)
    n_sorted_tiles = r // TILE + N_GROUPS
    cnt_c = counts[:N_CLASSES, 0].astype(jnp.int32).reshape(N_GROUPS, n_pairs)
    cnt = jnp.sum(cnt_c, axis=1)
    tiles_g = (cnt + TILE - 1) // TILE
    end_g = jnp.cumsum(tiles_g)
    off_g = (end_g - tiles_g) * TILE
    start_c = (off_g[:, None] + jnp.cumsum(cnt_c, axis=1) - cnt_c).reshape(N_CLASSES)
    end_c = start_c + cnt_c.reshape(N_CLASSES)
    cls_flat = cls[:, 0, :].reshape(r)
    in_class = cls_flat[:, None] == jnp.arange(N_CLASSES, dtype=jnp.int32)[None, :]
    pos = rank[:, 0, :].reshape(r) + jnp.sum(jnp.where(in_class, start_c[None, :], 0), axis=1)
    t_idx = jnp.arange(n_sorted_tiles, dtype=jnp.int32)
    tile_grp = jnp.zeros_like(t_idx)
    for g in range(N_GROUPS - 1):
        tile_grp = tile_grp + (t_idx >= end_g[g]).astype(jnp.int32)
    tile_on = (t_idx < end_g[N_GROUPS - 1]).astype(jnp.int32)
    lo = t_idx[:, None] * TILE
    overlap = ((start_c[None, :] < lo + TILE) & (end_c[None, :] > lo)
               & (end_c > start_c)[None, :])
    member = jnp.array([[int(e in pair) for e in range(EXPERTS_PER_GROUP)] for pair in EXPERT_PAIRS] * N_GROUPS,
                       dtype=jnp.int32)
    tile_need = jnp.max(overlap[:, :, None].astype(jnp.int32) * member[None], axis=1).reshape(-1)
    routed = _routed_call(tile_grp, tile_on, tile_need, pos, off_g + cnt, end_g * TILE,
                          n_sorted_tiles * TILE, h2e, *experts, layer)
    return _combine_call(pos, ysh, g2, routed, rp)


def _rot_half_cols(wcols):
    half = QK_ROPE_DIM // 2
    return jnp.concatenate([-wcols[:, half:], wcols[:, :half]], axis=1)


def _pad_cols(wcols, width):
    return jnp.pad(wcols, ((0, 0), (0, width - wcols.shape[1])))


def _split_bf16(w):
    hi = w.astype(BF16)
    lo = (w - hi.astype(F32)).astype(BF16)
    return jnp.concatenate([hi, lo], axis=1)


def _rope_tables(seq, past, s_len, n_seq):
    half = QK_ROPE_DIM // 2
    inv = ROPE_THETA ** (-jnp.arange(half, dtype=F32) / half)
    inv_wide = jnp.concatenate([inv, inv, jnp.zeros((LANES - QK_ROPE_DIM,), F32)])
    live = (jnp.arange(LANES) < QK_ROPE_DIM).astype(F32)

    def cos_sin(pos):
        ang = pos.astype(F32)[:, None] * inv_wide[None, :]
        return jnp.cos(ang), jnp.sin(ang)

    cc, sc = cos_sin(jnp.arange(0, seq, GMLP_CHUNK, dtype=jnp.int32))
    cf, sf = cos_sin(jnp.arange(GMLP_CHUNK, dtype=jnp.int32))
    cos_p = ((cc[:, None, :] * cf[None, :, :] - sc[:, None, :] * sf[None, :, :]) * live).reshape(seq, LANES)
    sin_p = ((sc[:, None, :] * cf[None, :, :] + cc[:, None, :] * sf[None, :, :]) * live).reshape(seq, LANES)
    cos_s, sin_s = cos_sin(past + jnp.arange(s_len, dtype=jnp.int32))
    return cos_p, jnp.tile(cos_s * live, (n_seq, 1)), sin_p, jnp.tile(sin_s * live, (n_seq, 1))


def _layer_weights(l, p, rope):
    o1 = GMLP_WIDTH
    o2 = 2 * GMLP_WIDTH
    o3 = o2 + Q_LORA_RANK
    o4 = o3 + KV_LORA_RANK
    w_in = p["w_in"][l]
    kpe_cols = w_in[:, o4:]
    w_in2 = jnp.concatenate([w_in[:, :o4], _pad_cols(kpe_cols, LANES),
                             _pad_cols(_rot_half_cols(kpe_cols), LANES)], axis=1).astype(BF16)
    w_qb = p["w_qb"][l]
    q_parts = []
    for hd in range(MLA_HEADS):
        base = hd * QK_HEAD_DIM
        rope_cols = w_qb[:, base + QK_NOPE_DIM:base + QK_HEAD_DIM]
        q_parts += [w_qb[:, base:base + QK_NOPE_DIM], _pad_cols(rope_cols, LANES),
                    _pad_cols(_rot_half_cols(rope_cols), LANES)]
    w_qb2 = jnp.concatenate(q_parts, axis=1).astype(BF16)
    w_kvb = p["w_kvb"][l].reshape(KV_LORA_RANK, MLA_HEADS, QK_NOPE_DIM + V_HEAD_DIM)
    w_kvb2 = jnp.concatenate([w_kvb[:, :, :QK_NOPE_DIM].reshape(KV_LORA_RANK, -1),
                              w_kvb[:, :, QK_NOPE_DIM:].reshape(KV_LORA_RANK, -1)], axis=1).astype(BF16)
    qscale = LOG2E / math.sqrt(QK_HEAD_DIM)
    gq = _pad_cols(p["q_norm_g"][l][None, :] * qscale, QK_PAD)
    gk = _pad_cols(p["k_norm_g"][l][None, :], QK_PAD)
    bound = (QK_HEAD_DIM * qscale * SCORE_BOUND_MARGIN * jnp.max(jnp.abs(p["q_norm_g"][l]))
             * jnp.max(jnp.abs(p["k_norm_g"][l])))
    bounded = bound <= MAX_SCORE_BOUND
    pad_lane = jnp.arange(LANES) == QK_ROPE_DIM
    off = jnp.stack([jnp.where(pad_lane, 1.0, 0.0),
                     jnp.where(pad_lane & bounded, -bound, 0.0)]).astype(F32)

    ws = p["gmlp_ws"][l]
    tri = jnp.tril(jnp.ones((GMLP_CHUNK, GMLP_CHUNK), dtype=bool))
    wt = jnp.where(tri[None], ws, 0.0)
    hc = GMLP_CHUNK // 2
    top = wt[:, :hc, :hc]
    zero = jnp.zeros_like(top)
    wt_s = jnp.concatenate([jnp.concatenate([top, zero], axis=2), jnp.concatenate([zero, top], axis=2)], axis=1)
    gw = jnp.stack([wt, wt_s]).astype(BF16)
    b = p["gmlp_b"][l]
    b_s = jnp.concatenate([b[:, :hc], b[:, :hc]], axis=1)
    gb = jnp.broadcast_to(jnp.stack([b, b_s])[..., None], (2, GMLP_GROUPS, GMLP_CHUNK, LANES)).astype(F32)

    w_out = p["w_out"][l].astype(BF16)
    return dict(
        sg=p["sh_w_gate"][l].astype(BF16),
        su=p["sh_w_up"][l].astype(BF16), sd=p["sh_w_down"][l].astype(BF16),
        ln1=p["ln1_g"][l][None, :], w_in=w_in2, rope=rope,
        gv=p["gmlp_v_g"][l].reshape(1, GMLP_WIDTH), gw=gw, gb=gb,
        q_a_g=p["q_a_g"][l][None, :], w_qb=w_qb2, kv_a_g=p["kv_a_g"][l][None, :], w_kvb=w_kvb2,
        gq=gq, gk=gk, w_out_a=w_out[:GMLP_WIDTH], w_out_b=w_out[GMLP_WIDTH:],
        ln2=p["ln2_g"][l][None, :], rw=_split_bf16(_pad_cols(p["router_w"], LANES)),
        rb=p["router_bias"].reshape(N_EXPERTS, 1), off=off,
        bounded=bounded.astype(jnp.int32).reshape(1),
    )


def kernel(x_prompt, x_sample, cache_kv_latent, cache_k_rope, c_prompt, c_sample, w_ada, b_ada, ln1_g, w_in,
           gmlp_v_g, gmlp_ws, gmlp_b, q_a_g, w_qb, kv_a_g, w_kvb, q_norm_g, k_norm_g, w_out, ln2_g, router_w,
           router_bias, exp_w_gate, exp_w_up, exp_w_down, sh_w_gate, sh_w_up, sh_w_down):
    p = dict(w_in=w_in, gmlp_v_g=gmlp_v_g, gmlp_ws=gmlp_ws, gmlp_b=gmlp_b, q_a_g=q_a_g, w_qb=w_qb,
             kv_a_g=kv_a_g, w_kvb=w_kvb, q_norm_g=q_norm_g, k_norm_g=k_norm_g, w_out=w_out, ln1_g=ln1_g,
             ln2_g=ln2_g, router_w=router_w, router_bias=router_bias, exp_w_gate=exp_w_gate,
             exp_w_up=exp_w_up, exp_w_down=exp_w_down, sh_w_gate=sh_w_gate, sh_w_up=sh_w_up,
             sh_w_down=sh_w_down)
    batch, seq, _ = x_prompt.shape
    n_seq, s_len, _ = x_sample.shape
    depth, _, past, _ = cache_kv_latent.shape
    assert batch == 1 and s_len == SUB and n_seq == N_SUB and n_seq * s_len == TILE
    assert seq % TILE == 0 and seq % ATT_TQ == 0 and past % GMLP_CHUNK == 0 and past % CHUNK == 0
    rp = seq
    rs = n_seq * s_len

    rope = _rope_tables(seq, past, s_len, n_seq)

    c_all = jnp.concatenate([jnp.broadcast_to(c_prompt, (N_SUB, D_MODEL)), c_sample], axis=0)
    mod = _ada_call(c_all, w_ada, b_ada)
    mod = mod.reshape(depth, 2, N_SUB, 6, D_MODEL)

    weights = [_layer_weights(l, p, rope) for l in range(depth)]
    experts = (exp_w_gate.reshape(depth * N_GROUPS, EXPERTS_PER_GROUP, D_MODEL, EXPERT_FF),
               exp_w_up.reshape(depth * N_GROUPS, EXPERTS_PER_GROUP, D_MODEL, EXPERT_FF),
               exp_w_down.reshape(depth * N_GROUPS, EXPERTS_PER_GROUP, EXPERT_FF, D_MODEL))
    lat_all = cache_kv_latent.reshape(depth, n_seq * past, KV_LORA_RANK)
    pe_all = jnp.swapaxes(cache_k_rope, 2, 3)
    k_past, v_past = _cache_kv_call(lat_all, pe_all, jnp.stack([w["w_kvb"] for w in weights]),
                                    jnp.stack([w["gk"] for w in weights]),
                                    jnp.stack([w["off"] for w in weights]))

    xp = x_prompt.reshape(rp, D_MODEL)
    xs = x_sample.reshape(rs, D_MODEL)
    planes = tuple(jnp.zeros((depth, rows, width), F32) for rows, width in
                   ((rp, KV_LORA_RANK), (rp, QK_ROPE_DIM), (rs, KV_LORA_RANK), (rs, QK_ROPE_DIM),
                    (rs, GMLP_WIDTH)))
    for l in range(depth):
        w = weights[l]
        sh1, sc1, g1, sh2, sc2, g2 = [mod[l, :, :, j, :] for j in range(6)]
        a, q, k, v, *planes = _mix_in_call(xp, xs, sc1, sh1, w, l, depth, tuple(planes))
        bp = _prompt_attn_call(w["bounded"], q, k, v, rp)
        bs = _sample_attn_call(l, q, k, v, k_past, v_past, rp, n_seq, s_len, past)
        ysh, h2e, grp, rank, counts = _mix_out_call(a, bp, bs, xp, xs, g1, sc2, sh2, g2, w)
        xp, xs = _moe(h2e, grp, rank, counts, ysh, g2, experts, l, rp)
    lat_p, pe_p, lat_s, pe_s, v_s = planes
    return (xp.reshape(batch, seq, D_MODEL), xs.reshape(n_seq, s_len, D_MODEL),
            lat_p.reshape(depth, batch, seq, KV_LORA_RANK), pe_p.reshape(depth, batch, seq, QK_ROPE_DIM),
            lat_s.reshape(depth, n_seq, s_len, KV_LORA_RANK), pe_s.reshape(depth, n_seq, s_len, QK_ROPE_DIM),
            v_s.reshape(depth, n_seq, s_len, GMLP_WIDTH))
```

```python
import functools
import math

import jax
import jax.numpy as jnp
from jax import lax
from jax.experimental import pallas as pl
from jax.experimental.pallas import tpu as pltpu

F32 = jnp.float32
BF16 = jnp.bfloat16

D_MODEL = 1024
CHUNK = 64
GMLP_WIDTH = 512
GMLP_GROUPS = 4
GMLP_CHUNK = 128
MLA_HEADS = 4
QK_NOPE_DIM = 128
QK_ROPE_DIM = 64
QK_HEAD_DIM = 192
V_HEAD_DIM = 128
Q_LORA_RANK = 384
KV_LORA_RANK = 256
ROPE_THETA = 10000.0
N_EXPERTS = 16
N_GROUPS = 4
EXPERTS_PER_GROUP = 4
EXPERT_FF = 512
EPS = 1e-6

LANES = 128
TILE = 512
SUB = 64
N_SUB = TILE // SUB
QK_PAD = 256
IN_COLS = 2 * GMLP_WIDTH + Q_LORA_RANK + KV_LORA_RANK + 2 * LANES
Q_COLS = MLA_HEADS * 3 * LANES
H2E_COLS = D_MODEL + LANES
ROW_DMA_UNROLL = 8
CACHE_TILE = 4096
EXPERT_PAIRS = ((0, 1), (0, 2), (1, 2), (1, 3), (0, 3), (2, 3))
assert sorted(EXPERT_PAIRS) == [(a, b) for a in range(EXPERTS_PER_GROUP) for b in range(a + 1, EXPERTS_PER_GROUP)]
N_CLASSES = N_GROUPS * len(EXPERT_PAIRS)
CLASS_ROWS = 32
ATT_TK = 512
ATT_TQ = 2 * ATT_TK
ATT_UNROLL = 8
ATT_HEADS = 2
VMEM_LIMIT = 56 * 1024 * 1024
ROUTED_VMEM_LIMIT = 62 * 1024 * 1024
LOG2E = 1.4426950408889634
SCORE_BOUND_MARGIN = 1.02
MAX_SCORE_BOUND = 48.0


def _cparams(n_axes):
    return pltpu.CompilerParams(dimension_semantics=("arbitrary",) * n_axes,
                                vmem_limit_bytes=VMEM_LIMIT)


def _rms(x, eps=EPS):
    return x * lax.rsqrt(jnp.mean(x * x, axis=-1, keepdims=True) + eps)


def _gelu(x):
    c = math.sqrt(2.0 / math.pi)
    return 0.5 * x * (1.0 + jnp.tanh(c * (x + 0.044715 * (x * x * x))))


def _modulate(h, scale, shift):
    h3 = h.reshape(h.shape[0] // SUB, SUB, h.shape[-1])
    h3 = h3 * (1.0 + scale[:, None, :]) + shift[:, None, :]
    return h3.reshape(h.shape)


def _gated_add(x, gate, y):
    y3 = y.reshape(y.shape[0] // SUB, SUB, y.shape[-1]) * gate[:, None, :]
    return x + y3.reshape(y.shape)


def _ada_kernel(c_ref, w_ref, b_ref, o_ref):
    c = c_ref[...]
    cs = (c / (1.0 + jnp.exp(-c))).astype(BF16)
    w = w_ref[0].astype(BF16)
    o_ref[0] = jnp.dot(cs, w, preferred_element_type=F32) + b_ref[0]


def _ada_call(c_all, w_ada, b_ada):
    depth = w_ada.shape[0]
    nblk = w_ada.shape[2] // D_MODEL
    return pl.pallas_call(
        _ada_kernel,
        grid=(depth, nblk),
        in_specs=[
            pl.BlockSpec((16, D_MODEL), lambda l, j: (0, 0)),
            pl.BlockSpec((1, D_MODEL, D_MODEL), lambda l, j: (l, 0, j)),
            pl.BlockSpec((1, 1, D_MODEL), lambda l, j: (l, 0, j)),
        ],
        out_specs=pl.BlockSpec((1, 16, D_MODEL), lambda l, j: (l, 0, j)),
        out_shape=jax.ShapeDtypeStruct((depth, 16, w_ada.shape[2]), F32),
        compiler_params=_cparams(2),
        name="ada_mod",
    )(c_all, w_ada, b_ada.reshape(depth, 1, -1))


def _mix_in_kernel(n_prompt_tiles, n_prev, xp_ref, xs_ref, sc_ref, sh_ref, ln_ref, win_ref,
                   cosp_ref, coss_ref, sinp_ref, sins_ref,
                   gv_ref, gw_ref, gb_ref, qag_ref, wqb_ref, kvag_ref, wkvb_ref, gq_ref, gk_ref, off_ref,
                   *refs):
    a_ref, q_ref, k_ref, v_ref, latp_ref, pep_ref, lats_ref, pes_ref, vn_ref = refs[n_prev:]
    i = pl.program_id(0)
    is_sample = i >= n_prompt_tiles
    o2 = 2 * GMLP_WIDTH
    o3 = o2 + Q_LORA_RANK
    o4 = o3 + KV_LORA_RANK
    part = TILE

    def rows_of(p):
        r0 = p * part
        rows = slice(r0, r0 + part)
        x = jnp.where(is_sample, xs_ref[rows, :], xp_ref[rows, :])
        h = _modulate(_rms(x) * ln_ref[...], sc_ref[0], sh_ref[0])
        z = jnp.dot(h.astype(BF16), win_ref[...], preferred_element_type=F32)

        u = _gelu(z[:, :GMLP_WIDTH])
        v = _gelu(z[:, GMLP_WIDTH:o2])
        gv = gv_ref[...]
        vn_parts = []
        for g in range(GMLP_GROUPS):
            sl = slice(g * LANES, (g + 1) * LANES)
            vn_parts.append(_rms(v[:, sl]) * gv[:, sl])
        for g in range(GMLP_GROUPS):
            sl = slice(g * LANES, (g + 1) * LANES)
            vb = vn_parts[g].astype(BF16)
            for c in range(part // GMLP_CHUNK):
                crow = slice(c * GMLP_CHUNK, (c + 1) * GMLP_CHUNK)
                s = jnp.dot(gw_ref[0, g], vb[crow], preferred_element_type=F32) + gb_ref[0, g]
                a_ref[r0 + c * GMLP_CHUNK:r0 + (c + 1) * GMLP_CHUNK, sl] = (u[crow, sl] * s).astype(BF16)

        cos = jnp.where(is_sample, coss_ref[rows, :], cosp_ref[rows, :])
        sin = jnp.where(is_sample, sins_ref[rows, :], sinp_ref[rows, :])

        ql = _rms(z[:, o2:o3]) * qag_ref[...]
        qq = jnp.dot(ql.astype(BF16), wqb_ref[...], preferred_element_type=F32)
        gq = gq_ref[...]
        for hd in range(MLA_HEADS):
            base = hd * 3 * LANES
            nope = qq[:, base:base + LANES]
            rope = qq[:, base + LANES:base + 2 * LANES] * cos + qq[:, base + 2 * LANES:base + 3 * LANES] * sin
            ss = jnp.sum(nope * nope, axis=-1, keepdims=True) + jnp.sum(rope * rope, axis=-1, keepdims=True)
            rinv = lax.rsqrt(ss * (1.0 / QK_HEAD_DIM) + EPS)
            q_ref[hd, rows, :LANES] = (nope * rinv * gq[:, :LANES]).astype(BF16)
            q_ref[hd, rows, LANES:] = (rope * rinv * gq[:, LANES:] + off_ref[0:1, :]).astype(BF16)

        ckv = _rms(z[:, o3:o4]) * kvag_ref[...]
        kpe = z[:, o4:o4 + LANES] * cos + z[:, o4 + LANES:o4 + 2 * LANES] * sin
        kv = jnp.dot(ckv.astype(BF16), wkvb_ref[...], preferred_element_type=F32)
        gk = gk_ref[...]
        kpe_ss = jnp.sum(kpe * kpe, axis=-1, keepdims=True)
        kpe_g = kpe * gk[:, LANES:]
        for hd in range(MLA_HEADS):
            nope = kv[:, hd * LANES:(hd + 1) * LANES]
            ss = jnp.sum(nope * nope, axis=-1, keepdims=True) + kpe_ss
            rinv = lax.rsqrt(ss * (1.0 / QK_HEAD_DIM) + EPS)
            k_ref[hd, rows, :LANES] = (nope * rinv * gk[:, :LANES]).astype(BF16)
            k_ref[hd, rows, LANES:] = (kpe_g * rinv + off_ref[1:2, :]).astype(BF16)
            v_ref[hd, rows, :] = kv[:, (MLA_HEADS + hd) * LANES:(MLA_HEADS + hd + 1) * LANES].astype(BF16)
        return rows, vn_parts, ckv, kpe

    per_layer = [rows_of(0)]

    @pl.when(jnp.logical_not(is_sample))
    def _():
        for rows, _, ckv, kpe in per_layer:
            latp_ref[0, rows, :] = ckv
            pep_ref[0, rows, :] = kpe[:, :QK_ROPE_DIM]

    @pl.when(is_sample)
    def _():
        for rows, vn_parts, ckv, kpe in per_layer:
            lats_ref[0, rows, :] = ckv
            pes_ref[0, rows, :] = kpe[:, :QK_ROPE_DIM]
            for g in range(GMLP_GROUPS):
                vn_ref[0, rows, g * LANES:(g + 1) * LANES] = vn_parts[g]


def _mix_in_call(xp, xs, sc1, sh1, w, layer, depth, prev):
    rp, rs = xp.shape[0], xs.shape[0]
    npt = rp // TILE
    nt = npt + rs // TILE
    r = rp + rs
    last = npt - 1

    def full(a):
        nd = a.ndim
        return pl.BlockSpec(a.shape, lambda i: (0,) * nd)

    def variant(a):
        nd = a.ndim
        return pl.BlockSpec((1,) + a.shape[1:], lambda i: (i // npt,) + (0,) * (nd - 1))

    row = lambda width: pl.BlockSpec((TILE, width), lambda i: (i, 0))
    head = lambda width: pl.BlockSpec((MLA_HEADS, TILE, width), lambda i: (0, i, 0))
    in_specs = [
        pl.BlockSpec((TILE, D_MODEL), lambda i: (jnp.minimum(i, last), 0)),
        pl.BlockSpec((TILE, D_MODEL), lambda i: (0, 0)),
        variant(sc1), variant(sh1), full(w["ln1"]), full(w["w_in"]),
        pl.BlockSpec((TILE, LANES), lambda i: (jnp.minimum(i, last), 0)),
        pl.BlockSpec((TILE, LANES), lambda i: (0, 0)),
        pl.BlockSpec((TILE, LANES), lambda i: (jnp.minimum(i, last), 0)),
        pl.BlockSpec((TILE, LANES), lambda i: (0, 0)),
        full(w["gv"]), variant(w["gw"]), variant(w["gb"]),
        full(w["q_a_g"]), full(w["w_qb"]), full(w["kv_a_g"]), full(w["w_kvb"]),
        full(w["gq"]), full(w["gk"]), full(w["off"]),
    ] + [pl.BlockSpec(memory_space=pl.ANY)] * len(prev)
    prompt_plane = lambda width: pl.BlockSpec((1, TILE, width), lambda i: (layer, jnp.minimum(i, last), 0))
    sample_plane = lambda width: pl.BlockSpec((1, TILE, width), lambda i: (layer, 0, 0))
    out_specs = [
        row(GMLP_WIDTH), head(QK_PAD), head(QK_PAD), head(V_HEAD_DIM),
        prompt_plane(KV_LORA_RANK), prompt_plane(QK_ROPE_DIM),
        sample_plane(KV_LORA_RANK), sample_plane(QK_ROPE_DIM), sample_plane(GMLP_WIDTH),
    ]
    out_shape = [
        jax.ShapeDtypeStruct((r, GMLP_WIDTH), BF16),
        jax.ShapeDtypeStruct((MLA_HEADS, r, QK_PAD), BF16),
        jax.ShapeDtypeStruct((MLA_HEADS, r, QK_PAD), BF16),
        jax.ShapeDtypeStruct((MLA_HEADS, r, V_HEAD_DIM), BF16),
        jax.ShapeDtypeStruct((depth, rp, KV_LORA_RANK), F32),
        jax.ShapeDtypeStruct((depth, rp, QK_ROPE_DIM), F32),
        jax.ShapeDtypeStruct((depth, rs, KV_LORA_RANK), F32),
        jax.ShapeDtypeStruct((depth, rs, QK_ROPE_DIM), F32),
        jax.ShapeDtypeStruct((depth, rs, GMLP_WIDTH), F32),
    ]
    n_fixed = len(in_specs) - len(prev)
    n_own = 4
    return pl.pallas_call(
        functools.partial(_mix_in_kernel, npt, len(prev)),
        grid=(nt,), in_specs=in_specs, out_specs=out_specs, out_shape=out_shape,
        input_output_aliases={n_fixed + j: n_own + j for j in range(len(prev))},
        compiler_params=_cparams(1), name="mix_in",
    )(xp, xs, sc1, sh1, w["ln1"], w["w_in"], *w["rope"], w["gv"], w["gw"], w["gb"],
      w["q_a_g"], w["w_qb"], w["kv_a_g"], w["w_kvb"], w["gq"], w["gk"], w["off"], *prev)


def _cache_kv_kernel(lat_ref, pe_ref, wkvb_ref, gk_ref, off_ref, k_ref, v_ref):
    lat = lat_ref[0]
    pe_t = pe_ref[0, 0]
    kpe = jnp.concatenate([pe_t, jnp.zeros_like(pe_t)], axis=0).T
    kv = jnp.dot(lat.astype(BF16), wkvb_ref[0], preferred_element_type=F32)
    gk = gk_ref[0]
    kpe_ss = jnp.sum(kpe * kpe, axis=-1, keepdims=True)
    kpe_g = kpe * gk[:, LANES:]
    for hd in range(MLA_HEADS):
        nope = kv[:, hd * LANES:(hd + 1) * LANES]
        ss = jnp.sum(nope * nope, axis=-1, keepdims=True) + kpe_ss
        rinv = lax.rsqrt(ss * (1.0 / QK_HEAD_DIM) + EPS)
        k_ref[0, hd, :, :LANES] = (nope * rinv * gk[:, :LANES]).astype(BF16)
        k_ref[0, hd, :, LANES:] = (kpe_g * rinv + off_ref[0, 1:2, :]).astype(BF16)
        v_ref[0, hd] = kv[:, (MLA_HEADS + hd) * LANES:(MLA_HEADS + hd + 1) * LANES].astype(BF16)


def _cache_kv_call(lat, pe_t, w_kvb, gk, off):
    depth, rows, _ = lat.shape
    t = min(CACHE_TILE, pe_t.shape[-1])
    per_stream = pe_t.shape[-1] // t
    return pl.pallas_call(
        _cache_kv_kernel,
        grid=(depth, rows // t),
        in_specs=[
            pl.BlockSpec((1, t, KV_LORA_RANK), lambda l, i: (l, i, 0)),
            pl.BlockSpec((1, 1, QK_ROPE_DIM, t), lambda l, i: (l, i // per_stream, 0, i % per_stream)),
            pl.BlockSpec((1,) + w_kvb.shape[1:], lambda l, i: (l, 0, 0)),
            pl.BlockSpec((1, 1, QK_PAD), lambda l, i: (l, 0, 0)),
            pl.BlockSpec((1, 2, LANES), lambda l, i: (l, 0, 0)),
        ],
        out_specs=[
            pl.BlockSpec((1, MLA_HEADS, t, QK_PAD), lambda l, i: (l, 0, i, 0)),
            pl.BlockSpec((1, MLA_HEADS, t, V_HEAD_DIM), lambda l, i: (l, 0, i, 0)),
        ],
        out_shape=[
            jax.ShapeDtypeStruct((depth, MLA_HEADS, rows, QK_PAD), BF16),
            jax.ShapeDtypeStruct((depth, MLA_HEADS, rows, V_HEAD_DIM), BF16),
        ],
        compiler_params=_cparams(2), name="cache_kv",
    )(lat, pe_t, w_kvb, gk, off)


def _attn_step(q, k, v, m, l, acc, mask):
    s = lax.dot_general(q, k, (((1,), (1,)), ((), ())), preferred_element_type=F32)
    if mask is not None:
        s = jnp.where(mask, s, -1e30)
    m_new = jnp.maximum(m, jnp.max(s, axis=-1, keepdims=True))
    alpha = jnp.exp2(m - m_new)
    p = jnp.exp2(s - m_new)
    l = alpha * l + jnp.sum(p, axis=-1, keepdims=True)
    acc = alpha * acc + jnp.dot(p.astype(BF16), v, preferred_element_type=F32)
    return m_new, l, acc


def _bounded_step(q, k, v_aug, acc, mask):
    s = lax.dot_general(q, k, (((1,), (1,)), ((), ())), preferred_element_type=F32)
    if mask is not None:
        s = jnp.where(mask, s, -1e30)
    return acc + jnp.dot(jnp.exp2(s).astype(BF16), v_aug, preferred_element_type=F32)


def _prompt_attn_kernel(bounded_ref, q_ref, k_ref, v_ref, o_ref):
    qi = pl.program_id(1)
    row = lax.broadcasted_iota(jnp.int32, (ATT_TK, ATT_TK), 0)
    col = lax.broadcasted_iota(jnp.int32, (ATT_TK, ATT_TK), 1)
    mask = (col // CHUNK) <= (row // CHUNK)

    def q_half(h, half):
        return q_ref[h, half * ATT_TK:(half + 1) * ATT_TK, :]

    def kv(h, j):
        off = pl.multiple_of(j * ATT_TK, ATT_TK)
        return k_ref[h, pl.ds(off, ATT_TK), :], v_ref[h, pl.ds(off, ATT_TK), :]

    def out(h, half):
        return o_ref.at[half * ATT_TK:(half + 1) * ATT_TK, h * V_HEAD_DIM:(h + 1) * V_HEAD_DIM]

    @pl.when(bounded_ref[0] > 0)
    def _():
        lane = lax.broadcasted_iota(jnp.int32, (ATT_TK, LANES), 1)
        ones_col = jnp.where(lane == 0, 1.0, 0.0).astype(BF16)

        def kv_aug(h, j):
            k, v = kv(h, j)
            return k, jnp.concatenate([v, ones_col], axis=1)

        def blocks(first, count, accs):
            accs = [list(a) for a in accs]
            for u in range(count):
                for h in range(ATT_HEADS):
                    k, v = kv_aug(h, first + u)
                    for half in range(2):
                        accs[h][half] = _bounded_step(q_half(h, half), k, v, accs[h][half], None)
            return tuple(tuple(a) for a in accs)

        zero = jnp.zeros((ATT_TK, 2 * LANES), F32)
        n_full = 2 * qi
        n_trips = n_full // ATT_UNROLL
        accs = lax.fori_loop(0, n_trips, lambda j, c: blocks(ATT_UNROLL * j, ATT_UNROLL, c),
                             ((zero, zero),) * ATT_HEADS)
        done = ATT_UNROLL * n_trips
        part = ATT_UNROLL // 2
        while part >= 2:
            accs = lax.cond((n_full & part) != 0, lambda c, d=done, p=part: blocks(d, p, c), lambda c: c, accs)
            done = done + (n_full & part)
            part //= 2
        for h in range(ATT_HEADS):
            acc_a, acc_b = accs[h]
            k, v = kv_aug(h, 2 * qi)
            acc_a = _bounded_step(q_half(h, 0), k, v, acc_a, mask)
            acc_b = _bounded_step(q_half(h, 1), k, v, acc_b, None)
            k, v = kv_aug(h, 2 * qi + 1)
            acc_b = _bounded_step(q_half(h, 1), k, v, acc_b, mask)
            out(h, 0)[...] = (acc_a[:, :LANES] / acc_a[:, LANES:LANES + 1]).astype(BF16)
            out(h, 1)[...] = (acc_b[:, :LANES] / acc_b[:, LANES:LANES + 1]).astype(BF16)

    @pl.when(bounded_ref[0] == 0)
    def _():
        def init():
            return (jnp.full((ATT_TK, 1), -1e30, F32), jnp.zeros((ATT_TK, 1), F32),
                    jnp.zeros((ATT_TK, V_HEAD_DIM), F32))

        for h in range(ATT_HEADS):
            qa, qb = q_half(h, 0), q_half(h, 1)

            def body(j, carry, h=h, qa=qa, qb=qb):
                k, v = kv(h, j)
                return _attn_step(qa, k, v, *carry[0], None), _attn_step(qb, k, v, *carry[1], None)

            sa, sb = lax.fori_loop(0, 2 * qi, body, (init(), init()))
            k, v = kv(h, 2 * qi)
            sa = _attn_step(qa, k, v, *sa, mask)
            sb = _attn_step(qb, k, v, *sb, None)
            k, v = kv(h, 2 * qi + 1)
            sb = _attn_step(qb, k, v, *sb, mask)
            out(h, 0)[...] = (sa[2] / sa[1]).astype(BF16)
            out(h, 1)[...] = (sb[2] / sb[1]).astype(BF16)


def _prompt_attn_call(bounded, q, k, v, rp):
    r = q.shape[1]
    resident = lambda arr: pl.BlockSpec((ATT_HEADS,) + arr.shape[1:], lambda h, i, b: (h, 0, 0),
                                        pipeline_mode=pl.Buffered(1))
    return pl.pallas_call(
        _prompt_attn_kernel,
        grid_spec=pltpu.PrefetchScalarGridSpec(
            num_scalar_prefetch=1, grid=(MLA_HEADS // ATT_HEADS, rp // ATT_TQ),
            in_specs=[
                pl.BlockSpec((ATT_HEADS, ATT_TQ, QK_PAD), lambda h, i, b: (h, i, 0)),
                resident(k), resident(v),
            ],
            out_specs=pl.BlockSpec((ATT_TQ, ATT_HEADS * V_HEAD_DIM), lambda h, i, b: (i, h))),
        out_shape=jax.ShapeDtypeStruct((rp, MLA_HEADS * V_HEAD_DIM), BF16),
        compiler_params=_cparams(2), name="prompt_attn",
    )(bounded, q, k, v)


def _sample_attn_kernel(q_ref, kp_ref, vp_ref, kn_ref, vn_ref, o_ref):
    nt = (((1,), (1,)), ((), ()))
    for h in range(MLA_HEADS):
        q = q_ref[h]
        s1 = lax.dot_general(q, kp_ref[0, h], nt, preferred_element_type=F32)
        s2 = lax.dot_general(q, kn_ref[h], nt, preferred_element_type=F32)
        m = jnp.maximum(jnp.max(s1, axis=-1, keepdims=True), jnp.max(s2, axis=-1, keepdims=True))
        p1 = jnp.exp2(s1 - m)
        p2 = jnp.exp2(s2 - m)
        l = jnp.sum(p1, axis=-1, keepdims=True) + jnp.sum(p2, axis=-1, keepdims=True)
        o = (jnp.dot(p1.astype(BF16), vp_ref[0, h], preferred_element_type=F32)
             + jnp.dot(p2.astype(BF16), vn_ref[h], preferred_element_type=F32))
        o_ref[:, h * V_HEAD_DIM:(h + 1) * V_HEAD_DIM] = (o / l).astype(BF16)


def _sample_attn_call(layer, q, k, v, k_past, v_past, rp, n_seq, s_len, past):
    first = rp // s_len
    new = lambda width: pl.BlockSpec((MLA_HEADS, s_len, width), lambda b: (0, first + b, 0))
    cached = lambda width: pl.BlockSpec((1, MLA_HEADS, past, width), lambda b: (layer, 0, b, 0))
    return pl.pallas_call(
        _sample_attn_kernel,
        grid=(n_seq,),
        in_specs=[new(QK_PAD), cached(QK_PAD), cached(V_HEAD_DIM), new(QK_PAD), new(V_HEAD_DIM)],
        out_specs=pl.BlockSpec((s_len, MLA_HEADS * V_HEAD_DIM), lambda b: (b, 0)),
        out_shape=jax.ShapeDtypeStruct((n_seq * s_len, MLA_HEADS * V_HEAD_DIM), BF16),
        compiler_params=_cparams(1), name="sample_attn",
    )(q, k_past, v_past, k, v)


def _route(logits_t, bias_col):
    scores = 1.0 / (1.0 + jnp.exp(-logits_t))
    biased = scores + bias_col
    s_rows = [scores[e:e + 1, :] for e in range(N_EXPERTS)]
    b_rows = [biased[e:e + 1, :] for e in range(N_EXPERTS)]
    group_score = []
    for g in range(N_GROUPS):
        rows = b_rows[g * EXPERTS_PER_GROUP:(g + 1) * EXPERTS_PER_GROUP]
        best = None
        for a in range(EXPERTS_PER_GROUP):
            for b in range(a + 1, EXPERTS_PER_GROUP):
                pair = rows[a] + rows[b]
                best = pair if best is None else jnp.maximum(best, pair)
        group_score.append(best)
    best_group = jnp.zeros_like(group_score[0], dtype=jnp.int32)
    best_val = group_score[0]
    for g in range(1, N_GROUPS):
        better = group_score[g] > best_val
        best_group = jnp.where(better, g, best_group)
        best_val = jnp.where(better, group_score[g], best_val)
    selected = []
    for e in range(N_EXPERTS):
        g = e // EXPERTS_PER_GROUP
        rank = jnp.zeros_like(best_group)
        for j in range(g * EXPERTS_PER_GROUP, (g + 1) * EXPERTS_PER_GROUP):
            if j == e:
                continue
            ahead = b_rows[j] > b_rows[e]
            if j < e:
                ahead = ahead | (b_rows[j] == b_rows[e])
            rank = rank + ahead.astype(jnp.int32)
        selected.append((best_group == g) & (rank < 2))
    denom = jnp.zeros_like(s_rows[0])
    for e in range(N_EXPERTS):
        denom = denom + jnp.where(selected[e], s_rows[e], 0.0)
    classes = [selected[g * EXPERTS_PER_GROUP + a] & selected[g * EXPERTS_PER_GROUP + b]
               for g in range(N_GROUPS) for a, b in EXPERT_PAIRS]
    return [jnp.where(selected[e], s_rows[e] / denom, 0.0) for e in range(N_EXPERTS)], classes


def _mix_out_kernel(n_prompt_tiles, a_ref, bp_ref, bs_ref, xp_ref, xs_ref, wa_ref, wb_ref, g1_ref, sc_ref,
                    sh_ref, g2_ref, ln_ref, rw_ref, rb_ref, sg_ref, su_ref, sd_ref, upper_ref,
                    ysh_ref, h2e_ref, grp_ref, rank_ref, cnt_ref, gt_ref, oh_ref, carry_ref):
    i = pl.program_id(0)
    is_sample = i >= n_prompt_tiles

    @pl.when(i == 0)
    def _():
        carry_ref[...] = jnp.zeros_like(carry_ref)

    x = jnp.where(is_sample, xs_ref[...], xp_ref[...])
    b = jnp.where(is_sample, bs_ref[...], bp_ref[...])
    mix = (jnp.dot(a_ref[...], wa_ref[...], preferred_element_type=F32)
           + jnp.dot(b, wb_ref[...], preferred_element_type=F32))
    xm = _gated_add(x, g1_ref[0], mix)
    h2 = _modulate(_rms(xm) * ln_ref[...], sc_ref[0], sh_ref[0])
    h2e_ref[:, :D_MODEL] = h2
    h2_hi = h2.astype(BF16)
    act = _silu(jnp.dot(h2_hi, sg_ref[...], preferred_element_type=F32)) * jnp.dot(
        h2_hi, su_ref[...], preferred_element_type=F32)
    shared = jnp.dot(act.astype(BF16), sd_ref[...], preferred_element_type=F32)
    ysh_ref[...] = _gated_add(xm, g2_ref[0], shared)
    h2_lo = (h2 - h2_hi.astype(F32)).astype(BF16)
    by_hi = jnp.dot(h2_hi, rw_ref[...], preferred_element_type=F32)
    logits = (by_hi[:, :LANES] + by_hi[:, LANES:]
              + jnp.dot(h2_lo, rw_ref[:, :LANES], preferred_element_type=F32))
    gate_rows, classes = _route(logits.T[:N_EXPERTS, :], rb_ref[...])
    gt_ref[...] = jnp.zeros_like(gt_ref)
    for e in range(N_EXPERTS):
        gt_ref[e:e + 1, :] = gate_rows[e]
    h2e_ref[:, D_MODEL:] = gt_ref[...].T

    oh_ref[...] = jnp.zeros_like(oh_ref)
    for c in range(N_CLASSES):
        oh_ref[c:c + 1, :] = jnp.where(classes[c], 1.0, 0.0)
    onehot = oh_ref[...]
    cum = jnp.dot(onehot.astype(BF16), upper_ref[...], preferred_element_type=F32)
    carry = carry_ref[...]
    rank = jnp.sum(onehot * (cum - 1.0 + carry[:, :1]), axis=0, keepdims=True)
    cls_id = lax.broadcasted_iota(jnp.int32, (CLASS_ROWS, TILE), 0).astype(F32)
    cls_row = jnp.sum(onehot * cls_id, axis=0, keepdims=True).astype(jnp.int32)
    grp_ref[0] = jnp.broadcast_to(cls_row, (8, TILE))
    rank_ref[0] = jnp.broadcast_to(rank.astype(jnp.int32), (8, TILE))
    carry = carry + jnp.sum(onehot, axis=1, keepdims=True)
    carry_ref[...] = carry
    cnt_ref[...] = carry


def _mix_out_call(a, bp, bs, xp, xs, g1, sc2, sh2, g2, w):
    rp, rs = xp.shape[0], xs.shape[0]
    upper = jnp.triu(jnp.ones((TILE, TILE), BF16))
    npt = rp // TILE
    nt = npt + rs // TILE
    r = rp + rs
    last = npt - 1

    def full(arr):
        nd = arr.ndim
        return pl.BlockSpec(arr.shape, lambda i: (0,) * nd)

    def variant(arr):
        nd = arr.ndim
        return pl.BlockSpec((1,) + arr.shape[1:], lambda i: (i // npt,) + (0,) * (nd - 1))

    row = lambda width: pl.BlockSpec((TILE, width), lambda i: (i, 0))
    prow = lambda width: pl.BlockSpec((TILE, width), lambda i: (jnp.minimum(i, last), 0))
    srow = lambda width: pl.BlockSpec((TILE, width), lambda i: (0, 0))
    return pl.pallas_call(
        functools.partial(_mix_out_kernel, npt),
        grid=(nt,),
        in_specs=[row(GMLP_WIDTH), prow(GMLP_WIDTH), srow(GMLP_WIDTH), prow(D_MODEL), srow(D_MODEL),
                  full(w["w_out_a"]), full(w["w_out_b"]), variant(g1), variant(sc2), variant(sh2),
                  variant(g2), full(w["ln2"]), full(w["rw"]), full(w["rb"]),
                  full(w["sg"]), full(w["su"]), full(w["sd"]), full(upper)],
        out_specs=[row(D_MODEL), row(H2E_COLS),
                   pl.BlockSpec((1, 8, TILE), lambda i: (i, 0, 0)),
                   pl.BlockSpec((1, 8, TILE), lambda i: (i, 0, 0)),
                   pl.BlockSpec((CLASS_ROWS, LANES), lambda i: (0, 0))],
        out_shape=[jax.ShapeDtypeStruct((r, D_MODEL), F32),
                   jax.ShapeDtypeStruct((r, H2E_COLS), F32),
                   jax.ShapeDtypeStruct((nt, 8, TILE), jnp.int32),
                   jax.ShapeDtypeStruct((nt, 8, TILE), jnp.int32),
                   jax.ShapeDtypeStruct((CLASS_ROWS, LANES), F32)],
        scratch_shapes=[pltpu.VMEM((LANES, TILE), F32), pltpu.VMEM((CLASS_ROWS, TILE), F32),
                        pltpu.VMEM((CLASS_ROWS, LANES), F32)],
        compiler_params=_cparams(1), name="mix_out",
    )(a, bp, bs, xp, xs, w["w_out_a"], w["w_out_b"], g1, sc2, sh2, g2, w["ln2"], w["rw"], w["rb"],
      w["sg"], w["su"], w["sd"], upper)


def _silu(x):
    return x / (1.0 + jnp.exp(-x))


def _invert_positions(pos_ref, pad_lo_ref, pad_hi_ref, src_ref):
    def clear(s, carry):
        src_ref[s] = 0
        return carry

    for g in range(N_GROUPS):
        lax.fori_loop(pad_lo_ref[g], pad_hi_ref[g], clear, 0)
    lax.fori_loop(pad_hi_ref[N_GROUPS - 1], src_ref.shape[0], clear, 0)

    def place(i, carry):
        src_ref[pos_ref[i]] = i
        return carry

    lax.fori_loop(0, pos_ref.shape[0], place, 0, unroll=ROW_DMA_UNROLL)


def _gather_rows(idx_ref, base, src_ref, dst_ref, sem):
    for r in range(TILE):
        pltpu.make_async_copy(src_ref.at[pl.ds(idx_ref[base + r], 1)], dst_ref.at[pl.ds(r, 1)],
                              sem).start(priority=r % 2)


def _wait_rows(src_ref, dst_ref, sem):
    pltpu.make_async_copy(src_ref.at[pl.ds(0, TILE)], dst_ref, sem).wait()


def _routed_kernel(grp_ref, on_ref, need_ref, pos_ref, pad_lo_ref, pad_hi_ref, h_ref, wg_ref, wu_ref, wd_ref,
                   o_ref, buf_ref, src_ref, sem):
    t = pl.program_id(0)
    n = pl.num_programs(0)
    slot = t % 2
    on = on_ref[t] > 0

    @pl.when(t == 0)
    def _():
        _invert_positions(pos_ref, pad_lo_ref, pad_hi_ref, src_ref)

    @pl.when((t == 0) & on)
    def _():
        _gather_rows(src_ref, 0, h_ref, buf_ref.at[0], sem.at[0])

    @pl.when((on_ref[jnp.minimum(t + 1, n - 1)] > 0) & (t + 1 < n))
    def _():
        _gather_rows(src_ref, (t + 1) * TILE, h_ref, buf_ref.at[1 - slot], sem.at[1 - slot])

    @pl.when(jnp.logical_not(on))
    def _():
        o_ref[...] = jnp.zeros_like(o_ref)

    @pl.when(on)
    def _():
        _wait_rows(h_ref, buf_ref.at[slot], sem.at[slot])
        first = grp_ref[t] * EXPERTS_PER_GROUP
        o_ref[...] = jnp.zeros_like(o_ref)
        for e in range(EXPERTS_PER_GROUP):
            @pl.when(need_ref[t * EXPERTS_PER_GROUP + e] > 0)
            def _(e=e):
                h = buf_ref[slot, :, :D_MODEL]
                hg = jnp.dot(h, wg_ref[0, e], preferred_element_type=F32)
                hu = jnp.dot(h, wu_ref[0, e], preferred_element_type=F32)
                lane = lax.broadcasted_iota(jnp.int32, (TILE, LANES), 1)
                gate = jnp.sum(jnp.where(lane == first + e, buf_ref[slot, :, D_MODEL:], 0.0),
                               axis=-1, keepdims=True)
                o_ref[...] += jnp.dot(_silu(hg) * hu, wd_ref[0, e], preferred_element_type=F32) * gate


def _routed_call(tile_grp, tile_on, tile_need, pos, pad_lo, pad_hi, n_slots, h2e, wg, wu, wd, layer):
    nts = n_slots // TILE
    wspec = lambda arr: pl.BlockSpec((1,) + arr.shape[1:],
                                     lambda t, grp, *_: (layer * N_GROUPS + grp[t], 0, 0, 0))
    return pl.pallas_call(
        _routed_kernel,
        grid_spec=pltpu.PrefetchScalarGridSpec(
            num_scalar_prefetch=6, grid=(nts,),
            in_specs=[pl.BlockSpec(memory_space=pl.ANY), wspec(wg), wspec(wu), wspec(wd)],
            out_specs=pl.BlockSpec((TILE, D_MODEL), lambda t, *_: (t, 0)),
            scratch_shapes=[pltpu.VMEM((2, TILE, H2E_COLS), F32), pltpu.SMEM((n_slots,), jnp.int32),
                            pltpu.SemaphoreType.DMA((2,))]),
        out_shape=jax.ShapeDtypeStruct((n_slots, D_MODEL), F32),
        compiler_params=pltpu.CompilerParams(dimension_semantics=("arbitrary",),
                                             vmem_limit_bytes=ROUTED_VMEM_LIMIT), name="moe_routed",
    )(tile_grp, tile_on, tile_need, pos, pad_lo, pad_hi, h2e, wg, wu, wd)


def _combine_kernel(n_prompt_tiles, pos_ref, ysh_ref, g2_ref, routed_ref, yp_ref, ys_ref, buf_ref, sem):
    i = pl.program_id(0)
    n = pl.num_programs(0)
    slot = i % 2

    @pl.when(i == 0)
    def _():
        _gather_rows(pos_ref, 0, routed_ref, buf_ref.at[0], sem.at[0])

    @pl.when(i + 1 < n)
    def _():
        _gather_rows(pos_ref, (i + 1) * TILE, routed_ref, buf_ref.at[1 - slot], sem.at[1 - slot])

    _wait_rows(routed_ref, buf_ref.at[slot], sem.at[slot])
    y = _gated_add(ysh_ref[...], g2_ref[0], buf_ref[slot])

    @pl.when(i < n_prompt_tiles)
    def _():
        yp_ref[...] = y

    @pl.when(i >= n_prompt_tiles)
    def _():
        ys_ref[...] = y


def _combine_call(pos, ysh, g2, routed, rp):
    r = ysh.shape[0]
    rs = r - rp
    npt = rp // TILE
    last = npt - 1
    return pl.pallas_call(
        functools.partial(_combine_kernel, npt),
        grid_spec=pltpu.PrefetchScalarGridSpec(
            num_scalar_prefetch=1, grid=(r // TILE,),
            in_specs=[pl.BlockSpec((TILE, D_MODEL), lambda i, pos: (i, 0)),
                      pl.BlockSpec((1,) + g2.shape[1:], lambda i, pos: (i // npt, 0, 0)),
                      pl.BlockSpec(memory_space=pl.ANY)],
            out_specs=[pl.BlockSpec((TILE, D_MODEL), lambda i, pos: (jnp.minimum(i, last), 0)),
                       pl.BlockSpec((TILE, D_MODEL), lambda i, pos: (0, 0))],
            scratch_shapes=[pltpu.VMEM((2, TILE, D_MODEL), F32), pltpu.SemaphoreType.DMA((2,))]),
        out_shape=[jax.ShapeDtypeStruct((rp, D_MODEL), F32),
                   jax.ShapeDtypeStruct((rs, D_MODEL), F32)],
        compiler_params=_cparams(1), name="moe_combine",
    )(pos, ysh, g2, routed)


def _moe(h2e, cls, rank, counts, ysh, g2, experts, layer, rp):
    r = h2e.shape[0]
    n_pairs = len(EXPERT_PAIRS)
    n_sorted_tiles = r // TILE + N_GROUPS
    cnt_c = counts[:N_CLASSES, 0].astype(jnp.int32).reshape(N_GROUPS, n_pairs)
    cnt = jnp.sum(cnt_c, axis=1)
    tiles_g = (cnt + TILE - 1) // TILE
    end_g = jnp.cumsum(tiles_g)
    off_g = (end_g - tiles_g) * TILE
    start_c = (off_g[:, None] + jnp.cumsum(cnt_c, axis=1) - cnt_c).reshape(N_CLASSES)
    end_c = start_c + cnt_c.reshape(N_CLASSES)
    cls_flat = cls[:, 0, :].reshape(r)
    in_class = cls_flat[:, None] == jnp.arange(N_CLASSES, dtype=jnp.int32)[None, :]
    pos = rank[:, 0, :].reshape(r) + jnp.sum(jnp.where(in_class, start_c[None, :], 0), axis=1)
    t_idx = jnp.arange(n_sorted_tiles, dtype=jnp.int32)
    tile_grp = jnp.zeros_like(t_idx)
    for g in range(N_GROUPS - 1):
        tile_grp = tile_grp + (t_idx >= end_g[g]).astype(jnp.int32)
    tile_on = (t_idx < end_g[N_GROUPS - 1]).astype(jnp.int32)
    lo = t_idx[:, None] * TILE
    overlap = ((start_c[None, :] < lo + TILE) & (end_c[None, :] > lo)
               & (end_c > start_c)[None, :])
    member = jnp.array([[int(e in pair) for e in range(EXPERTS_PER_GROUP)] for pair in EXPERT_PAIRS] * N_GROUPS,
                       dtype=jnp.int32)
    tile_need = jnp.max(overlap[:, :, None].astype(jnp.int32) * member[None], axis=1).reshape(-1)
    routed = _routed_call(tile_grp, tile_on, tile_need, pos, off_g + cnt, end_g * TILE,
                          n_sorted_tiles * TILE, h2e, *experts, layer)
    return _combine_call(pos, ysh, g2, routed, rp)


def _rot_half_cols(wcols):
    half = QK_ROPE_DIM // 2
    return jnp.concatenate([-wcols[:, half:], wcols[:, :half]], axis=1)


def _pad_cols(wcols, width):
    return jnp.pad(wcols, ((0, 0), (0, width - wcols.shape[1])))


def _split_bf16(w):
    hi = w.astype(BF16)
    lo = (w - hi.astype(F32)).astype(BF16)
    return jnp.concatenate([hi, lo], axis=1)


def _rope_tables(seq, past, s_len, n_seq):
    half = QK_ROPE_DIM // 2
    inv = ROPE_THETA ** (-jnp.arange(half, dtype=F32) / half)
    inv_wide = jnp.concatenate([inv, inv, jnp.zeros((LANES - QK_ROPE_DIM,), F32)])
    live = (jnp.arange(LANES) < QK_ROPE_DIM).astype(F32)

    def cos_sin(pos):
        ang = pos.astype(F32)[:, None] * inv_wide[None, :]
        return jnp.cos(ang), jnp.sin(ang)

    cc, sc = cos_sin(jnp.arange(0, seq, GMLP_CHUNK, dtype=jnp.int32))
    cf, sf = cos_sin(jnp.arange(GMLP_CHUNK, dtype=jnp.int32))
    cos_p = ((cc[:, None, :] * cf[None, :, :] - sc[:, None, :] * sf[None, :, :]) * live).reshape(seq, LANES)
    sin_p = ((sc[:, None, :] * cf[None, :, :] + cc[:, None, :] * sf[None, :, :]) * live).reshape(seq, LANES)
    cos_s, sin_s = cos_sin(past + jnp.arange(s_len, dtype=jnp.int32))
    return cos_p, jnp.tile(cos_s * live, (n_seq, 1)), sin_p, jnp.tile(sin_s * live, (n_seq, 1))


def _layer_weights(l, p, rope):
    o1 = GMLP_WIDTH
    o2 = 2 * GMLP_WIDTH
    o3 = o2 + Q_LORA_RANK
    o4 = o3 + KV_LORA_RANK
    w_in = p["w_in"][l]
    kpe_cols = w_in[:, o4:]
    w_in2 = jnp.concatenate([w_in[:, :o4], _pad_cols(kpe_cols, LANES),
                             _pad_cols(_rot_half_cols(kpe_cols), LANES)], axis=1).astype(BF16)
    w_qb = p["w_qb"][l]
    q_parts = []
    for hd in range(MLA_HEADS):
        base = hd * QK_HEAD_DIM
        rope_cols = w_qb[:, base + QK_NOPE_DIM:base + QK_HEAD_DIM]
        q_parts += [w_qb[:, base:base + QK_NOPE_DIM], _pad_cols(rope_cols, LANES),
                    _pad_cols(_rot_half_cols(rope_cols), LANES)]
    w_qb2 = jnp.concatenate(q_parts, axis=1).astype(BF16)
    w_kvb = p["w_kvb"][l].reshape(KV_LORA_RANK, MLA_HEADS, QK_NOPE_DIM + V_HEAD_DIM)
    w_kvb2 = jnp.concatenate([w_kvb[:, :, :QK_NOPE_DIM].reshape(KV_LORA_RANK, -1),
                              w_kvb[:, :, QK_NOPE_DIM:].reshape(KV_LORA_RANK, -1)], axis=1).astype(BF16)
    qscale = LOG2E / math.sqrt(QK_HEAD_DIM)
    gq = _pad_cols(p["q_norm_g"][l][None, :] * qscale, QK_PAD)
    gk = _pad_cols(p["k_norm_g"][l][None, :], QK_PAD)
    bound = (QK_HEAD_DIM * qscale * SCORE_BOUND_MARGIN * jnp.max(jnp.abs(p["q_norm_g"][l]))
             * jnp.max(jnp.abs(p["k_norm_g"][l])))
    bounded = bound <= MAX_SCORE_BOUND
    pad_lane = jnp.arange(LANES) == QK_ROPE_DIM
    off = jnp.stack([jnp.where(pad_lane, 1.0, 0.0),
                     jnp.where(pad_lane & bounded, -bound, 0.0)]).astype(F32)

    ws = p["gmlp_ws"][l]
    tri = jnp.tril(jnp.ones((GMLP_CHUNK, GMLP_CHUNK), dtype=bool))
    wt = jnp.where(tri[None], ws, 0.0)
    hc = GMLP_CHUNK // 2
    top = wt[:, :hc, :hc]
    zero = jnp.zeros_like(top)
    wt_s = jnp.concatenate([jnp.concatenate([top, zero], axis=2), jnp.concatenate([zero, top], axis=2)], axis=1)
    gw = jnp.stack([wt, wt_s]).astype(BF16)
    b = p["gmlp_b"][l]
    b_s = jnp.concatenate([b[:, :hc], b[:, :hc]], axis=1)
    gb = jnp.broadcast_to(jnp.stack([b, b_s])[..., None], (2, GMLP_GROUPS, GMLP_CHUNK, LANES)).astype(F32)

    w_out = p["w_out"][l].astype(BF16)
    return dict(
        sg=p["sh_w_gate"][l].astype(BF16),
        su=p["sh_w_up"][l].astype(BF16), sd=p["sh_w_down"][l].astype(BF16),
        ln1=p["ln1_g"][l][None, :], w_in=w_in2, rope=rope,
        gv=p["gmlp_v_g"][l].reshape(1, GMLP_WIDTH), gw=gw, gb=gb,
        q_a_g=p["q_a_g"][l][None, :], w_qb=w_qb2, kv_a_g=p["kv_a_g"][l][None, :], w_kvb=w_kvb2,
        gq=gq, gk=gk, w_out_a=w_out[:GMLP_WIDTH], w_out_b=w_out[GMLP_WIDTH:],
        ln2=p["ln2_g"][l][None, :], rw=_split_bf16(_pad_cols(p["router_w"], LANES)),
        rb=p["router_bias"].reshape(N_EXPERTS, 1), off=off,
        bounded=bounded.astype(jnp.int32).reshape(1),
    )


def kernel(x_prompt, x_sample, cache_kv_latent, cache_k_rope, c_prompt, c_sample, w_ada, b_ada, ln1_g, w_in,
           gmlp_v_g, gmlp_ws, gmlp_b, q_a_g, w_qb, kv_a_g, w_kvb, q_norm_g, k_norm_g, w_out, ln2_g, router_w,
           router_bias, exp_w_gate, exp_w_up, exp_w_down, sh_w_gate, sh_w_up, sh_w_down):
    p = dict(w_in=w_in, gmlp_v_g=gmlp_v_g, gmlp_ws=gmlp_ws, gmlp_b=gmlp_b, q_a_g=q_a_g, w_qb=w_qb,
             kv_a_g=kv_a_g, w_kvb=w_kvb, q_norm_g=q_norm_g, k_norm_g=k_norm_g, w_out=w_out, ln1_g=ln1_g,
             ln2_g=ln2_g, router_w=router_w, router_bias=router_bias, exp_w_gate=exp_w_gate,
             exp_w_up=exp_w_up, exp_w_down=exp_w_down, sh_w_gate=sh_w_gate, sh_w_up=sh_w_up,
             sh_w_down=sh_w_down)
    batch, seq, _ = x_prompt.shape
    n_seq, s_len, _ = x_sample.shape
    depth, _, past, _ = cache_kv_latent.shape
    assert batch == 1 and s_len == SUB and n_seq == N_SUB and n_seq * s_len == TILE
    assert seq % TILE == 0 and seq % ATT_TQ == 0 and past % GMLP_CHUNK == 0 and past % CHUNK == 0
    rp = seq
    rs = n_seq * s_len

    rope = _rope_tables(seq, past, s_len, n_seq)

    c_all = jnp.concatenate([jnp.broadcast_to(c_prompt, (N_SUB, D_MODEL)), c_sample], axis=0)
    mod = _ada_call(c_all, w_ada, b_ada)
    mod = mod.reshape(depth, 2, N_SUB, 6, D_MODEL)

    weights = [_layer_weights(l, p, rope) for l in range(depth)]
    experts = (exp_w_gate.reshape(depth * N_GROUPS, EXPERTS_PER_GROUP, D_MODEL, EXPERT_FF),
               exp_w_up.reshape(depth * N_GROUPS, EXPERTS_PER_GROUP, D_MODEL, EXPERT_FF),
               exp_w_down.reshape(depth * N_GROUPS, EXPERTS_PER_GROUP, EXPERT_FF, D_MODEL))
    lat_all = cache_kv_latent.reshape(depth, n_seq * past, KV_LORA_RANK)
    pe_all = jnp.swapaxes(cache_k_rope, 2, 3)
    k_past, v_past = _cache_kv_call(lat_all, pe_all, jnp.stack([w["w_kvb"] for w in weights]),
                                    jnp.stack([w["gk"] for w in weights]),
                                    jnp.stack([w["off"] for w in weights]))

    xp = x_prompt.reshape(rp, D_MODEL)
    xs = x_sample.reshape(rs, D_MODEL)
    planes = tuple(jnp.zeros((depth, rows, width), F32) for rows, width in
                   ((rp, KV_LORA_RANK), (rp, QK_ROPE_DIM), (rs, KV_LORA_RANK), (rs, QK_ROPE_DIM),
                    (rs, GMLP_WIDTH)))
    for l in range(depth):
        w = weights[l]
        sh1, sc1, g1, sh2, sc2, g2 = [mod[l, :, :, j, :] for j in range(6)]
        a, q, k, v, *planes = _mix_in_call(xp, xs, sc1, sh1, w, l, depth, tuple(planes))
        bp = _prompt_attn_call(w["bounded"], q, k, v, rp)
        bs = _sample_attn_call(l, q, k, v, k_past, v_past, rp, n_seq, s_len, past)
        ysh, h2e, grp, rank, counts = _mix_out_call(a, bp, bs, xp, xs, g1, sc2, sh2, g2, w)
        xp, xs = _moe(h2e, grp, rank, counts, ysh, g2, experts, l, rp)
    lat_p, pe_p, lat_s, pe_s, v_s = planes
    return (xp.reshape(batch, seq, D_MODEL), xs.reshape(n_seq, s_len, D_MODEL),
            lat_p.reshape(depth, batch, seq, KV_LORA_RANK), pe_p.reshape(depth, batch, seq, QK_ROPE_DIM),
            lat_s.reshape(depth, n_seq, s_len, KV_LORA_RANK), pe_s.reshape(depth, n_seq, s_len, QK_ROPE_DIM),
            v_s.reshape(depth, n_seq, s_len, GMLP_WIDTH))
```
